```python
import math
import jax, jax.numpy as jnp
from jax import lax
import numpy as np

D_MODEL = 2048
BATCH = 1
SEQ = 8192
DEPTH = 1

ATT_HEAD_DIM = 128
ATT_HEADS_PER_GROUP = 8
DILATION_GROUPS = ((128, 1), (512, 4), (2048, 16))
N_ATT_GROUPS = len(DILATION_GROUPS)
ATT_GROUP_WIDTH = ATT_HEADS_PER_GROUP * ATT_HEAD_DIM
ROPE_DIM = ATT_HEAD_DIM // 4
ROPE_THETA = 500000.0
ATT_BLOCK = 128

RET_HEADS = 8
RET_QK_DIM = D_MODEL // 16
RET_V_DIM = 2 * RET_QK_DIM
RET_CHUNK = 128
RET_ROT_BASE = 10000.0

N_EXPERT_GROUPS = 4
EXPERTS_PER_GROUP = 8
N_EXPERTS = N_EXPERT_GROUPS * EXPERTS_PER_GROUP
TOP_K = 2
EXPERT_HIDDEN = D_MODEL // 4
MOE_BLOCK = 128

ALPHA = (2.0 * DEPTH) ** 0.25
BETA = (8.0 * DEPTH) ** -0.25
LN_EPS = 1e-5
NEG_INF = -1e30

IN_SPLITS = (N_ATT_GROUPS * ATT_GROUP_WIDTH,) * 3 + (RET_HEADS * RET_QK_DIM,) * 2 + (RET_HEADS * RET_V_DIM,) * 2 + (D_MODEL,) * 2
IN_WIDTH = sum(IN_SPLITS)

kernel_name = "hybrid_dilated_attn_retention_hmoe_deepnorm"


def layer_norm(x, scale, bias):
    xf = x.astype(jnp.float32)
    mu = jnp.mean(xf, axis=-1, keepdims=True)
    var = jnp.mean(jnp.square(xf - mu), axis=-1, keepdims=True)
    y = (xf - mu) * lax.rsqrt(var + LN_EPS) * scale.astype(jnp.float32) + bias.astype(jnp.float32)
    return y.astype(x.dtype)


def rotary(x, positions, inv_freq):
    rot = 2 * inv_freq.shape[0]
    ang = positions.astype(jnp.float32)[:, :, None, None] * inv_freq
    cos, sin = jnp.cos(ang), jnp.sin(ang)
    xr = x[..., :rot].astype(jnp.float32)
    x1, x2 = xr[..., : rot // 2], xr[..., rot // 2:]
    rotated = jnp.concatenate([x1 * cos - x2 * sin, x2 * cos + x1 * sin], axis=-1).astype(x.dtype)
    return jnp.concatenate([rotated, x[..., rot:]], axis=-1)


def dilated_window_attention(q, k, v, dilation, steps):
    B, S, H, Dh = q.shape
    M = -(-S // (dilation * ATT_BLOCK)) * ATT_BLOCK
    pad = M * dilation - S
    nb = M // ATT_BLOCK

    def to_blocks(t):
        t = jnp.pad(t, ((0, 0), (0, pad), (0, 0), (0, 0)))
        t = t.reshape(B, M, dilation, H, Dh).transpose(0, 2, 1, 3, 4)
        return t.reshape(B * dilation, nb, ATT_BLOCK, H, Dh)

    def with_prev(t):
        prev = jnp.pad(t[:, :-1], ((0, 0), (1, 0), (0, 0), (0, 0), (0, 0)))
        return jnp.concatenate([prev, t], axis=2)

    qb = to_blocks(q)
    kw = with_prev(to_blocks(k))
    vw = with_prev(to_blocks(v))
    s = jnp.einsum('bnqhd,bnkhd->bnhqk', qb, kw, preferred_element_type=jnp.float32) * (Dh ** -0.5)
    i = jnp.arange(ATT_BLOCK)[:, None]
    j = jnp.arange(2 * ATT_BLOCK)[None, :]
    dist = i + ATT_BLOCK - j
    band = (dist >= 0) & (dist <= steps)
    has_prev = (jnp.arange(nb) > 0)[:, None, None] | (j >= ATT_BLOCK)[None]
    mask = band[None] & has_prev
    s = jnp.where(mask[None, :, None], s, NEG_INF)
    lse = jax.nn.logsumexp(s, axis=-1)
    p = jnp.exp(s - lse[..., None])
    o = jnp.einsum('bnhqk,bnkhd->bnqhd', p.astype(v.dtype), vw)
    o = o.reshape(B, dilation, M, H, Dh).transpose(0, 2, 1, 3, 4).reshape(B, M * dilation, H, Dh)[:, :S]
    lse = lse.transpose(0, 1, 3, 2).reshape(B, dilation, M, H).transpose(0, 2, 1, 3).reshape(B, M * dilation, H)[:, :S]
    return o, lse


def retention_chunkwise(q, k, v, log_decay):
    B, S, H, dk = q.shape
    dv = v.shape[-1]
    nc = S // RET_CHUNK
    qc = q.reshape(B, nc, RET_CHUNK, H, dk)
    kc = k.reshape(B, nc, RET_CHUNK, H, dk)
    vc = v.reshape(B, nc, RET_CHUNK, H, dv)
    n = jnp.arange(RET_CHUNK, dtype=jnp.float32)
    diff = n[:, None] - n[None, :]
    intra = jnp.where(diff >= 0, jnp.exp(jnp.maximum(diff, 0.0) * log_decay[:, None, None]), 0.0)
    scores = jnp.einsum('bcnhd,bcmhd->bchnm', qc, kc) * intra
    inner = jnp.einsum('bchnm,bcmhe->bcnhe', scores, vc)
    k_decay = jnp.exp((RET_CHUNK - 1 - n)[None, :] * log_decay[:, None])
    kv = jnp.einsum('bcmhd,hm,bcmhe->bchde', kc, k_decay, vc)
    chunk_decay = jnp.exp(RET_CHUNK * log_decay)[None, :, None, None]

    def step(state, kv_c):
        return chunk_decay * state + kv_c, state

    _, prev_states = lax.scan(step, jnp.zeros((B, H, dk, dv), jnp.float32), kv.transpose(1, 0, 2, 3, 4))
    q_decay = jnp.exp((n + 1.0)[None, :] * log_decay[:, None])
    cross = jnp.einsum('bcnhd,hn,cbhde->bcnhe', qc, q_decay, prev_states)
    return (inner + cross).reshape(B, S, H, dv)


def mixer_sublayer(h, positions, w_in, w_branch_attn, w_branch_ret, w_out):
    B, S, _ = h.shape
    proj = jnp.einsum('bsd,df->bsf', h, w_in)
    split_points = list(np.cumsum(IN_SPLITS)[:-1])
    q_a, k_a, v_a, q_r, k_r, v_r, g_r, gate_a, gate_b = jnp.split(proj, split_points, axis=-1)

    att_shape = (B, S, N_ATT_GROUPS * ATT_HEADS_PER_GROUP, ATT_HEAD_DIM)
    inv_a = ROPE_THETA ** (-jnp.arange(0, ROPE_DIM, 2, dtype=jnp.float32) / ROPE_DIM)
    grp_shape = (B, S, N_ATT_GROUPS, ATT_HEADS_PER_GROUP, ATT_HEAD_DIM)
    q_a = rotary(q_a.reshape(att_shape), positions, inv_a).reshape(grp_shape)
    k_a = rotary(k_a.reshape(att_shape), positions, inv_a).reshape(grp_shape)
    v_a = v_a.reshape(grp_shape)
    outs, lses = [], []
    for g, (window, dilation) in enumerate(DILATION_GROUPS):
        o_g, lse_g = dilated_window_attention(q_a[:, :, g], k_a[:, :, g], v_a[:, :, g], dilation, window // dilation)
        outs.append(o_g.astype(jnp.float32))
        lses.append(lse_g)
    mix_w = jax.nn.softmax(jnp.stack(lses, axis=0), axis=0)
    o_a = jnp.einsum('gbsh,gbshd->bshd', mix_w, jnp.stack(outs, axis=0))
    o_a = o_a.reshape(B, S, ATT_GROUP_WIDTH).astype(h.dtype)

    inv_r = RET_ROT_BASE ** (-jnp.linspace(0.0, 1.0, RET_QK_DIM // 2, dtype=jnp.float32))
    q_r = rotary(q_r.reshape(B, S, RET_HEADS, RET_QK_DIM), positions, inv_r).astype(jnp.float32)
    k_r = rotary(k_r.reshape(B, S, RET_HEADS, RET_QK_DIM), positions, inv_r).astype(jnp.float32) * (RET_QK_DIM ** -0.5)
    v_r = v_r.reshape(B, S, RET_HEADS, RET_V_DIM).astype(jnp.float32)
    log_decay = jnp.log1p(-jnp.exp2(-5.0 - jnp.arange(RET_HEADS, dtype=jnp.float32)))
    o_r = retention_chunkwise(q_r, k_r, v_r, log_decay)
    mu = jnp.mean(o_r, axis=-1, keepdims=True)
    var = jnp.mean(jnp.square(o_r - mu), axis=-1, keepdims=True)
    o_r = ((o_r - mu) * lax.rsqrt(var + LN_EPS)).reshape(B, S, RET_HEADS * RET_V_DIM)
    o_r = (jax.nn.silu(g_r.astype(jnp.float32)) * o_r).astype(h.dtype)

    y_a = jnp.einsum('bsf,fd->bsd', o_a, w_branch_attn)
    y_r = jnp.einsum('bsf,fd->bsd', o_r, w_branch_ret)
    merged = jax.nn.sigmoid(gate_a) * y_a + jax.nn.sigmoid(gate_b) * y_r
    return jnp.einsum('bsd,de->bse', merged, w_out)


def hierarchical_moe(h, w_group_router, b_group_router, w_expert_router, b_expert_router, w_gate, w_up, w_down):
    B, S, D = h.shape
    T = B * S
    xt = h.reshape(T, D)
    xf = xt.astype(jnp.float32)
    g_logits = xf @ w_group_router.astype(jnp.float32) + b_group_router.astype(jnp.float32)
    g_prob = jax.nn.softmax(g_logits, axis=-1)
    g_sel = jnp.argmax(g_logits, axis=-1)
    p_group = jnp.take_along_axis(g_prob, g_sel[:, None], axis=1)[:, 0]
    e_logits = jnp.einsum('td,gde->tge', xf, w_expert_router.astype(jnp.float32)) + b_expert_router.astype(jnp.float32)
    idx = jnp.broadcast_to(g_sel[:, None, None], (T, 1, EXPERTS_PER_GROUP))
    e_logits = jnp.take_along_axis(e_logits, idx, axis=1)[:, 0]
    top_vals, top_idx = lax.top_k(e_logits, TOP_K)
    weights = p_group[:, None] * jax.nn.softmax(top_vals, axis=-1)
    expert_id = g_sel[:, None].astype(jnp.int32) * EXPERTS_PER_GROUP + top_idx.astype(jnp.int32)

    A = T * TOP_K
    e_flat = expert_id.reshape(A)
    t_flat = jnp.repeat(jnp.arange(T, dtype=jnp.int32), TOP_K)
    w_flat = weights.reshape(A)
    order = jnp.argsort(e_flat)
    e_s, t_s, w_s = e_flat[order], t_flat[order], w_flat[order]
    counts = jnp.zeros((N_EXPERTS,), jnp.int32).at[e_flat].add(1)
    starts = jnp.cumsum(counts) - counts
    pcounts = (counts + MOE_BLOCK - 1) // MOE_BLOCK * MOE_BLOCK
    pends = jnp.cumsum(pcounts)
    pstarts = pends - pcounts
    dest = pstarts[e_s] + (jnp.arange(A, dtype=jnp.int32) - starts[e_s])
    n_blocks = -(-A // MOE_BLOCK) + N_EXPERTS
    R = n_blocks * MOE_BLOCK
    row_tok = jnp.full((R,), T, jnp.int32).at[dest].set(t_s)
    row_w = jnp.zeros((R,), jnp.float32).at[dest].set(w_s)
    block_expert = jnp.minimum(jnp.searchsorted(pends, jnp.arange(n_blocks, dtype=jnp.int32) * MOE_BLOCK, side='right'), N_EXPERTS - 1).astype(jnp.int32)
    x_pad = jnp.concatenate([xt, jnp.zeros((1, D), xt.dtype)], axis=0)
    x_rows = x_pad[row_tok].reshape(n_blocks, MOE_BLOCK, D)

    def expert_block(args):
        xb, e = args
        hid = jax.nn.silu(xb @ w_gate[e]) * (xb @ w_up[e])
        return hid @ w_down[e]

    y = lax.map(expert_block, (x_rows, block_expert)).reshape(R, D)
    y = y * row_w[:, None].astype(y.dtype)
    out = jax.ops.segment_sum(y, row_tok, num_segments=T + 1)[:T]
    return out.reshape(B, S, D).astype(h.dtype)


def setup_inputs(seed: int = 0) -> dict:
    key = jax.random.key(seed)
    ks = jax.random.split(key, 17)
    f32 = jnp.float32
    L, D = DEPTH, D_MODEL
    nrm = lambda k, shape, scale: jax.random.normal(k, shape, f32) * scale
    return {
        "x": nrm(ks[0], (BATCH, SEQ, D), 1.0),
        "positions": jnp.broadcast_to(jnp.arange(SEQ, dtype=jnp.int32), (BATCH, SEQ)),
        "w_in": nrm(ks[1], (L, D, IN_WIDTH), D ** -0.5),
        "w_branch_attn": nrm(ks[2], (L, ATT_GROUP_WIDTH, D), ATT_GROUP_WIDTH ** -0.5),
        "w_branch_ret": nrm(ks[3], (L, RET_HEADS * RET_V_DIM, D), (RET_HEADS * RET_V_DIM) ** -0.5),
        "w_out": nrm(ks[4], (L, D, D), BETA * D ** -0.5),
        "ln1_scale": 1.0 + nrm(ks[5], (L, D), 0.02),
        "ln1_bias": nrm(ks[6], (L, D), 0.02),
        "w_group_router": nrm(ks[7], (L, D, N_EXPERT_GROUPS), D ** -0.5),
        "b_group_router": nrm(ks[8], (L, N_EXPERT_GROUPS), 0.01),
        "w_expert_router": nrm(ks[9], (L, N_EXPERT_GROUPS, D, EXPERTS_PER_GROUP), D ** -0.5),
        "b_expert_router": nrm(ks[10], (L, N_EXPERT_GROUPS, EXPERTS_PER_GROUP), 0.01),
        "w_expert_gate": nrm(ks[11], (L, N_EXPERTS, D, EXPERT_HIDDEN), D ** -0.5),
        "w_expert_up": nrm(ks[12], (L, N_EXPERTS, D, EXPERT_HIDDEN), D ** -0.5),
        "w_expert_down": nrm(ks[13], (L, N_EXPERTS, EXPERT_HIDDEN, D), BETA * EXPERT_HIDDEN ** -0.5),
        "ln2_scale": 1.0 + nrm(ks[14], (L, D), 0.02),
        "ln2_bias": nrm(ks[15], (L, D), 0.02),
    }


def reference(x, positions, w_in, w_branch_attn, w_branch_ret, w_out, ln1_scale, ln1_bias, w_group_router, b_group_router, w_expert_router, b_expert_router, w_expert_gate, w_expert_up, w_expert_down, ln2_scale, ln2_bias):
    for l in range(DEPTH):
        y = mixer_sublayer(x, positions, w_in[l], w_branch_attn[l], w_branch_ret[l], w_out[l])
        x = layer_norm(ALPHA * x + y, ln1_scale[l], ln1_bias[l])
        y = hierarchical_moe(x, w_group_router[l], b_group_router[l], w_expert_router[l], b_expert_router[l], w_expert_gate[l], w_expert_up[l], w_expert_down[l])
        x = layer_norm(ALPHA * x + y, ln2_scale[l], ln2_bias[l])
    return x
```

```python
import functools
import math

import jax
import jax.numpy as jnp
from jax import lax
from jax.experimental import pallas as pl
from jax.experimental.pallas import tpu as pltpu

F32 = jnp.float32
BF16 = jnp.bfloat16

D_MODEL = 2048
ATT_HEAD_DIM = 128
ATT_HEADS_PER_GROUP = 8
DILATIONS = (1, 4, 16)
ATT_STEPS = 128
N_ATT_GROUPS = 3
ATT_GROUP_WIDTH = ATT_HEADS_PER_GROUP * ATT_HEAD_DIM
ROPE_DIM = ATT_HEAD_DIM // 4
ROPE_THETA = 500000.0
ATT_BLOCK = 128

RET_HEADS = 8
RET_QK_DIM = 128
RET_V_DIM = 256
RET_CHUNK = 128
RET_ROT_BASE = 10000.0
RET_LOG_DECAY = tuple(math.log1p(-(2.0 ** (-5.0 - h))) for h in range(RET_HEADS))

N_EXPERT_GROUPS = 4
EXPERTS_PER_GROUP = 8
N_EXPERTS = N_EXPERT_GROUPS * EXPERTS_PER_GROUP
TOP_K = 2
EXPERT_HIDDEN = 512
MOE_BLOCK = 128

DEPTH = 1
ALPHA = (2.0 * DEPTH) ** 0.25
LN_EPS = 1e-5
NEG_INF = -1e30

COL_BLOCK = 1024
IN_WIDTH = 19456
N_COL_BLOCKS = IN_WIDTH // COL_BLOCK
CB_QA, CB_KA, CB_VA = 0, 3, 6
CB_QR, CB_KR, CB_VR, CB_GR = 9, 10, 11, 13
CB_GATE_A, CB_GATE_B = 15, 17

LANES = 128
VMEM_LIMIT = 48 * 1024 * 1024


def _cparams(n_grid_dims):
    return pltpu.CompilerParams(
        dimension_semantics=("arbitrary",) * n_grid_dims,
        vmem_limit_bytes=VMEM_LIMIT,
    )


def _inproj_kernel(x_ref, w_ref, o_ref, wbf_ref):
    @pl.when(pl.program_id(1) == 0)
    def _():
        wbf_ref[...] = w_ref[...].astype(BF16)

    o_ref[...] = jnp.dot(x_ref[...], wbf_ref[...], preferred_element_type=F32).astype(o_ref.dtype)


def _input_projection(xb, w_in):
    S, D = xb.shape
    N = w_in.shape[1]
    tm = min(1024, S)
    tn = COL_BLOCK
    return pl.pallas_call(
        _inproj_kernel,
        grid=(N // tn, S // tm),
        in_specs=[
            pl.BlockSpec((tm, D), lambda j, i: (i, 0)),
            pl.BlockSpec((D, tn), lambda j, i: (0, j)),
        ],
        out_specs=pl.BlockSpec((tm, tn), lambda j, i: (i, j)),
        out_shape=jax.ShapeDtypeStruct((S, N), BF16),
        scratch_shapes=[pltpu.VMEM((D, tn), BF16)],
        compiler_params=_cparams(2),
        name="in_proj",
    )(xb, w_in)


def _rot_att(x, c, sa, sb):
    return x * c + pltpu.roll(x, LANES - ROPE_DIM // 2, 1) * sa + pltpu.roll(x, ROPE_DIM // 2, 1) * sb


def _attn_kernel(q_ref, kc_ref, kp_ref, vc_ref, vp_ref,
                 cc_ref, sac_ref, sbc_ref, cp_ref, sap_ref, sbp_ref,
                 o_ref, lse_ref):
    n = pl.program_id(1)
    cc, sac, sbc = cc_ref[...], sac_ref[...], sbc_ref[...]
    cp, sap, sbp = cp_ref[...], sap_ref[...], sbp_ref[...]
    qi = lax.broadcasted_iota(jnp.int32, (ATT_BLOCK, 2 * ATT_BLOCK), 0)
    kj = lax.broadcasted_iota(jnp.int32, (ATT_BLOCK, 2 * ATT_BLOCK), 1)
    dist = qi + ATT_BLOCK - kj
    band = (dist >= 0) & (dist <= ATT_STEPS)
    mask = band & ((kj >= ATT_BLOCK) | (n > 0))
    lane = lax.broadcasted_iota(jnp.int32, (ATT_BLOCK, LANES), 1)
    lse_slab = jnp.zeros((ATT_BLOCK, LANES), F32)
    scale = ATT_HEAD_DIM ** -0.5
    for h in range(ATT_HEADS_PER_GROUP):
        sl = slice(h * ATT_HEAD_DIM, (h + 1) * ATT_HEAD_DIM)
        q = _rot_att(q_ref[:, sl].astype(F32), cc, sac, sbc).astype(BF16)
        kc = _rot_att(kc_ref[:, sl].astype(F32), cc, sac, sbc).astype(BF16)
        kp = _rot_att(kp_ref[:, sl].astype(F32), cp, sap, sbp).astype(BF16)
        k2 = jnp.concatenate([kp, kc], axis=0)
        v2 = jnp.concatenate([vp_ref[:, sl], vc_ref[:, sl]], axis=0)
        s = lax.dot_general(q, k2, (((1,), (1,)), ((), ())), preferred_element_type=F32) * scale
        s = jnp.where(mask, s, NEG_INF)
        m = jnp.max(s, axis=1, keepdims=True)
        p = jnp.exp(s - m)
        l = jnp.sum(p, axis=1, keepdims=True)
        o = jnp.dot(p.astype(BF16), v2, preferred_element_type=F32)
        o_ref[:, sl] = o / l
        lse_slab = jnp.where(lane == h, m + jnp.log(l), lse_slab)
    lse_ref[...] = lse_slab


def _dilated_attention(proj, tabs, g, S):
    d = DILATIONS[g]
    M = S // d
    nb = M // ATT_BLOCK
    projv = proj.reshape(M, d * IN_WIDTH)
    tabv = [t.reshape(M, d * LANES) for t in tabs]

    def col(cb):
        return lambda r, n: (n, r * N_COL_BLOCKS + cb + g)

    def col_prev(cb):
        return lambda r, n: (jnp.maximum(n - 1, 0), r * N_COL_BLOCKS + cb + g)

    blk = (ATT_BLOCK, ATT_GROUP_WIDTH)
    tblk = (ATT_BLOCK, LANES)
    cur = lambda r, n: (n, r)
    prev = lambda r, n: (jnp.maximum(n - 1, 0), r)
    o, lse = pl.pallas_call(
        _attn_kernel,
        grid=(d, nb),
        in_specs=[
            pl.BlockSpec(blk, col(CB_QA)),
            pl.BlockSpec(blk, col(CB_KA)),
            pl.BlockSpec(blk, col_prev(CB_KA)),
            pl.BlockSpec(blk, col(CB_VA)),
            pl.BlockSpec(blk, col_prev(CB_VA)),
            pl.BlockSpec(tblk, cur), pl.BlockSpec(tblk, cur), pl.BlockSpec(tblk, cur),
            pl.BlockSpec(tblk, prev), pl.BlockSpec(tblk, prev), pl.BlockSpec(tblk, prev),
        ],
        out_specs=[pl.BlockSpec(blk, cur), pl.BlockSpec(tblk, cur)],
        out_shape=[
            jax.ShapeDtypeStruct((M, d * ATT_GROUP_WIDTH), F32),
            jax.ShapeDtypeStruct((M, d * LANES), F32),
        ],
        compiler_params=_cparams(2),
        name=f"dilated_attn_g{g}",
    )(projv, projv, projv, projv, projv, *tabv, *tabv)
    return o.reshape(S, ATT_GROUP_WIDTH), lse.reshape(S, LANES)


def _retention_kernel(q_ref, k_ref, vlo_ref, vhi_ref, glo_ref, ghi_ref, c_ref, s_ref,
                      o_ref, state_ref):
    @pl.when(pl.program_id(0) == 0)
    def _():
        state_ref[...] = jnp.zeros_like(state_ref)

    cos, sin = c_ref[...], s_ref[...]
    ri = lax.broadcasted_iota(jnp.int32, (RET_CHUNK, RET_CHUNK), 0)
    ci = lax.broadcasted_iota(jnp.int32, (RET_CHUNK, RET_CHUNK), 1)
    diff = (ri - ci).astype(F32)
    nrow = lax.broadcasted_iota(jnp.int32, (RET_CHUNK, 1), 0).astype(F32)
    k_scale = RET_QK_DIM ** -0.5
    half = RET_HEADS // 2
    for h in range(RET_HEADS):
        ld = RET_LOG_DECAY[h]
        sl = slice(h * RET_QK_DIM, (h + 1) * RET_QK_DIM)
        vsl = slice((h % half) * RET_V_DIM, (h % half + 1) * RET_V_DIM)
        v_ref, g_ref = (vlo_ref, glo_ref) if h < half else (vhi_ref, ghi_ref)
        q = q_ref[:, sl].astype(F32)
        k = k_ref[:, sl].astype(F32)
        q = q * cos + pltpu.roll(q, RET_QK_DIM // 2, 1) * sin
        k = (k * cos + pltpu.roll(k, RET_QK_DIM // 2, 1) * sin) * k_scale
        v = v_ref[:, vsl]
        intra = jnp.where(diff >= 0, jnp.exp(jnp.maximum(diff, 0.0) * ld), 0.0)
        scores = lax.dot_general(q.astype(BF16), k.astype(BF16), (((1,), (1,)), ((), ())),
                                 preferred_element_type=F32) * intra
        inner = jnp.dot(scores.astype(BF16), v, preferred_element_type=F32)
        state = state_ref[h]
        q_dec = jnp.exp((nrow + 1.0) * ld)
        cross = jnp.dot((q * q_dec).astype(BF16), state.astype(BF16), preferred_element_type=F32)
        k_dec = jnp.exp((RET_CHUNK - 1.0 - nrow) * ld)
        kv = lax.dot_general((k * k_dec).astype(BF16), v, (((0,), (0,)), ((), ())),
                             preferred_element_type=F32)
        state_ref[h] = math.exp(RET_CHUNK * ld) * state + kv
        o = inner + cross
        mu = jnp.mean(o, axis=1, keepdims=True)
        var = jnp.mean(jnp.square(o - mu), axis=1, keepdims=True)
        o = (o - mu) * lax.rsqrt(var + LN_EPS)
        gate = g_ref[:, vsl].astype(F32)
        gate = gate * (1.0 / (1.0 + jnp.exp(-gate)))
        o_ref[:, h * RET_V_DIM:(h + 1) * RET_V_DIM] = (gate * o).astype(o_ref.dtype)


def _retention(proj, cos_r, sin_r, S):
    nc = S // RET_CHUNK
    blk = (RET_CHUNK, COL_BLOCK)
    tblk = (RET_CHUNK, LANES)

    def col(cb):
        return lambda c: (c, cb)

    return pl.pallas_call(
        _retention_kernel,
        grid=(nc,),
        in_specs=[
            pl.BlockSpec(blk, col(CB_QR)),
            pl.BlockSpec(blk, col(CB_KR)),
            pl.BlockSpec(blk, col(CB_VR)),
            pl.BlockSpec(blk, col(CB_VR + 1)),
            pl.BlockSpec(blk, col(CB_GR)),
            pl.BlockSpec(blk, col(CB_GR + 1)),
            pl.BlockSpec(tblk, lambda c: (c, 0)),
            pl.BlockSpec(tblk, lambda c: (c, 0)),
        ],
        out_specs=pl.BlockSpec((RET_CHUNK, RET_HEADS * RET_V_DIM), lambda c: (c, 0)),
        out_shape=jax.ShapeDtypeStruct((S, RET_HEADS * RET_V_DIM), BF16),
        scratch_shapes=[pltpu.VMEM((RET_HEADS, RET_QK_DIM, RET_V_DIM), F32)],
        compiler_params=_cparams(1),
        name="retention",
    )(proj, proj, proj, proj, proj, proj, cos_r, sin_r)


def _layer_norm(z, scale, bias):
    mu = jnp.mean(z, axis=1, keepdims=True)
    var = jnp.mean(jnp.square(z - mu), axis=1, keepdims=True)
    return (z - mu) * lax.rsqrt(var + LN_EPS) * scale + bias


def _sigmoid(x):
    return 1.0 / (1.0 + jnp.exp(-x))


def _merge_kernel(o0_ref, o1_ref, o2_ref, l0_ref, l1_ref, l2_ref, or_ref,
                  ga0_ref, ga1_ref, gb0_ref, gb1_ref, x_ref,
                  wa_ref, wr_ref, wo_ref, lns_ref, lnb_ref, wrt_ref, brt_ref,
                  x1_ref, route_ref):
    l0, l1, l2 = l0_ref[...], l1_ref[...], l2_ref[...]
    lm = jnp.maximum(jnp.maximum(l0, l1), l2)
    e0, e1, e2 = jnp.exp(l0 - lm), jnp.exp(l1 - lm), jnp.exp(l2 - lm)
    den = e0 + e1 + e2
    w0, w1, w2 = e0 / den, e1 / den, e2 / den
    tm = o0_ref.shape[0]
    parts = []
    for h in range(ATT_HEADS_PER_GROUP):
        sl = slice(h * ATT_HEAD_DIM, (h + 1) * ATT_HEAD_DIM)
        b0 = jnp.broadcast_to(w0[:, h:h + 1], (tm, ATT_HEAD_DIM))
        b1 = jnp.broadcast_to(w1[:, h:h + 1], (tm, ATT_HEAD_DIM))
        b2 = jnp.broadcast_to(w2[:, h:h + 1], (tm, ATT_HEAD_DIM))
        parts.append((b0 * o0_ref[:, sl] + b1 * o1_ref[:, sl] + b2 * o2_ref[:, sl]).astype(BF16))
    o_a = jnp.concatenate(parts, axis=1)
    y_a = jnp.dot(o_a, wa_ref[...], preferred_element_type=F32)
    y_r = jnp.dot(or_ref[...], wr_ref[...], preferred_element_type=F32)
    gate_a = jnp.concatenate([ga0_ref[...], ga1_ref[...]], axis=1).astype(F32)
    gate_b = jnp.concatenate([gb0_ref[...], gb1_ref[...]], axis=1).astype(F32)
    merged = _sigmoid(gate_a) * y_a + _sigmoid(gate_b) * y_r
    y = jnp.dot(merged.astype(BF16), wo_ref[...], preferred_element_type=F32)
    x1 = _layer_norm(ALPHA * x_ref[...] + y, lns_ref[...], lnb_ref[...])
    x1_ref[...] = x1

    logits = jnp.dot(x1, wrt_ref[...], precision=lax.Precision.HIGHEST,
                     preferred_element_type=F32) + brt_ref[...]
    lane = lax.broadcasted_iota(jnp.int32, logits.shape, 1)
    is_g = lane < N_EXPERT_GROUPS
    gl = jnp.where(is_g, logits, NEG_INF)
    gmax = jnp.max(gl, axis=1, keepdims=True)
    g_sel = jnp.min(jnp.where(is_g & (gl == gmax), lane, LANES), axis=1, keepdims=True)
    p_group = 1.0 / jnp.sum(jnp.where(is_g, jnp.exp(gl - gmax), 0.0), axis=1, keepdims=True)
    lo = N_EXPERT_GROUPS + EXPERTS_PER_GROUP * g_sel
    in_grp = (lane >= lo) & (lane < lo + EXPERTS_PER_GROUP)
    el = jnp.where(in_grp, logits, NEG_INF)
    top1 = jnp.max(el, axis=1, keepdims=True)
    idx1 = jnp.min(jnp.where(in_grp & (el == top1), lane, LANES), axis=1, keepdims=True)
    el2 = jnp.where(lane == idx1, NEG_INF, el)
    top2 = jnp.max(el2, axis=1, keepdims=True)
    idx2 = jnp.min(jnp.where(in_grp & (lane != idx1) & (el2 == top2), lane, LANES), axis=1, keepdims=True)
    t = jnp.exp(top2 - top1)
    wt1 = p_group * (1.0 / (1.0 + t))
    wt2 = p_group * (t / (1.0 + t))
    slab = jnp.where(lane == 0, (idx1 - N_EXPERT_GROUPS).astype(F32), 0.0)
    slab = jnp.where(lane == 1, (idx2 - N_EXPERT_GROUPS).astype(F32), slab)
    slab = jnp.where(lane == 2, wt1, slab)
    slab = jnp.where(lane == 3, wt2, slab)
    route_ref[...] = slab


def _merge(o_g, lse_g, o_r, proj, x2d, wa, wr, wo, lns, lnb, wrt, brt, S):
    tm = min(256, S)
    D = D_MODEL

    def rows(width):
        return pl.BlockSpec((tm, width), lambda i: (i, 0))

    def pcol(cb):
        return pl.BlockSpec((tm, COL_BLOCK), lambda i: (i, cb))

    def const(shape):
        return pl.BlockSpec(shape, lambda i: (0, 0), pipeline_mode=pl.Buffered(1))

    return pl.pallas_call(
        _merge_kernel,
        grid=(S // tm,),
        in_specs=[
            rows(ATT_GROUP_WIDTH), rows(ATT_GROUP_WIDTH), rows(ATT_GROUP_WIDTH),
            rows(LANES), rows(LANES), rows(LANES),
            rows(RET_HEADS * RET_V_DIM),
            pcol(CB_GATE_A), pcol(CB_GATE_A + 1), pcol(CB_GATE_B), pcol(CB_GATE_B + 1),
            rows(D),
            const((ATT_GROUP_WIDTH, D)), const((RET_HEADS * RET_V_DIM, D)), const((D, D)),
            const((1, D)), const((1, D)), const((D, LANES)), const((1, LANES)),
        ],
        out_specs=[rows(D), rows(LANES)],
        out_shape=[jax.ShapeDtypeStruct((S, D), F32), jax.ShapeDtypeStruct((S, LANES), F32)],
        compiler_params=_cparams(1),
        name="merge_ln_router",
    )(*o_g, *lse_g, o_r, proj, proj, proj, proj, x2d, wa, wr, wo, lns, lnb, wrt, brt)


def _expert_kernel(be_ref, rt_ref, nu_ref, x_hbm, wg_ref, wu_ref, wd_ref, y_ref,
                   xg_ref, wgb_ref, wub_ref, wdb_ref, sem):
    b = pl.program_id(0)
    n_used = nu_ref[0]

    def row_copy(blk, slot, i):
        tok = rt_ref[blk * MOE_BLOCK + i]
        return pltpu.make_async_copy(x_hbm.at[pl.ds(tok, 1)], xg_ref.at[slot, pl.ds(i, 1)], sem.at[slot])

    def issue(blk, slot):
        def body(i, carry):
            row_copy(blk, slot, i).start()
            return carry
        lax.fori_loop(0, MOE_BLOCK, body, 0)

    @pl.when((b == 0) & (n_used > 0))
    def _():
        issue(0, 0)

    @pl.when(b + 1 < n_used)
    def _():
        issue(b + 1, (b + 1) % 2)

    changed = (b == 0) | (be_ref[b] != be_ref[jnp.maximum(b - 1, 0)])

    @pl.when(changed & (b < n_used))
    def _():
        wgb_ref[...] = wg_ref[...].astype(BF16)
        wub_ref[...] = wu_ref[...].astype(BF16)
        wdb_ref[...] = wd_ref[...].astype(BF16)

    @pl.when(b < n_used)
    def _():
        slot = b % 2
        pltpu.make_async_copy(x_hbm.at[pl.ds(0, MOE_BLOCK)], xg_ref.at[slot], sem.at[slot]).wait()
        xb = xg_ref[slot].astype(BF16)
        hg = jnp.dot(xb, wgb_ref[...], preferred_element_type=F32)
        hu = jnp.dot(xb, wub_ref[...], preferred_element_type=F32)
        hid = (hg * _sigmoid(hg) * hu).astype(BF16)
        y_ref[...] = jnp.dot(hid, wdb_ref[...], preferred_element_type=F32)

    @pl.when(b >= n_used)
    def _():
        y_ref[...] = jnp.zeros_like(y_ref)


def _experts(x1, block_expert, row_tok, n_used, w_gate, w_up, w_down):
    T, D = x1.shape
    n_blocks = block_expert.shape[0]
    H = EXPERT_HIDDEN
    grid_spec = pltpu.PrefetchScalarGridSpec(
        num_scalar_prefetch=3,
        grid=(n_blocks,),
        in_specs=[
            pl.BlockSpec(memory_space=pl.ANY),
            pl.BlockSpec((None, D, H), lambda b, be, rt, nu: (be[b], 0, 0)),
            pl.BlockSpec((None, D, H), lambda b, be, rt, nu: (be[b], 0, 0)),
            pl.BlockSpec((None, H, D), lambda b, be, rt, nu: (be[b], 0, 0)),
        ],
        out_specs=pl.BlockSpec((MOE_BLOCK, D), lambda b, be, rt, nu: (b, 0)),
        scratch_shapes=[
            pltpu.VMEM((2, MOE_BLOCK, D), F32),
            pltpu.VMEM((D, H), BF16),
            pltpu.VMEM((D, H), BF16),
            pltpu.VMEM((H, D), BF16),
            pltpu.SemaphoreType.DMA((2,)),
        ],
    )
    return pl.pallas_call(
        _expert_kernel,
        grid_spec=grid_spec,
        out_shape=jax.ShapeDtypeStruct((n_blocks * MOE_BLOCK, D), F32),
        compiler_params=_cparams(1),
        name="moe_experts",
    )(block_expert, row_tok, n_used, x1, w_gate, w_up, w_down)


COMBINE_TOKENS = 64


def _combine_kernel(dest_ref, y_hbm, x1_ref, route_ref, lns_ref, lnb_ref, o_ref, yg_ref, sem):
    i = pl.program_id(0)
    n = pl.num_programs(0)
    tt = COMBINE_TOKENS

    def issue(step, slot):
        def body(r, carry):
            for k in range(TOP_K):
                row = dest_ref[(step * tt + r) * TOP_K + k]
                pltpu.make_async_copy(y_hbm.at[pl.ds(row, 1)], yg_ref.at[slot, k, pl.ds(r, 1)],
                                      sem.at[slot]).start()
            return carry
        lax.fori_loop(0, tt, body, 0)

    @pl.when(i == 0)
    def _():
        issue(0, 0)

    @pl.when(i + 1 < n)
    def _():
        issue(i + 1, (i + 1) % 2)

    slot = i % 2
    for k in range(TOP_K):
        pltpu.make_async_copy(y_hbm.at[pl.ds(0, tt)], yg_ref.at[slot, k], sem.at[slot]).wait()
    route = route_ref[...]
    w1 = route[:, 2:3]
    w2 = route[:, 3:4]
    moe = w1 * yg_ref[slot, 0] + w2 * yg_ref[slot, 1]
    o_ref[...] = _layer_norm(ALPHA * x1_ref[...] + moe, lns_ref[...], lnb_ref[...])


def _combine(dest, y, x1, route, lns, lnb):
    T, D = x1.shape
    tt = COMBINE_TOKENS
    grid_spec = pltpu.PrefetchScalarGridSpec(
        num_scalar_prefetch=1,
        grid=(T // tt,),
        in_specs=[
            pl.BlockSpec(memory_space=pl.ANY),
            pl.BlockSpec((tt, D), lambda i, d: (i, 0)),
            pl.BlockSpec((tt, LANES), lambda i, d: (i, 0)),
            pl.BlockSpec((1, D), lambda i, d: (0, 0)),
            pl.BlockSpec((1, D), lambda i, d: (0, 0)),
        ],
        out_specs=pl.BlockSpec((tt, D), lambda i, d: (i, 0)),
        scratch_shapes=[
            pltpu.VMEM((2, TOP_K, tt, D), F32),
            pltpu.SemaphoreType.DMA((2,)),
        ],
    )
    return pl.pallas_call(
        _combine_kernel,
        grid_spec=grid_spec,
        out_shape=jax.ShapeDtypeStruct((T, D), F32),
        compiler_params=_cparams(1),
        name="moe_combine_ln",
    )(dest, y, x1, route, lns, lnb)


def _rotary_tables(positions):
    pos = positions.reshape(-1).astype(F32)
    S = pos.shape[0]
    inv_a = ROPE_THETA ** (-jnp.arange(0, ROPE_DIM, 2, dtype=F32) / ROPE_DIM)
    ang = pos[:, None] * inv_a
    cos, sin = jnp.cos(ang), jnp.sin(ang)
    half = ROPE_DIM // 2
    pad1 = jnp.ones((S, LANES - ROPE_DIM), F32)
    pad0 = jnp.zeros((S, LANES - ROPE_DIM), F32)
    zh = jnp.zeros((S, half), F32)
    cos_a = jnp.concatenate([cos, cos, pad1], axis=1)
    sin_lo = jnp.concatenate([-sin, zh, pad0], axis=1)
    sin_hi = jnp.concatenate([zh, sin, pad0], axis=1)
    inv_r = RET_ROT_BASE ** (-jnp.linspace(0.0, 1.0, RET_QK_DIM // 2, dtype=F32))
    ang_r = pos[:, None] * inv_r
    cr, sr = jnp.cos(ang_r), jnp.sin(ang_r)
    cos_r = jnp.concatenate([cr, cr], axis=1)
    sin_r = jnp.concatenate([-sr, sr], axis=1)
    return (cos_a, sin_lo, sin_hi), cos_r, sin_r


def _dispatch(route, T):
    A = T * TOP_K
    e_flat = route[:, :TOP_K].astype(jnp.int32).reshape(A)
    onehot = (e_flat[:, None] == jnp.arange(N_EXPERTS, dtype=jnp.int32)[None, :]).astype(jnp.int32)
    csum = jnp.cumsum(onehot, axis=0)
    rank = jnp.sum(csum * onehot, axis=1) - 1
    counts = csum[-1]
    pcounts = (counts + MOE_BLOCK - 1) // MOE_BLOCK * MOE_BLOCK
    pends = jnp.cumsum(pcounts)
    pstarts = pends - pcounts
    dest = (pstarts[e_flat] + rank).astype(jnp.int32)
    n_blocks = -(-A // MOE_BLOCK) + N_EXPERTS
    R = n_blocks * MOE_BLOCK
    t_flat = jnp.repeat(jnp.arange(T, dtype=jnp.int32), TOP_K)
    row_tok = jnp.zeros((R,), jnp.int32).at[dest].set(t_flat)
    block_expert = jnp.minimum(
        jnp.searchsorted(pends, jnp.arange(n_blocks, dtype=jnp.int32) * MOE_BLOCK, side='right'),
        N_EXPERTS - 1).astype(jnp.int32)
    n_used = (pends[-1] // MOE_BLOCK).astype(jnp.int32).reshape(1)
    return dest, row_tok, block_expert, n_used


def kernel(x, positions, w_in, w_branch_attn, w_branch_ret, w_out, ln1_scale, ln1_bias, w_group_router, b_group_router, w_expert_router, b_expert_router, w_expert_gate, w_expert_up, w_expert_down, ln2_scale, ln2_bias):
    B, S, D = x.shape
    assert B == 1 and D == D_MODEL and w_in.shape[0] == DEPTH == 1
    x2d = x.reshape(S, D)
    att_tabs, cos_r, sin_r = _rotary_tables(positions)

    proj = _input_projection(x2d.astype(BF16), w_in[0])
    att = [_dilated_attention(proj, att_tabs, g, S) for g in range(N_ATT_GROUPS)]
    o_r = _retention(proj, cos_r, sin_r, S)

    w_route = jnp.concatenate(
        [w_group_router[0], w_expert_router[0].transpose(1, 0, 2).reshape(D, N_EXPERTS),
         jnp.zeros((D, LANES - N_EXPERT_GROUPS - N_EXPERTS), F32)], axis=1)
    b_route = jnp.concatenate(
        [b_group_router[0], b_expert_router[0].reshape(N_EXPERTS),
         jnp.zeros((LANES - N_EXPERT_GROUPS - N_EXPERTS,), F32)]).reshape(1, LANES)
    x1, route = _merge(
        [a[0] for a in att], [a[1] for a in att], o_r, proj, x2d,
        w_branch_attn[0].astype(BF16), w_branch_ret[0].astype(BF16), w_out[0].astype(BF16),
        ln1_scale[0].reshape(1, D), ln1_bias[0].reshape(1, D), w_route, b_route, S)

    dest, row_tok, block_expert, n_used = _dispatch(route, S)
    y = _experts(x1, block_expert, row_tok, n_used, w_expert_gate[0], w_expert_up[0], w_expert_down[0])
    out = _combine(dest, y, x1, route, ln2_scale[0].reshape(1, D), ln2_bias[0].reshape(1, D))
    return out.reshape(B, S, D)
```

```python
import functools
import math

import jax
import jax.numpy as jnp
from jax import lax
from jax.experimental import pallas as pl
from jax.experimental.pallas import tpu as pltpu

F32 = jnp.float32
BF16 = jnp.bfloat16

D_MODEL = 2048
ATT_HEAD_DIM = 128
ATT_HEADS_PER_GROUP = 8
DILATIONS = (1, 4, 16)
ATT_STEPS = 128
N_ATT_GROUPS = 3
ATT_GROUP_WIDTH = ATT_HEADS_PER_GROUP * ATT_HEAD_DIM
ROPE_DIM = ATT_HEAD_DIM // 4
ROPE_THETA = 500000.0
ATT_BLOCK = 128

RET_HEADS = 8
RET_QK_DIM = 128
RET_V_DIM = 256
RET_CHUNK = 128
RET_ROT_BASE = 10000.0
RET_LOG_DECAY = tuple(math.log1p(-(2.0 ** (-5.0 - h))) for h in range(RET_HEADS))

N_EXPERT_GROUPS = 4
EXPERTS_PER_GROUP = 8
N_EXPERTS = N_EXPERT_GROUPS * EXPERTS_PER_GROUP
TOP_K = 2
EXPERT_HIDDEN = 512
MOE_BLOCK = 128

DEPTH = 1
ALPHA = (2.0 * DEPTH) ** 0.25
LN_EPS = 1e-5
NEG_INF = -1e30

COL_BLOCK = 1024
CB_QA, CB_KA, CB_VA = 0, 3, 6
CB_QR, CB_KR, CB_VR, CB_GR = 9, 10, 11, 13
CB_GATE_A, CB_GATE_B = 15, 17

MODE_PLAIN, MODE_ATT_Q, MODE_ATT_K, MODE_RET_Q, MODE_RET_K = 0, 1, 2, 3, 4

NAT_COLS = (CB_QA, CB_KA, CB_VA, CB_QR, CB_KR, CB_VR, CB_VR + 1, CB_GR, CB_GR + 1,
            CB_GATE_A, CB_GATE_A + 1, CB_GATE_B, CB_GATE_B + 1)
NAT_MODES = (MODE_ATT_Q, MODE_ATT_K, MODE_PLAIN, MODE_RET_Q, MODE_RET_K) + (MODE_PLAIN,) * 8
NC_Q0, NC_K0, NC_V0, NC_QR, NC_KR, NC_VR, NC_GR, NC_GATE_A, NC_GATE_B = 0, 1, 2, 3, 4, 5, 7, 9, 11

LANES = 128
VMEM_LIMIT = 52 * 1024 * 1024
PROJ_ROWS = 1024


def _cparams(n_grid_dims):
    return pltpu.CompilerParams(
        dimension_semantics=("arbitrary",) * n_grid_dims,
        vmem_limit_bytes=VMEM_LIMIT,
    )


def _inproj_kernel(cols_ref, modes_ref, x_ref, w_ref, ta_ref, tr_ref, o_ref, wbf_ref, acc_ref, *, d):
    del cols_ref
    j = pl.program_id(0)

    @pl.when(pl.program_id(1) == 0)
    def _():
        wbf_ref[...] = w_ref[...].astype(BF16)

    n_heads, tm, _ = acc_ref.shape
    acc = jnp.dot(x_ref[...], wbf_ref[...], preferred_element_type=F32)
    for h in range(n_heads):
        acc_ref[h] = acc[:, h * LANES:(h + 1) * LANES]
    mode = modes_ref[j]

    def att_rotary(scale):
        c = ta_ref[:, 0:LANES] * scale
        sa = ta_ref[:, LANES:2 * LANES] * scale
        sb = ta_ref[:, 2 * LANES:3 * LANES] * scale
        for h in range(n_heads):
            a = acc_ref[h]
            acc_ref[h] = (a * c + pltpu.roll(a, LANES - ROPE_DIM // 2, 1) * sa
                          + pltpu.roll(a, ROPE_DIM // 2, 1) * sb)

    def ret_rotary(scale):
        c = tr_ref[:, 0:LANES] * scale
        s = tr_ref[:, LANES:2 * LANES] * scale
        for h in range(n_heads):
            a = acc_ref[h]
            acc_ref[h] = a * c + pltpu.roll(a, RET_QK_DIM // 2, 1) * s

    pl.when(mode == MODE_ATT_Q)(lambda: att_rotary(ATT_HEAD_DIM ** -0.5))
    pl.when(mode == MODE_ATT_K)(lambda: att_rotary(1.0))
    pl.when(mode == MODE_RET_Q)(lambda: ret_rotary(1.0))
    pl.when(mode == MODE_RET_K)(lambda: ret_rotary(RET_QK_DIM ** -0.5))

    for h in range(n_heads):
        sl = slice(h * LANES, (h + 1) * LANES)
        if d == 1:
            o_ref[:, sl] = acc_ref[h].astype(o_ref.dtype)
        elif d == 4:
            P = ATT_BLOCK * d
            for sb in range(tm // P):
                for r in range(d):
                    o_ref[sb, r, :, sl] = acc_ref[h, pl.ds(sb * P + r, ATT_BLOCK, stride=d), :].astype(o_ref.dtype)
        else:
            for r in range(d):
                o_ref[r, :, sl] = acc_ref[h, pl.ds(r, tm // d, stride=d), :].astype(o_ref.dtype)


def _input_projection(xb, w_in, tabs_a, tabs_r, cols, modes, d):
    S, D = xb.shape
    tm, tn = PROJ_ROWS, COL_BLOCK
    n = len(cols)
    P = ATT_BLOCK * d
    if d == 1:
        out_shape = jax.ShapeDtypeStruct((S, n * tn), BF16)
        out_spec = pl.BlockSpec((tm, tn), lambda j, i, c, m: (i, j))
    elif d == 4:
        out_shape = jax.ShapeDtypeStruct((S // P, d, ATT_BLOCK, n * tn), BF16)
        out_spec = pl.BlockSpec((tm // P, d, ATT_BLOCK, tn), lambda j, i, c, m: (i, 0, 0, j))
    else:
        assert d == 16 and P == 2 * tm
        out_shape = jax.ShapeDtypeStruct((S // P, d, ATT_BLOCK, n * tn), BF16)
        out_spec = pl.BlockSpec((None, d, tm // d, tn), lambda j, i, c, m: (i // 2, 0, i % 2, j))

    def tab_rows(j, i, c, m):
        return (jnp.where(m[j] != MODE_PLAIN, i, 0), 0)

    grid_spec = pltpu.PrefetchScalarGridSpec(
        num_scalar_prefetch=2,
        grid=(n, S // tm),
        in_specs=[
            pl.BlockSpec((tm, D), lambda j, i, c, m: (i, 0)),
            pl.BlockSpec((D, tn), lambda j, i, c, m: (0, c[j])),
            pl.BlockSpec((tm, 3 * LANES), tab_rows),
            pl.BlockSpec((tm, 2 * LANES), tab_rows),
        ],
        out_specs=out_spec,
        scratch_shapes=[pltpu.VMEM((D, tn), BF16), pltpu.VMEM((tn // LANES, tm, LANES), F32)],
    )
    return pl.pallas_call(
        functools.partial(_inproj_kernel, d=d),
        grid_spec=grid_spec,
        out_shape=out_shape,
        compiler_params=_cparams(2),
        name=f"in_proj_d{d}",
    )(jnp.asarray(cols, jnp.int32), jnp.asarray(modes, jnp.int32), xb, w_in, tabs_a, tabs_r)


def _attn_kernel(q_ref, kc_ref, kp_ref, vc_ref, vp_ref, o_ref, lse_ref):
    n = pl.program_id(1)
    qi = lax.broadcasted_iota(jnp.int32, (ATT_BLOCK, 2 * ATT_BLOCK), 0)
    kj = lax.broadcasted_iota(jnp.int32, (ATT_BLOCK, 2 * ATT_BLOCK), 1)
    dist = qi + ATT_BLOCK - kj
    band = (dist >= 0) & (dist <= ATT_STEPS)
    mask = band & ((kj >= ATT_BLOCK) | (n > 0))
    lane = lax.broadcasted_iota(jnp.int32, (ATT_BLOCK, LANES), 1)
    lse_slab = jnp.zeros((ATT_BLOCK, LANES), F32)
    for h in range(ATT_HEADS_PER_GROUP):
        sl = slice(h * ATT_HEAD_DIM, (h + 1) * ATT_HEAD_DIM)
        k2 = jnp.concatenate([kp_ref[:, sl], kc_ref[:, sl]], axis=0)
        v2 = jnp.concatenate([vp_ref[:, sl], vc_ref[:, sl]], axis=0)
        s = lax.dot_general(q_ref[:, sl], k2, (((1,), (1,)), ((), ())), preferred_element_type=F32)
        s = jnp.where(mask, s, NEG_INF)
        m = jnp.max(s, axis=1, keepdims=True)
        p = jnp.exp(s - m)
        l = jnp.sum(p, axis=1, keepdims=True)
        o = jnp.dot(p.astype(BF16), v2, preferred_element_type=F32)
        o_ref[:, sl] = o / l
        lse_slab = jnp.where(lane == h, m + jnp.log(l), lse_slab)
    lse_ref[...] = lse_slab


def _dilated_attention(qkv, g, S, cq, ck, cv):
    d = DILATIONS[g]
    W = ATT_GROUP_WIDTH
    if d == 1:
        nb = S // ATT_BLOCK
        grid = (1, nb)
        blk = (ATT_BLOCK, W)
        cur = lambda c: (lambda r, n: (n, c))
        prev = lambda c: (lambda r, n: (jnp.maximum(n - 1, 0), c))
        o_spec = pl.BlockSpec(blk, lambda r, n: (n, 0))
        l_spec = pl.BlockSpec((ATT_BLOCK, LANES), lambda r, n: (n, 0))
        o_shape = jax.ShapeDtypeStruct((S, W), F32)
        l_shape = jax.ShapeDtypeStruct((S, LANES), F32)
    else:
        nb = S // (ATT_BLOCK * d)
        grid = (d, nb)
        blk = (None, None, ATT_BLOCK, W)
        cur = lambda c: (lambda r, n: (n, r, 0, c))
        prev = lambda c: (lambda r, n: (jnp.maximum(n - 1, 0), r, 0, c))
        o_spec = pl.BlockSpec(blk, lambda r, n: (n, r, 0, 0))
        l_spec = pl.BlockSpec((None, None, ATT_BLOCK, LANES), lambda r, n: (n, r, 0, 0))
        o_shape = jax.ShapeDtypeStruct((nb, d, ATT_BLOCK, W), F32)
        l_shape = jax.ShapeDtypeStruct((nb, d, ATT_BLOCK, LANES), F32)
    return pl.pallas_call(
        _attn_kernel,
        grid=grid,
        in_specs=[
            pl.BlockSpec(blk, cur(cq)),
            pl.BlockSpec(blk, cur(ck)),
            pl.BlockSpec(blk, prev(ck)),
            pl.BlockSpec(blk, cur(cv)),
            pl.BlockSpec(blk, prev(cv)),
        ],
        out_specs=[o_spec, l_spec],
        out_shape=[o_shape, l_shape],
        compiler_params=_cparams(2),
        name=f"dilated_attn_g{g}",
    )(qkv, qkv, qkv, qkv, qkv)


def _retention_kernel(q_ref, k_ref, vlo_ref, vhi_ref, glo_ref, ghi_ref, o_ref, state_ref):
    @pl.when(pl.program_id(0) == 0)
    def _():
        state_ref[...] = jnp.zeros_like(state_ref)

    ri = lax.broadcasted_iota(jnp.int32, (RET_CHUNK, RET_CHUNK), 0)
    ci = lax.broadcasted_iota(jnp.int32, (RET_CHUNK, RET_CHUNK), 1)
    diff = (ri - ci).astype(F32)
    nrow = lax.broadcasted_iota(jnp.int32, (RET_CHUNK, 1), 0).astype(F32)
    half = RET_HEADS // 2
    for h in range(RET_HEADS):
        ld = RET_LOG_DECAY[h]
        sl = slice(h * RET_QK_DIM, (h + 1) * RET_QK_DIM)
        vsl = slice((h % half) * RET_V_DIM, (h % half + 1) * RET_V_DIM)
        v_ref, g_ref = (vlo_ref, glo_ref) if h < half else (vhi_ref, ghi_ref)
        q = q_ref[:, sl]
        k = k_ref[:, sl]
        v = v_ref[:, vsl]
        intra = jnp.where(diff >= 0, jnp.exp(jnp.maximum(diff, 0.0) * ld), 0.0)
        scores = lax.dot_general(q, k, (((1,), (1,)), ((), ())), preferred_element_type=F32) * intra
        inner = jnp.dot(scores.astype(BF16), v, preferred_element_type=F32)
        state = state_ref[h]
        q_dec = jnp.exp((nrow + 1.0) * ld)
        cross = jnp.dot((q.astype(F32) * q_dec).astype(BF16), state.astype(BF16),
                        preferred_element_type=F32)
        k_dec = jnp.exp((RET_CHUNK - 1.0 - nrow) * ld)
        kv = lax.dot_general((k.astype(F32) * k_dec).astype(BF16), v, (((0,), (0,)), ((), ())),
                             preferred_element_type=F32)
        state_ref[h] = math.exp(RET_CHUNK * ld) * state + kv
        o = inner + cross
        mu = jnp.mean(o, axis=1, keepdims=True)
        var = jnp.mean(jnp.square(o - mu), axis=1, keepdims=True)
        o = (o - mu) * lax.rsqrt(var + LN_EPS)
        gate = g_ref[:, vsl].astype(F32)
        gate = gate * (1.0 / (1.0 + jnp.exp(-gate)))
        o_ref[:, h * RET_V_DIM:(h + 1) * RET_V_DIM] = (gate * o).astype(o_ref.dtype)


def _retention(nat, S):
    nc = S // RET_CHUNK
    blk = (RET_CHUNK, COL_BLOCK)

    def col(cb):
        return pl.BlockSpec(blk, lambda c: (c, cb))

    return pl.pallas_call(
        _retention_kernel,
        grid=(nc,),
        in_specs=[col(NC_QR), col(NC_KR), col(NC_VR), col(NC_VR + 1), col(NC_GR), col(NC_GR + 1)],
        out_specs=pl.BlockSpec((RET_CHUNK, RET_HEADS * RET_V_DIM), lambda c: (c, 0)),
        out_shape=jax.ShapeDtypeStruct((S, RET_HEADS * RET_V_DIM), BF16),
        scratch_shapes=[pltpu.VMEM((RET_HEADS, RET_QK_DIM, RET_V_DIM), F32)],
        compiler_params=_cparams(1),
        name="retention",
    )(nat, nat, nat, nat, nat, nat)


MERGE_ROWS = 256


def _layer_norm(z, scale, bias):
    mu = jnp.mean(z, axis=1, keepdims=True)
    var = jnp.mean(jnp.square(z - mu), axis=1, keepdims=True)
    return (z - mu) * lax.rsqrt(var + LN_EPS) * scale + bias


def _sigmoid(x):
    return 1.0 / (1.0 + jnp.exp(-x))


def _merge_kernel(o0_ref, o1_ref, o2_ref, l0_ref, l1_ref, l2_ref, or_ref,
                  ga0_ref, ga1_ref, gb0_ref, gb1_ref, x_ref,
                  wa_ref, wr_ref, wo_ref, lns_ref, lnb_ref, wrt_ref, brt_ref,
                  x1_ref, route_ref, s1_ref, s2_ref, ls1_ref, ls2_ref):
    tm = o0_ref.shape[0]
    for d, src, lsrc, dst, ldst in ((DILATIONS[1], o1_ref, l1_ref, s1_ref, ls1_ref),
                                    (DILATIONS[2], o2_ref, l2_ref, s2_ref, ls2_ref)):
        for r in range(d):
            for h in range(ATT_HEADS_PER_GROUP):
                dst[h, pl.ds(r, tm // d, stride=d), :] = src[r, :, h * ATT_HEAD_DIM:(h + 1) * ATT_HEAD_DIM]
            ldst[pl.ds(r, tm // d, stride=d), :] = lsrc[r]
    l0, l1, l2 = l0_ref[...], ls1_ref[...], ls2_ref[...]
    lm = jnp.maximum(jnp.maximum(l0, l1), l2)
    e0, e1, e2 = jnp.exp(l0 - lm), jnp.exp(l1 - lm), jnp.exp(l2 - lm)
    den = e0 + e1 + e2
    w0, w1, w2 = e0 / den, e1 / den, e2 / den
    parts = []
    for h in range(ATT_HEADS_PER_GROUP):
        sl = slice(h * ATT_HEAD_DIM, (h + 1) * ATT_HEAD_DIM)
        b0 = jnp.broadcast_to(w0[:, h:h + 1], (tm, ATT_HEAD_DIM))
        b1 = jnp.broadcast_to(w1[:, h:h + 1], (tm, ATT_HEAD_DIM))
        b2 = jnp.broadcast_to(w2[:, h:h + 1], (tm, ATT_HEAD_DIM))
        parts.append((b0 * o0_ref[:, sl] + b1 * s1_ref[h] + b2 * s2_ref[h]).astype(BF16))
    o_a = jnp.concatenate(parts, axis=1)
    y_a = jnp.dot(o_a, wa_ref[...], preferred_element_type=F32)
    y_r = jnp.dot(or_ref[...], wr_ref[...], preferred_element_type=F32)
    gate_a = jnp.concatenate([ga0_ref[...], ga1_ref[...]], axis=1).astype(F32)
    gate_b = jnp.concatenate([gb0_ref[...], gb1_ref[...]], axis=1).astype(F32)
    merged = _sigmoid(gate_a) * y_a + _sigmoid(gate_b) * y_r
    y = jnp.dot(merged.astype(BF16), wo_ref[...], preferred_element_type=F32)
    x1 = _layer_norm(ALPHA * x_ref[...] + y, lns_ref[...], lnb_ref[...])
    x1_ref[...] = x1

    logits = jnp.dot(x1, wrt_ref[...], precision=lax.Precision.HIGHEST,
                     preferred_element_type=F32) + brt_ref[...]
    lane = lax.broadcasted_iota(jnp.int32, logits.shape, 1)
    is_g = lane < N_EXPERT_GROUPS
    gl = jnp.where(is_g, logits, NEG_INF)
    gmax = jnp.max(gl, axis=1, keepdims=True)
    g_sel = jnp.min(jnp.where(is_g & (gl == gmax), lane, LANES), axis=1, keepdims=True)
    p_group = 1.0 / jnp.sum(jnp.where(is_g, jnp.exp(gl - gmax), 0.0), axis=1, keepdims=True)
    lo = N_EXPERT_GROUPS + EXPERTS_PER_GROUP * g_sel
    in_grp = (lane >= lo) & (lane < lo + EXPERTS_PER_GROUP)
    el = jnp.where(in_grp, logits, NEG_INF)
    top1 = jnp.max(el, axis=1, keepdims=True)
    idx1 = jnp.min(jnp.where(in_grp & (el == top1), lane, LANES), axis=1, keepdims=True)
    el2 = jnp.where(lane == idx1, NEG_INF, el)
    top2 = jnp.max(el2, axis=1, keepdims=True)
    idx2 = jnp.min(jnp.where(in_grp & (lane != idx1) & (el2 == top2), lane, LANES), axis=1, keepdims=True)
    t = jnp.exp(top2 - top1)
    wt1 = p_group * (1.0 / (1.0 + t))
    wt2 = p_group * (t / (1.0 + t))
    slab = jnp.where(lane == 0, (idx1 - N_EXPERT_GROUPS).astype(F32), 0.0)
    slab = jnp.where(lane == 1, (idx2 - N_EXPERT_GROUPS).astype(F32), slab)
    slab = jnp.where(lane == 2, wt1, slab)
    slab = jnp.where(lane == 3, wt2, slab)
    route_ref[...] = slab


def _merge(o_g, lse_g, o_r, nat, x2d, wa, wr, wo, lns, lnb, wrt, brt, S):
    tm = MERGE_ROWS
    D = D_MODEL
    W = ATT_GROUP_WIDTH

    def rows(width):
        return pl.BlockSpec((tm, width), lambda i: (i, 0))

    def streams(g, width):
        d = DILATIONS[g]
        per = (ATT_BLOCK * d) // tm
        return pl.BlockSpec((None, d, tm // d, width), lambda i: (i // per, 0, i % per, 0))

    def ncol(cb):
        return pl.BlockSpec((tm, COL_BLOCK), lambda i: (i, cb))

    def const(shape):
        return pl.BlockSpec(shape, lambda i: (0, 0), pipeline_mode=pl.Buffered(1))

    return pl.pallas_call(
        _merge_kernel,
        grid=(S // tm,),
        in_specs=[
            rows(W), streams(1, W), streams(2, W),
            rows(LANES), streams(1, LANES), streams(2, LANES),
            rows(RET_HEADS * RET_V_DIM),
            ncol(NC_GATE_A), ncol(NC_GATE_A + 1), ncol(NC_GATE_B), ncol(NC_GATE_B + 1),
            rows(D),
            const((W, D)), const((RET_HEADS * RET_V_DIM, D)), const((D, D)),
            const((1, D)), const((1, D)), const((D, LANES)), const((1, LANES)),
        ],
        out_specs=[rows(D), rows(LANES)],
        out_shape=[jax.ShapeDtypeStruct((S, D), F32), jax.ShapeDtypeStruct((S, LANES), F32)],
        scratch_shapes=[pltpu.VMEM((ATT_HEADS_PER_GROUP, tm, ATT_HEAD_DIM), F32),
                        pltpu.VMEM((ATT_HEADS_PER_GROUP, tm, ATT_HEAD_DIM), F32),
                        pltpu.VMEM((tm, LANES), F32), pltpu.VMEM((tm, LANES), F32)],
        compiler_params=_cparams(1),
        name="merge_ln_router",
    )(*o_g, *lse_g, o_r, nat, nat, nat, nat, x2d, wa, wr, wo, lns, lnb, wrt, brt)


def _expert_kernel(be_ref, rt_ref, nu_ref, x_hbm, wg_ref, wu_ref, wd_ref, y_ref,
                   xg_ref, wgb_ref, wub_ref, wdb_ref, sem):
    b = pl.program_id(0)
    n_used = nu_ref[0]

    def row_copy(blk, slot, i):
        tok = rt_ref[blk * MOE_BLOCK + i]
        return pltpu.make_async_copy(x_hbm.at[pl.ds(tok, 1)], xg_ref.at[slot, pl.ds(i, 1)], sem.at[slot])

    def issue(blk, slot):
        def body(i, carry):
            row_copy(blk, slot, i).start()
            return carry
        lax.fori_loop(0, MOE_BLOCK, body, 0)

    @pl.when((b == 0) & (n_used > 0))
    def _():
        issue(0, 0)

    @pl.when(b + 1 < n_used)
    def _():
        issue(b + 1, (b + 1) % 2)

    changed = (b == 0) | (be_ref[b] != be_ref[jnp.maximum(b - 1, 0)])

    @pl.when(changed & (b < n_used))
    def _():
        wgb_ref[...] = wg_ref[...].astype(BF16)
        wub_ref[...] = wu_ref[...].astype(BF16)
        wdb_ref[...] = wd_ref[...].astype(BF16)

    @pl.when(b < n_used)
    def _():
        slot = b % 2
        pltpu.make_async_copy(x_hbm.at[pl.ds(0, MOE_BLOCK)], xg_ref.at[slot], sem.at[slot]).wait()
        xb = xg_ref[slot].astype(BF16)
        hg = jnp.dot(xb, wgb_ref[...], preferred_element_type=F32)
        hu = jnp.dot(xb, wub_ref[...], preferred_element_type=F32)
        hid = (hg * _sigmoid(hg) * hu).astype(BF16)
        y_ref[...] = jnp.dot(hid, wdb_ref[...], preferred_element_type=F32)

    @pl.when(b >= n_used)
    def _():
        y_ref[...] = jnp.zeros_like(y_ref)


def _experts(x1, block_expert, row_tok, n_used, w_gate, w_up, w_down):
    T, D = x1.shape
    n_blocks = block_expert.shape[0]
    H = EXPERT_HIDDEN
    grid_spec = pltpu.PrefetchScalarGridSpec(
        num_scalar_prefetch=3,
        grid=(n_blocks,),
        in_specs=[
            pl.BlockSpec(memory_space=pl.ANY),
            pl.BlockSpec((None, D, H), lambda b, be, rt, nu: (be[b], 0, 0)),
            pl.BlockSpec((None, D, H), lambda b, be, rt, nu: (be[b], 0, 0)),
            pl.BlockSpec((None, H, D), lambda b, be, rt, nu: (be[b], 0, 0)),
        ],
        out_specs=pl.BlockSpec((MOE_BLOCK, D), lambda b, be, rt, nu: (b, 0)),
        scratch_shapes=[
            pltpu.VMEM((2, MOE_BLOCK, D), F32),
            pltpu.VMEM((D, H), BF16),
            pltpu.VMEM((D, H), BF16),
            pltpu.VMEM((H, D), BF16),
            pltpu.SemaphoreType.DMA((2,)),
        ],
    )
    return pl.pallas_call(
        _expert_kernel,
        grid_spec=grid_spec,
        out_shape=jax.ShapeDtypeStruct((n_blocks * MOE_BLOCK, D), F32),
        compiler_params=_cparams(1),
        name="moe_experts",
    )(block_expert, row_tok, n_used, x1, w_gate, w_up, w_down)


COMBINE_TOKENS = 64


def _combine_kernel(dest_ref, y_hbm, x1_ref, route_ref, lns_ref, lnb_ref, o_ref, yg_ref, sem):
    i = pl.program_id(0)
    n = pl.num_programs(0)
    tt = COMBINE_TOKENS

    def issue(step, slot):
        def body(r, carry):
            for k in range(TOP_K):
                row = dest_ref[(step * tt + r) * TOP_K + k]
                pltpu.make_async_copy(y_hbm.at[pl.ds(row, 1)], yg_ref.at[slot, k, pl.ds(r, 1)],
                                      sem.at[slot]).start()
            return carry
        lax.fori_loop(0, tt, body, 0)

    @pl.when(i == 0)
    def _():
        issue(0, 0)

    @pl.when(i + 1 < n)
    def _():
        issue(i + 1, (i + 1) % 2)

    slot = i % 2
    for k in range(TOP_K):
        pltpu.make_async_copy(y_hbm.at[pl.ds(0, tt)], yg_ref.at[slot, k], sem.at[slot]).wait()
    route = route_ref[...]
    w1 = route[:, 2:3]
    w2 = route[:, 3:4]
    moe = w1 * yg_ref[slot, 0] + w2 * yg_ref[slot, 1]
    o_ref[...] = _layer_norm(ALPHA * x1_ref[...] + moe, lns_ref[...], lnb_ref[...])


def _combine(dest, y, x1, route, lns, lnb):
    T, D = x1.shape
    tt = COMBINE_TOKENS
    grid_spec = pltpu.PrefetchScalarGridSpec(
        num_scalar_prefetch=1,
        grid=(T // tt,),
        in_specs=[
            pl.BlockSpec(memory_space=pl.ANY),
            pl.BlockSpec((tt, D), lambda i, d: (i, 0)),
            pl.BlockSpec((tt, LANES), lambda i, d: (i, 0)),
            pl.BlockSpec((1, D), lambda i, d: (0, 0)),
            pl.BlockSpec((1, D), lambda i, d: (0, 0)),
        ],
        out_specs=pl.BlockSpec((tt, D), lambda i, d: (i, 0)),
        scratch_shapes=[
            pltpu.VMEM((2, TOP_K, tt, D), F32),
            pltpu.SemaphoreType.DMA((2,)),
        ],
    )
    return pl.pallas_call(
        _combine_kernel,
        grid_spec=grid_spec,
        out_shape=jax.ShapeDtypeStruct((T, D), F32),
        compiler_params=_cparams(1),
        name="moe_combine_ln",
    )(dest, y, x1, route, lns, lnb)


def _rotary_tables(positions):
    pos = positions.reshape(-1).astype(F32)
    S = pos.shape[0]
    inv_a = ROPE_THETA ** (-jnp.arange(0, ROPE_DIM, 2, dtype=F32) / ROPE_DIM)
    ang = pos[:, None] * inv_a
    cos, sin = jnp.cos(ang), jnp.sin(ang)
    half = ROPE_DIM // 2
    pad1 = jnp.ones((S, LANES - ROPE_DIM), F32)
    pad0 = jnp.zeros((S, LANES - ROPE_DIM), F32)
    zh = jnp.zeros((S, half), F32)
    cos_a = jnp.concatenate([cos, cos, pad1], axis=1)
    sin_lo = jnp.concatenate([-sin, zh, pad0], axis=1)
    sin_hi = jnp.concatenate([zh, sin, pad0], axis=1)
    inv_r = RET_ROT_BASE ** (-jnp.linspace(0.0, 1.0, RET_QK_DIM // 2, dtype=F32))
    ang_r = pos[:, None] * inv_r
    cr, sr = jnp.cos(ang_r), jnp.sin(ang_r)
    tabs_a = jnp.concatenate([cos_a, sin_lo, sin_hi], axis=1)
    tabs_r = jnp.concatenate([cr, cr, -sr, sr], axis=1)
    return tabs_a, tabs_r


def _dispatch(route, T):
    A = T * TOP_K
    e_flat = route[:, :TOP_K].astype(jnp.int32).reshape(A)
    onehot = (e_flat[:, None] == jnp.arange(N_EXPERTS, dtype=jnp.int32)[None, :]).astype(jnp.int32)
    csum = jnp.cumsum(onehot, axis=0)
    rank = jnp.sum(csum * onehot, axis=1) - 1
    counts = csum[-1]
    pcounts = (counts + MOE_BLOCK - 1) // MOE_BLOCK * MOE_BLOCK
    pends = jnp.cumsum(pcounts)
    pstarts = pends - pcounts
    dest = (pstarts[e_flat] + rank).astype(jnp.int32)
    n_blocks = -(-A // MOE_BLOCK) + N_EXPERTS
    R = n_blocks * MOE_BLOCK
    t_flat = jnp.repeat(jnp.arange(T, dtype=jnp.int32), TOP_K)
    row_tok = jnp.zeros((R,), jnp.int32).at[dest].set(t_flat)
    block_expert = jnp.minimum(
        jnp.searchsorted(pends, jnp.arange(n_blocks, dtype=jnp.int32) * MOE_BLOCK, side='right'),
        N_EXPERTS - 1).astype(jnp.int32)
    n_used = (pends[-1] // MOE_BLOCK).astype(jnp.int32).reshape(1)
    return dest, row_tok, block_expert, n_used


def kernel(x, positions, w_in, w_branch_attn, w_branch_ret, w_out, ln1_scale, ln1_bias, w_group_router, b_group_router, w_expert_router, b_expert_router, w_expert_gate, w_expert_up, w_expert_down, ln2_scale, ln2_bias):
    B, S, D = x.shape
    assert B == 1 and D == D_MODEL and w_in.shape[0] == DEPTH == 1
    assert S % (ATT_BLOCK * DILATIONS[2]) == 0
    x2d = x.reshape(S, D)
    xb = x2d.astype(BF16)
    tabs_a, tabs_r = _rotary_tables(positions)

    nat = _input_projection(xb, w_in[0], tabs_a, tabs_r, NAT_COLS, NAT_MODES, 1)
    qkv_modes = (MODE_ATT_Q, MODE_ATT_K, MODE_PLAIN)
    qkv = [None] + [
        _input_projection(xb, w_in[0], tabs_a, tabs_r, (CB_QA + g, CB_KA + g, CB_VA + g), qkv_modes,
                          DILATIONS[g])
        for g in (1, 2)]
    att = [_dilated_attention(nat, 0, S, NC_Q0, NC_K0, NC_V0)]
    att += [_dilated_attention(qkv[g], g, S, 0, 1, 2) for g in (1, 2)]
    o_r = _retention(nat, S)

    w_route = jnp.concatenate(
        [w_group_router[0], w_expert_router[0].transpose(1, 0, 2).reshape(D, N_EXPERTS),
         jnp.zeros((D, LANES - N_EXPERT_GROUPS - N_EXPERTS), F32)], axis=1)
    b_route = jnp.concatenate(
        [b_group_router[0], b_expert_router[0].reshape(N_EXPERTS),
         jnp.zeros((LANES - N_EXPERT_GROUPS - N_EXPERTS,), F32)]).reshape(1, LANES)
    x1, route = _merge(
        [a[0] for a in att], [a[1] for a in att], o_r, nat, x2d,
        w_branch_attn[0].astype(BF16), w_branch_ret[0].astype(BF16), w_out[0].astype(BF16),
        ln1_scale[0].reshape(1, D), ln1_bias[0].reshape(1, D), w_route, b_route, S)

    dest, row_tok, block_expert, n_used = _dispatch(route, S)
    y = _experts(x1, block_expert, row_tok, n_used, w_expert_gate[0], w_expert_up[0], w_expert_down[0])
    out = _combine(dest, y, x1, route, ln2_scale[0].reshape(1, D), ln2_bias[0].reshape(1, D))
    return out.reshape(B, S, D)
```

```python
import functools
import math

import jax
import jax.numpy as jnp
from jax import lax
from jax.experimental import pallas as pl
from jax.experimental.pallas import tpu as pltpu

F32 = jnp.float32
BF16 = jnp.bfloat16

D_MODEL = 2048
ATT_HEAD_DIM = 128
ATT_HEADS_PER_GROUP = 8
DILATIONS = (1, 4, 16)
ATT_STEPS = 128
N_ATT_GROUPS = 3
ATT_GROUP_WIDTH = ATT_HEADS_PER_GROUP * ATT_HEAD_DIM
ROPE_DIM = ATT_HEAD_DIM // 4
ROPE_THETA = 500000.0
ATT_BLOCK = 128

RET_HEADS = 8
RET_QK_DIM = 128
RET_V_DIM = 256
RET_CHUNK = 128
RET_ROT_BASE = 10000.0
RET_LOG_DECAY = tuple(math.log1p(-(2.0 ** (-5.0 - h))) for h in range(RET_HEADS))

N_EXPERT_GROUPS = 4
EXPERTS_PER_GROUP = 8
N_EXPERTS = N_EXPERT_GROUPS * EXPERTS_PER_GROUP
TOP_K = 2
EXPERT_HIDDEN = 512
MOE_BLOCK = 128

DEPTH = 1
ALPHA = (2.0 * DEPTH) ** 0.25
LN_EPS = 1e-5
NEG_INF = -1e30

COL_BLOCK = 1024
CB_QA, CB_KA, CB_VA = 0, 3, 6
CB_QR, CB_KR, CB_VR, CB_GR = 9, 10, 11, 13
CB_GATE_A, CB_GATE_B = 15, 17

MODE_PLAIN, MODE_ATT_Q, MODE_ATT_K, MODE_RET_Q, MODE_RET_K = 0, 1, 2, 3, 4

NAT_COLS = (CB_QA, CB_KA, CB_VA, CB_QR, CB_KR, CB_VR, CB_VR + 1, CB_GR, CB_GR + 1,
            CB_GATE_A, CB_GATE_A + 1, CB_GATE_B, CB_GATE_B + 1)
NAT_MODES = (MODE_ATT_Q, MODE_ATT_K, MODE_PLAIN, MODE_RET_Q, MODE_RET_K) + (MODE_PLAIN,) * 8
NC_Q0, NC_K0, NC_V0, NC_QR, NC_KR, NC_VR, NC_GR, NC_GATE_A, NC_GATE_B = 0, 1, 2, 3, 4, 5, 7, 9, 11

LANES = 128
VMEM_LIMIT = 52 * 1024 * 1024
PROJ_ROWS = 1024


def _cparams(n_grid_dims):
    return pltpu.CompilerParams(
        dimension_semantics=("arbitrary",) * n_grid_dims,
        vmem_limit_bytes=VMEM_LIMIT,
    )


def _inproj_kernel(cols_ref, modes_ref, x_ref, w_ref, ta_ref, tr_ref, o_ref, wbf_ref, acc_ref, *, d):
    del cols_ref
    j = pl.program_id(0)

    @pl.when(pl.program_id(1) == 0)
    def _():
        wbf_ref[...] = w_ref[...].astype(BF16)

    n_heads, tm, _ = acc_ref.shape
    acc = jnp.dot(x_ref[...], wbf_ref[...], preferred_element_type=F32)
    for h in range(n_heads):
        acc_ref[h] = acc[:, h * LANES:(h + 1) * LANES]
    mode = modes_ref[j]

    def att_rotary(scale):
        c = ta_ref[:, 0:LANES] * scale
        sa = ta_ref[:, LANES:2 * LANES] * scale
        sb = ta_ref[:, 2 * LANES:3 * LANES] * scale
        for h in range(n_heads):
            a = acc_ref[h]
            acc_ref[h] = (a * c + pltpu.roll(a, LANES - ROPE_DIM // 2, 1) * sa
                          + pltpu.roll(a, ROPE_DIM // 2, 1) * sb)

    def ret_rotary(scale):
        c = tr_ref[:, 0:LANES] * scale
        s = tr_ref[:, LANES:2 * LANES] * scale
        for h in range(n_heads):
            a = acc_ref[h]
            acc_ref[h] = a * c + pltpu.roll(a, RET_QK_DIM // 2, 1) * s

    pl.when(mode == MODE_ATT_Q)(lambda: att_rotary(ATT_HEAD_DIM ** -0.5))
    pl.when(mode == MODE_ATT_K)(lambda: att_rotary(1.0))
    pl.when(mode == MODE_RET_Q)(lambda: ret_rotary(1.0))
    pl.when(mode == MODE_RET_K)(lambda: ret_rotary(RET_QK_DIM ** -0.5))

    for h in range(n_heads):
        sl = slice(h * LANES, (h + 1) * LANES)
        if d == 1:
            o_ref[:, sl] = acc_ref[h].astype(o_ref.dtype)
        elif d == 4:
            P = ATT_BLOCK * d
            for sb in range(tm // P):
                for r in range(d):
                    o_ref[sb, r, :, sl] = acc_ref[h, pl.ds(sb * P + r, ATT_BLOCK, stride=d), :].astype(o_ref.dtype)
        else:
            for r in range(d):
                o_ref[r, :, sl] = acc_ref[h, pl.ds(r, tm // d, stride=d), :].astype(o_ref.dtype)


def _input_projection(xb, w_in, tabs_a, tabs_r, cols, modes, d):
    S, D = xb.shape
    tm, tn = PROJ_ROWS, COL_BLOCK
    n = len(cols)
    P = ATT_BLOCK * d
    if d == 1:
        out_shape = jax.ShapeDtypeStruct((S, n * tn), BF16)
        out_spec = pl.BlockSpec((tm, tn), lambda j, i, c, m: (i, j))
    elif d == 4:
        out_shape = jax.ShapeDtypeStruct((S // P, d, ATT_BLOCK, n * tn), BF16)
        out_spec = pl.BlockSpec((tm // P, d, ATT_BLOCK, tn), lambda j, i, c, m: (i, 0, 0, j))
    else:
        assert d == 16 and P == 2 * tm
        out_shape = jax.ShapeDtypeStruct((S // P, d, ATT_BLOCK, n * tn), BF16)
        out_spec = pl.BlockSpec((None, d, tm // d, tn), lambda j, i, c, m: (i // 2, 0, i % 2, j))

    def tab_rows(j, i, c, m):
        return (jnp.where(m[j] != MODE_PLAIN, i, 0), 0)

    grid_spec = pltpu.PrefetchScalarGridSpec(
        num_scalar_prefetch=2,
        grid=(n, S // tm),
        in_specs=[
            pl.BlockSpec((tm, D), lambda j, i, c, m: (i, 0)),
            pl.BlockSpec((D, tn), lambda j, i, c, m: (0, c[j])),
            pl.BlockSpec((tm, 3 * LANES), tab_rows),
            pl.BlockSpec((tm, 2 * LANES), tab_rows),
        ],
        out_specs=out_spec,
        scratch_shapes=[pltpu.VMEM((D, tn), BF16), pltpu.VMEM((tn // LANES, tm, LANES), F32)],
    )
    return pl.pallas_call(
        functools.partial(_inproj_kernel, d=d),
        grid_spec=grid_spec,
        out_shape=out_shape,
        compiler_params=_cparams(2),
        name=f"in_proj_d{d}",
    )(jnp.asarray(cols, jnp.int32), jnp.asarray(modes, jnp.int32), xb, w_in, tabs_a, tabs_r)


def _attn_kernel(q_ref, kc_ref, kp_ref, vc_ref, vp_ref, o_ref, lse_ref):
    n = pl.program_id(1)
    qi = lax.broadcasted_iota(jnp.int32, (ATT_BLOCK, 2 * ATT_BLOCK), 0)
    kj = lax.broadcasted_iota(jnp.int32, (ATT_BLOCK, 2 * ATT_BLOCK), 1)
    dist = qi + ATT_BLOCK - kj
    band = (dist >= 0) & (dist <= ATT_STEPS)
    mask = band & ((kj >= ATT_BLOCK) | (n > 0))
    lane = lax.broadcasted_iota(jnp.int32, (ATT_BLOCK, LANES), 1)
    lse_slab = jnp.zeros((ATT_BLOCK, LANES), F32)
    for h in range(ATT_HEADS_PER_GROUP):
        sl = slice(h * ATT_HEAD_DIM, (h + 1) * ATT_HEAD_DIM)
        k2 = jnp.concatenate([kp_ref[:, sl], kc_ref[:, sl]], axis=0)
        v2 = jnp.concatenate([vp_ref[:, sl], vc_ref[:, sl]], axis=0)
        s = lax.dot_general(q_ref[:, sl], k2, (((1,), (1,)), ((), ())), preferred_element_type=F32)
        s = jnp.where(mask, s, NEG_INF)
        m = jnp.max(s, axis=1, keepdims=True)
        p = jnp.exp(s - m)
        l = jnp.sum(p, axis=1, keepdims=True)
        o = jnp.dot(p.astype(BF16), v2, preferred_element_type=F32)
        o_ref[:, sl] = o / l
        lse_slab = jnp.where(lane == h, m + jnp.log(l), lse_slab)
    lse_ref[...] = lse_slab


def _dilated_attention(qkv, g, S, cq, ck, cv):
    d = DILATIONS[g]
    W = ATT_GROUP_WIDTH
    if d == 1:
        nb = S // ATT_BLOCK
        grid = (1, nb)
        blk = (ATT_BLOCK, W)
        cur = lambda c: (lambda r, n: (n, c))
        prev = lambda c: (lambda r, n: (jnp.maximum(n - 1, 0), c))
        o_spec = pl.BlockSpec(blk, lambda r, n: (n, 0))
        l_spec = pl.BlockSpec((ATT_BLOCK, LANES), lambda r, n: (n, 0))
        o_shape = jax.ShapeDtypeStruct((S, W), F32)
        l_shape = jax.ShapeDtypeStruct((S, LANES), F32)
    else:
        nb = S // (ATT_BLOCK * d)
        grid = (d, nb)
        blk = (None, None, ATT_BLOCK, W)
        cur = lambda c: (lambda r, n: (n, r, 0, c))
        prev = lambda c: (lambda r, n: (jnp.maximum(n - 1, 0), r, 0, c))
        o_spec = pl.BlockSpec(blk, lambda r, n: (n, r, 0, 0))
        l_spec = pl.BlockSpec((None, None, ATT_BLOCK, LANES), lambda r, n: (n, r, 0, 0))
        o_shape = jax.ShapeDtypeStruct((nb, d, ATT_BLOCK, W), F32)
        l_shape = jax.ShapeDtypeStruct((nb, d, ATT_BLOCK, LANES), F32)
    return pl.pallas_call(
        _attn_kernel,
        grid=grid,
        in_specs=[
            pl.BlockSpec(blk, cur(cq)),
            pl.BlockSpec(blk, cur(ck)),
            pl.BlockSpec(blk, prev(ck)),
            pl.BlockSpec(blk, cur(cv)),
            pl.BlockSpec(blk, prev(cv)),
        ],
        out_specs=[o_spec, l_spec],
        out_shape=[o_shape, l_shape],
        compiler_params=_cparams(2),
        name=f"dilated_attn_g{g}",
    )(qkv, qkv, qkv, qkv, qkv)


def _retention_kernel(q_ref, k_ref, vlo_ref, vhi_ref, glo_ref, ghi_ref, o_ref, state_ref):
    @pl.when(pl.program_id(0) == 0)
    def _():
        state_ref[...] = jnp.zeros_like(state_ref)

    ri = lax.broadcasted_iota(jnp.int32, (RET_CHUNK, RET_CHUNK), 0)
    ci = lax.broadcasted_iota(jnp.int32, (RET_CHUNK, RET_CHUNK), 1)
    diff = (ri - ci).astype(F32)
    nrow = lax.broadcasted_iota(jnp.int32, (RET_CHUNK, 1), 0).astype(F32)
    half = RET_HEADS // 2
    for h in range(RET_HEADS):
        ld = RET_LOG_DECAY[h]
        sl = slice(h * RET_QK_DIM, (h + 1) * RET_QK_DIM)
        vsl = slice((h % half) * RET_V_DIM, (h % half + 1) * RET_V_DIM)
        v_ref, g_ref = (vlo_ref, glo_ref) if h < half else (vhi_ref, ghi_ref)
        q = q_ref[:, sl]
        k = k_ref[:, sl]
        v = v_ref[:, vsl]
        intra = jnp.where(diff >= 0, jnp.exp(jnp.maximum(diff, 0.0) * ld), 0.0)
        scores = lax.dot_general(q, k, (((1,), (1,)), ((), ())), preferred_element_type=F32) * intra
        inner = jnp.dot(scores.astype(BF16), v, preferred_element_type=F32)
        state = state_ref[h]
        q_dec = jnp.exp((nrow + 1.0) * ld)
        cross = jnp.dot((q.astype(F32) * q_dec).astype(BF16), state.astype(BF16),
                        preferred_element_type=F32)
        k_dec = jnp.exp((RET_CHUNK - 1.0 - nrow) * ld)
        kv = lax.dot_general((k.astype(F32) * k_dec).astype(BF16), v, (((0,), (0,)), ((), ())),
                             preferred_element_type=F32)
        state_ref[h] = math.exp(RET_CHUNK * ld) * state + kv
        o = inner + cross
        mu = jnp.mean(o, axis=1, keepdims=True)
        var = jnp.mean(jnp.square(o - mu), axis=1, keepdims=True)
        o = (o - mu) * lax.rsqrt(var + LN_EPS)
        gate = g_ref[:, vsl].astype(F32)
        gate = gate * (1.0 / (1.0 + jnp.exp(-gate)))
        o_ref[:, h * RET_V_DIM:(h + 1) * RET_V_DIM] = (gate * o).astype(o_ref.dtype)


def _retention(nat, S):
    nc = S // RET_CHUNK
    blk = (RET_CHUNK, COL_BLOCK)

    def col(cb):
        return pl.BlockSpec(blk, lambda c: (c, cb))

    return pl.pallas_call(
        _retention_kernel,
        grid=(nc,),
        in_specs=[col(NC_QR), col(NC_KR), col(NC_VR), col(NC_VR + 1), col(NC_GR), col(NC_GR + 1)],
        out_specs=pl.BlockSpec((RET_CHUNK, RET_HEADS * RET_V_DIM), lambda c: (c, 0)),
        out_shape=jax.ShapeDtypeStruct((S, RET_HEADS * RET_V_DIM), BF16),
        scratch_shapes=[pltpu.VMEM((RET_HEADS, RET_QK_DIM, RET_V_DIM), F32)],
        compiler_params=_cparams(1),
        name="retention",
    )(nat, nat, nat, nat, nat, nat)


MERGE_ROWS = 256


def _layer_norm(z, scale, bias):
    mu = jnp.mean(z, axis=1, keepdims=True)
    var = jnp.mean(jnp.square(z - mu), axis=1, keepdims=True)
    return (z - mu) * lax.rsqrt(var + LN_EPS) * scale + bias


def _sigmoid(x):
    return 1.0 / (1.0 + jnp.exp(-x))


def _merge_kernel(o0_ref, o1_ref, o2_ref, l0_ref, l1_ref, l2_ref, or_ref,
                  ga0_ref, ga1_ref, gb0_ref, gb1_ref, x_ref,
                  wa_ref, wr_ref, wo_ref, lns_ref, lnb_ref, wrt_ref, brt_ref,
                  x1_ref, route_ref, cnt_ref, s1_ref, s2_ref, ls1_ref, ls2_ref):
    tm = o0_ref.shape[0]
    for d, src, lsrc, dst, ldst in ((DILATIONS[1], o1_ref, l1_ref, s1_ref, ls1_ref),
                                    (DILATIONS[2], o2_ref, l2_ref, s2_ref, ls2_ref)):
        for r in range(d):
            for h in range(ATT_HEADS_PER_GROUP):
                dst[h, pl.ds(r, tm // d, stride=d), :] = src[r, :, h * ATT_HEAD_DIM:(h + 1) * ATT_HEAD_DIM]
            ldst[pl.ds(r, tm // d, stride=d), :] = lsrc[r]
    l0, l1, l2 = l0_ref[...], ls1_ref[...], ls2_ref[...]
    lm = jnp.maximum(jnp.maximum(l0, l1), l2)
    e0, e1, e2 = jnp.exp(l0 - lm), jnp.exp(l1 - lm), jnp.exp(l2 - lm)
    den = e0 + e1 + e2
    w0, w1, w2 = e0 / den, e1 / den, e2 / den
    parts = []
    for h in range(ATT_HEADS_PER_GROUP):
        sl = slice(h * ATT_HEAD_DIM, (h + 1) * ATT_HEAD_DIM)
        b0 = jnp.broadcast_to(w0[:, h:h + 1], (tm, ATT_HEAD_DIM))
        b1 = jnp.broadcast_to(w1[:, h:h + 1], (tm, ATT_HEAD_DIM))
        b2 = jnp.broadcast_to(w2[:, h:h + 1], (tm, ATT_HEAD_DIM))
        parts.append((b0 * o0_ref[:, sl] + b1 * s1_ref[h] + b2 * s2_ref[h]).astype(BF16))
    o_a = jnp.concatenate(parts, axis=1)
    y_a = jnp.dot(o_a, wa_ref[...], preferred_element_type=F32)
    y_r = jnp.dot(or_ref[...], wr_ref[...], preferred_element_type=F32)
    gate_a = jnp.concatenate([ga0_ref[...], ga1_ref[...]], axis=1).astype(F32)
    gate_b = jnp.concatenate([gb0_ref[...], gb1_ref[...]], axis=1).astype(F32)
    merged = _sigmoid(gate_a) * y_a + _sigmoid(gate_b) * y_r
    y = jnp.dot(merged.astype(BF16), wo_ref[...], preferred_element_type=F32)
    x1 = _layer_norm(ALPHA * x_ref[...] + y, lns_ref[...], lnb_ref[...])
    x1_ref[...] = x1

    logits = jnp.dot(x1, wrt_ref[...], precision=lax.Precision.HIGHEST,
                     preferred_element_type=F32) + brt_ref[...]
    lane = lax.broadcasted_iota(jnp.int32, logits.shape, 1)
    is_g = lane < N_EXPERT_GROUPS
    gl = jnp.where(is_g, logits, NEG_INF)
    gmax = jnp.max(gl, axis=1, keepdims=True)
    g_sel = jnp.min(jnp.where(is_g & (gl == gmax), lane, LANES), axis=1, keepdims=True)
    p_group = 1.0 / jnp.sum(jnp.where(is_g, jnp.exp(gl - gmax), 0.0), axis=1, keepdims=True)
    lo = N_EXPERT_GROUPS + EXPERTS_PER_GROUP * g_sel
    in_grp = (lane >= lo) & (lane < lo + EXPERTS_PER_GROUP)
    el = jnp.where(in_grp, logits, NEG_INF)
    top1 = jnp.max(el, axis=1, keepdims=True)
    idx1 = jnp.min(jnp.where(in_grp & (el == top1), lane, LANES), axis=1, keepdims=True)
    el2 = jnp.where(lane == idx1, NEG_INF, el)
    top2 = jnp.max(el2, axis=1, keepdims=True)
    idx2 = jnp.min(jnp.where(in_grp & (lane != idx1) & (el2 == top2), lane, LANES), axis=1, keepdims=True)
    t = jnp.exp(top2 - top1)
    wt1 = p_group * (1.0 / (1.0 + t))
    wt2 = p_group * (t / (1.0 + t))
    slab = jnp.where(lane == 0, (idx1 - N_EXPERT_GROUPS).astype(F32), 0.0)
    slab = jnp.where(lane == 1, (idx2 - N_EXPERT_GROUPS).astype(F32), slab)
    slab = jnp.where(lane == 2, wt1, slab)
    slab = jnp.where(lane == 3, wt2, slab)

    @pl.when(pl.program_id(0) == 0)
    def _():
        cnt_ref[...] = jnp.zeros_like(cnt_ref)

    hot = jnp.where((lane == idx1) | (lane == idx2), 1.0, 0.0)
    ri = lax.broadcasted_iota(jnp.int32, (tm, tm), 0)
    ci = lax.broadcasted_iota(jnp.int32, (tm, tm), 1)
    ltri = jnp.where(ri > ci, 1.0, 0.0).astype(BF16)
    prefix = jnp.dot(ltri, hot.astype(BF16), preferred_element_type=F32) + cnt_ref[...]
    rank1 = jnp.sum(jnp.where(lane == idx1, prefix, 0.0), axis=1, keepdims=True)
    rank2 = jnp.sum(jnp.where(lane == idx2, prefix, 0.0), axis=1, keepdims=True)
    cnt_ref[...] = cnt_ref[...] + jnp.sum(hot, axis=0, keepdims=True)
    slab = jnp.where(lane == 4, rank1, slab)
    slab = jnp.where(lane == 5, rank2, slab)
    route_ref[...] = slab


def _merge(o_g, lse_g, o_r, nat, x2d, wa, wr, wo, lns, lnb, wrt, brt, S):
    tm = MERGE_ROWS
    D = D_MODEL
    W = ATT_GROUP_WIDTH

    def rows(width):
        return pl.BlockSpec((tm, width), lambda i: (i, 0))

    def streams(g, width):
        d = DILATIONS[g]
        per = (ATT_BLOCK * d) // tm
        return pl.BlockSpec((None, d, tm // d, width), lambda i: (i // per, 0, i % per, 0))

    def ncol(cb):
        return pl.BlockSpec((tm, COL_BLOCK), lambda i: (i, cb))

    def const(shape):
        return pl.BlockSpec(shape, lambda i: (0, 0), pipeline_mode=pl.Buffered(1))

    return pl.pallas_call(
        _merge_kernel,
        grid=(S // tm,),
        in_specs=[
            rows(W), streams(1, W), streams(2, W),
            rows(LANES), streams(1, LANES), streams(2, LANES),
            rows(RET_HEADS * RET_V_DIM),
            ncol(NC_GATE_A), ncol(NC_GATE_A + 1), ncol(NC_GATE_B), ncol(NC_GATE_B + 1),
            rows(D),
            const((W, D)), const((RET_HEADS * RET_V_DIM, D)), const((D, D)),
            const((1, D)), const((1, D)), const((D, LANES)), const((1, LANES)),
        ],
        out_specs=[rows(D), rows(LANES), pl.BlockSpec((1, LANES), lambda i: (0, 0))],
        out_shape=[jax.ShapeDtypeStruct((S, D), F32), jax.ShapeDtypeStruct((S, LANES), F32),
                   jax.ShapeDtypeStruct((1, LANES), F32)],
        scratch_shapes=[pltpu.VMEM((ATT_HEADS_PER_GROUP, tm, ATT_HEAD_DIM), F32),
                        pltpu.VMEM((ATT_HEADS_PER_GROUP, tm, ATT_HEAD_DIM), F32),
                        pltpu.VMEM((tm, LANES), F32), pltpu.VMEM((tm, LANES), F32)],
        compiler_params=_cparams(1),
        name="merge_ln_router",
    )(*o_g, *lse_g, o_r, nat, nat, nat, nat, x2d, wa, wr, wo, lns, lnb, wrt, brt)


DISPATCH_TOKENS = 256
META_N_USED, META_NEXT = 0, 1


def _dispatch_kernel(e_ref, rank_ref, cnt_ref, x1_hbm, xs_hbm, dest_ref, be_ref, meta_ref,
                     pstart_ref, zeros_ref, sem, zsem):
    i = pl.program_id(0)
    n_steps = pl.num_programs(0)
    tt = DISPATCH_TOKENS
    n_blocks = be_ref.shape[0]
    pad_rows = n_blocks * MOE_BLOCK - dest_ref.shape[0]

    def zero_fill(off, size):
        pltpu.make_async_copy(zeros_ref.at[pl.ds(0, size)], xs_hbm.at[pl.ds(off, size)], zsem).start()

    @pl.when(i == 0)
    def _():
        zeros_ref[...] = jnp.zeros_like(zeros_ref)

        def per_expert(e, start):
            c = cnt_ref[e]
            nb = lax.shift_right_logical(c + (MOE_BLOCK - 1), 7)
            pstart_ref[e] = start

            def fill(b, carry):
                be_ref[lax.shift_right_logical(start, 7) + b] = e
                return carry
            lax.fori_loop(0, nb, fill, 0)
            npad = nb * MOE_BLOCK - c
            off = start + c
            head = jnp.minimum((-off) & 7, npad)
            for r in range(7):
                @pl.when(r < head)
                def _():
                    zero_fill(off + r, 1)
            off = off + head
            rem = npad - head
            for p in (64, 32, 16, 8):
                hit = (rem & p) != 0

                @pl.when(hit)
                def _():
                    zero_fill(pl.multiple_of(off, 8), p)
                off = off + jnp.where(hit, p, 0)
            return start + nb * MOE_BLOCK
        total = lax.fori_loop(0, N_EXPERTS, per_expert, 0)
        n_used = lax.shift_right_logical(total, 7)
        meta_ref[META_N_USED] = n_used

        def tail(b, carry):
            be_ref[b] = be_ref[jnp.maximum(n_used - 1, 0)]
            zero_fill(pl.multiple_of(b * MOE_BLOCK, MOE_BLOCK), MOE_BLOCK)
            return carry
        lax.fori_loop(n_used, n_blocks, tail, 0)

        def nxt(k, cur):
            e = N_EXPERTS - 1 - k
            meta_ref[META_NEXT + e] = cur
            return jnp.where(cnt_ref[e] > 0, e, cur)
        lax.fori_loop(0, N_EXPERTS, nxt, N_EXPERTS)

        for _ in range(pad_rows // MOE_BLOCK):
            pltpu.make_async_copy(zeros_ref, xs_hbm.at[pl.ds(0, MOE_BLOCK)], zsem).wait()

    def wait_rows():
        pltpu.make_async_copy(x1_hbm.at[pl.ds(0, TOP_K * tt)], xs_hbm.at[pl.ds(0, TOP_K * tt)], sem).wait()

    @pl.when(i > 0)
    def _():
        wait_rows()

    def body(t, carry):
        tok = i * tt + t
        for k in range(TOP_K):
            a = tok * TOP_K + k
            row = pstart_ref[e_ref[a]] + rank_ref[a]
            dest_ref[a] = row
            pltpu.make_async_copy(x1_hbm.at[pl.ds(tok, 1)], xs_hbm.at[pl.ds(row, 1)], sem).start()
        return carry
    lax.fori_loop(0, tt, body, 0)

    @pl.when(i == n_steps - 1)
    def _():
        wait_rows()


def _dispatch(x1, e_flat, rank_flat, counts):
    T, D = x1.shape
    A = T * TOP_K
    n_blocks = A // MOE_BLOCK + N_EXPERTS
    smem = pl.BlockSpec(memory_space=pltpu.SMEM)
    grid_spec = pltpu.PrefetchScalarGridSpec(
        num_scalar_prefetch=3,
        grid=(T // DISPATCH_TOKENS,),
        in_specs=[pl.BlockSpec(memory_space=pl.ANY)],
        out_specs=[pl.BlockSpec(memory_space=pl.ANY), smem, smem, smem],
        scratch_shapes=[
            pltpu.SMEM((N_EXPERTS,), jnp.int32),
            pltpu.VMEM((MOE_BLOCK, D), F32),
            pltpu.SemaphoreType.DMA(()),
            pltpu.SemaphoreType.DMA(()),
        ],
    )
    return pl.pallas_call(
        _dispatch_kernel,
        grid_spec=grid_spec,
        out_shape=[
            jax.ShapeDtypeStruct((n_blocks * MOE_BLOCK, D), F32),
            jax.ShapeDtypeStruct((A,), jnp.int32),
            jax.ShapeDtypeStruct((n_blocks,), jnp.int32),
            jax.ShapeDtypeStruct((META_NEXT + N_EXPERTS,), jnp.int32),
        ],
        compiler_params=_cparams(1),
        name="moe_dispatch",
    )(e_flat, rank_flat, counts, x1)


def _expert_kernel(be_ref, meta_ref, xs_ref, wg_hbm, wu_hbm, wd_hbm, y_ref,
                   wgf_ref, wuf_ref, wdf_ref, wgb_ref, wub_ref, wdb_ref, sem, slot_ref):
    b = pl.program_id(0)
    n_used = meta_ref[META_N_USED]
    e = be_ref[b]
    first = (b == 0) | (e != be_ref[jnp.maximum(b - 1, 0)])

    def copies(ex, s):
        return (pltpu.make_async_copy(wg_hbm.at[ex], wgf_ref.at[s], sem.at[s]),
                pltpu.make_async_copy(wu_hbm.at[ex], wuf_ref.at[s], sem.at[s]),
                pltpu.make_async_copy(wd_hbm.at[ex], wdf_ref.at[s], sem.at[s]))

    @pl.when((b == 0) & (n_used > 0))
    def _():
        slot_ref[0] = 0
        for c in copies(e, 0):
            c.start()

    @pl.when(first & (b < n_used))
    def _():
        s = slot_ref[0]
        nx = meta_ref[META_NEXT + e]

        @pl.when(nx < N_EXPERTS)
        def _():
            for c in copies(nx, 1 - s):
                c.start()
        for c in copies(e, s):
            c.wait()
        wgb_ref[...] = wgf_ref[s].astype(BF16)
        wub_ref[...] = wuf_ref[s].astype(BF16)
        wdb_ref[...] = wdf_ref[s].astype(BF16)
        slot_ref[0] = 1 - s

    @pl.when(b < n_used)
    def _():
        xb = xs_ref[...].astype(BF16)
        hg = jnp.dot(xb, wgb_ref[...], preferred_element_type=F32)
        hu = jnp.dot(xb, wub_ref[...], preferred_element_type=F32)
        hid = (hg * _sigmoid(hg) * hu).astype(BF16)
        y_ref[...] = jnp.dot(hid, wdb_ref[...], preferred_element_type=F32)

    @pl.when(b >= n_used)
    def _():
        y_ref[...] = jnp.zeros_like(y_ref)


def _experts(xs, block_expert, meta, w_gate, w_up, w_down):
    R, D = xs.shape
    n_blocks = block_expert.shape[0]
    H = EXPERT_HIDDEN

    def x_rows(b, be, meta):
        return (jnp.minimum(b, jnp.maximum(meta[META_N_USED] - 1, 0)), 0)

    grid_spec = pltpu.PrefetchScalarGridSpec(
        num_scalar_prefetch=2,
        grid=(n_blocks,),
        in_specs=[
            pl.BlockSpec((MOE_BLOCK, D), x_rows),
            pl.BlockSpec(memory_space=pl.ANY),
            pl.BlockSpec(memory_space=pl.ANY),
            pl.BlockSpec(memory_space=pl.ANY),
        ],
        out_specs=pl.BlockSpec((MOE_BLOCK, D), lambda b, be, meta: (b, 0)),
        scratch_shapes=[
            pltpu.VMEM((2, D, H), F32),
            pltpu.VMEM((2, D, H), F32),
            pltpu.VMEM((2, H, D), F32),
            pltpu.VMEM((D, H), BF16),
            pltpu.VMEM((D, H), BF16),
            pltpu.VMEM((H, D), BF16),
            pltpu.SemaphoreType.DMA((2,)),
            pltpu.SMEM((1,), jnp.int32),
        ],
    )
    return pl.pallas_call(
        _expert_kernel,
        grid_spec=grid_spec,
        out_shape=jax.ShapeDtypeStruct((R, D), F32),
        compiler_params=_cparams(1),
        name="moe_experts",
    )(block_expert, meta, xs, w_gate, w_up, w_down)


COMBINE_TOKENS = 64


def _combine_kernel(dest_ref, y_hbm, x1_ref, route_ref, lns_ref, lnb_ref, o_ref, yg_ref, sem):
    i = pl.program_id(0)
    n = pl.num_programs(0)
    tt = COMBINE_TOKENS

    def issue(step, slot):
        def body(r, carry):
            for k in range(TOP_K):
                row = dest_ref[(step * tt + r) * TOP_K + k]
                pltpu.make_async_copy(y_hbm.at[pl.ds(row, 1)], yg_ref.at[slot, k, pl.ds(r, 1)],
                                      sem.at[slot]).start()
            return carry
        lax.fori_loop(0, tt, body, 0)

    @pl.when(i == 0)
    def _():
        issue(0, 0)

    @pl.when(i + 1 < n)
    def _():
        issue(i + 1, (i + 1) % 2)

    slot = i % 2
    for k in range(TOP_K):
        pltpu.make_async_copy(y_hbm.at[pl.ds(0, tt)], yg_ref.at[slot, k], sem.at[slot]).wait()
    route = route_ref[...]
    w1 = route[:, 2:3]
    w2 = route[:, 3:4]
    moe = w1 * yg_ref[slot, 0] + w2 * yg_ref[slot, 1]
    o_ref[...] = _layer_norm(ALPHA * x1_ref[...] + moe, lns_ref[...], lnb_ref[...])


def _combine(dest, y, x1, route, lns, lnb):
    T, D = x1.shape
    tt = COMBINE_TOKENS
    grid_spec = pltpu.PrefetchScalarGridSpec(
        num_scalar_prefetch=1,
        grid=(T // tt,),
        in_specs=[
            pl.BlockSpec(memory_space=pl.ANY),
            pl.BlockSpec((tt, D), lambda i, d: (i, 0)),
            pl.BlockSpec((tt, LANES), lambda i, d: (i, 0)),
            pl.BlockSpec((1, D), lambda i, d: (0, 0)),
            pl.BlockSpec((1, D), lambda i, d: (0, 0)),
        ],
        out_specs=pl.BlockSpec((tt, D), lambda i, d: (i, 0)),
        scratch_shapes=[
            pltpu.VMEM((2, TOP_K, tt, D), F32),
            pltpu.SemaphoreType.DMA((2,)),
        ],
    )
    return pl.pallas_call(
        _combine_kernel,
        grid_spec=grid_spec,
        out_shape=jax.ShapeDtypeStruct((T, D), F32),
        compiler_params=_cparams(1),
        name="moe_combine_ln",
    )(dest, y, x1, route, lns, lnb)


def _rotary_tables(positions):
    pos = positions.reshape(-1).astype(F32)
    S = pos.shape[0]
    inv_a = ROPE_THETA ** (-jnp.arange(0, ROPE_DIM, 2, dtype=F32) / ROPE_DIM)
    ang = pos[:, None] * inv_a
    cos, sin = jnp.cos(ang), jnp.sin(ang)
    half = ROPE_DIM // 2
    pad1 = jnp.ones((S, LANES - ROPE_DIM), F32)
    pad0 = jnp.zeros((S, LANES - ROPE_DIM), F32)
    zh = jnp.zeros((S, half), F32)
    cos_a = jnp.concatenate([cos, cos, pad1], axis=1)
    sin_lo = jnp.concatenate([-sin, zh, pad0], axis=1)
    sin_hi = jnp.concatenate([zh, sin, pad0], axis=1)
    inv_r = RET_ROT_BASE ** (-jnp.linspace(0.0, 1.0, RET_QK_DIM // 2, dtype=F32))
    ang_r = pos[:, None] * inv_r
    cr, sr = jnp.cos(ang_r), jnp.sin(ang_r)
    tabs_a = jnp.concatenate([cos_a, sin_lo, sin_hi], axis=1)
    tabs_r = jnp.concatenate([cr, cr, -sr, sr], axis=1)
    return tabs_a, tabs_r


def kernel(x, positions, w_in, w_branch_attn, w_branch_ret, w_out, ln1_scale, ln1_bias, w_group_router, b_group_router, w_expert_router, b_expert_router, w_expert_gate, w_expert_up, w_expert_down, ln2_scale, ln2_bias):
    B, S, D = x.shape
    assert B == 1 and D == D_MODEL and w_in.shape[0] == DEPTH == 1
    assert S % (ATT_BLOCK * DILATIONS[2]) == 0
    x2d = x.reshape(S, D)
    xb = x2d.astype(BF16)
    tabs_a, tabs_r = _rotary_tables(positions)

    nat = _input_projection(xb, w_in[0], tabs_a, tabs_r, NAT_COLS, NAT_MODES, 1)
    qkv_modes = (MODE_ATT_Q, MODE_ATT_K, MODE_PLAIN)
    qkv = [None] + [
        _input_projection(xb, w_in[0], tabs_a, tabs_r, (CB_QA + g, CB_KA + g, CB_VA + g), qkv_modes,
                          DILATIONS[g])
        for g in (1, 2)]
    att = [_dilated_attention(nat, 0, S, NC_Q0, NC_K0, NC_V0)]
    att += [_dilated_attention(qkv[g], g, S, 0, 1, 2) for g in (1, 2)]
    o_r = _retention(nat, S)

    w_route = jnp.concatenate(
        [w_group_router[0], w_expert_router[0].transpose(1, 0, 2).reshape(D, N_EXPERTS),
         jnp.zeros((D, LANES - N_EXPERT_GROUPS - N_EXPERTS), F32)], axis=1)
    b_route = jnp.concatenate(
        [b_group_router[0], b_expert_router[0].reshape(N_EXPERTS),
         jnp.zeros((LANES - N_EXPERT_GROUPS - N_EXPERTS,), F32)]).reshape(1, LANES)
    x1, route, cnt = _merge(
        [a[0] for a in att], [a[1] for a in att], o_r, nat, x2d,
        w_branch_attn[0].astype(BF16), w_branch_ret[0].astype(BF16), w_out[0].astype(BF16),
        ln1_scale[0].reshape(1, D), ln1_bias[0].reshape(1, D), w_route, b_route, S)

    e_flat = route[:, 0:TOP_K].astype(jnp.int32).reshape(S * TOP_K)
    rank_flat = route[:, 4:4 + TOP_K].astype(jnp.int32).reshape(S * TOP_K)
    counts = cnt[0, N_EXPERT_GROUPS:N_EXPERT_GROUPS + N_EXPERTS].astype(jnp.int32)
    xs, dest, block_expert, meta = _dispatch(x1, e_flat, rank_flat, counts)
    y = _experts(xs, block_expert, meta, w_expert_gate[0], w_expert_up[0], w_expert_down[0])
    out = _combine(dest, y, x1, route, ln2_scale[0].reshape(1, D), ln2_bias[0].reshape(1, D))
    return out.reshape(B, S, D)
```

```python
import functools
import math

import jax
import jax.numpy as jnp
from jax import lax
from jax.experimental import pallas as pl
from jax.experimental.pallas import tpu as pltpu

F32 = jnp.float32
BF16 = jnp.bfloat16

D_MODEL = 2048
ATT_HEAD_DIM = 128
ATT_HEADS_PER_GROUP = 8
DILATIONS = (1, 4, 16)
ATT_STEPS = 128
N_ATT_GROUPS = 3
ATT_GROUP_WIDTH = ATT_HEADS_PER_GROUP * ATT_HEAD_DIM
ROPE_DIM = ATT_HEAD_DIM // 4
ROPE_THETA = 500000.0
ATT_BLOCK = 128

RET_HEADS = 8
RET_QK_DIM = 128
RET_V_DIM = 256
RET_CHUNK = 128
RET_ROT_BASE = 10000.0
RET_LOG_DECAY = tuple(math.log1p(-(2.0 ** (-5.0 - h))) for h in range(RET_HEADS))

N_EXPERT_GROUPS = 4
EXPERTS_PER_GROUP = 8
N_EXPERTS = N_EXPERT_GROUPS * EXPERTS_PER_GROUP
TOP_K = 2
EXPERT_HIDDEN = 512
MOE_BLOCK = 128

DEPTH = 1
ALPHA = (2.0 * DEPTH) ** 0.25
LN_EPS = 1e-5
NEG_INF = -1e30

COL_BLOCK = 1024
CB_QA, CB_KA, CB_VA = 0, 3, 6
CB_QR, CB_KR, CB_VR, CB_GR = 9, 10, 11, 13
CB_GATE_A, CB_GATE_B = 15, 17

MODE_PLAIN, MODE_ATT_Q, MODE_ATT_K, MODE_RET_Q, MODE_RET_K = 0, 1, 2, 3, 4

NAT_COLS = (CB_QA, CB_KA, CB_VA, CB_QR, CB_KR, CB_VR, CB_VR + 1, CB_GR, CB_GR + 1,
            CB_GATE_A, CB_GATE_A + 1, CB_GATE_B, CB_GATE_B + 1)
NAT_MODES = (MODE_ATT_Q, MODE_ATT_K, MODE_PLAIN, MODE_RET_Q, MODE_RET_K) + (MODE_PLAIN,) * 8
NC_Q0, NC_K0, NC_V0, NC_QR, NC_KR, NC_VR, NC_GR, NC_GATE_A, NC_GATE_B = 0, 1, 2, 3, 4, 5, 7, 9, 11

LANES = 128
VMEM_LIMIT = 52 * 1024 * 1024
PROJ_ROWS = 1024


def _cparams(n_grid_dims):
    return pltpu.CompilerParams(
        dimension_semantics=("arbitrary",) * n_grid_dims,
        vmem_limit_bytes=VMEM_LIMIT,
    )


def _inproj_kernel(cols_ref, modes_ref, x_ref, w_ref, ta_ref, tr_ref, o_ref, wbf_ref, acc_ref, *, d):
    del cols_ref
    j = pl.program_id(0)

    @pl.when(pl.program_id(1) == 0)
    def _():
        wbf_ref[...] = w_ref[...].astype(BF16)

    n_heads, tm, _ = acc_ref.shape
    acc = jnp.dot(x_ref[...], wbf_ref[...], preferred_element_type=F32)
    for h in range(n_heads):
        acc_ref[h] = acc[:, h * LANES:(h + 1) * LANES]
    mode = modes_ref[j]

    def att_rotary(scale):
        c = ta_ref[:, 0:LANES] * scale
        sa = ta_ref[:, LANES:2 * LANES] * scale
        sb = ta_ref[:, 2 * LANES:3 * LANES] * scale
        for h in range(n_heads):
            a = acc_ref[h]
            acc_ref[h] = (a * c + pltpu.roll(a, LANES - ROPE_DIM // 2, 1) * sa
                          + pltpu.roll(a, ROPE_DIM // 2, 1) * sb)

    def ret_rotary(scale):
        c = tr_ref[:, 0:LANES] * scale
        s = tr_ref[:, LANES:2 * LANES] * scale
        for h in range(n_heads):
            a = acc_ref[h]
            acc_ref[h] = a * c + pltpu.roll(a, RET_QK_DIM // 2, 1) * s

    pl.when(mode == MODE_ATT_Q)(lambda: att_rotary(ATT_HEAD_DIM ** -0.5))
    pl.when(mode == MODE_ATT_K)(lambda: att_rotary(1.0))
    pl.when(mode == MODE_RET_Q)(lambda: ret_rotary(1.0))
    pl.when(mode == MODE_RET_K)(lambda: ret_rotary(RET_QK_DIM ** -0.5))

    for h in range(n_heads):
        sl = slice(h * LANES, (h + 1) * LANES)
        if d == 1:
            o_ref[:, sl] = acc_ref[h].astype(o_ref.dtype)
        elif d == 4:
            P = ATT_BLOCK * d
            for sb in range(tm // P):
                for r in range(d):
                    o_ref[sb, r, :, sl] = acc_ref[h, pl.ds(sb * P + r, ATT_BLOCK, stride=d), :].astype(o_ref.dtype)
        else:
            for r in range(d):
                o_ref[r, :, sl] = acc_ref[h, pl.ds(r, tm // d, stride=d), :].astype(o_ref.dtype)


def _input_projection(xb, w_in, tabs_a, tabs_r, cols, modes, d):
    S, D = xb.shape
    tm, tn = PROJ_ROWS, COL_BLOCK
    n = len(cols)
    P = ATT_BLOCK * d
    if d == 1:
        out_shape = jax.ShapeDtypeStruct((S, n * tn), BF16)
        out_spec = pl.BlockSpec((tm, tn), lambda j, i, c, m: (i, j))
    elif d == 4:
        out_shape = jax.ShapeDtypeStruct((S // P, d, ATT_BLOCK, n * tn), BF16)
        out_spec = pl.BlockSpec((tm // P, d, ATT_BLOCK, tn), lambda j, i, c, m: (i, 0, 0, j))
    else:
        assert d == 16 and P == 2 * tm
        out_shape = jax.ShapeDtypeStruct((S // P, d, ATT_BLOCK, n * tn), BF16)
        out_spec = pl.BlockSpec((None, d, tm // d, tn), lambda j, i, c, m: (i // 2, 0, i % 2, j))

    def tab_rows(j, i, c, m):
        return (jnp.where(m[j] != MODE_PLAIN, i, 0), 0)

    grid_spec = pltpu.PrefetchScalarGridSpec(
        num_scalar_prefetch=2,
        grid=(n, S // tm),
        in_specs=[
            pl.BlockSpec((tm, D), lambda j, i, c, m: (i, 0)),
            pl.BlockSpec((D, tn), lambda j, i, c, m: (0, c[j])),
            pl.BlockSpec((tm, 3 * LANES), tab_rows),
            pl.BlockSpec((tm, 2 * LANES), tab_rows),
        ],
        out_specs=out_spec,
        scratch_shapes=[pltpu.VMEM((D, tn), BF16), pltpu.VMEM((tn // LANES, tm, LANES), F32)],
    )
    return pl.pallas_call(
        functools.partial(_inproj_kernel, d=d),
        grid_spec=grid_spec,
        out_shape=out_shape,
        compiler_params=_cparams(2),
        name=f"in_proj_d{d}",
    )(jnp.asarray(cols, jnp.int32), jnp.asarray(modes, jnp.int32), xb, w_in, tabs_a, tabs_r)


def _attn_kernel(q_ref, kc_ref, kp_ref, vc_ref, vp_ref, o_ref, lse_ref):
    n = pl.program_id(1)
    qi = lax.broadcasted_iota(jnp.int32, (ATT_BLOCK, 2 * ATT_BLOCK), 0)
    kj = lax.broadcasted_iota(jnp.int32, (ATT_BLOCK, 2 * ATT_BLOCK), 1)
    dist = qi + ATT_BLOCK - kj
    band = (dist >= 0) & (dist <= ATT_STEPS)
    mask = band & ((kj >= ATT_BLOCK) | (n > 0))
    lane = lax.broadcasted_iota(jnp.int32, (ATT_BLOCK, LANES), 1)
    lse_slab = jnp.zeros((ATT_BLOCK, LANES), F32)
    for h in range(ATT_HEADS_PER_GROUP):
        sl = slice(h * ATT_HEAD_DIM, (h + 1) * ATT_HEAD_DIM)
        k2 = jnp.concatenate([kp_ref[:, sl], kc_ref[:, sl]], axis=0)
        v2 = jnp.concatenate([vp_ref[:, sl], vc_ref[:, sl]], axis=0)
        s = lax.dot_general(q_ref[:, sl], k2, (((1,), (1,)), ((), ())), preferred_element_type=F32)
        s = jnp.where(mask, s, NEG_INF)
        m = jnp.max(s, axis=1, keepdims=True)
        p = jnp.exp(s - m)
        l = jnp.sum(p, axis=1, keepdims=True)
        o = jnp.dot(p.astype(BF16), v2, preferred_element_type=F32)
        o_ref[:, sl] = o / l
        lse_slab = jnp.where(lane == h, m + jnp.log(l), lse_slab)
    lse_ref[...] = lse_slab


def _dilated_attention(qkv, g, S, cq, ck, cv):
    d = DILATIONS[g]
    W = ATT_GROUP_WIDTH
    if d == 1:
        nb = S // ATT_BLOCK
        grid = (1, nb)
        blk = (ATT_BLOCK, W)
        cur = lambda c: (lambda r, n: (n, c))
        prev = lambda c: (lambda r, n: (jnp.maximum(n - 1, 0), c))
        o_spec = pl.BlockSpec(blk, lambda r, n: (n, 0))
        l_spec = pl.BlockSpec((ATT_BLOCK, LANES), lambda r, n: (n, 0))
        o_shape = jax.ShapeDtypeStruct((S, W), F32)
        l_shape = jax.ShapeDtypeStruct((S, LANES), F32)
    else:
        nb = S // (ATT_BLOCK * d)
        grid = (d, nb)
        blk = (None, None, ATT_BLOCK, W)
        cur = lambda c: (lambda r, n: (n, r, 0, c))
        prev = lambda c: (lambda r, n: (jnp.maximum(n - 1, 0), r, 0, c))
        o_spec = pl.BlockSpec(blk, lambda r, n: (n, r, 0, 0))
        l_spec = pl.BlockSpec((None, None, ATT_BLOCK, LANES), lambda r, n: (n, r, 0, 0))
        o_shape = jax.ShapeDtypeStruct((nb, d, ATT_BLOCK, W), F32)
        l_shape = jax.ShapeDtypeStruct((nb, d, ATT_BLOCK, LANES), F32)
    return pl.pallas_call(
        _attn_kernel,
        grid=grid,
        in_specs=[
            pl.BlockSpec(blk, cur(cq)),
            pl.BlockSpec(blk, cur(ck)),
            pl.BlockSpec(blk, prev(ck)),
            pl.BlockSpec(blk, cur(cv)),
            pl.BlockSpec(blk, prev(cv)),
        ],
        out_specs=[o_spec, l_spec],
        out_shape=[o_shape, l_shape],
        compiler_params=_cparams(2),
        name=f"dilated_attn_g{g}",
    )(qkv, qkv, qkv, qkv, qkv)


def _retention_kernel(q_ref, k_ref, vlo_ref, vhi_ref, glo_ref, ghi_ref, o_ref, state_ref):
    @pl.when(pl.program_id(0) == 0)
    def _():
        state_ref[...] = jnp.zeros_like(state_ref)

    ri = lax.broadcasted_iota(jnp.int32, (RET_CHUNK, RET_CHUNK), 0)
    ci = lax.broadcasted_iota(jnp.int32, (RET_CHUNK, RET_CHUNK), 1)
    diff = (ri - ci).astype(F32)
    nrow = lax.broadcasted_iota(jnp.int32, (RET_CHUNK, 1), 0).astype(F32)
    half = RET_HEADS // 2
    for h in range(RET_HEADS):
        ld = RET_LOG_DECAY[h]
        sl = slice(h * RET_QK_DIM, (h + 1) * RET_QK_DIM)
        vsl = slice((h % half) * RET_V_DIM, (h % half + 1) * RET_V_DIM)
        v_ref, g_ref = (vlo_ref, glo_ref) if h < half else (vhi_ref, ghi_ref)
        q = q_ref[:, sl]
        k = k_ref[:, sl]
        v = v_ref[:, vsl]
        intra = jnp.where(diff >= 0, jnp.exp(jnp.maximum(diff, 0.0) * ld), 0.0)
        scores = lax.dot_general(q, k, (((1,), (1,)), ((), ())), preferred_element_type=F32) * intra
        inner = jnp.dot(scores.astype(BF16), v, preferred_element_type=F32)
        state = state_ref[h]
        q_dec = jnp.exp((nrow + 1.0) * ld)
        cross = jnp.dot((q.astype(F32) * q_dec).astype(BF16), state.astype(BF16),
                        preferred_element_type=F32)
        k_dec = jnp.exp((RET_CHUNK - 1.0 - nrow) * ld)
        kv = lax.dot_general((k.astype(F32) * k_dec).astype(BF16), v, (((0,), (0,)), ((), ())),
                             preferred_element_type=F32)
        state_ref[h] = math.exp(RET_CHUNK * ld) * state + kv
        o = inner + cross
        mu = jnp.mean(o, axis=1, keepdims=True)
        var = jnp.mean(jnp.square(o - mu), axis=1, keepdims=True)
        o = (o - mu) * lax.rsqrt(var + LN_EPS)
        gate = g_ref[:, vsl].astype(F32)
        gate = gate * (1.0 / (1.0 + jnp.exp(-gate)))
        o_ref[:, h * RET_V_DIM:(h + 1) * RET_V_DIM] = (gate * o).astype(o_ref.dtype)


def _retention(nat, S):
    nc = S // RET_CHUNK
    blk = (RET_CHUNK, COL_BLOCK)

    def col(cb):
        return pl.BlockSpec(blk, lambda c: (c, cb))

    return pl.pallas_call(
        _retention_kernel,
        grid=(nc,),
        in_specs=[col(NC_QR), col(NC_KR), col(NC_VR), col(NC_VR + 1), col(NC_GR), col(NC_GR + 1)],
        out_specs=pl.BlockSpec((RET_CHUNK, RET_HEADS * RET_V_DIM), lambda c: (c, 0)),
        out_shape=jax.ShapeDtypeStruct((S, RET_HEADS * RET_V_DIM), BF16),
        scratch_shapes=[pltpu.VMEM((RET_HEADS, RET_QK_DIM, RET_V_DIM), F32)],
        compiler_params=_cparams(1),
        name="retention",
    )(nat, nat, nat, nat, nat, nat)


MERGE_ROWS = 256


def _layer_norm(z, scale, bias):
    mu = jnp.mean(z, axis=1, keepdims=True)
    var = jnp.mean(jnp.square(z - mu), axis=1, keepdims=True)
    return (z - mu) * lax.rsqrt(var + LN_EPS) * scale + bias


def _sigmoid(x):
    return 1.0 / (1.0 + jnp.exp(-x))


def _merge_kernel(o0_ref, o1_ref, o2_ref, l0_ref, l1_ref, l2_ref, or_ref,
                  ga0_ref, ga1_ref, gb0_ref, gb1_ref, x_ref,
                  wa_ref, wr_ref, wo_ref, lns_ref, lnb_ref, wrt_ref, brt_ref,
                  x1_ref, route_ref, cnt_ref, s1_ref, s2_ref, ls1_ref, ls2_ref):
    tm = o0_ref.shape[0]
    for d, src, lsrc, dst, ldst in ((DILATIONS[1], o1_ref, l1_ref, s1_ref, ls1_ref),
                                    (DILATIONS[2], o2_ref, l2_ref, s2_ref, ls2_ref)):
        for r in range(d):
            for h in range(ATT_HEADS_PER_GROUP):
                dst[h, pl.ds(r, tm // d, stride=d), :] = src[r, :, h * ATT_HEAD_DIM:(h + 1) * ATT_HEAD_DIM]
            ldst[pl.ds(r, tm // d, stride=d), :] = lsrc[r]
    l0, l1, l2 = l0_ref[...], ls1_ref[...], ls2_ref[...]
    lm = jnp.maximum(jnp.maximum(l0, l1), l2)
    e0, e1, e2 = jnp.exp(l0 - lm), jnp.exp(l1 - lm), jnp.exp(l2 - lm)
    den = e0 + e1 + e2
    w0, w1, w2 = e0 / den, e1 / den, e2 / den
    parts = []
    for h in range(ATT_HEADS_PER_GROUP):
        sl = slice(h * ATT_HEAD_DIM, (h + 1) * ATT_HEAD_DIM)
        b0 = jnp.broadcast_to(w0[:, h:h + 1], (tm, ATT_HEAD_DIM))
        b1 = jnp.broadcast_to(w1[:, h:h + 1], (tm, ATT_HEAD_DIM))
        b2 = jnp.broadcast_to(w2[:, h:h + 1], (tm, ATT_HEAD_DIM))
        parts.append((b0 * o0_ref[:, sl] + b1 * s1_ref[h] + b2 * s2_ref[h]).astype(BF16))
    o_a = jnp.concatenate(parts, axis=1)
    y_a = jnp.dot(o_a, wa_ref[...], preferred_element_type=F32)
    y_r = jnp.dot(or_ref[...], wr_ref[...], preferred_element_type=F32)
    gate_a = jnp.concatenate([ga0_ref[...], ga1_ref[...]], axis=1).astype(F32)
    gate_b = jnp.concatenate([gb0_ref[...], gb1_ref[...]], axis=1).astype(F32)
    merged = _sigmoid(gate_a) * y_a + _sigmoid(gate_b) * y_r
    y = jnp.dot(merged.astype(BF16), wo_ref[...], preferred_element_type=F32)
    x1 = _layer_norm(ALPHA * x_ref[...] + y, lns_ref[...], lnb_ref[...])
    x1_ref[...] = x1

    logits = jnp.dot(x1, wrt_ref[...], precision=lax.Precision.HIGHEST,
                     preferred_element_type=F32) + brt_ref[...]
    lane = lax.broadcasted_iota(jnp.int32, logits.shape, 1)
    is_g = lane < N_EXPERT_GROUPS
    gl = jnp.where(is_g, logits, NEG_INF)
    gmax = jnp.max(gl, axis=1, keepdims=True)
    g_sel = jnp.min(jnp.where(is_g & (gl == gmax), lane, LANES), axis=1, keepdims=True)
    p_group = 1.0 / jnp.sum(jnp.where(is_g, jnp.exp(gl - gmax), 0.0), axis=1, keepdims=True)
    lo = N_EXPERT_GROUPS + EXPERTS_PER_GROUP * g_sel
    in_grp = (lane >= lo) & (lane < lo + EXPERTS_PER_GROUP)
    el = jnp.where(in_grp, logits, NEG_INF)
    top1 = jnp.max(el, axis=1, keepdims=True)
    idx1 = jnp.min(jnp.where(in_grp & (el == top1), lane, LANES), axis=1, keepdims=True)
    el2 = jnp.where(lane == idx1, NEG_INF, el)
    top2 = jnp.max(el2, axis=1, keepdims=True)
    idx2 = jnp.min(jnp.where(in_grp & (lane != idx1) & (el2 == top2), lane, LANES), axis=1, keepdims=True)
    t = jnp.exp(top2 - top1)
    wt1 = p_group * (1.0 / (1.0 + t))
    wt2 = p_group * (t / (1.0 + t))
    slab = jnp.where(lane == 0, (idx1 - N_EXPERT_GROUPS).astype(F32), 0.0)
    slab = jnp.where(lane == 1, (idx2 - N_EXPERT_GROUPS).astype(F32), slab)
    slab = jnp.where(lane == 2, wt1, slab)
    slab = jnp.where(lane == 3, wt2, slab)

    @pl.when(pl.program_id(0) == 0)
    def _():
        cnt_ref[...] = jnp.zeros_like(cnt_ref)

    hot = jnp.where((lane == idx1) | (lane == idx2), 1.0, 0.0)
    ri = lax.broadcasted_iota(jnp.int32, (tm, tm), 0)
    ci = lax.broadcasted_iota(jnp.int32, (tm, tm), 1)
    ltri = jnp.where(ri > ci, 1.0, 0.0).astype(BF16)
    prefix = jnp.dot(ltri, hot.astype(BF16), preferred_element_type=F32) + cnt_ref[...]
    rank1 = jnp.sum(jnp.where(lane == idx1, prefix, 0.0), axis=1, keepdims=True)
    rank2 = jnp.sum(jnp.where(lane == idx2, prefix, 0.0), axis=1, keepdims=True)
    cnt_ref[...] = cnt_ref[...] + jnp.sum(hot, axis=0, keepdims=True)
    slab = jnp.where(lane == 4, rank1, slab)
    slab = jnp.where(lane == 5, rank2, slab)
    route_ref[...] = slab


def _merge(o_g, lse_g, o_r, nat, x2d, wa, wr, wo, lns, lnb, wrt, brt, S):
    tm = MERGE_ROWS
    D = D_MODEL
    W = ATT_GROUP_WIDTH

    def rows(width):
        return pl.BlockSpec((tm, width), lambda i: (i, 0))

    def streams(g, width):
        d = DILATIONS[g]
        per = (ATT_BLOCK * d) // tm
        return pl.BlockSpec((None, d, tm // d, width), lambda i: (i // per, 0, i % per, 0))

    def ncol(cb):
        return pl.BlockSpec((tm, COL_BLOCK), lambda i: (i, cb))

    def const(shape):
        return pl.BlockSpec(shape, lambda i: (0, 0), pipeline_mode=pl.Buffered(1))

    return pl.pallas_call(
        _merge_kernel,
        grid=(S // tm,),
        in_specs=[
            rows(W), streams(1, W), streams(2, W),
            rows(LANES), streams(1, LANES), streams(2, LANES),
            rows(RET_HEADS * RET_V_DIM),
            ncol(NC_GATE_A), ncol(NC_GATE_A + 1), ncol(NC_GATE_B), ncol(NC_GATE_B + 1),
            rows(D),
            const((W, D)), const((RET_HEADS * RET_V_DIM, D)), const((D, D)),
            const((1, D)), const((1, D)), const((D, LANES)), const((1, LANES)),
        ],
        out_specs=[rows(D), rows(LANES), pl.BlockSpec((1, LANES), lambda i: (0, 0))],
        out_shape=[jax.ShapeDtypeStruct((S, D), F32), jax.ShapeDtypeStruct((S, LANES), F32),
                   jax.ShapeDtypeStruct((1, LANES), F32)],
        scratch_shapes=[pltpu.VMEM((ATT_HEADS_PER_GROUP, tm, ATT_HEAD_DIM), F32),
                        pltpu.VMEM((ATT_HEADS_PER_GROUP, tm, ATT_HEAD_DIM), F32),
                        pltpu.VMEM((tm, LANES), F32), pltpu.VMEM((tm, LANES), F32)],
        compiler_params=_cparams(1),
        name="merge_ln_router",
    )(*o_g, *lse_g, o_r, nat, nat, nat, nat, x2d, wa, wr, wo, lns, lnb, wrt, brt)


META_N_USED, META_NEXT = 0, 1


def _plan_kernel(e_ref, rank_ref, cnt_ref, dest_ref, rt_ref, be_ref, meta_ref, pstart_ref):
    n_blocks = be_ref.shape[0]

    def pad_row(r, carry):
        rt_ref[r] = 0
        return carry

    def per_expert(e, start):
        nb = lax.shift_right_logical(cnt_ref[e] + (MOE_BLOCK - 1), 7)
        pstart_ref[e] = start

        def fill(b, carry):
            be_ref[lax.shift_right_logical(start, 7) + b] = e
            return carry
        lax.fori_loop(0, nb, fill, 0)
        lax.fori_loop(start + cnt_ref[e], start + nb * MOE_BLOCK, pad_row, 0)
        return start + nb * MOE_BLOCK
    total = lax.fori_loop(0, N_EXPERTS, per_expert, 0)
    n_used = lax.shift_right_logical(total, 7)
    meta_ref[META_N_USED] = n_used

    def tail(b, carry):
        be_ref[b] = be_ref[jnp.maximum(n_used - 1, 0)]
        return carry
    lax.fori_loop(n_used, n_blocks, tail, 0)
    lax.fori_loop(total, rt_ref.shape[0], pad_row, 0)

    def nxt(k, cur):
        e = N_EXPERTS - 1 - k
        meta_ref[META_NEXT + e] = cur
        return jnp.where(cnt_ref[e] > 0, e, cur)
    lax.fori_loop(0, N_EXPERTS, nxt, N_EXPERTS)

    def assign(a, carry):
        row = pstart_ref[e_ref[a]] + rank_ref[a]
        dest_ref[a] = row
        rt_ref[row] = lax.shift_right_logical(a, 1)
        return carry
    lax.fori_loop(0, dest_ref.shape[0], assign, 0, unroll=8)


def _plan(e_flat, rank_flat, counts):
    A = e_flat.shape[0]
    n_blocks = A // MOE_BLOCK + N_EXPERTS
    smem = pl.BlockSpec(memory_space=pltpu.SMEM)
    return pl.pallas_call(
        _plan_kernel,
        in_specs=[smem, smem, smem],
        out_specs=[smem, smem, smem, smem],
        out_shape=[
            jax.ShapeDtypeStruct((A,), jnp.int32),
            jax.ShapeDtypeStruct((n_blocks * MOE_BLOCK,), jnp.int32),
            jax.ShapeDtypeStruct((n_blocks,), jnp.int32),
            jax.ShapeDtypeStruct((META_NEXT + N_EXPERTS,), jnp.int32),
        ],
        scratch_shapes=[pltpu.SMEM((N_EXPERTS,), jnp.int32)],
        name="moe_plan",
    )(e_flat, rank_flat, counts)


def _expert_kernel(be_ref, meta_ref, rt_ref, x_hbm, wg_hbm, wu_hbm, wd_hbm, y_ref,
                   xg_ref, wgf_ref, wuf_ref, wdf_ref, wgb_ref, wub_ref, wdb_ref, xsem, sem, slot_ref):
    b = pl.program_id(0)
    n_used = meta_ref[META_N_USED]
    e = be_ref[b]
    first = (b == 0) | (e != be_ref[jnp.maximum(b - 1, 0)])

    def gather(blk, slot):
        def body(i, carry):
            tok = rt_ref[blk * MOE_BLOCK + i]
            pltpu.make_async_copy(x_hbm.at[pl.ds(tok, 1)], xg_ref.at[slot, pl.ds(i, 1)],
                                  xsem.at[slot]).start()
            return carry
        lax.fori_loop(0, MOE_BLOCK, body, 0, unroll=8)

    @pl.when((b == 0) & (n_used > 0))
    def _():
        gather(0, 0)

    @pl.when(b + 1 < n_used)
    def _():
        gather(b + 1, (b + 1) % 2)

    def copies(ex, s):
        return (pltpu.make_async_copy(wg_hbm.at[ex], wgf_ref.at[s], sem.at[s]),
                pltpu.make_async_copy(wu_hbm.at[ex], wuf_ref.at[s], sem.at[s]),
                pltpu.make_async_copy(wd_hbm.at[ex], wdf_ref.at[s], sem.at[s]))

    @pl.when((b == 0) & (n_used > 0))
    def _():
        slot_ref[0] = 0
        for c in copies(e, 0):
            c.start()

    @pl.when(first & (b < n_used))
    def _():
        s = slot_ref[0]
        nx = meta_ref[META_NEXT + e]

        @pl.when(nx < N_EXPERTS)
        def _():
            for c in copies(nx, 1 - s):
                c.start()
        for c in copies(e, s):
            c.wait()
        wgb_ref[...] = wgf_ref[s].astype(BF16)
        wub_ref[...] = wuf_ref[s].astype(BF16)
        wdb_ref[...] = wdf_ref[s].astype(BF16)
        slot_ref[0] = 1 - s

    @pl.when(b < n_used)
    def _():
        slot = b % 2
        pltpu.make_async_copy(x_hbm.at[pl.ds(0, MOE_BLOCK)], xg_ref.at[slot], xsem.at[slot]).wait()
        xb = xg_ref[slot].astype(BF16)
        hg = jnp.dot(xb, wgb_ref[...], preferred_element_type=F32)
        hu = jnp.dot(xb, wub_ref[...], preferred_element_type=F32)
        hid = (hg * _sigmoid(hg) * hu).astype(BF16)
        y_ref[...] = jnp.dot(hid, wdb_ref[...], preferred_element_type=F32)

    @pl.when(b >= n_used)
    def _():
        y_ref[...] = jnp.zeros_like(y_ref)


def _experts(x1, block_expert, meta, row_tok, w_gate, w_up, w_down):
    T, D = x1.shape
    n_blocks = block_expert.shape[0]
    R = n_blocks * MOE_BLOCK
    H = EXPERT_HIDDEN
    grid_spec = pltpu.PrefetchScalarGridSpec(
        num_scalar_prefetch=3,
        grid=(n_blocks,),
        in_specs=[
            pl.BlockSpec(memory_space=pl.ANY),
            pl.BlockSpec(memory_space=pl.ANY),
            pl.BlockSpec(memory_space=pl.ANY),
            pl.BlockSpec(memory_space=pl.ANY),
        ],
        out_specs=pl.BlockSpec((MOE_BLOCK, D), lambda b, be, meta, rt: (b, 0)),
        scratch_shapes=[
            pltpu.VMEM((2, MOE_BLOCK, D), F32),
            pltpu.VMEM((2, D, H), F32),
            pltpu.VMEM((2, D, H), F32),
            pltpu.VMEM((2, H, D), F32),
            pltpu.VMEM((D, H), BF16),
            pltpu.VMEM((D, H), BF16),
            pltpu.VMEM((H, D), BF16),
            pltpu.SemaphoreType.DMA((2,)),
            pltpu.SemaphoreType.DMA((2,)),
            pltpu.SMEM((1,), jnp.int32),
        ],
    )
    return pl.pallas_call(
        _expert_kernel,
        grid_spec=grid_spec,
        out_shape=jax.ShapeDtypeStruct((R, D), F32),
        compiler_params=_cparams(1),
        name="moe_experts",
    )(block_expert, meta, row_tok, x1, w_gate, w_up, w_down)


COMBINE_TOKENS = 64


def _combine_kernel(dest_ref, y_hbm, x1_ref, route_ref, lns_ref, lnb_ref, o_ref, yg_ref, sem):
    i = pl.program_id(0)
    n = pl.num_programs(0)
    tt = COMBINE_TOKENS

    def issue(step, slot):
        def body(r, carry):
            for k in range(TOP_K):
                row = dest_ref[(step * tt + r) * TOP_K + k]
                pltpu.make_async_copy(y_hbm.at[pl.ds(row, 1)], yg_ref.at[slot, k, pl.ds(r, 1)],
                                      sem.at[slot]).start()
            return carry
        lax.fori_loop(0, tt, body, 0, unroll=4)

    @pl.when(i == 0)
    def _():
        issue(0, 0)

    @pl.when(i + 1 < n)
    def _():
        issue(i + 1, (i + 1) % 2)

    slot = i % 2
    for k in range(TOP_K):
        pltpu.make_async_copy(y_hbm.at[pl.ds(0, tt)], yg_ref.at[slot, k], sem.at[slot]).wait()
    route = route_ref[...]
    w1 = route[:, 2:3]
    w2 = route[:, 3:4]
    moe = w1 * yg_ref[slot, 0] + w2 * yg_ref[slot, 1]
    o_ref[...] = _layer_norm(ALPHA * x1_ref[...] + moe, lns_ref[...], lnb_ref[...])


def _combine(dest, y, x1, route, lns, lnb):
    T, D = x1.shape
    tt = COMBINE_TOKENS
    grid_spec = pltpu.PrefetchScalarGridSpec(
        num_scalar_prefetch=1,
        grid=(T // tt,),
        in_specs=[
            pl.BlockSpec(memory_space=pl.ANY),
            pl.BlockSpec((tt, D), lambda i, d: (i, 0)),
            pl.BlockSpec((tt, LANES), lambda i, d: (i, 0)),
            pl.BlockSpec((1, D), lambda i, d: (0, 0)),
            pl.BlockSpec((1, D), lambda i, d: (0, 0)),
        ],
        out_specs=pl.BlockSpec((tt, D), lambda i, d: (i, 0)),
        scratch_shapes=[
            pltpu.VMEM((2, TOP_K, tt, D), F32),
            pltpu.SemaphoreType.DMA((2,)),
        ],
    )
    return pl.pallas_call(
        _combine_kernel,
        grid_spec=grid_spec,
        out_shape=jax.ShapeDtypeStruct((T, D), F32),
        compiler_params=_cparams(1),
        name="moe_combine_ln",
    )(dest, y, x1, route, lns, lnb)


def _rotary_tables(positions):
    pos = positions.reshape(-1).astype(F32)
    S = pos.shape[0]
    inv_a = ROPE_THETA ** (-jnp.arange(0, ROPE_DIM, 2, dtype=F32) / ROPE_DIM)
    ang = pos[:, None] * inv_a
    cos, sin = jnp.cos(ang), jnp.sin(ang)
    half = ROPE_DIM // 2
    pad1 = jnp.ones((S, LANES - ROPE_DIM), F32)
    pad0 = jnp.zeros((S, LANES - ROPE_DIM), F32)
    zh = jnp.zeros((S, half), F32)
    cos_a = jnp.concatenate([cos, cos, pad1], axis=1)
    sin_lo = jnp.concatenate([-sin, zh, pad0], axis=1)
    sin_hi = jnp.concatenate([zh, sin, pad0], axis=1)
    inv_r = RET_ROT_BASE ** (-jnp.linspace(0.0, 1.0, RET_QK_DIM // 2, dtype=F32))
    ang_r = pos[:, None] * inv_r
    cr, sr = jnp.cos(ang_r), jnp.sin(ang_r)
    tabs_a = jnp.concatenate([cos_a, sin_lo, sin_hi], axis=1)
    tabs_r = jnp.concatenate([cr, cr, -sr, sr], axis=1)
    return tabs_a, tabs_r


def kernel(x, positions, w_in, w_branch_attn, w_branch_ret, w_out, ln1_scale, ln1_bias, w_group_router, b_group_router, w_expert_router, b_expert_router, w_expert_gate, w_expert_up, w_expert_down, ln2_scale, ln2_bias):
    B, S, D = x.shape
    assert B == 1 and D == D_MODEL and w_in.shape[0] == DEPTH == 1
    assert S % (ATT_BLOCK * DILATIONS[2]) == 0
    x2d = x.reshape(S, D)
    xb = x2d.astype(BF16)
    tabs_a, tabs_r = _rotary_tables(positions)

    nat = _input_projection(xb, w_in[0], tabs_a, tabs_r, NAT_COLS, NAT_MODES, 1)
    qkv_modes = (MODE_ATT_Q, MODE_ATT_K, MODE_PLAIN)
    qkv = [None] + [
        _input_projection(xb, w_in[0], tabs_a, tabs_r, (CB_QA + g, CB_KA + g, CB_VA + g), qkv_modes,
                          DILATIONS[g])
        for g in (1, 2)]
    att = [_dilated_attention(nat, 0, S, NC_Q0, NC_K0, NC_V0)]
    att += [_dilated_attention(qkv[g], g, S, 0, 1, 2) for g in (1, 2)]
    o_r = _retention(nat, S)

    w_route = jnp.concatenate(
        [w_group_router[0], w_expert_router[0].transpose(1, 0, 2).reshape(D, N_EXPERTS),
         jnp.zeros((D, LANES - N_EXPERT_GROUPS - N_EXPERTS), F32)], axis=1)
    b_route = jnp.concatenate(
        [b_group_router[0], b_expert_router[0].reshape(N_EXPERTS),
         jnp.zeros((LANES - N_EXPERT_GROUPS - N_EXPERTS,), F32)]).reshape(1, LANES)
    x1, route, cnt = _merge(
        [a[0] for a in att], [a[1] for a in att], o_r, nat, x2d,
        w_branch_attn[0].astype(BF16), w_branch_ret[0].astype(BF16), w_out[0].astype(BF16),
        ln1_scale[0].reshape(1, D), ln1_bias[0].reshape(1, D), w_route, b_route, S)

    e_flat = route[:, 0:TOP_K].astype(jnp.int32).reshape(S * TOP_K)
    rank_flat = route[:, 4:4 + TOP_K].astype(jnp.int32).reshape(S * TOP_K)
    counts = cnt[0, N_EXPERT_GROUPS:N_EXPERT_GROUPS + N_EXPERTS].astype(jnp.int32)
    dest, row_tok, block_expert, meta = _plan(e_flat, rank_flat, counts)
    y = _experts(x1, block_expert, meta, row_tok, w_expert_gate[0], w_expert_up[0], w_expert_down[0])
    out = _combine(dest, y, x1, route, ln2_scale[0].reshape(1, D), ln2_bias[0].reshape(1, D))
    return out.reshape(B, S, D)
```

```python
import functools
import math

import jax
import jax.numpy as jnp
from jax import lax
from jax.experimental import pallas as pl
from jax.experimental.pallas import tpu as pltpu

F32 = jnp.float32
BF16 = jnp.bfloat16

D_MODEL = 2048
ATT_HEAD_DIM = 128
ATT_HEADS_PER_GROUP = 8
DILATIONS = (1, 4, 16)
ATT_STEPS = 128
N_ATT_GROUPS = 3
ATT_GROUP_WIDTH = ATT_HEADS_PER_GROUP * ATT_HEAD_DIM
ROPE_DIM = ATT_HEAD_DIM // 4
ROPE_THETA = 500000.0
ATT_BLOCK = 128

RET_HEADS = 8
RET_QK_DIM = 128
RET_V_DIM = 256
RET_CHUNK = 128
RET_ROT_BASE = 10000.0
RET_LOG_DECAY = tuple(math.log1p(-(2.0 ** (-5.0 - h))) for h in range(RET_HEADS))

N_EXPERT_GROUPS = 4
EXPERTS_PER_GROUP = 8
N_EXPERTS = N_EXPERT_GROUPS * EXPERTS_PER_GROUP
TOP_K = 2
EXPERT_HIDDEN = 512
MOE_BLOCK = 128

DEPTH = 1
ALPHA = (2.0 * DEPTH) ** 0.25
LN_EPS = 1e-5
NEG_INF = -1e30

COL_BLOCK = 1024
CB_QA, CB_KA, CB_VA = 0, 3, 6
CB_QR, CB_KR, CB_VR, CB_GR = 9, 10, 11, 13
CB_GATE_A, CB_GATE_B = 15, 17

MODE_PLAIN, MODE_ATT_Q, MODE_ATT_K, MODE_RET_Q, MODE_RET_K = 0, 1, 2, 3, 4

NAT_COLS = (CB_QA, CB_KA, CB_VA, CB_QR, CB_KR, CB_VR, CB_VR + 1, CB_GR, CB_GR + 1,
            CB_GATE_A, CB_GATE_A + 1, CB_GATE_B, CB_GATE_B + 1)
NAT_MODES = (MODE_ATT_Q, MODE_ATT_K, MODE_PLAIN, MODE_RET_Q, MODE_RET_K) + (MODE_PLAIN,) * 8
NC_Q0, NC_K0, NC_V0, NC_QR, NC_KR, NC_VR, NC_GR, NC_GATE_A, NC_GATE_B = 0, 1, 2, 3, 4, 5, 7, 9, 11

LANES = 128
VMEM_LIMIT = 52 * 1024 * 1024
PROJ_ROWS = 1024


def _cparams(n_grid_dims):
    return pltpu.CompilerParams(
        dimension_semantics=("arbitrary",) * n_grid_dims,
        vmem_limit_bytes=VMEM_LIMIT,
    )


def _inproj_kernel(cols_ref, modes_ref, x_ref, w_ref, ta_ref, tr_ref, pa_ref, pr_ref, o_ref,
                   wbf_ref, *acc, d):
    del cols_ref
    j = pl.program_id(0)

    @pl.when(pl.program_id(1) == 0)
    def _():
        wbf_ref[...] = w_ref[...].astype(BF16)

    mode = modes_ref[j]
    tm, tn = x_ref.shape[0], wbf_ref.shape[1]
    pair = 2 * LANES

    def project():
        return jnp.dot(x_ref[...], wbf_ref[...], preferred_element_type=F32)

    def emit(p, piece):
        if d == 1:
            o_ref[:, p * pair:(p + 1) * pair] = piece.astype(o_ref.dtype)
        else:
            acc[0][2 * p] = piece[:, :LANES]
            acc[0][2 * p + 1] = piece[:, LANES:]

    def plain():
        res = project()
        for p in range(tn // pair):
            emit(p, res[:, p * pair:(p + 1) * pair])

    def rotary(tab_ref, perm_ref, scale):
        res = project()
        c = jnp.concatenate([tab_ref[:, :LANES]] * 2, axis=1) * scale
        s = jnp.concatenate([tab_ref[:, LANES:]] * 2, axis=1) * scale
        for p in range(tn // pair):
            a = res[:, p * pair:(p + 1) * pair]
            partner = jnp.dot(a.astype(BF16), perm_ref[...], preferred_element_type=F32)
            emit(p, a * c + partner * s)

    is_att = (mode == MODE_ATT_Q) | (mode == MODE_ATT_K)
    is_ret = (mode == MODE_RET_Q) | (mode == MODE_RET_K)
    pl.when(mode == MODE_PLAIN)(plain)
    pl.when(is_att)(lambda: rotary(ta_ref, pa_ref, jnp.where(mode == MODE_ATT_Q, ATT_HEAD_DIM ** -0.5, 1.0)))
    if d == 1:
        pl.when(is_ret)(lambda: rotary(tr_ref, pr_ref, jnp.where(mode == MODE_RET_K, RET_QK_DIM ** -0.5, 1.0)))
        return

    acc_ref = acc[0]
    for h in range(tn // LANES):
        sl = slice(h * LANES, (h + 1) * LANES)
        if d == 4:
            P = ATT_BLOCK * d
            for sb in range(tm // P):
                for r in range(d):
                    o_ref[sb, r, :, sl] = acc_ref[h, pl.ds(sb * P + r, ATT_BLOCK, stride=d), :].astype(o_ref.dtype)
        else:
            for r in range(d):
                o_ref[r, :, sl] = acc_ref[h, pl.ds(r, tm // d, stride=d), :].astype(o_ref.dtype)


def _rotation_matrix(half, width):
    k = jnp.arange(2 * LANES, dtype=jnp.int32)[:, None]
    l = jnp.arange(2 * LANES, dtype=jnp.int32)[None, :]
    same_head = (k // LANES) == (l // LANES)
    kk, ll = k % LANES, l % LANES
    neg = same_head & (ll < half) & (kk == ll + half)
    pos = same_head & (ll >= half) & (ll < width) & (kk == ll - half)
    return (pos.astype(F32) - neg.astype(F32)).astype(BF16)


def _input_projection(xb, w_in, tabs_a, tabs_r, cols, modes, d):
    S, D = xb.shape
    tm, tn = PROJ_ROWS, COL_BLOCK
    n = len(cols)
    P = ATT_BLOCK * d
    if d == 1:
        out_shape = jax.ShapeDtypeStruct((S, n * tn), BF16)
        out_spec = pl.BlockSpec((tm, tn), lambda j, i, c, m: (i, j))
    elif d == 4:
        out_shape = jax.ShapeDtypeStruct((S // P, d, ATT_BLOCK, n * tn), BF16)
        out_spec = pl.BlockSpec((tm // P, d, ATT_BLOCK, tn), lambda j, i, c, m: (i, 0, 0, j))
    else:
        assert d == 16 and P == 2 * tm
        out_shape = jax.ShapeDtypeStruct((S // P, d, ATT_BLOCK, n * tn), BF16)
        out_spec = pl.BlockSpec((None, d, tm // d, tn), lambda j, i, c, m: (i // 2, 0, i % 2, j))

    def tab_rows(j, i, c, m):
        return (jnp.where(m[j] != MODE_PLAIN, i, 0), 0)

    grid_spec = pltpu.PrefetchScalarGridSpec(
        num_scalar_prefetch=2,
        grid=(n, S // tm),
        in_specs=[
            pl.BlockSpec((tm, D), lambda j, i, c, m: (i, 0)),
            pl.BlockSpec((D, tn), lambda j, i, c, m: (0, c[j])),
            pl.BlockSpec((tm, 2 * LANES), tab_rows),
            pl.BlockSpec((tm, 2 * LANES), tab_rows),
            pl.BlockSpec((2 * LANES, 2 * LANES), lambda j, i, c, m: (0, 0)),
            pl.BlockSpec((2 * LANES, 2 * LANES), lambda j, i, c, m: (0, 0)),
        ],
        out_specs=out_spec,
        scratch_shapes=[pltpu.VMEM((D, tn), BF16)]
        + ([] if d == 1 else [pltpu.VMEM((tn // LANES, tm, LANES), F32)]),
    )
    return pl.pallas_call(
        functools.partial(_inproj_kernel, d=d),
        grid_spec=grid_spec,
        out_shape=out_shape,
        compiler_params=_cparams(2),
        name=f"in_proj_d{d}",
    )(jnp.asarray(cols, jnp.int32), jnp.asarray(modes, jnp.int32), xb, w_in, tabs_a, tabs_r,
      _rotation_matrix(ROPE_DIM // 2, ROPE_DIM), _rotation_matrix(RET_QK_DIM // 2, RET_QK_DIM))


def _attn_kernel(q_ref, kc_ref, kp_ref, vc_ref, vp_ref, o_ref, lse_ref):
    n = pl.program_id(1)
    qi = lax.broadcasted_iota(jnp.int32, (ATT_BLOCK, 2 * ATT_BLOCK), 0)
    kj = lax.broadcasted_iota(jnp.int32, (ATT_BLOCK, 2 * ATT_BLOCK), 1)
    dist = qi + ATT_BLOCK - kj
    band = (dist >= 0) & (dist <= ATT_STEPS)
    mask = band & ((kj >= ATT_BLOCK) | (n > 0))
    lane = lax.broadcasted_iota(jnp.int32, (ATT_BLOCK, LANES), 1)
    lse_slab = jnp.zeros((ATT_BLOCK, LANES), F32)
    for h in range(ATT_HEADS_PER_GROUP):
        sl = slice(h * ATT_HEAD_DIM, (h + 1) * ATT_HEAD_DIM)
        k2 = jnp.concatenate([kp_ref[:, sl], kc_ref[:, sl]], axis=0)
        v2 = jnp.concatenate([vp_ref[:, sl], vc_ref[:, sl]], axis=0)
        s = lax.dot_general(q_ref[:, sl], k2, (((1,), (1,)), ((), ())), preferred_element_type=F32)
        s = jnp.where(mask, s, NEG_INF)
        m = jnp.max(s, axis=1, keepdims=True)
        p = jnp.exp(s - m)
        l = jnp.sum(p, axis=1, keepdims=True)
        o = jnp.dot(p.astype(BF16), v2, preferred_element_type=F32)
        o_ref[:, sl] = o / l
        lse_slab = jnp.where(lane == h, m + jnp.log(l), lse_slab)
    lse_ref[...] = lse_slab


def _dilated_attention(qkv, g, S, cq, ck, cv):
    d = DILATIONS[g]
    W = ATT_GROUP_WIDTH
    if d == 1:
        nb = S // ATT_BLOCK
        grid = (1, nb)
        blk = (ATT_BLOCK, W)
        cur = lambda c: (lambda r, n: (n, c))
        prev = lambda c: (lambda r, n: (jnp.maximum(n - 1, 0), c))
        o_spec = pl.BlockSpec(blk, lambda r, n: (n, 0))
        l_spec = pl.BlockSpec((ATT_BLOCK, LANES), lambda r, n: (n, 0))
        o_shape = jax.ShapeDtypeStruct((S, W), F32)
        l_shape = jax.ShapeDtypeStruct((S, LANES), F32)
    else:
        nb = S // (ATT_BLOCK * d)
        grid = (d, nb)
        blk = (None, None, ATT_BLOCK, W)
        cur = lambda c: (lambda r, n: (n, r, 0, c))
        prev = lambda c: (lambda r, n: (jnp.maximum(n - 1, 0), r, 0, c))
        o_spec = pl.BlockSpec(blk, lambda r, n: (n, r, 0, 0))
        l_spec = pl.BlockSpec((None, None, ATT_BLOCK, LANES), lambda r, n: (n, r, 0, 0))
        o_shape = jax.ShapeDtypeStruct((nb, d, ATT_BLOCK, W), F32)
        l_shape = jax.ShapeDtypeStruct((nb, d, ATT_BLOCK, LANES), F32)
    return pl.pallas_call(
        _attn_kernel,
        grid=grid,
        in_specs=[
            pl.BlockSpec(blk, cur(cq)),
            pl.BlockSpec(blk, cur(ck)),
            pl.BlockSpec(blk, prev(ck)),
            pl.BlockSpec(blk, cur(cv)),
            pl.BlockSpec(blk, prev(cv)),
        ],
        out_specs=[o_spec, l_spec],
        out_shape=[o_shape, l_shape],
        compiler_params=_cparams(2),
        name=f"dilated_attn_g{g}",
    )(qkv, qkv, qkv, qkv, qkv)


def _retention_kernel(q_ref, k_ref, vlo_ref, vhi_ref, glo_ref, ghi_ref, o_ref, state_ref):
    @pl.when(pl.program_id(0) == 0)
    def _():
        state_ref[...] = jnp.zeros_like(state_ref)

    ri = lax.broadcasted_iota(jnp.int32, (RET_CHUNK, RET_CHUNK), 0)
    ci = lax.broadcasted_iota(jnp.int32, (RET_CHUNK, RET_CHUNK), 1)
    diff = (ri - ci).astype(F32)
    nrow = lax.broadcasted_iota(jnp.int32, (RET_CHUNK, 1), 0).astype(F32)
    half = RET_HEADS // 2
    for h in range(RET_HEADS):
        ld = RET_LOG_DECAY[h]
        sl = slice(h * RET_QK_DIM, (h + 1) * RET_QK_DIM)
        vsl = slice((h % half) * RET_V_DIM, (h % half + 1) * RET_V_DIM)
        v_ref, g_ref = (vlo_ref, glo_ref) if h < half else (vhi_ref, ghi_ref)
        q = q_ref[:, sl]
        k = k_ref[:, sl]
        v = v_ref[:, vsl]
        intra = jnp.where(diff >= 0, jnp.exp(jnp.maximum(diff, 0.0) * ld), 0.0)
        scores = lax.dot_general(q, k, (((1,), (1,)), ((), ())), preferred_element_type=F32) * intra
        inner = jnp.dot(scores.astype(BF16), v, preferred_element_type=F32)
        state = state_ref[h]
        q_dec = jnp.exp((nrow + 1.0) * ld)
        cross = jnp.dot((q.astype(F32) * q_dec).astype(BF16), state.astype(BF16),
                        preferred_element_type=F32)
        k_dec = jnp.exp((RET_CHUNK - 1.0 - nrow) * ld)
        kv = lax.dot_general((k.astype(F32) * k_dec).astype(BF16), v, (((0,), (0,)), ((), ())),
                             preferred_element_type=F32)
        state_ref[h] = math.exp(RET_CHUNK * ld) * state + kv
        o = inner + cross
        mu = jnp.mean(o, axis=1, keepdims=True)
        var = jnp.mean(jnp.square(o - mu), axis=1, keepdims=True)
        o = (o - mu) * lax.rsqrt(var + LN_EPS)
        gate = g_ref[:, vsl].astype(F32)
        gate = gate * (1.0 / (1.0 + jnp.exp(-gate)))
        o_ref[:, h * RET_V_DIM:(h + 1) * RET_V_DIM] = (gate * o).astype(o_ref.dtype)


def _retention(nat, S):
    nc = S // RET_CHUNK
    blk = (RET_CHUNK, COL_BLOCK)

    def col(cb):
        return pl.BlockSpec(blk, lambda c: (c, cb))

    return pl.pallas_call(
        _retention_kernel,
        grid=(nc,),
        in_specs=[col(NC_QR), col(NC_KR), col(NC_VR), col(NC_VR + 1), col(NC_GR), col(NC_GR + 1)],
        out_specs=pl.BlockSpec((RET_CHUNK, RET_HEADS * RET_V_DIM), lambda c: (c, 0)),
        out_shape=jax.ShapeDtypeStruct((S, RET_HEADS * RET_V_DIM), BF16),
        scratch_shapes=[pltpu.VMEM((RET_HEADS, RET_QK_DIM, RET_V_DIM), F32)],
        compiler_params=_cparams(1),
        name="retention",
    )(nat, nat, nat, nat, nat, nat)


MERGE_ROWS = 256
MERGE_CHAIN_ROWS = 128


def _layer_norm(z, scale, bias):
    mu = jnp.mean(z, axis=1, keepdims=True)
    var = jnp.mean(jnp.square(z - mu), axis=1, keepdims=True)
    return (z - mu) * lax.rsqrt(var + LN_EPS) * scale + bias


def _sigmoid(x):
    return 1.0 / (1.0 + jnp.exp(-x))


def _split_bf16(v):
    hi = v.astype(BF16)
    return hi, (v - hi.astype(F32)).astype(BF16)


def _merge_kernel(o0_ref, o1_ref, o2_ref, l0_ref, l1_ref, l2_ref, or_ref,
                  ga0_ref, ga1_ref, gb0_ref, gb1_ref, x_ref,
                  wa_ref, wr_ref, wo_ref, lns_ref, lnb_ref, wrt_ref, brt_ref, hexp_ref,
                  x1_ref, route_ref, cnt_ref, s1_ref, s2_ref, ls1_ref, ls2_ref):
    tm = o0_ref.shape[0]
    for d, src, lsrc, dst, ldst in ((DILATIONS[1], o1_ref, l1_ref, s1_ref, ls1_ref),
                                    (DILATIONS[2], o2_ref, l2_ref, s2_ref, ls2_ref)):
        for r in range(d):
            for h in range(ATT_HEADS_PER_GROUP):
                dst[h, pl.ds(r, tm // d, stride=d), :] = src[r, :, h * ATT_HEAD_DIM:(h + 1) * ATT_HEAD_DIM]
            ldst[pl.ds(r, tm // d, stride=d), :] = lsrc[r]

    @pl.when(pl.program_id(0) == 0)
    def _():
        cnt_ref[...] = jnp.zeros_like(cnt_ref)

    for c in range(tm // MERGE_CHAIN_ROWS):
        _merge_rows(slice(c * MERGE_CHAIN_ROWS, (c + 1) * MERGE_CHAIN_ROWS),
                    o0_ref, l0_ref, or_ref, ga0_ref, ga1_ref, gb0_ref, gb1_ref, x_ref,
                    wa_ref, wr_ref, wo_ref, lns_ref, lnb_ref, wrt_ref, brt_ref, hexp_ref,
                    x1_ref, route_ref, cnt_ref, s1_ref, s2_ref, ls1_ref, ls2_ref)


def _merge_rows(rs, o0_ref, l0_ref, or_ref, ga0_ref, ga1_ref, gb0_ref, gb1_ref, x_ref,
                wa_ref, wr_ref, wo_ref, lns_ref, lnb_ref, wrt_ref, brt_ref, hexp_ref,
                x1_ref, route_ref, cnt_ref, s1_ref, s2_ref, ls1_ref, ls2_ref):
    rc = rs.stop - rs.start
    l0, l1, l2 = l0_ref[rs, :], ls1_ref[rs, :], ls2_ref[rs, :]
    lm = jnp.maximum(jnp.maximum(l0, l1), l2)
    e0, e1, e2 = jnp.exp(l0 - lm), jnp.exp(l1 - lm), jnp.exp(l2 - lm)
    den = e0 + e1 + e2

    def lanes_of_head(w):
        hi, lo = _split_bf16(w)
        return (jnp.dot(hi, hexp_ref[...], preferred_element_type=F32)
                + jnp.dot(lo, hexp_ref[...], preferred_element_type=F32))

    heads = range(ATT_HEADS_PER_GROUP)
    o1 = jnp.concatenate([s1_ref[h, rs, :] for h in heads], axis=1)
    o2 = jnp.concatenate([s2_ref[h, rs, :] for h in heads], axis=1)
    o_a = (lanes_of_head(e0 / den) * o0_ref[rs, :] + lanes_of_head(e1 / den) * o1
           + lanes_of_head(e2 / den) * o2).astype(BF16)
    y_a = jnp.dot(o_a, wa_ref[...], preferred_element_type=F32)
    y_r = jnp.dot(or_ref[rs, :], wr_ref[...], preferred_element_type=F32)
    gate_a = jnp.concatenate([ga0_ref[rs, :], ga1_ref[rs, :]], axis=1).astype(F32)
    gate_b = jnp.concatenate([gb0_ref[rs, :], gb1_ref[rs, :]], axis=1).astype(F32)
    merged = _sigmoid(gate_a) * y_a + _sigmoid(gate_b) * y_r
    y = jnp.dot(merged.astype(BF16), wo_ref[...], preferred_element_type=F32)
    x1 = _layer_norm(ALPHA * x_ref[rs, :] + y, lns_ref[...], lnb_ref[...])
    x1_ref[rs, :] = x1

    x_hi, x_lo = _split_bf16(x1)
    hl = jnp.dot(x_hi, wrt_ref[...], preferred_element_type=F32)
    logits = (hl[:, :LANES] + hl[:, LANES:]
              + jnp.dot(x_lo, wrt_ref[:, :LANES], preferred_element_type=F32) + brt_ref[...])
    lane = lax.broadcasted_iota(jnp.int32, logits.shape, 1)
    is_g = lane < N_EXPERT_GROUPS
    gl = jnp.where(is_g, logits, NEG_INF)
    gmax = jnp.max(gl, axis=1, keepdims=True)
    g_sel = jnp.min(jnp.where(is_g & (gl == gmax), lane, LANES), axis=1, keepdims=True)
    p_group = 1.0 / jnp.sum(jnp.where(is_g, jnp.exp(gl - gmax), 0.0), axis=1, keepdims=True)
    lo = N_EXPERT_GROUPS + EXPERTS_PER_GROUP * g_sel
    in_grp = (lane >= lo) & (lane < lo + EXPERTS_PER_GROUP)
    el = jnp.where(in_grp, logits, NEG_INF)
    top1 = jnp.max(el, axis=1, keepdims=True)
    idx1 = jnp.min(jnp.where(in_grp & (el == top1), lane, LANES), axis=1, keepdims=True)
    el2 = jnp.where(lane == idx1, NEG_INF, el)
    top2 = jnp.max(el2, axis=1, keepdims=True)
    idx2 = jnp.min(jnp.where(in_grp & (lane != idx1) & (el2 == top2), lane, LANES), axis=1, keepdims=True)
    t = jnp.exp(top2 - top1)
    wt1 = p_group * (1.0 / (1.0 + t))
    wt2 = p_group * (t / (1.0 + t))
    slab = jnp.where(lane == 0, (idx1 - N_EXPERT_GROUPS).astype(F32), 0.0)
    slab = jnp.where(lane == 1, (idx2 - N_EXPERT_GROUPS).astype(F32), slab)
    slab = jnp.where(lane == 2, wt1, slab)
    slab = jnp.where(lane == 3, wt2, slab)

    hot = jnp.where((lane == idx1) | (lane == idx2), 1.0, 0.0)
    ri = lax.broadcasted_iota(jnp.int32, (rc, rc), 0)
    ci = lax.broadcasted_iota(jnp.int32, (rc, rc), 1)
    ltri = jnp.where(ri > ci, 1.0, 0.0).astype(BF16)
    prefix = jnp.dot(ltri, hot.astype(BF16), preferred_element_type=F32) + cnt_ref[...]
    rank1 = jnp.sum(jnp.where(lane == idx1, prefix, 0.0), axis=1, keepdims=True)
    rank2 = jnp.sum(jnp.where(lane == idx2, prefix, 0.0), axis=1, keepdims=True)
    cnt_ref[...] = cnt_ref[...] + jnp.sum(hot, axis=0, keepdims=True)
    slab = jnp.where(lane == 4, rank1, slab)
    slab = jnp.where(lane == 5, rank2, slab)
    route_ref[rs, :] = slab


def _merge(o_g, lse_g, o_r, nat, x2d, wa, wr, wo, lns, lnb, wrt, brt, S):
    tm = MERGE_ROWS
    D = D_MODEL
    W = ATT_GROUP_WIDTH
    head_expand = (jnp.arange(W, dtype=jnp.int32)[None, :] // ATT_HEAD_DIM
                   == jnp.arange(LANES, dtype=jnp.int32)[:, None]).astype(BF16)

    def rows(width):
        return pl.BlockSpec((tm, width), lambda i: (i, 0))

    def streams(g, width):
        d = DILATIONS[g]
        per = (ATT_BLOCK * d) // tm
        return pl.BlockSpec((None, d, tm // d, width), lambda i: (i // per, 0, i % per, 0))

    def ncol(cb):
        return pl.BlockSpec((tm, COL_BLOCK), lambda i: (i, cb))

    def const(shape):
        return pl.BlockSpec(shape, lambda i: (0, 0), pipeline_mode=pl.Buffered(1))

    return pl.pallas_call(
        _merge_kernel,
        grid=(S // tm,),
        in_specs=[
            rows(W), streams(1, W), streams(2, W),
            rows(LANES), streams(1, LANES), streams(2, LANES),
            rows(RET_HEADS * RET_V_DIM),
            ncol(NC_GATE_A), ncol(NC_GATE_A + 1), ncol(NC_GATE_B), ncol(NC_GATE_B + 1),
            rows(D),
            const((W, D)), const((RET_HEADS * RET_V_DIM, D)), const((D, D)),
            const((1, D)), const((1, D)), const((D, 2 * LANES)), const((1, LANES)), const((LANES, W)),
        ],
        out_specs=[rows(D), rows(LANES), pl.BlockSpec((1, LANES), lambda i: (0, 0))],
        out_shape=[jax.ShapeDtypeStruct((S, D), F32), jax.ShapeDtypeStruct((S, LANES), F32),
                   jax.ShapeDtypeStruct((1, LANES), F32)],
        scratch_shapes=[pltpu.VMEM((ATT_HEADS_PER_GROUP, tm, ATT_HEAD_DIM), F32),
                        pltpu.VMEM((ATT_HEADS_PER_GROUP, tm, ATT_HEAD_DIM), F32),
                        pltpu.VMEM((tm, LANES), F32), pltpu.VMEM((tm, LANES), F32)],
        compiler_params=_cparams(1),
        name="merge_ln_router",
    )(*o_g, *lse_g, o_r, nat, nat, nat, nat, x2d, wa, wr, wo, lns, lnb, wrt, brt, head_expand)


META_N_USED, META_NEXT = 0, 1


def _plan_kernel(e_ref, rank_ref, cnt_ref, dest_ref, rt_ref, be_ref, meta_ref, pstart_ref):
    n_blocks = be_ref.shape[0]

    def pad_row(r, carry):
        rt_ref[r] = 0
        return carry

    def per_expert(e, start):
        nb = lax.shift_right_logical(cnt_ref[e] + (MOE_BLOCK - 1), 7)
        pstart_ref[e] = start

        def fill(b, carry):
            be_ref[lax.shift_right_logical(start, 7) + b] = e
            return carry
        lax.fori_loop(0, nb, fill, 0)
        lax.fori_loop(start + cnt_ref[e], start + nb * MOE_BLOCK, pad_row, 0)
        return start + nb * MOE_BLOCK
    total = lax.fori_loop(0, N_EXPERTS, per_expert, 0)
    n_used = lax.shift_right_logical(total, 7)
    meta_ref[META_N_USED] = n_used

    def tail(b, carry):
        be_ref[b] = be_ref[jnp.maximum(n_used - 1, 0)]
        return carry
    lax.fori_loop(n_used, n_blocks, tail, 0)
    lax.fori_loop(total, rt_ref.shape[0], pad_row, 0)

    def nxt(k, cur):
        e = N_EXPERTS - 1 - k
        meta_ref[META_NEXT + e] = cur
        return jnp.where(cnt_ref[e] > 0, e, cur)
    lax.fori_loop(0, N_EXPERTS, nxt, N_EXPERTS)

    def assign(a, carry):
        row = pstart_ref[e_ref[a]] + rank_ref[a]
        dest_ref[a] = row
        rt_ref[row] = lax.shift_right_logical(a, 1)
        return carry
    lax.fori_loop(0, dest_ref.shape[0], assign, 0, unroll=8)


def _plan(e_flat, rank_flat, counts):
    A = e_flat.shape[0]
    n_blocks = A // MOE_BLOCK + N_EXPERTS
    smem = pl.BlockSpec(memory_space=pltpu.SMEM)
    return pl.pallas_call(
        _plan_kernel,
        in_specs=[smem, smem, smem],
        out_specs=[smem, smem, smem, smem],
        out_shape=[
            jax.ShapeDtypeStruct((A,), jnp.int32),
            jax.ShapeDtypeStruct((n_blocks * MOE_BLOCK,), jnp.int32),
            jax.ShapeDtypeStruct((n_blocks,), jnp.int32),
            jax.ShapeDtypeStruct((META_NEXT + N_EXPERTS,), jnp.int32),
        ],
        scratch_shapes=[pltpu.SMEM((N_EXPERTS,), jnp.int32)],
        name="moe_plan",
    )(e_flat, rank_flat, counts)


def _expert_kernel(be_ref, meta_ref, rt_ref, x_hbm, wg_hbm, wu_hbm, wd_hbm, y_ref,
                   xg_ref, wgf_ref, wuf_ref, wdf_ref, wgb_ref, wub_ref, wdb_ref, xsem, sem, slot_ref):
    b = pl.program_id(0)
    n_used = meta_ref[META_N_USED]
    e = be_ref[b]
    first = (b == 0) | (e != be_ref[jnp.maximum(b - 1, 0)])

    def gather(blk, slot):
        def body(i, carry):
            tok = rt_ref[blk * MOE_BLOCK + i]
            pltpu.make_async_copy(x_hbm.at[pl.ds(tok, 1)], xg_ref.at[slot, pl.ds(i, 1)],
                                  xsem.at[slot]).start()
            return carry
        lax.fori_loop(0, MOE_BLOCK, body, 0, unroll=8)

    @pl.when((b == 0) & (n_used > 0))
    def _():
        gather(0, 0)

    @pl.when(b + 1 < n_used)
    def _():
        gather(b + 1, (b + 1) % 2)

    def copies(ex, s):
        return (pltpu.make_async_copy(wg_hbm.at[ex], wgf_ref.at[s], sem.at[s]),
                pltpu.make_async_copy(wu_hbm.at[ex], wuf_ref.at[s], sem.at[s]),
                pltpu.make_async_copy(wd_hbm.at[ex], wdf_ref.at[s], sem.at[s]))

    @pl.when((b == 0) & (n_used > 0))
    def _():
        slot_ref[0] = 0
        for c in copies(e, 0):
            c.start()

    @pl.when(first & (b < n_used))
    def _():
        s = slot_ref[0]
        nx = meta_ref[META_NEXT + e]

        @pl.when(nx < N_EXPERTS)
        def _():
            for c in copies(nx, 1 - s):
                c.start()
        for c in copies(e, s):
            c.wait()
        wgb_ref[...] = wgf_ref[s].astype(BF16)
        wub_ref[...] = wuf_ref[s].astype(BF16)
        wdb_ref[...] = wdf_ref[s].astype(BF16)
        slot_ref[0] = 1 - s

    @pl.when(b < n_used)
    def _():
        slot = b % 2
        pltpu.make_async_copy(x_hbm.at[pl.ds(0, MOE_BLOCK)], xg_ref.at[slot], xsem.at[slot]).wait()
        xb = xg_ref[slot].astype(BF16)
        hg = jnp.dot(xb, wgb_ref[...], preferred_element_type=F32)
        hu = jnp.dot(xb, wub_ref[...], preferred_element_type=F32)
        hid = (hg * _sigmoid(hg) * hu).astype(BF16)
        y_ref[...] = jnp.dot(hid, wdb_ref[...], preferred_element_type=F32)

    @pl.when(b >= n_used)
    def _():
        y_ref[...] = jnp.zeros_like(y_ref)


def _experts(x1, block_expert, meta, row_tok, w_gate, w_up, w_down):
    T, D = x1.shape
    n_blocks = block_expert.shape[0]
    R = n_blocks * MOE_BLOCK
    H = EXPERT_HIDDEN
    grid_spec = pltpu.PrefetchScalarGridSpec(
        num_scalar_prefetch=3,
        grid=(n_blocks,),
        in_specs=[
            pl.BlockSpec(memory_space=pl.ANY),
            pl.BlockSpec(memory_space=pl.ANY),
            pl.BlockSpec(memory_space=pl.ANY),
            pl.BlockSpec(memory_space=pl.ANY),
        ],
        out_specs=pl.BlockSpec((MOE_BLOCK, D), lambda b, be, meta, rt: (b, 0)),
        scratch_shapes=[
            pltpu.VMEM((2, MOE_BLOCK, D), F32),
            pltpu.VMEM((2, D, H), F32),
            pltpu.VMEM((2, D, H), F32),
            pltpu.VMEM((2, H, D), F32),
            pltpu.VMEM((D, H), BF16),
            pltpu.VMEM((D, H), BF16),
            pltpu.VMEM((H, D), BF16),
            pltpu.SemaphoreType.DMA((2,)),
            pltpu.SemaphoreType.DMA((2,)),
            pltpu.SMEM((1,), jnp.int32),
        ],
    )
    return pl.pallas_call(
        _expert_kernel,
        grid_spec=grid_spec,
        out_shape=jax.ShapeDtypeStruct((R, D), F32),
        compiler_params=_cparams(1),
        name="moe_experts",
    )(block_expert, meta, row_tok, x1, w_gate, w_up, w_down)


COMBINE_TOKENS = 64


def _combine_kernel(dest_ref, y_hbm, x1_ref, route_ref, lns_ref, lnb_ref, o_ref, yg_ref, sem):
    i = pl.program_id(0)
    n = pl.num_programs(0)
    tt = COMBINE_TOKENS

    def issue(step, slot):
        def body(r, carry):
            for k in range(TOP_K):
                row = dest_ref[(step * tt + r) * TOP_K + k]
                pltpu.make_async_copy(y_hbm.at[pl.ds(row, 1)], yg_ref.at[slot, k, pl.ds(r, 1)],
                                      sem.at[slot]).start()
            return carry
        lax.fori_loop(0, tt, body, 0, unroll=4)

    @pl.when(i == 0)
    def _():
        issue(0, 0)

    @pl.when(i + 1 < n)
    def _():
        issue(i + 1, (i + 1) % 2)

    slot = i % 2
    for k in range(TOP_K):
        pltpu.make_async_copy(y_hbm.at[pl.ds(0, tt)], yg_ref.at[slot, k], sem.at[slot]).wait()
    route = route_ref[...]
    w1 = route[:, 2:3]
    w2 = route[:, 3:4]
    moe = w1 * yg_ref[slot, 0] + w2 * yg_ref[slot, 1]
    o_ref[...] = _layer_norm(ALPHA * x1_ref[...] + moe, lns_ref[...], lnb_ref[...])


def _combine(dest, y, x1, route, lns, lnb):
    T, D = x1.shape
    tt = COMBINE_TOKENS
    grid_spec = pltpu.PrefetchScalarGridSpec(
        num_scalar_prefetch=1,
        grid=(T // tt,),
        in_specs=[
            pl.BlockSpec(memory_space=pl.ANY),
            pl.BlockSpec((tt, D), lambda i, d: (i, 0)),
            pl.BlockSpec((tt, LANES), lambda i, d: (i, 0)),
            pl.BlockSpec((1, D), lambda i, d: (0, 0)),
            pl.BlockSpec((1, D), lambda i, d: (0, 0)),
        ],
        out_specs=pl.BlockSpec((tt, D), lambda i, d: (i, 0)),
        scratch_shapes=[
            pltpu.VMEM((2, TOP_K, tt, D), F32),
            pltpu.SemaphoreType.DMA((2,)),
        ],
    )
    return pl.pallas_call(
        _combine_kernel,
        grid_spec=grid_spec,
        out_shape=jax.ShapeDtypeStruct((T, D), F32),
        compiler_params=_cparams(1),
        name="moe_combine_ln",
    )(dest, y, x1, route, lns, lnb)


def _rotary_tables(positions):
    pos = positions.reshape(-1).astype(F32)
    S = pos.shape[0]
    inv_a = ROPE_THETA ** (-jnp.arange(0, ROPE_DIM, 2, dtype=F32) / ROPE_DIM)
    ang = pos[:, None] * inv_a
    cos, sin = jnp.cos(ang), jnp.sin(ang)
    pad1 = jnp.ones((S, LANES - ROPE_DIM), F32)
    pad0 = jnp.zeros((S, LANES - ROPE_DIM), F32)
    inv_r = RET_ROT_BASE ** (-jnp.linspace(0.0, 1.0, RET_QK_DIM // 2, dtype=F32))
    ang_r = pos[:, None] * inv_r
    cr, sr = jnp.cos(ang_r), jnp.sin(ang_r)
    tabs_a = jnp.concatenate([cos, cos, pad1, sin, sin, pad0], axis=1)
    tabs_r = jnp.concatenate([cr, cr, sr, sr], axis=1)
    return tabs_a, tabs_r


def kernel(x, positions, w_in, w_branch_attn, w_branch_ret, w_out, ln1_scale, ln1_bias, w_group_router, b_group_router, w_expert_router, b_expert_router, w_expert_gate, w_expert_up, w_expert_down, ln2_scale, ln2_bias):
    B, S, D = x.shape
    assert B == 1 and D == D_MODEL and w_in.shape[0] == DEPTH == 1
    assert S % (ATT_BLOCK * DILATIONS[2]) == 0
    x2d = x.reshape(S, D)
    xb = x2d.astype(BF16)
    tabs_a, tabs_r = _rotary_tables(positions)

    nat = _input_projection(xb, w_in[0], tabs_a, tabs_r, NAT_COLS, NAT_MODES, 1)
    qkv_modes = (MODE_ATT_Q, MODE_ATT_K, MODE_PLAIN)
    qkv = [None] + [
        _input_projection(xb, w_in[0], tabs_a, tabs_r, (CB_QA + g, CB_KA + g, CB_VA + g), qkv_modes,
                          DILATIONS[g])
        for g in (1, 2)]
    att = [_dilated_attention(nat, 0, S, NC_Q0, NC_K0, NC_V0)]
    att += [_dilated_attention(qkv[g], g, S, 0, 1, 2) for g in (1, 2)]
    o_r = _retention(nat, S)

    w_route = jnp.concatenate(
        [w_group_router[0], w_expert_router[0].transpose(1, 0, 2).reshape(D, N_EXPERTS),
         jnp.zeros((D, LANES - N_EXPERT_GROUPS - N_EXPERTS), F32)], axis=1)
    b_route = jnp.concatenate(
        [b_group_router[0], b_expert_router[0].reshape(N_EXPERTS),
         jnp.zeros((LANES - N_EXPERT_GROUPS - N_EXPERTS,), F32)]).reshape(1, LANES)
    x1, route, cnt = _merge(
        [a[0] for a in att], [a[1] for a in att], o_r, nat, x2d,
        w_branch_attn[0].astype(BF16), w_branch_ret[0].astype(BF16), w_out[0].astype(BF16),
        ln1_scale[0].reshape(1, D), ln1_bias[0].reshape(1, D),
        jnp.concatenate(_split_bf16(w_route), axis=1), b_route, S)

    e_flat = route[:, 0:TOP_K].astype(jnp.int32).reshape(S * TOP_K)
    rank_flat = route[:, 4:4 + TOP_K].astype(jnp.int32).reshape(S * TOP_K)
    counts = cnt[0, N_EXPERT_GROUPS:N_EXPERT_GROUPS + N_EXPERTS].astype(jnp.int32)
    dest, row_tok, block_expert, meta = _plan(e_flat, rank_flat, counts)
    y = _experts(x1, block_expert, meta, row_tok, w_expert_gate[0], w_expert_up[0], w_expert_down[0])
    out = _combine(dest, y, x1, route, ln2_scale[0].reshape(1, D), ln2_bias[0].reshape(1, D))
    return out.reshape(B, S, D)
```

```python
import functools
import math

import jax
import jax.numpy as jnp
from jax import lax
from jax.experimental import pallas as pl
from jax.experimental.pallas import tpu as pltpu

F32 = jnp.float32
BF16 = jnp.bfloat16

D_MODEL = 2048
ATT_HEAD_DIM = 128
ATT_HEADS_PER_GROUP = 8
DILATIONS = (1, 4, 16)
ATT_STEPS = 128
N_ATT_GROUPS = 3
ATT_GROUP_WIDTH = ATT_HEADS_PER_GROUP * ATT_HEAD_DIM
ROPE_DIM = ATT_HEAD_DIM // 4
ROPE_THETA = 500000.0
ATT_BLOCK = 128

RET_HEADS = 8
RET_QK_DIM = 128
RET_V_DIM = 256
RET_CHUNK = 128
RET_ROT_BASE = 10000.0
RET_LOG_DECAY = tuple(math.log1p(-(2.0 ** (-5.0 - h))) for h in range(RET_HEADS))

N_EXPERT_GROUPS = 4
EXPERTS_PER_GROUP = 8
N_EXPERTS = N_EXPERT_GROUPS * EXPERTS_PER_GROUP
TOP_K = 2
EXPERT_HIDDEN = 512
MOE_BLOCK = 128

DEPTH = 1
ALPHA = (2.0 * DEPTH) ** 0.25
LN_EPS = 1e-5
NEG_INF = -1e30

COL_BLOCK = 1024
CB_QA, CB_KA, CB_VA = 0, 3, 6
CB_QR, CB_KR, CB_VR, CB_GR = 9, 10, 11, 13
CB_GATE_A, CB_GATE_B = 15, 17

MODE_PLAIN, MODE_ATT_Q, MODE_ATT_K, MODE_RET_Q, MODE_RET_K = 0, 1, 2, 3, 4

NAT_COLS = (CB_QA, CB_KA, CB_VA, CB_QR, CB_KR, CB_VR, CB_VR + 1, CB_GR, CB_GR + 1,
            CB_GATE_A, CB_GATE_A + 1, CB_GATE_B, CB_GATE_B + 1)
NAT_MODES = (MODE_ATT_Q, MODE_ATT_K, MODE_PLAIN, MODE_RET_Q, MODE_RET_K) + (MODE_PLAIN,) * 8
NC_Q0, NC_K0, NC_V0, NC_QR, NC_KR, NC_VR, NC_GR, NC_GATE_A, NC_GATE_B = 0, 1, 2, 3, 4, 5, 7, 9, 11

LANES = 128
VMEM_LIMIT = 52 * 1024 * 1024
PROJ_ROWS = 1024


def _cparams(n_grid_dims):
    return pltpu.CompilerParams(
        dimension_semantics=("arbitrary",) * n_grid_dims,
        vmem_limit_bytes=VMEM_LIMIT,
    )


def _inproj_kernel(cols_ref, modes_ref, x_ref, w_ref, ta_ref, tr_ref, pa_ref, pr_ref, o_ref,
                   wbf_ref, *acc, d):
    del cols_ref
    j = pl.program_id(0)

    @pl.when(pl.program_id(1) == 0)
    def _():
        wbf_ref[...] = w_ref[...].astype(BF16)

    mode = modes_ref[j]
    tm, tn = x_ref.shape[0], wbf_ref.shape[1]
    pair = 2 * LANES

    def project():
        return jnp.dot(x_ref[...], wbf_ref[...], preferred_element_type=F32)

    def emit(p, piece):
        if d == 1:
            o_ref[:, p * pair:(p + 1) * pair] = piece.astype(o_ref.dtype)
        else:
            acc[0][2 * p] = piece[:, :LANES]
            acc[0][2 * p + 1] = piece[:, LANES:]

    def plain():
        res = project()
        for p in range(tn // pair):
            emit(p, res[:, p * pair:(p + 1) * pair])

    def rotary(tab_ref, perm_ref, scale):
        res = project()
        c = jnp.concatenate([tab_ref[:, :LANES]] * 2, axis=1) * scale
        s = jnp.concatenate([tab_ref[:, LANES:]] * 2, axis=1) * scale
        for p in range(tn // pair):
            a = res[:, p * pair:(p + 1) * pair]
            partner = jnp.dot(a.astype(BF16), perm_ref[...], preferred_element_type=F32)
            emit(p, a * c + partner * s)

    is_att = (mode == MODE_ATT_Q) | (mode == MODE_ATT_K)
    is_ret = (mode == MODE_RET_Q) | (mode == MODE_RET_K)
    pl.when(mode == MODE_PLAIN)(plain)
    pl.when(is_att)(lambda: rotary(ta_ref, pa_ref, jnp.where(mode == MODE_ATT_Q, ATT_HEAD_DIM ** -0.5, 1.0)))
    if d == 1:
        pl.when(is_ret)(lambda: rotary(tr_ref, pr_ref, jnp.where(mode == MODE_RET_K, RET_QK_DIM ** -0.5, 1.0)))
        return

    acc_ref = acc[0]
    for h in range(tn // LANES):
        sl = slice(h * LANES, (h + 1) * LANES)
        if d == 4:
            P = ATT_BLOCK * d
            for sb in range(tm // P):
                for r in range(d):
                    o_ref[sb, r, :, sl] = acc_ref[h, pl.ds(sb * P + r, ATT_BLOCK, stride=d), :].astype(o_ref.dtype)
        else:
            for r in range(d):
                o_ref[r, :, sl] = acc_ref[h, pl.ds(r, tm // d, stride=d), :].astype(o_ref.dtype)


def _rotation_matrix(half, width):
    k = jnp.arange(2 * LANES, dtype=jnp.int32)[:, None]
    l = jnp.arange(2 * LANES, dtype=jnp.int32)[None, :]
    same_head = (k // LANES) == (l // LANES)
    kk, ll = k % LANES, l % LANES
    neg = same_head & (ll < half) & (kk == ll + half)
    pos = same_head & (ll >= half) & (ll < width) & (kk == ll - half)
    return (pos.astype(F32) - neg.astype(F32)).astype(BF16)


def _input_projection(xb, w_in, tabs_a, tabs_r, cols, modes, d):
    S, D = xb.shape
    tm, tn = PROJ_ROWS, COL_BLOCK
    n = len(cols)
    P = ATT_BLOCK * d
    if d == 1:
        out_shape = jax.ShapeDtypeStruct((S, n * tn), BF16)
        out_spec = pl.BlockSpec((tm, tn), lambda j, i, c, m: (i, j))
    elif d == 4:
        out_shape = jax.ShapeDtypeStruct((S // P, d, ATT_BLOCK, n * tn), BF16)
        out_spec = pl.BlockSpec((tm // P, d, ATT_BLOCK, tn), lambda j, i, c, m: (i, 0, 0, j))
    else:
        assert d == 16 and P == 2 * tm
        out_shape = jax.ShapeDtypeStruct((S // P, d, ATT_BLOCK, n * tn), BF16)
        out_spec = pl.BlockSpec((None, d, tm // d, tn), lambda j, i, c, m: (i // 2, 0, i % 2, j))

    def tab_rows(j, i, c, m):
        return (jnp.where(m[j] != MODE_PLAIN, i, 0), 0)

    grid_spec = pltpu.PrefetchScalarGridSpec(
        num_scalar_prefetch=2,
        grid=(n, S // tm),
        in_specs=[
            pl.BlockSpec((tm, D), lambda j, i, c, m: (i, 0)),
            pl.BlockSpec((D, tn), lambda j, i, c, m: (0, c[j])),
            pl.BlockSpec((tm, 2 * LANES), tab_rows),
            pl.BlockSpec((tm, 2 * LANES), tab_rows),
            pl.BlockSpec((2 * LANES, 2 * LANES), lambda j, i, c, m: (0, 0)),
            pl.BlockSpec((2 * LANES, 2 * LANES), lambda j, i, c, m: (0, 0)),
        ],
        out_specs=out_spec,
        scratch_shapes=[pltpu.VMEM((D, tn), BF16)]
        + ([] if d == 1 else [pltpu.VMEM((tn // LANES, tm, LANES), F32)]),
    )
    return pl.pallas_call(
        functools.partial(_inproj_kernel, d=d),
        grid_spec=grid_spec,
        out_shape=out_shape,
        compiler_params=_cparams(2),
        name=f"in_proj_d{d}",
    )(jnp.asarray(cols, jnp.int32), jnp.asarray(modes, jnp.int32), xb, w_in, tabs_a, tabs_r,
      _rotation_matrix(ROPE_DIM // 2, ROPE_DIM), _rotation_matrix(RET_QK_DIM // 2, RET_QK_DIM))


def _attn_kernel(q_ref, kc_ref, kp_ref, vc_ref, vp_ref, o_ref, lse_ref):
    n = pl.program_id(1)
    qi = lax.broadcasted_iota(jnp.int32, (ATT_BLOCK, 2 * ATT_BLOCK), 0)
    kj = lax.broadcasted_iota(jnp.int32, (ATT_BLOCK, 2 * ATT_BLOCK), 1)
    dist = qi + ATT_BLOCK - kj
    band = (dist >= 0) & (dist <= ATT_STEPS)
    mask = band & ((kj >= ATT_BLOCK) | (n > 0))
    lane = lax.broadcasted_iota(jnp.int32, (ATT_BLOCK, LANES), 1)
    lse_slab = jnp.zeros((ATT_BLOCK, LANES), F32)
    for h in range(ATT_HEADS_PER_GROUP):
        sl = slice(h * ATT_HEAD_DIM, (h + 1) * ATT_HEAD_DIM)
        k2 = jnp.concatenate([kp_ref[:, sl], kc_ref[:, sl]], axis=0)
        v2 = jnp.concatenate([vp_ref[:, sl], vc_ref[:, sl]], axis=0)
        s = lax.dot_general(q_ref[:, sl], k2, (((1,), (1,)), ((), ())), preferred_element_type=F32)
        s = jnp.where(mask, s, NEG_INF)
        m = jnp.max(s, axis=1, keepdims=True)
        p = jnp.exp(s - m)
        l = jnp.sum(p, axis=1, keepdims=True)
        o = jnp.dot(p.astype(BF16), v2, preferred_element_type=F32)
        o_ref[:, sl] = o / l
        lse_slab = jnp.where(lane == h, m + jnp.log(l), lse_slab)
    lse_ref[...] = lse_slab


def _dilated_attention(qkv, g, S, cq, ck, cv):
    d = DILATIONS[g]
    W = ATT_GROUP_WIDTH
    if d == 1:
        nb = S // ATT_BLOCK
        grid = (1, nb)
        blk = (ATT_BLOCK, W)
        cur = lambda c: (lambda r, n: (n, c))
        prev = lambda c: (lambda r, n: (jnp.maximum(n - 1, 0), c))
        o_spec = pl.BlockSpec(blk, lambda r, n: (n, 0))
        l_spec = pl.BlockSpec((ATT_BLOCK, LANES), lambda r, n: (n, 0))
        o_shape = jax.ShapeDtypeStruct((S, W), F32)
        l_shape = jax.ShapeDtypeStruct((S, LANES), F32)
    else:
        nb = S // (ATT_BLOCK * d)
        grid = (d, nb)
        blk = (None, None, ATT_BLOCK, W)
        cur = lambda c: (lambda r, n: (n, r, 0, c))
        prev = lambda c: (lambda r, n: (jnp.maximum(n - 1, 0), r, 0, c))
        o_spec = pl.BlockSpec(blk, lambda r, n: (n, r, 0, 0))
        l_spec = pl.BlockSpec((None, None, ATT_BLOCK, LANES), lambda r, n: (n, r, 0, 0))
        o_shape = jax.ShapeDtypeStruct((nb, d, ATT_BLOCK, W), F32)
        l_shape = jax.ShapeDtypeStruct((nb, d, ATT_BLOCK, LANES), F32)
    return pl.pallas_call(
        _attn_kernel,
        grid=grid,
        in_specs=[
            pl.BlockSpec(blk, cur(cq)),
            pl.BlockSpec(blk, cur(ck)),
            pl.BlockSpec(blk, prev(ck)),
            pl.BlockSpec(blk, cur(cv)),
            pl.BlockSpec(blk, prev(cv)),
        ],
        out_specs=[o_spec, l_spec],
        out_shape=[o_shape, l_shape],
        compiler_params=_cparams(2),
        name=f"dilated_attn_g{g}",
    )(qkv, qkv, qkv, qkv, qkv)


def _retention_kernel(q_ref, k_ref, vlo_ref, vhi_ref, glo_ref, ghi_ref, o_ref, state_ref):
    @pl.when(pl.program_id(0) == 0)
    def _():
        state_ref[...] = jnp.zeros_like(state_ref)

    ri = lax.broadcasted_iota(jnp.int32, (RET_CHUNK, RET_CHUNK), 0)
    ci = lax.broadcasted_iota(jnp.int32, (RET_CHUNK, RET_CHUNK), 1)
    diff = (ri - ci).astype(F32)
    nrow = lax.broadcasted_iota(jnp.int32, (RET_CHUNK, 1), 0).astype(F32)
    half = RET_HEADS // 2
    for h in range(RET_HEADS):
        ld = RET_LOG_DECAY[h]
        sl = slice(h * RET_QK_DIM, (h + 1) * RET_QK_DIM)
        vsl = slice((h % half) * RET_V_DIM, (h % half + 1) * RET_V_DIM)
        v_ref, g_ref = (vlo_ref, glo_ref) if h < half else (vhi_ref, ghi_ref)
        q = q_ref[:, sl]
        k = k_ref[:, sl]
        v = v_ref[:, vsl]
        intra = jnp.where(diff >= 0, jnp.exp(jnp.maximum(diff, 0.0) * ld), 0.0)
        scores = lax.dot_general(q, k, (((1,), (1,)), ((), ())), preferred_element_type=F32) * intra
        inner = jnp.dot(scores.astype(BF16), v, preferred_element_type=F32)
        state = state_ref[h]
        q_dec = jnp.exp((nrow + 1.0) * ld)
        cross = jnp.dot((q.astype(F32) * q_dec).astype(BF16), state.astype(BF16),
                        preferred_element_type=F32)
        k_dec = jnp.exp((RET_CHUNK - 1.0 - nrow) * ld)
        kv = lax.dot_general((k.astype(F32) * k_dec).astype(BF16), v, (((0,), (0,)), ((), ())),
                             preferred_element_type=F32)
        state_ref[h] = math.exp(RET_CHUNK * ld) * state + kv
        o = inner + cross
        mu = jnp.mean(o, axis=1, keepdims=True)
        var = jnp.mean(jnp.square(o - mu), axis=1, keepdims=True)
        o = (o - mu) * lax.rsqrt(var + LN_EPS)
        gate = g_ref[:, vsl].astype(F32)
        gate = gate * (1.0 / (1.0 + jnp.exp(-gate)))
        o_ref[:, h * RET_V_DIM:(h + 1) * RET_V_DIM] = (gate * o).astype(o_ref.dtype)


def _retention(nat, S):
    nc = S // RET_CHUNK
    blk = (RET_CHUNK, COL_BLOCK)

    def col(cb):
        return pl.BlockSpec(blk, lambda c: (c, cb))

    return pl.pallas_call(
        _retention_kernel,
        grid=(nc,),
        in_specs=[col(NC_QR), col(NC_KR), col(NC_VR), col(NC_VR + 1), col(NC_GR), col(NC_GR + 1)],
        out_specs=pl.BlockSpec((RET_CHUNK, RET_HEADS * RET_V_DIM), lambda c: (c, 0)),
        out_shape=jax.ShapeDtypeStruct((S, RET_HEADS * RET_V_DIM), BF16),
        scratch_shapes=[pltpu.VMEM((RET_HEADS, RET_QK_DIM, RET_V_DIM), F32)],
        compiler_params=_cparams(1),
        name="retention",
    )(nat, nat, nat, nat, nat, nat)


MERGE_ROWS = 256
MERGE_CHAIN_ROWS = 128


def _layer_norm(z, scale, bias):
    mu = jnp.mean(z, axis=1, keepdims=True)
    var = jnp.mean(jnp.square(z - mu), axis=1, keepdims=True)
    return (z - mu) * lax.rsqrt(var + LN_EPS) * scale + bias


def _sigmoid(x):
    return 1.0 / (1.0 + jnp.exp(-x))


def _pack_bf16_pair(a, b):
    abits = lax.bitcast_convert_type(a.astype(BF16).astype(F32), jnp.uint32)
    bbits = lax.bitcast_convert_type(b.astype(BF16).astype(F32), jnp.uint32)
    return (abits >> 16) | (bbits & jnp.uint32(0xFFFF0000))


def _unpack_bf16_pair(w):
    a = lax.bitcast_convert_type(w << 16, F32).astype(BF16)
    b = lax.bitcast_convert_type(w & jnp.uint32(0xFFFF0000), F32).astype(BF16)
    return a, b


def _split_bf16(v):
    hi = v.astype(BF16)
    return hi, (v - hi.astype(F32)).astype(BF16)


def _merge_kernel(o0_ref, o1_ref, o2_ref, l0_ref, l1_ref, l2_ref, or_ref,
                  ga0_ref, ga1_ref, gb0_ref, gb1_ref, x_ref,
                  wa_ref, wr_ref, wo_ref, lns_ref, lnb_ref, wrt_ref, brt_ref, hexp_ref,
                  x1_ref, x1p_ref, route_ref, cnt_ref, s1_ref, s2_ref, ls1_ref, ls2_ref):
    tm = o0_ref.shape[0]
    for d, src, lsrc, dst, ldst in ((DILATIONS[1], o1_ref, l1_ref, s1_ref, ls1_ref),
                                    (DILATIONS[2], o2_ref, l2_ref, s2_ref, ls2_ref)):
        for r in range(d):
            for h in range(ATT_HEADS_PER_GROUP):
                dst[h, pl.ds(r, tm // d, stride=d), :] = src[r, :, h * ATT_HEAD_DIM:(h + 1) * ATT_HEAD_DIM]
            ldst[pl.ds(r, tm // d, stride=d), :] = lsrc[r]

    @pl.when(pl.program_id(0) == 0)
    def _():
        cnt_ref[...] = jnp.zeros_like(cnt_ref)

    for c in range(tm // MERGE_CHAIN_ROWS):
        _merge_rows(slice(c * MERGE_CHAIN_ROWS, (c + 1) * MERGE_CHAIN_ROWS),
                    o0_ref, l0_ref, or_ref, ga0_ref, ga1_ref, gb0_ref, gb1_ref, x_ref,
                    wa_ref, wr_ref, wo_ref, lns_ref, lnb_ref, wrt_ref, brt_ref, hexp_ref,
                    x1_ref, x1p_ref, route_ref, cnt_ref, s1_ref, s2_ref, ls1_ref, ls2_ref)


def _merge_rows(rs, o0_ref, l0_ref, or_ref, ga0_ref, ga1_ref, gb0_ref, gb1_ref, x_ref,
                wa_ref, wr_ref, wo_ref, lns_ref, lnb_ref, wrt_ref, brt_ref, hexp_ref,
                x1_ref, x1p_ref, route_ref, cnt_ref, s1_ref, s2_ref, ls1_ref, ls2_ref):
    rc = rs.stop - rs.start
    l0, l1, l2 = l0_ref[rs, :], ls1_ref[rs, :], ls2_ref[rs, :]
    lm = jnp.maximum(jnp.maximum(l0, l1), l2)
    e0, e1, e2 = jnp.exp(l0 - lm), jnp.exp(l1 - lm), jnp.exp(l2 - lm)
    den = e0 + e1 + e2

    def lanes_of_head(w):
        hi, lo = _split_bf16(w)
        return (jnp.dot(hi, hexp_ref[...], preferred_element_type=F32)
                + jnp.dot(lo, hexp_ref[...], preferred_element_type=F32))

    heads = range(ATT_HEADS_PER_GROUP)
    o1 = jnp.concatenate([s1_ref[h, rs, :] for h in heads], axis=1)
    o2 = jnp.concatenate([s2_ref[h, rs, :] for h in heads], axis=1)
    o_a = (lanes_of_head(e0 / den) * o0_ref[rs, :] + lanes_of_head(e1 / den) * o1
           + lanes_of_head(e2 / den) * o2).astype(BF16)
    y_a = jnp.dot(o_a, wa_ref[...], preferred_element_type=F32)
    y_r = jnp.dot(or_ref[rs, :], wr_ref[...], preferred_element_type=F32)
    gate_a = jnp.concatenate([ga0_ref[rs, :], ga1_ref[rs, :]], axis=1).astype(F32)
    gate_b = jnp.concatenate([gb0_ref[rs, :], gb1_ref[rs, :]], axis=1).astype(F32)
    merged = _sigmoid(gate_a) * y_a + _sigmoid(gate_b) * y_r
    y = jnp.dot(merged.astype(BF16), wo_ref[...], preferred_element_type=F32)
    x1 = _layer_norm(ALPHA * x_ref[rs, :] + y, lns_ref[...], lnb_ref[...])
    x1_ref[rs, :] = x1
    x1p_ref[rs, :] = _pack_bf16_pair(x1[:, :D_MODEL // 2], x1[:, D_MODEL // 2:])

    x_hi, x_lo = _split_bf16(x1)
    hl = jnp.dot(x_hi, wrt_ref[...], preferred_element_type=F32)
    logits = (hl[:, :LANES] + hl[:, LANES:]
              + jnp.dot(x_lo, wrt_ref[:, :LANES], preferred_element_type=F32) + brt_ref[...])
    lane = lax.broadcasted_iota(jnp.int32, logits.shape, 1)
    is_g = lane < N_EXPERT_GROUPS
    gl = jnp.where(is_g, logits, NEG_INF)
    gmax = jnp.max(gl, axis=1, keepdims=True)
    g_sel = jnp.min(jnp.where(is_g & (gl == gmax), lane, LANES), axis=1, keepdims=True)
    p_group = 1.0 / jnp.sum(jnp.where(is_g, jnp.exp(gl - gmax), 0.0), axis=1, keepdims=True)
    lo = N_EXPERT_GROUPS + EXPERTS_PER_GROUP * g_sel
    in_grp = (lane >= lo) & (lane < lo + EXPERTS_PER_GROUP)
    el = jnp.where(in_grp, logits, NEG_INF)
    top1 = jnp.max(el, axis=1, keepdims=True)
    idx1 = jnp.min(jnp.where(in_grp & (el == top1), lane, LANES), axis=1, keepdims=True)
    el2 = jnp.where(lane == idx1, NEG_INF, el)
    top2 = jnp.max(el2, axis=1, keepdims=True)
    idx2 = jnp.min(jnp.where(in_grp & (lane != idx1) & (el2 == top2), lane, LANES), axis=1, keepdims=True)
    t = jnp.exp(top2 - top1)
    wt1 = p_group * (1.0 / (1.0 + t))
    wt2 = p_group * (t / (1.0 + t))
    slab = jnp.where(lane == 0, (idx1 - N_EXPERT_GROUPS).astype(F32), 0.0)
    slab = jnp.where(lane == 1, (idx2 - N_EXPERT_GROUPS).astype(F32), slab)
    slab = jnp.where(lane == 2, wt1, slab)
    slab = jnp.where(lane == 3, wt2, slab)

    hot = jnp.where((lane == idx1) | (lane == idx2), 1.0, 0.0)
    ri = lax.broadcasted_iota(jnp.int32, (rc, rc), 0)
    ci = lax.broadcasted_iota(jnp.int32, (rc, rc), 1)
    ltri = jnp.where(ri > ci, 1.0, 0.0).astype(BF16)
    prefix = jnp.dot(ltri, hot.astype(BF16), preferred_element_type=F32) + cnt_ref[...]
    rank1 = jnp.sum(jnp.where(lane == idx1, prefix, 0.0), axis=1, keepdims=True)
    rank2 = jnp.sum(jnp.where(lane == idx2, prefix, 0.0), axis=1, keepdims=True)
    cnt_ref[...] = cnt_ref[...] + jnp.sum(hot, axis=0, keepdims=True)
    slab = jnp.where(lane == 4, rank1, slab)
    slab = jnp.where(lane == 5, rank2, slab)
    route_ref[rs, :] = slab


def _merge(o_g, lse_g, o_r, nat, x2d, wa, wr, wo, lns, lnb, wrt, brt, S):
    tm = MERGE_ROWS
    D = D_MODEL
    W = ATT_GROUP_WIDTH
    head_expand = (jnp.arange(W, dtype=jnp.int32)[None, :] // ATT_HEAD_DIM
                   == jnp.arange(LANES, dtype=jnp.int32)[:, None]).astype(BF16)

    def rows(width):
        return pl.BlockSpec((tm, width), lambda i: (i, 0))

    def streams(g, width):
        d = DILATIONS[g]
        per = (ATT_BLOCK * d) // tm
        return pl.BlockSpec((None, d, tm // d, width), lambda i: (i // per, 0, i % per, 0))

    def ncol(cb):
        return pl.BlockSpec((tm, COL_BLOCK), lambda i: (i, cb))

    def const(shape):
        return pl.BlockSpec(shape, lambda i: (0, 0), pipeline_mode=pl.Buffered(1))

    return pl.pallas_call(
        _merge_kernel,
        grid=(S // tm,),
        in_specs=[
            rows(W), streams(1, W), streams(2, W),
            rows(LANES), streams(1, LANES), streams(2, LANES),
            rows(RET_HEADS * RET_V_DIM),
            ncol(NC_GATE_A), ncol(NC_GATE_A + 1), ncol(NC_GATE_B), ncol(NC_GATE_B + 1),
            rows(D),
            const((W, D)), const((RET_HEADS * RET_V_DIM, D)), const((D, D)),
            const((1, D)), const((1, D)), const((D, 2 * LANES)), const((1, LANES)), const((LANES, W)),
        ],
        out_specs=[rows(D), rows(D // 2), rows(LANES), pl.BlockSpec((1, LANES), lambda i: (0, 0))],
        out_shape=[jax.ShapeDtypeStruct((S, D), F32), jax.ShapeDtypeStruct((S, D // 2), jnp.uint32),
                   jax.ShapeDtypeStruct((S, LANES), F32),
                   jax.ShapeDtypeStruct((1, LANES), F32)],
        scratch_shapes=[pltpu.VMEM((ATT_HEADS_PER_GROUP, tm, ATT_HEAD_DIM), F32),
                        pltpu.VMEM((ATT_HEADS_PER_GROUP, tm, ATT_HEAD_DIM), F32),
                        pltpu.VMEM((tm, LANES), F32), pltpu.VMEM((tm, LANES), F32)],
        compiler_params=_cparams(1),
        name="merge_ln_router",
    )(*o_g, *lse_g, o_r, nat, nat, nat, nat, x2d, wa, wr, wo, lns, lnb, wrt, brt, head_expand)


META_N_USED, META_NEXT = 0, 1
WEIGHT_DMA_PRIORITY = 1


def _plan_kernel(e_ref, rank_ref, cnt_ref, dest_ref, rt_ref, be_ref, meta_ref, pstart_ref):
    n_blocks = be_ref.shape[0]

    def pad_row(r, carry):
        rt_ref[r] = 0
        return carry

    def per_expert(e, start):
        nb = lax.shift_right_logical(cnt_ref[e] + (MOE_BLOCK - 1), 7)
        pstart_ref[e] = start

        def fill(b, carry):
            be_ref[lax.shift_right_logical(start, 7) + b] = e
            return carry
        lax.fori_loop(0, nb, fill, 0)
        lax.fori_loop(start + cnt_ref[e], start + nb * MOE_BLOCK, pad_row, 0)
        return start + nb * MOE_BLOCK
    total = lax.fori_loop(0, N_EXPERTS, per_expert, 0)
    n_used = lax.shift_right_logical(total, 7)
    meta_ref[META_N_USED] = n_used

    def tail(b, carry):
        be_ref[b] = be_ref[jnp.maximum(n_used - 1, 0)]
        return carry
    lax.fori_loop(n_used, n_blocks, tail, 0)
    lax.fori_loop(total, rt_ref.shape[0], pad_row, 0)

    def nxt(k, cur):
        e = N_EXPERTS - 1 - k
        meta_ref[META_NEXT + e] = cur
        return jnp.where(cnt_ref[e] > 0, e, cur)
    lax.fori_loop(0, N_EXPERTS, nxt, N_EXPERTS)

    def assign(a, carry):
        row = pstart_ref[e_ref[a]] + rank_ref[a]
        dest_ref[a] = row
        rt_ref[row] = lax.shift_right_logical(a, 1)
        return carry
    lax.fori_loop(0, dest_ref.shape[0], assign, 0, unroll=8)


def _plan(e_flat, rank_flat, counts):
    A = e_flat.shape[0]
    n_blocks = A // MOE_BLOCK + N_EXPERTS
    smem = pl.BlockSpec(memory_space=pltpu.SMEM)
    return pl.pallas_call(
        _plan_kernel,
        in_specs=[smem, smem, smem],
        out_specs=[smem, smem, smem, smem],
        out_shape=[
            jax.ShapeDtypeStruct((A,), jnp.int32),
            jax.ShapeDtypeStruct((n_blocks * MOE_BLOCK,), jnp.int32),
            jax.ShapeDtypeStruct((n_blocks,), jnp.int32),
            jax.ShapeDtypeStruct((META_NEXT + N_EXPERTS,), jnp.int32),
        ],
        scratch_shapes=[pltpu.SMEM((N_EXPERTS,), jnp.int32)],
        name="moe_plan",
    )(e_flat, rank_flat, counts)


def _expert_kernel(be_ref, meta_ref, rt_ref, x_hbm, wg_hbm, wu_hbm, wd_hbm, y_ref,
                   xg_ref, wgf_ref, wuf_ref, wdf_ref, wgb_ref, wub_ref, wdb_ref, xsem, sem, slot_ref):
    b = pl.program_id(0)
    n_used = meta_ref[META_N_USED]
    e = be_ref[b]
    first = (b == 0) | (e != be_ref[jnp.maximum(b - 1, 0)])

    def gather(blk, slot):
        def body(i, carry):
            tok = rt_ref[blk * MOE_BLOCK + i]
            pltpu.make_async_copy(x_hbm.at[pl.ds(tok, 1)], xg_ref.at[slot, pl.ds(i, 1)],
                                  xsem.at[slot]).start()
            return carry
        lax.fori_loop(0, MOE_BLOCK, body, 0, unroll=8)

    @pl.when((b == 0) & (n_used > 0))
    def _():
        gather(0, 0)

    @pl.when(b + 1 < n_used)
    def _():
        gather(b + 1, (b + 1) % 2)

    def copies(ex, s):
        return (pltpu.make_async_copy(wg_hbm.at[ex], wgf_ref.at[s], sem.at[s]),
                pltpu.make_async_copy(wu_hbm.at[ex], wuf_ref.at[s], sem.at[s]),
                pltpu.make_async_copy(wd_hbm.at[ex], wdf_ref.at[s], sem.at[s]))

    @pl.when((b == 0) & (n_used > 0))
    def _():
        slot_ref[0] = 0
        for c in copies(e, 0):
            c.start(priority=WEIGHT_DMA_PRIORITY)

    @pl.when(first & (b < n_used))
    def _():
        s = slot_ref[0]
        nx = meta_ref[META_NEXT + e]

        @pl.when(nx < N_EXPERTS)
        def _():
            for c in copies(nx, 1 - s):
                c.start(priority=WEIGHT_DMA_PRIORITY)
        for c in copies(e, s):
            c.wait()
        wgb_ref[...] = wgf_ref[s].astype(BF16)
        wub_ref[...] = wuf_ref[s].astype(BF16)
        wdb_ref[...] = wdf_ref[s].astype(BF16)
        slot_ref[0] = 1 - s

    @pl.when(b < n_used)
    def _():
        slot = b % 2
        pltpu.make_async_copy(x_hbm.at[pl.ds(0, MOE_BLOCK)], xg_ref.at[slot], xsem.at[slot]).wait()
        xb = jnp.concatenate(_unpack_bf16_pair(xg_ref[slot]), axis=1)
        hg = jnp.dot(xb, wgb_ref[...], preferred_element_type=F32)
        hu = jnp.dot(xb, wub_ref[...], preferred_element_type=F32)
        hid = (hg * _sigmoid(hg) * hu).astype(BF16)
        y_ref[...] = jnp.dot(hid, wdb_ref[...], preferred_element_type=F32)

    @pl.when(b >= n_used)
    def _():
        y_ref[...] = jnp.zeros_like(y_ref)


def _experts(x1p, block_expert, meta, row_tok, w_gate, w_up, w_down):
    D = D_MODEL
    assert x1p.shape[1] == D // 2 and x1p.dtype == jnp.uint32
    n_blocks = block_expert.shape[0]
    R = n_blocks * MOE_BLOCK
    H = EXPERT_HIDDEN
    grid_spec = pltpu.PrefetchScalarGridSpec(
        num_scalar_prefetch=3,
        grid=(n_blocks,),
        in_specs=[
            pl.BlockSpec(memory_space=pl.ANY),
            pl.BlockSpec(memory_space=pl.ANY),
            pl.BlockSpec(memory_space=pl.ANY),
            pl.BlockSpec(memory_space=pl.ANY),
        ],
        out_specs=pl.BlockSpec((MOE_BLOCK, D), lambda b, be, meta, rt: (b, 0)),
        scratch_shapes=[
            pltpu.VMEM((2, MOE_BLOCK, D // 2), jnp.uint32),
            pltpu.VMEM((2, D, H), F32),
            pltpu.VMEM((2, D, H), F32),
            pltpu.VMEM((2, H, D), F32),
            pltpu.VMEM((D, H), BF16),
            pltpu.VMEM((D, H), BF16),
            pltpu.VMEM((H, D), BF16),
            pltpu.SemaphoreType.DMA((2,)),
            pltpu.SemaphoreType.DMA((2,)),
            pltpu.SMEM((1,), jnp.int32),
        ],
    )
    return pl.pallas_call(
        _expert_kernel,
        grid_spec=grid_spec,
        out_shape=jax.ShapeDtypeStruct((R, D), F32),
        compiler_params=_cparams(1),
        name="moe_experts",
    )(block_expert, meta, row_tok, x1p, w_gate, w_up, w_down)


COMBINE_TOKENS = 64


def _combine_kernel(dest_ref, y_hbm, x1_ref, route_ref, lns_ref, lnb_ref, o_ref, yg_ref, sem):
    i = pl.program_id(0)
    n = pl.num_programs(0)
    tt = COMBINE_TOKENS

    def issue(step, slot):
        def body(r, carry):
            for k in range(TOP_K):
                row = dest_ref[(step * tt + r) * TOP_K + k]
                pltpu.make_async_copy(y_hbm.at[pl.ds(row, 1)], yg_ref.at[slot, k, pl.ds(r, 1)],
                                      sem.at[slot]).start(priority=k)
            return carry
        lax.fori_loop(0, tt, body, 0, unroll=4)

    @pl.when(i == 0)
    def _():
        issue(0, 0)

    @pl.when(i + 1 < n)
    def _():
        issue(i + 1, (i + 1) % 2)

    slot = i % 2
    for k in range(TOP_K):
        pltpu.make_async_copy(y_hbm.at[pl.ds(0, tt)], yg_ref.at[slot, k], sem.at[slot]).wait()
    route = route_ref[...]
    w1 = route[:, 2:3]
    w2 = route[:, 3:4]
    moe = w1 * yg_ref[slot, 0] + w2 * yg_ref[slot, 1]
    o_ref[...] = _layer_norm(ALPHA * x1_ref[...] + moe, lns_ref[...], lnb_ref[...])


def _combine(dest, y, x1, route, lns, lnb):
    T, D = x1.shape
    tt = COMBINE_TOKENS
    grid_spec = pltpu.PrefetchScalarGridSpec(
        num_scalar_prefetch=1,
        grid=(T // tt,),
        in_specs=[
            pl.BlockSpec(memory_space=pl.ANY),
            pl.BlockSpec((tt, D), lambda i, d: (i, 0)),
            pl.BlockSpec((tt, LANES), lambda i, d: (i, 0)),
            pl.BlockSpec((1, D), lambda i, d: (0, 0)),
            pl.BlockSpec((1, D), lambda i, d: (0, 0)),
        ],
        out_specs=pl.BlockSpec((tt, D), lambda i, d: (i, 0)),
        scratch_shapes=[
            pltpu.VMEM((2, TOP_K, tt, D), F32),
            pltpu.SemaphoreType.DMA((2,)),
        ],
    )
    return pl.pallas_call(
        _combine_kernel,
        grid_spec=grid_spec,
        out_shape=jax.ShapeDtypeStruct((T, D), F32),
        compiler_params=_cparams(1),
        name="moe_combine_ln",
    )(dest, y, x1, route, lns, lnb)


def _rotary_tables(positions):
    pos = positions.reshape(-1).astype(F32)
    S = pos.shape[0]
    inv_a = ROPE_THETA ** (-jnp.arange(0, ROPE_DIM, 2, dtype=F32) / ROPE_DIM)
    inv_r = RET_ROT_BASE ** (-jnp.linspace(0.0, 1.0, RET_QK_DIM // 2, dtype=F32))
    ang = jnp.concatenate([inv_a, inv_r])[:, None] * pos[None, :]
    cos_t, sin_t = lax.optimization_barrier((jnp.cos(ang), jnp.sin(ang)))
    na = ROPE_DIM // 2
    cos, sin, cr, sr = cos_t[:na].T, sin_t[:na].T, cos_t[na:].T, sin_t[na:].T
    pad1 = jnp.ones((S, LANES - ROPE_DIM), F32)
    pad0 = jnp.zeros((S, LANES - ROPE_DIM), F32)
    tabs_a = jnp.concatenate([cos, cos, pad1, sin, sin, pad0], axis=1)
    tabs_r = jnp.concatenate([cr, cr, sr, sr], axis=1)
    return tabs_a, tabs_r


def kernel(x, positions, w_in, w_branch_attn, w_branch_ret, w_out, ln1_scale, ln1_bias, w_group_router, b_group_router, w_expert_router, b_expert_router, w_expert_gate, w_expert_up, w_expert_down, ln2_scale, ln2_bias):
    B, S, D = x.shape
    assert B == 1 and D == D_MODEL and w_in.shape[0] == DEPTH == 1
    assert S % (ATT_BLOCK * DILATIONS[2]) == 0
    x2d = x.reshape(S, D)
    xb = x2d.astype(BF16)
    tabs_a, tabs_r = _rotary_tables(positions)

    nat = _input_projection(xb, w_in[0], tabs_a, tabs_r, NAT_COLS, NAT_MODES, 1)
    qkv_modes = (MODE_ATT_Q, MODE_ATT_K, MODE_PLAIN)
    qkv = [None] + [
        _input_projection(xb, w_in[0], tabs_a, tabs_r, (CB_QA + g, CB_KA + g, CB_VA + g), qkv_modes,
                          DILATIONS[g])
        for g in (1, 2)]
    att = [_dilated_attention(nat, 0, S, NC_Q0, NC_K0, NC_V0)]
    att += [_dilated_attention(qkv[g], g, S, 0, 1, 2) for g in (1, 2)]
    o_r = _retention(nat, S)

    w_route = jnp.concatenate(
        [w_group_router[0], w_expert_router[0].transpose(1, 0, 2).reshape(D, N_EXPERTS),
         jnp.zeros((D, LANES - N_EXPERT_GROUPS - N_EXPERTS), F32)], axis=1)
    b_route = jnp.concatenate(
        [b_group_router[0], b_expert_router[0].reshape(N_EXPERTS),
         jnp.zeros((LANES - N_EXPERT_GROUPS - N_EXPERTS,), F32)]).reshape(1, LANES)
    x1, x1p, route, cnt = _merge(
        [a[0] for a in att], [a[1] for a in att], o_r, nat, x2d,
        w_branch_attn[0].astype(BF16), w_branch_ret[0].astype(BF16), w_out[0].astype(BF16),
        ln1_scale[0].reshape(1, D), ln1_bias[0].reshape(1, D),
        jnp.concatenate(_split_bf16(w_route), axis=1), b_route, S)

    e_flat = route[:, 0:TOP_K].astype(jnp.int32).reshape(S * TOP_K)
    rank_flat = route[:, 4:4 + TOP_K].astype(jnp.int32).reshape(S * TOP_K)
    counts = cnt[0, N_EXPERT_GROUPS:N_EXPERT_GROUPS + N_EXPERTS].astype(jnp.int32)
    dest, row_tok, block_expert, meta = _plan(e_flat, rank_flat, counts)
    y = _experts(x1p, block_expert, meta, row_tok, w_expert_gate[0], w_expert_up[0], w_expert_down[0])
    out = _combine(dest, y, x1, route, ln2_scale[0].reshape(1, D), ln2_bias[0].reshape(1, D))
    return out.reshape(B, S, D)
```

```python
import functools
import math

import jax
import jax.numpy as jnp
from jax import lax
from jax.experimental import pallas as pl
from jax.experimental.pallas import tpu as pltpu

F32 = jnp.float32
BF16 = jnp.bfloat16

D_MODEL = 2048
ATT_HEAD_DIM = 128
ATT_HEADS_PER_GROUP = 8
DILATIONS = (1, 4, 16)
ATT_STEPS = 128
N_ATT_GROUPS = 3
ATT_GROUP_WIDTH = ATT_HEADS_PER_GROUP * ATT_HEAD_DIM
ROPE_DIM = ATT_HEAD_DIM // 4
ROPE_THETA = 500000.0
ATT_BLOCK = 128

RET_HEADS = 8
RET_QK_DIM = 128
RET_V_DIM = 256
RET_CHUNK = 128
RET_ROT_BASE = 10000.0
RET_LOG_DECAY = tuple(math.log1p(-(2.0 ** (-5.0 - h))) for h in range(RET_HEADS))

N_EXPERT_GROUPS = 4
EXPERTS_PER_GROUP = 8
N_EXPERTS = N_EXPERT_GROUPS * EXPERTS_PER_GROUP
TOP_K = 2
EXPERT_HIDDEN = 512
MOE_BLOCK = 128

DEPTH = 1
ALPHA = (2.0 * DEPTH) ** 0.25
LN_EPS = 1e-5
NEG_INF = -1e30

COL_BLOCK = 1024
CB_QA, CB_KA, CB_VA = 0, 3, 6
CB_QR, CB_KR, CB_VR, CB_GR = 9, 10, 11, 13
CB_GATE_A, CB_GATE_B = 15, 17

MODE_PLAIN, MODE_ATT_Q, MODE_ATT_K, MODE_RET_Q, MODE_RET_K = 0, 1, 2, 3, 4

NAT_COLS = (CB_QA, CB_KA, CB_VA, CB_QR, CB_KR, CB_VR, CB_VR + 1, CB_GR, CB_GR + 1,
            CB_GATE_A, CB_GATE_A + 1, CB_GATE_B, CB_GATE_B + 1)
NAT_MODES = (MODE_ATT_Q, MODE_ATT_K, MODE_PLAIN, MODE_RET_Q, MODE_RET_K) + (MODE_PLAIN,) * 8
NC_Q0, NC_K0, NC_V0, NC_QR, NC_KR, NC_VR, NC_GR, NC_GATE_A, NC_GATE_B = 0, 1, 2, 3, 4, 5, 7, 9, 11

LANES = 128
VMEM_LIMIT = 52 * 1024 * 1024
PROJ_ROWS = 1024


def _cparams(n_grid_dims):
    return pltpu.CompilerParams(
        dimension_semantics=("arbitrary",) * n_grid_dims,
        vmem_limit_bytes=VMEM_LIMIT,
    )


def _inproj_kernel(cols_ref, modes_ref, x_ref, w_ref, ta_ref, tr_ref, pa_ref, pr_ref, o_ref,
                   wbf_ref, *acc, d):
    del cols_ref
    j = pl.program_id(0)

    @pl.when(pl.program_id(1) == 0)
    def _():
        wbf_ref[...] = w_ref[...].astype(BF16)

    mode = modes_ref[j]
    tm, tn = x_ref.shape[0], wbf_ref.shape[1]
    pair = 2 * LANES

    def project():
        return jnp.dot(x_ref[...], wbf_ref[...], preferred_element_type=F32)

    def emit(p, piece):
        if d == 1:
            o_ref[:, p * pair:(p + 1) * pair] = piece.astype(o_ref.dtype)
        else:
            acc[0][2 * p] = piece[:, :LANES]
            acc[0][2 * p + 1] = piece[:, LANES:]

    def plain():
        res = project()
        for p in range(tn // pair):
            emit(p, res[:, p * pair:(p + 1) * pair])

    def rotary(tab_ref, perm_ref, scale):
        res = project()
        c = jnp.concatenate([tab_ref[:, :LANES]] * 2, axis=1) * scale
        s = jnp.concatenate([tab_ref[:, LANES:]] * 2, axis=1) * scale
        for p in range(tn // pair):
            a = res[:, p * pair:(p + 1) * pair]
            partner = jnp.dot(a.astype(BF16), perm_ref[...], preferred_element_type=F32)
            emit(p, a * c + partner * s)

    is_att = (mode == MODE_ATT_Q) | (mode == MODE_ATT_K)
    is_ret = (mode == MODE_RET_Q) | (mode == MODE_RET_K)
    pl.when(mode == MODE_PLAIN)(plain)
    pl.when(is_att)(lambda: rotary(ta_ref, pa_ref, jnp.where(mode == MODE_ATT_Q, ATT_HEAD_DIM ** -0.5, 1.0)))
    if d == 1:
        pl.when(is_ret)(lambda: rotary(tr_ref, pr_ref, jnp.where(mode == MODE_RET_K, RET_QK_DIM ** -0.5, 1.0)))
        return

    acc_ref = acc[0]
    for h in range(tn // LANES):
        sl = slice(h * LANES, (h + 1) * LANES)
        if d == 4:
            P = ATT_BLOCK * d
            for sb in range(tm // P):
                for r in range(d):
                    o_ref[sb, r, :, sl] = acc_ref[h, pl.ds(sb * P + r, ATT_BLOCK, stride=d), :].astype(o_ref.dtype)
        else:
            for r in range(d):
                o_ref[r, :, sl] = acc_ref[h, pl.ds(r, tm // d, stride=d), :].astype(o_ref.dtype)


def _rotation_matrix(half, width):
    k = jnp.arange(2 * LANES, dtype=jnp.int32)[:, None]
    l = jnp.arange(2 * LANES, dtype=jnp.int32)[None, :]
    same_head = (k // LANES) == (l // LANES)
    kk, ll = k % LANES, l % LANES
    neg = same_head & (ll < half) & (kk == ll + half)
    pos = same_head & (ll >= half) & (ll < width) & (kk == ll - half)
    return (pos.astype(F32) - neg.astype(F32)).astype(BF16)


def _input_projection(xb, w_in, tabs_a, tabs_r, cols, modes, d):
    S, D = xb.shape
    tm, tn = PROJ_ROWS, COL_BLOCK
    n = len(cols)
    P = ATT_BLOCK * d
    if d == 1:
        out_shape = jax.ShapeDtypeStruct((S, n * tn), BF16)
        out_spec = pl.BlockSpec((tm, tn), lambda j, i, c, m: (i, j))
    elif d == 4:
        out_shape = jax.ShapeDtypeStruct((S // P, d, ATT_BLOCK, n * tn), BF16)
        out_spec = pl.BlockSpec((tm // P, d, ATT_BLOCK, tn), lambda j, i, c, m: (i, 0, 0, j))
    else:
        assert d == 16 and P == 2 * tm
        out_shape = jax.ShapeDtypeStruct((S // P, d, ATT_BLOCK, n * tn), BF16)
        out_spec = pl.BlockSpec((None, d, tm // d, tn), lambda j, i, c, m: (i // 2, 0, i % 2, j))

    def tab_rows(j, i, c, m):
        return (jnp.where(m[j] != MODE_PLAIN, i, 0), 0)

    grid_spec = pltpu.PrefetchScalarGridSpec(
        num_scalar_prefetch=2,
        grid=(n, S // tm),
        in_specs=[
            pl.BlockSpec((tm, D), lambda j, i, c, m: (i, 0)),
            pl.BlockSpec((D, tn), lambda j, i, c, m: (0, c[j])),
            pl.BlockSpec((tm, 2 * LANES), tab_rows),
            pl.BlockSpec((tm, 2 * LANES), tab_rows),
            pl.BlockSpec((2 * LANES, 2 * LANES), lambda j, i, c, m: (0, 0)),
            pl.BlockSpec((2 * LANES, 2 * LANES), lambda j, i, c, m: (0, 0)),
        ],
        out_specs=out_spec,
        scratch_shapes=[pltpu.VMEM((D, tn), BF16)]
        + ([] if d == 1 else [pltpu.VMEM((tn // LANES, tm, LANES), F32)]),
    )
    return pl.pallas_call(
        functools.partial(_inproj_kernel, d=d),
        grid_spec=grid_spec,
        out_shape=out_shape,
        compiler_params=_cparams(2),
        name=f"in_proj_d{d}",
    )(jnp.asarray(cols, jnp.int32), jnp.asarray(modes, jnp.int32), xb, w_in, tabs_a, tabs_r,
      _rotation_matrix(ROPE_DIM // 2, ROPE_DIM), _rotation_matrix(RET_QK_DIM // 2, RET_QK_DIM))


ATT_Q_BLOCKS = 2


def _attn_kernel(q_ref, k_ref, kp_ref, v_ref, vp_ref, o_ref, lse_ref):
    n = pl.program_id(1)
    qi = lax.broadcasted_iota(jnp.int32, (ATT_BLOCK, 2 * ATT_BLOCK), 0)
    kj = lax.broadcasted_iota(jnp.int32, (ATT_BLOCK, 2 * ATT_BLOCK), 1)
    dist = qi + ATT_BLOCK - kj
    band = (dist >= 0) & (dist <= ATT_STEPS)
    lane = lax.broadcasted_iota(jnp.int32, (ATT_BLOCK, LANES), 1)
    for sub in range(ATT_Q_BLOCKS):
        mask = band & ((kj >= ATT_BLOCK) | (n > 0)) if sub == 0 else band
        lse_slab = jnp.zeros((ATT_BLOCK, LANES), F32)
        for h in range(ATT_HEADS_PER_GROUP):
            sl = slice(h * ATT_HEAD_DIM, (h + 1) * ATT_HEAD_DIM)
            k_prev = kp_ref[:, sl] if sub == 0 else k_ref[sub - 1, :, sl]
            v_prev = vp_ref[:, sl] if sub == 0 else v_ref[sub - 1, :, sl]
            k2 = jnp.concatenate([k_prev, k_ref[sub, :, sl]], axis=0)
            v2 = jnp.concatenate([v_prev, v_ref[sub, :, sl]], axis=0)
            s = lax.dot_general(q_ref[sub, :, sl], k2, (((1,), (1,)), ((), ())),
                                preferred_element_type=F32)
            s = jnp.where(mask, s, NEG_INF)
            m = jnp.max(s, axis=1, keepdims=True)
            p = jnp.exp(s - m)
            l = jnp.sum(p, axis=1, keepdims=True)
            o = jnp.dot(p.astype(BF16), v2, preferred_element_type=F32)
            o_ref[sub, :, sl] = o / l
            lse_slab = jnp.where(lane == h, m + jnp.log(l), lse_slab)
        lse_ref[sub] = lse_slab


def _dilated_attention(qkv, g, cq, ck, cv):
    d = DILATIONS[g]
    W = ATT_GROUP_WIDTH
    nb = qkv.shape[0]
    nq = ATT_Q_BLOCKS
    assert qkv.shape[1] == d and nb % nq == 0
    blk = (nq, None, ATT_BLOCK, W)
    pblk = (None, None, ATT_BLOCK, W)
    cur = lambda c: (lambda r, n: (n, r, 0, c))
    prev = lambda c: (lambda r, n: (jnp.maximum(nq * n - 1, 0), r, 0, c))
    return pl.pallas_call(
        _attn_kernel,
        grid=(d, nb // nq),
        in_specs=[
            pl.BlockSpec(blk, cur(cq)),
            pl.BlockSpec(blk, cur(ck)),
            pl.BlockSpec(pblk, prev(ck)),
            pl.BlockSpec(blk, cur(cv)),
            pl.BlockSpec(pblk, prev(cv)),
        ],
        out_specs=[pl.BlockSpec(blk, lambda r, n: (n, r, 0, 0)),
                   pl.BlockSpec((nq, None, ATT_BLOCK, LANES), lambda r, n: (n, r, 0, 0))],
        out_shape=[jax.ShapeDtypeStruct((nb, d, ATT_BLOCK, W), F32),
                   jax.ShapeDtypeStruct((nb, d, ATT_BLOCK, LANES), F32)],
        compiler_params=_cparams(2),
        name=f"dilated_attn_g{g}",
    )(qkv, qkv, qkv, qkv, qkv)


def _retention_kernel(q_ref, k_ref, vlo_ref, vhi_ref, glo_ref, ghi_ref, o_ref, state_ref):
    @pl.when(pl.program_id(0) == 0)
    def _():
        state_ref[...] = jnp.zeros_like(state_ref)

    ri = lax.broadcasted_iota(jnp.int32, (RET_CHUNK, RET_CHUNK), 0)
    ci = lax.broadcasted_iota(jnp.int32, (RET_CHUNK, RET_CHUNK), 1)
    diff = (ri - ci).astype(F32)
    nrow = lax.broadcasted_iota(jnp.int32, (RET_CHUNK, 1), 0).astype(F32)
    half = RET_HEADS // 2
    for h in range(RET_HEADS):
        ld = RET_LOG_DECAY[h]
        sl = slice(h * RET_QK_DIM, (h + 1) * RET_QK_DIM)
        vsl = slice((h % half) * RET_V_DIM, (h % half + 1) * RET_V_DIM)
        v_ref, g_ref = (vlo_ref, glo_ref) if h < half else (vhi_ref, ghi_ref)
        q = q_ref[:, sl]
        k = k_ref[:, sl]
        v = v_ref[:, vsl]
        intra = jnp.where(diff >= 0, jnp.exp(jnp.maximum(diff, 0.0) * ld), 0.0)
        scores = lax.dot_general(q, k, (((1,), (1,)), ((), ())), preferred_element_type=F32) * intra
        inner = jnp.dot(scores.astype(BF16), v, preferred_element_type=F32)
        state = state_ref[h]
        q_dec = jnp.exp((nrow + 1.0) * ld)
        cross = jnp.dot((q.astype(F32) * q_dec).astype(BF16), state.astype(BF16),
                        preferred_element_type=F32)
        k_dec = jnp.exp((RET_CHUNK - 1.0 - nrow) * ld)
        kv = lax.dot_general((k.astype(F32) * k_dec).astype(BF16), v, (((0,), (0,)), ((), ())),
                             preferred_element_type=F32)
        state_ref[h] = math.exp(RET_CHUNK * ld) * state + kv
        o = inner + cross
        mu = jnp.mean(o, axis=1, keepdims=True)
        var = jnp.mean(jnp.square(o - mu), axis=1, keepdims=True)
        o = (o - mu) * lax.rsqrt(var + LN_EPS)
        gate = g_ref[:, vsl].astype(F32)
        gate = gate * (1.0 / (1.0 + jnp.exp(-gate)))
        o_ref[:, h * RET_V_DIM:(h + 1) * RET_V_DIM] = (gate * o).astype(o_ref.dtype)


def _retention(nat, S):
    nc = S // RET_CHUNK
    blk = (RET_CHUNK, COL_BLOCK)

    def col(cb):
        return pl.BlockSpec(blk, lambda c: (c, cb))

    return pl.pallas_call(
        _retention_kernel,
        grid=(nc,),
        in_specs=[col(NC_QR), col(NC_KR), col(NC_VR), col(NC_VR + 1), col(NC_GR), col(NC_GR + 1)],
        out_specs=pl.BlockSpec((RET_CHUNK, RET_HEADS * RET_V_DIM), lambda c: (c, 0)),
        out_shape=jax.ShapeDtypeStruct((S, RET_HEADS * RET_V_DIM), BF16),
        scratch_shapes=[pltpu.VMEM((RET_HEADS, RET_QK_DIM, RET_V_DIM), F32)],
        compiler_params=_cparams(1),
        name="retention",
    )(nat, nat, nat, nat, nat, nat)


MERGE_ROWS = 256
MERGE_CHAIN_ROWS = 256


def _layer_norm(z, scale, bias):
    mu = jnp.mean(z, axis=1, keepdims=True)
    var = jnp.mean(jnp.square(z - mu), axis=1, keepdims=True)
    return (z - mu) * lax.rsqrt(var + LN_EPS) * scale + bias


def _sigmoid(x):
    return 1.0 / (1.0 + jnp.exp(-x))


def _pack_bf16_pair(a, b):
    abits = lax.bitcast_convert_type(a.astype(BF16).astype(F32), jnp.uint32)
    bbits = lax.bitcast_convert_type(b.astype(BF16).astype(F32), jnp.uint32)
    return (abits >> 16) | (bbits & jnp.uint32(0xFFFF0000))


def _unpack_bf16_pair(w):
    a = lax.bitcast_convert_type(w << 16, F32).astype(BF16)
    b = lax.bitcast_convert_type(w & jnp.uint32(0xFFFF0000), F32).astype(BF16)
    return a, b


def _split_bf16(v):
    hi = v.astype(BF16)
    return hi, (v - hi.astype(F32)).astype(BF16)


def _merge_kernel(o0_ref, o1_ref, o2_ref, l0_ref, l1_ref, l2_ref, or_ref,
                  ga0_ref, ga1_ref, gb0_ref, gb1_ref, x_ref,
                  wa_ref, wr_ref, wo_ref, lns_ref, lnb_ref, wrt_ref, brt_ref, hexp_ref,
                  x1_ref, x1p_ref, route_ref, cnt_ref, s1_ref, s2_ref, ls1_ref, ls2_ref):
    tm = o0_ref.shape[0]
    for d, src, lsrc, dst, ldst in ((DILATIONS[1], o1_ref, l1_ref, s1_ref, ls1_ref),
                                    (DILATIONS[2], o2_ref, l2_ref, s2_ref, ls2_ref)):
        for r in range(d):
            for h in range(ATT_HEADS_PER_GROUP):
                dst[h, pl.ds(r, tm // d, stride=d), :] = src[r, :, h * ATT_HEAD_DIM:(h + 1) * ATT_HEAD_DIM]
            ldst[pl.ds(r, tm // d, stride=d), :] = lsrc[r]

    @pl.when(pl.program_id(0) == 0)
    def _():
        cnt_ref[...] = jnp.zeros_like(cnt_ref)

    for c in range(tm // MERGE_CHAIN_ROWS):
        _merge_rows(slice(c * MERGE_CHAIN_ROWS, (c + 1) * MERGE_CHAIN_ROWS),
                    o0_ref, l0_ref, or_ref, ga0_ref, ga1_ref, gb0_ref, gb1_ref, x_ref,
                    wa_ref, wr_ref, wo_ref, lns_ref, lnb_ref, wrt_ref, brt_ref, hexp_ref,
                    x1_ref, x1p_ref, route_ref, cnt_ref, s1_ref, s2_ref, ls1_ref, ls2_ref)


def _merge_rows(rs, o0_ref, l0_ref, or_ref, ga0_ref, ga1_ref, gb0_ref, gb1_ref, x_ref,
                wa_ref, wr_ref, wo_ref, lns_ref, lnb_ref, wrt_ref, brt_ref, hexp_ref,
                x1_ref, x1p_ref, route_ref, cnt_ref, s1_ref, s2_ref, ls1_ref, ls2_ref):
    rc = rs.stop - rs.start
    l0, l1, l2 = l0_ref[rs, :], ls1_ref[rs, :], ls2_ref[rs, :]
    lm = jnp.maximum(jnp.maximum(l0, l1), l2)
    e0, e1, e2 = jnp.exp(l0 - lm), jnp.exp(l1 - lm), jnp.exp(l2 - lm)
    den = e0 + e1 + e2

    def lanes_of_head(w):
        hi, lo = _split_bf16(w)
        return (jnp.dot(hi, hexp_ref[...], preferred_element_type=F32)
                + jnp.dot(lo, hexp_ref[...], preferred_element_type=F32))

    half = D_MODEL // 2
    halves = (slice(0, half), slice(half, D_MODEL))
    o_r = or_ref[rs, :]
    y_r = [jnp.dot(o_r, wr_ref[:, cs], preferred_element_type=F32) for cs in halves]
    heads = range(ATT_HEADS_PER_GROUP)
    o1 = jnp.concatenate([s1_ref[h, rs, :] for h in heads], axis=1)
    o2 = jnp.concatenate([s2_ref[h, rs, :] for h in heads], axis=1)
    o_a = (lanes_of_head(e0 / den) * o0_ref[rs, :] + lanes_of_head(e1 / den) * o1
           + lanes_of_head(e2 / den) * o2).astype(BF16)
    merged = []
    for n, cs in enumerate(halves):
        y_a = jnp.dot(o_a, wa_ref[:, cs], preferred_element_type=F32)
        gate_a = (ga0_ref, ga1_ref)[n][rs, :].astype(F32)
        gate_b = (gb0_ref, gb1_ref)[n][rs, :].astype(F32)
        merged.append((_sigmoid(gate_a) * y_a + _sigmoid(gate_b) * y_r[n]).astype(BF16))
    z = [ALPHA * x_ref[rs, cs]
         + jnp.dot(merged[0], wo_ref[:half, cs], preferred_element_type=F32)
         + jnp.dot(merged[1], wo_ref[half:, cs], preferred_element_type=F32) for cs in halves]
    mu = (jnp.sum(z[0], axis=1, keepdims=True) + jnp.sum(z[1], axis=1, keepdims=True)) / D_MODEL
    var = (jnp.sum(jnp.square(z[0] - mu), axis=1, keepdims=True)
           + jnp.sum(jnp.square(z[1] - mu), axis=1, keepdims=True)) / D_MODEL
    inv = lax.rsqrt(var + LN_EPS)
    x1 = [(z[n] - mu) * inv * lns_ref[:, cs] + lnb_ref[:, cs] for n, cs in enumerate(halves)]
    for n, cs in enumerate(halves):
        x1_ref[rs, cs] = x1[n]
    x1p_ref[rs, :] = _pack_bf16_pair(x1[0], x1[1])

    logits = brt_ref[...]
    for n, cs in enumerate(halves):
        x_hi, x_lo = _split_bf16(x1[n])
        hl = jnp.dot(x_hi, wrt_ref[cs, :], preferred_element_type=F32)
        logits = (logits + hl[:, :LANES] + hl[:, LANES:]
                  + jnp.dot(x_lo, wrt_ref[cs, :LANES], preferred_element_type=F32))
    lane = lax.broadcasted_iota(jnp.int32, logits.shape, 1)
    is_g = lane < N_EXPERT_GROUPS
    gl = jnp.where(is_g, logits, NEG_INF)
    gmax = jnp.max(gl, axis=1, keepdims=True)
    g_sel = jnp.min(jnp.where(is_g & (gl == gmax), lane, LANES), axis=1, keepdims=True)
    p_group = 1.0 / jnp.sum(jnp.where(is_g, jnp.exp(gl - gmax), 0.0), axis=1, keepdims=True)
    lo = N_EXPERT_GROUPS + EXPERTS_PER_GROUP * g_sel
    in_grp = (lane >= lo) & (lane < lo + EXPERTS_PER_GROUP)
    el = jnp.where(in_grp, logits, NEG_INF)
    top1 = jnp.max(el, axis=1, keepdims=True)
    idx1 = jnp.min(jnp.where(in_grp & (el == top1), lane, LANES), axis=1, keepdims=True)
    el2 = jnp.where(lane == idx1, NEG_INF, el)
    top2 = jnp.max(el2, axis=1, keepdims=True)
    idx2 = jnp.min(jnp.where(in_grp & (lane != idx1) & (el2 == top2), lane, LANES), axis=1, keepdims=True)
    t = jnp.exp(top2 - top1)
    wt1 = p_group * (1.0 / (1.0 + t))
    wt2 = p_group * (t / (1.0 + t))
    slab = jnp.where(lane == 0, (idx1 - N_EXPERT_GROUPS).astype(F32), 0.0)
    slab = jnp.where(lane == 1, (idx2 - N_EXPERT_GROUPS).astype(F32), slab)
    slab = jnp.where(lane == 2, wt1, slab)
    slab = jnp.where(lane == 3, wt2, slab)

    hot = jnp.where((lane == idx1) | (lane == idx2), 1.0, 0.0)
    ri = lax.broadcasted_iota(jnp.int32, (rc, rc), 0)
    ci = lax.broadcasted_iota(jnp.int32, (rc, rc), 1)
    ltri = jnp.where(ri > ci, 1.0, 0.0).astype(BF16)
    prefix = jnp.dot(ltri, hot.astype(BF16), preferred_element_type=F32) + cnt_ref[...]
    rank1 = jnp.sum(jnp.where(lane == idx1, prefix, 0.0), axis=1, keepdims=True)
    rank2 = jnp.sum(jnp.where(lane == idx2, prefix, 0.0), axis=1, keepdims=True)
    cnt_ref[...] = cnt_ref[...] + jnp.sum(hot, axis=0, keepdims=True)
    slab = jnp.where(lane == 4, rank1, slab)
    slab = jnp.where(lane == 5, rank2, slab)
    route_ref[rs, :] = slab


def _merge(o_g, lse_g, o_r, nat, x2d, wa, wr, wo, lns, lnb, wrt, brt, S):
    tm = MERGE_ROWS
    D = D_MODEL
    W = ATT_GROUP_WIDTH
    head_expand = (jnp.arange(W, dtype=jnp.int32)[None, :] // ATT_HEAD_DIM
                   == jnp.arange(LANES, dtype=jnp.int32)[:, None]).astype(BF16)

    def rows(width):
        return pl.BlockSpec((tm, width), lambda i: (i, 0))

    def streams(g, width):
        d = DILATIONS[g]
        per = (ATT_BLOCK * d) // tm
        return pl.BlockSpec((None, d, tm // d, width), lambda i: (i // per, 0, i % per, 0))

    def ncol(cb):
        return pl.BlockSpec((tm, COL_BLOCK), lambda i: (i, cb))

    def const(shape):
        return pl.BlockSpec(shape, lambda i: (0, 0), pipeline_mode=pl.Buffered(1))

    return pl.pallas_call(
        _merge_kernel,
        grid=(S // tm,),
        in_specs=[
            rows(W), streams(1, W), streams(2, W),
            rows(LANES), streams(1, LANES), streams(2, LANES),
            rows(RET_HEADS * RET_V_DIM),
            ncol(NC_GATE_A), ncol(NC_GATE_A + 1), ncol(NC_GATE_B), ncol(NC_GATE_B + 1),
            rows(D),
            const((W, D)), const((RET_HEADS * RET_V_DIM, D)), const((D, D)),
            const((1, D)), const((1, D)), const((D, 2 * LANES)), const((1, LANES)), const((LANES, W)),
        ],
        out_specs=[rows(D), rows(D // 2), rows(LANES), pl.BlockSpec((1, LANES), lambda i: (0, 0))],
        out_shape=[jax.ShapeDtypeStruct((S, D), F32), jax.ShapeDtypeStruct((S, D // 2), jnp.uint32),
                   jax.ShapeDtypeStruct((S, LANES), F32),
                   jax.ShapeDtypeStruct((1, LANES), F32)],
        scratch_shapes=[pltpu.VMEM((ATT_HEADS_PER_GROUP, tm, ATT_HEAD_DIM), F32),
                        pltpu.VMEM((ATT_HEADS_PER_GROUP, tm, ATT_HEAD_DIM), F32),
                        pltpu.VMEM((tm, LANES), F32), pltpu.VMEM((tm, LANES), F32)],
        compiler_params=_cparams(1),
        name="merge_ln_router",
    )(*o_g, *lse_g, o_r, nat, nat, nat, nat, x2d, wa, wr, wo, lns, lnb, wrt, brt, head_expand)


META_N_USED, META_NEXT = 0, 1
WEIGHT_DMA_PRIORITY = 1


def _plan_kernel(dest_ref, cnt_ref, rt_ref, be_ref, meta_ref):
    n_blocks = be_ref.shape[0]

    def pad_row(r, carry):
        rt_ref[r] = 0
        return carry

    def per_expert(e, start):
        nb = lax.shift_right_logical(cnt_ref[e] + (MOE_BLOCK - 1), 7)

        def fill(b, carry):
            be_ref[lax.shift_right_logical(start, 7) + b] = e
            return carry
        lax.fori_loop(0, nb, fill, 0)
        lax.fori_loop(start + cnt_ref[e], start + nb * MOE_BLOCK, pad_row, 0)
        return start + nb * MOE_BLOCK
    total = lax.fori_loop(0, N_EXPERTS, per_expert, 0)
    n_used = lax.shift_right_logical(total, 7)
    meta_ref[META_N_USED] = n_used

    def tail(b, carry):
        be_ref[b] = be_ref[jnp.maximum(n_used - 1, 0)]
        return carry
    lax.fori_loop(n_used, n_blocks, tail, 0)
    lax.fori_loop(total, rt_ref.shape[0], pad_row, 0)

    def nxt(k, cur):
        e = N_EXPERTS - 1 - k
        meta_ref[META_NEXT + e] = cur
        return jnp.where(cnt_ref[e] > 0, e, cur)
    lax.fori_loop(0, N_EXPERTS, nxt, N_EXPERTS)

    def assign(a, carry):
        rt_ref[dest_ref[a]] = lax.shift_right_logical(a, 1)
        return carry
    lax.fori_loop(0, dest_ref.shape[0], assign, 0, unroll=8)


def _plan(dest, counts):
    A = dest.shape[0]
    n_blocks = A // MOE_BLOCK + N_EXPERTS
    smem = pl.BlockSpec(memory_space=pltpu.SMEM)
    return pl.pallas_call(
        _plan_kernel,
        in_specs=[smem, smem],
        out_specs=[smem, smem, smem],
        out_shape=[
            jax.ShapeDtypeStruct((n_blocks * MOE_BLOCK,), jnp.int32),
            jax.ShapeDtypeStruct((n_blocks,), jnp.int32),
            jax.ShapeDtypeStruct((META_NEXT + N_EXPERTS,), jnp.int32),
        ],
        name="moe_plan",
    )(dest, counts)


def _expert_kernel(be_ref, meta_ref, rt_ref, x_hbm, wg_hbm, wu_hbm, wd_hbm, y_ref,
                   xg_ref, wgf_ref, wuf_ref, wdf_ref, wgb_ref, wub_ref, wdb_ref, xsem, sem, slot_ref):
    b = pl.program_id(0)
    n_used = meta_ref[META_N_USED]
    e = be_ref[b]
    first = (b == 0) | (e != be_ref[jnp.maximum(b - 1, 0)])

    def gather(blk, slot):
        def body(i, carry):
            tok = rt_ref[blk * MOE_BLOCK + i]
            pltpu.make_async_copy(x_hbm.at[pl.ds(tok, 1)], xg_ref.at[slot, pl.ds(i, 1)],
                                  xsem.at[slot]).start()
            return carry
        lax.fori_loop(0, MOE_BLOCK, body, 0, unroll=8)

    @pl.when((b == 0) & (n_used > 0))
    def _():
        gather(0, 0)

    @pl.when(b + 1 < n_used)
    def _():
        gather(b + 1, (b + 1) % 2)

    def copies(ex, s):
        return (pltpu.make_async_copy(wg_hbm.at[ex], wgf_ref.at[s], sem.at[s]),
                pltpu.make_async_copy(wu_hbm.at[ex], wuf_ref.at[s], sem.at[s]),
                pltpu.make_async_copy(wd_hbm.at[ex], wdf_ref.at[s], sem.at[s]))

    @pl.when((b == 0) & (n_used > 0))
    def _():
        slot_ref[0] = 0
        for c in copies(e, 0):
            c.start(priority=WEIGHT_DMA_PRIORITY)

    @pl.when(first & (b < n_used))
    def _():
        s = slot_ref[0]
        nx = meta_ref[META_NEXT + e]

        @pl.when(nx < N_EXPERTS)
        def _():
            for c in copies(nx, 1 - s):
                c.start(priority=WEIGHT_DMA_PRIORITY)
        for c in copies(e, s):
            c.wait()
        wgb_ref[...] = wgf_ref[s].astype(BF16)
        wub_ref[...] = wuf_ref[s].astype(BF16)
        wdb_ref[...] = wdf_ref[s].astype(BF16)
        slot_ref[0] = 1 - s

    @pl.when(b < n_used)
    def _():
        slot = b % 2
        pltpu.make_async_copy(x_hbm.at[pl.ds(0, MOE_BLOCK)], xg_ref.at[slot], xsem.at[slot]).wait()
        xb = jnp.concatenate(_unpack_bf16_pair(xg_ref[slot]), axis=1)
        hg = jnp.dot(xb, wgb_ref[...], preferred_element_type=F32)
        hu = jnp.dot(xb, wub_ref[...], preferred_element_type=F32)
        hid = (hg * _sigmoid(hg) * hu).astype(BF16)
        y_ref[...] = jnp.dot(hid, wdb_ref[...], preferred_element_type=F32)

    @pl.when(b >= n_used)
    def _():
        y_ref[...] = jnp.zeros_like(y_ref)


def _experts(x1p, block_expert, meta, row_tok, w_gate, w_up, w_down):
    D = D_MODEL
    assert x1p.shape[1] == D // 2 and x1p.dtype == jnp.uint32
    n_blocks = block_expert.shape[0]
    R = n_blocks * MOE_BLOCK
    H = EXPERT_HIDDEN
    grid_spec = pltpu.PrefetchScalarGridSpec(
        num_scalar_prefetch=3,
        grid=(n_blocks,),
        in_specs=[
            pl.BlockSpec(memory_space=pl.ANY),
            pl.BlockSpec(memory_space=pl.ANY),
            pl.BlockSpec(memory_space=pl.ANY),
            pl.BlockSpec(memory_space=pl.ANY),
        ],
        out_specs=pl.BlockSpec((MOE_BLOCK, D), lambda b, be, meta, rt: (b, 0)),
        scratch_shapes=[
            pltpu.VMEM((2, MOE_BLOCK, D // 2), jnp.uint32),
            pltpu.VMEM((2, D, H), F32),
            pltpu.VMEM((2, D, H), F32),
            pltpu.VMEM((2, H, D), F32),
            pltpu.VMEM((D, H), BF16),
            pltpu.VMEM((D, H), BF16),
            pltpu.VMEM((H, D), BF16),
            pltpu.SemaphoreType.DMA((2,)),
            pltpu.SemaphoreType.DMA((2,)),
            pltpu.SMEM((1,), jnp.int32),
        ],
    )
    return pl.pallas_call(
        _expert_kernel,
        grid_spec=grid_spec,
        out_shape=jax.ShapeDtypeStruct((R, D), F32),
        compiler_params=_cparams(1),
        name="moe_experts",
    )(block_expert, meta, row_tok, x1p, w_gate, w_up, w_down)


COMBINE_TOKENS = 64


def _combine_kernel(dest_ref, y_hbm, x1_ref, route_ref, lns_ref, lnb_ref, o_ref, yg_ref, sem):
    i = pl.program_id(0)
    tt = COMBINE_TOKENS
    n = dest_ref.shape[0] // (TOP_K * tt)

    def start_row(step, slot, r):
        for k in range(TOP_K):
            row = dest_ref[(step * tt + r) * TOP_K + k]
            pltpu.make_async_copy(y_hbm.at[pl.ds(row, 1)], yg_ref.at[slot, k, pl.ds(r, 1)],
                                  sem.at[slot]).start(priority=k)

    def wait_slot(slot):
        for k in range(TOP_K):
            pltpu.make_async_copy(y_hbm.at[pl.ds(0, tt)], yg_ref.at[slot, k], sem.at[slot]).wait()

    @pl.when(i == 0)
    def _():
        def body(r, carry):
            start_row(0, 0, r)
            return carry
        lax.fori_loop(0, tt, body, 0, unroll=4)

    def step(slot):
        wait_slot(slot)
        nxt = jnp.minimum(i + 1, n - 1)
        for r in range(tt):
            start_row(nxt, 1 - slot, r)
        route = route_ref[...]
        moe = route[:, 2:3] * yg_ref[slot, 0] + route[:, 3:4] * yg_ref[slot, 1]
        o_ref[...] = _layer_norm(ALPHA * x1_ref[...] + moe, lns_ref[...], lnb_ref[...])

    pl.when(i % 2 == 0)(lambda: step(0))
    pl.when(i % 2 == 1)(lambda: step(1))

    @pl.when(i == n - 1)
    def _():
        wait_slot(n % 2)


def _combine(dest, y, x1, route, lns, lnb):
    T, D = x1.shape
    tt = COMBINE_TOKENS
    grid_spec = pltpu.PrefetchScalarGridSpec(
        num_scalar_prefetch=1,
        grid=(T // tt,),
        in_specs=[
            pl.BlockSpec(memory_space=pl.ANY),
            pl.BlockSpec((tt, D), lambda i, d: (i, 0)),
            pl.BlockSpec((tt, LANES), lambda i, d: (i, 0)),
            pl.BlockSpec((1, D), lambda i, d: (0, 0)),
            pl.BlockSpec((1, D), lambda i, d: (0, 0)),
        ],
        out_specs=pl.BlockSpec((tt, D), lambda i, d: (i, 0)),
        scratch_shapes=[
            pltpu.VMEM((2, TOP_K, tt, D), F32),
            pltpu.SemaphoreType.DMA((2,)),
        ],
    )
    return pl.pallas_call(
        _combine_kernel,
        grid_spec=grid_spec,
        out_shape=jax.ShapeDtypeStruct((T, D), F32),
        compiler_params=_cparams(1),
        name="moe_combine_ln",
    )(dest, y, x1, route, lns, lnb)


def _rotary_tables(positions):
    pos = positions.reshape(-1).astype(F32)
    S = pos.shape[0]
    inv_a = ROPE_THETA ** (-jnp.arange(0, ROPE_DIM, 2, dtype=F32) / ROPE_DIM)
    inv_r = RET_ROT_BASE ** (-jnp.linspace(0.0, 1.0, RET_QK_DIM // 2, dtype=F32))
    ang = jnp.concatenate([inv_a, inv_r])[:, None] * pos[None, :]
    cos_t, sin_t = lax.optimization_barrier((jnp.cos(ang), jnp.sin(ang)))
    na = ROPE_DIM // 2
    cos, sin, cr, sr = cos_t[:na].T, sin_t[:na].T, cos_t[na:].T, sin_t[na:].T
    pad1 = jnp.ones((S, LANES - ROPE_DIM), F32)
    pad0 = jnp.zeros((S, LANES - ROPE_DIM), F32)
    tabs_a = jnp.concatenate([cos, cos, pad1, sin, sin, pad0], axis=1)
    tabs_r = jnp.concatenate([cr, cr, sr, sr], axis=1)
    return tabs_a, tabs_r


def kernel(x, positions, w_in, w_branch_attn, w_branch_ret, w_out, ln1_scale, ln1_bias, w_group_router, b_group_router, w_expert_router, b_expert_router, w_expert_gate, w_expert_up, w_expert_down, ln2_scale, ln2_bias):
    B, S, D = x.shape
    assert B == 1 and D == D_MODEL and w_in.shape[0] == DEPTH == 1
    assert S % (ATT_BLOCK * DILATIONS[2]) == 0
    x2d = x.reshape(S, D)
    xb = x2d.astype(BF16)
    tabs_a, tabs_r = _rotary_tables(positions)

    nat = _input_projection(xb, w_in[0], tabs_a, tabs_r, NAT_COLS, NAT_MODES, 1)
    qkv_modes = (MODE_ATT_Q, MODE_ATT_K, MODE_PLAIN)
    qkv = [None] + [
        _input_projection(xb, w_in[0], tabs_a, tabs_r, (CB_QA + g, CB_KA + g, CB_VA + g), qkv_modes,
                          DILATIONS[g])
        for g in (1, 2)]
    o0, l0 = _dilated_attention(nat.reshape(S // ATT_BLOCK, 1, ATT_BLOCK, nat.shape[1]), 0,
                                NC_Q0, NC_K0, NC_V0)
    att = [(o0.reshape(S, ATT_GROUP_WIDTH), l0.reshape(S, LANES))]
    att += [_dilated_attention(qkv[g], g, 0, 1, 2) for g in (1, 2)]
    o_r = _retention(nat, S)

    w_route = jnp.concatenate(
        [w_group_router[0], w_expert_router[0].transpose(1, 0, 2).reshape(D, N_EXPERTS),
         jnp.zeros((D, LANES - N_EXPERT_GROUPS - N_EXPERTS), F32)], axis=1)
    b_route = jnp.concatenate(
        [b_group_router[0], b_expert_router[0].reshape(N_EXPERTS),
         jnp.zeros((LANES - N_EXPERT_GROUPS - N_EXPERTS,), F32)]).reshape(1, LANES)
    x1, x1p, route, cnt = _merge(
        [a[0] for a in att], [a[1] for a in att], o_r, nat, x2d,
        w_branch_attn[0].astype(BF16), w_branch_ret[0].astype(BF16), w_out[0].astype(BF16),
        ln1_scale[0].reshape(1, D), ln1_bias[0].reshape(1, D),
        jnp.concatenate(_split_bf16(w_route), axis=1), b_route, S)

    e_flat = route[:, 0:TOP_K].astype(jnp.int32).reshape(S * TOP_K)
    rank_flat = route[:, 4:4 + TOP_K].astype(jnp.int32).reshape(S * TOP_K)
    counts = cnt[0, N_EXPERT_GROUPS:N_EXPERT_GROUPS + N_EXPERTS].astype(jnp.int32)
    padded = (counts + (MOE_BLOCK - 1)) // MOE_BLOCK * MOE_BLOCK
    pstart = jnp.cumsum(padded) - padded
    hot = e_flat[:, None] == jnp.arange(N_EXPERTS, dtype=jnp.int32)[None, :]
    dest = jnp.sum(jnp.where(hot, pstart[None, :], 0), axis=1) + rank_flat
    row_tok, block_expert, meta = _plan(dest, counts)
    y = _experts(x1p, block_expert, meta, row_tok, w_expert_gate[0], w_expert_up[0], w_expert_down[0])
    out = _combine(dest, y, x1, route, ln2_scale[0].reshape(1, D), ln2_bias[0].reshape(1, D))
    return out.reshape(B, S, D)
```

```python
import functools
import math

import jax
import jax.numpy as jnp
from jax import lax
from jax.experimental import pallas as pl
from jax.experimental.pallas import tpu as pltpu

F32 = jnp.float32
BF16 = jnp.bfloat16

D_MODEL = 2048
ATT_HEAD_DIM = 128
ATT_HEADS_PER_GROUP = 8
DILATIONS = (1, 4, 16)
ATT_STEPS = 128
N_ATT_GROUPS = 3
ATT_GROUP_WIDTH = ATT_HEADS_PER_GROUP * ATT_HEAD_DIM
ROPE_DIM = ATT_HEAD_DIM // 4
ROPE_THETA = 500000.0
ATT_BLOCK = 128

RET_HEADS = 8
RET_QK_DIM = 128
RET_V_DIM = 256
RET_CHUNK = 128
RET_ROT_BASE = 10000.0
RET_LOG_DECAY = tuple(math.log1p(-(2.0 ** (-5.0 - h))) for h in range(RET_HEADS))

N_EXPERT_GROUPS = 4
EXPERTS_PER_GROUP = 8
N_EXPERTS = N_EXPERT_GROUPS * EXPERTS_PER_GROUP
TOP_K = 2
EXPERT_HIDDEN = 512
MOE_BLOCK = 128

DEPTH = 1
ALPHA = (2.0 * DEPTH) ** 0.25
LN_EPS = 1e-5
NEG_INF = -1e30

COL_BLOCK = 1024
CB_QA, CB_KA, CB_VA = 0, 3, 6
CB_QR, CB_KR, CB_VR, CB_GR = 9, 10, 11, 13
CB_GATE_A, CB_GATE_B = 15, 17

MODE_PLAIN, MODE_ATT_Q, MODE_ATT_K, MODE_RET_Q, MODE_RET_K = 0, 1, 2, 3, 4

NAT_COLS = (CB_QA, CB_KA, CB_VA, CB_QR, CB_KR, CB_VR, CB_VR + 1, CB_GR, CB_GR + 1,
            CB_GATE_A, CB_GATE_A + 1, CB_GATE_B, CB_GATE_B + 1)
NAT_MODES = (MODE_ATT_Q, MODE_ATT_K, MODE_PLAIN, MODE_RET_Q, MODE_RET_K) + (MODE_PLAIN,) * 8
NC_Q0, NC_K0, NC_V0, NC_QR, NC_KR, NC_VR, NC_GR, NC_GATE_A, NC_GATE_B = 0, 1, 2, 3, 4, 5, 7, 9, 11

LANES = 128
ROW_PIECES = D_MODEL // LANES
PACKED_ROW_PIECES = ROW_PIECES // 2
VMEM_LIMIT = 52 * 1024 * 1024
PROJ_ROWS = 1024


def _cparams(n_grid_dims):
    return pltpu.CompilerParams(
        dimension_semantics=("arbitrary",) * n_grid_dims,
        vmem_limit_bytes=VMEM_LIMIT,
    )


def _inproj_kernel(cols_ref, modes_ref, x_ref, w_ref, ta_ref, tr_ref, pa_ref, pr_ref, o_ref,
                   wbf_ref, *acc, d):
    del cols_ref
    j = pl.program_id(0)

    @pl.when(pl.program_id(1) == 0)
    def _():
        wbf_ref[...] = w_ref[...].astype(BF16)

    mode = modes_ref[j]
    tm, tn = x_ref.shape[0], wbf_ref.shape[1]
    pair = 2 * LANES

    def project():
        return jnp.dot(x_ref[...], wbf_ref[...], preferred_element_type=F32)

    def emit(p, piece):
        if d == 1:
            o_ref[:, p * pair:(p + 1) * pair] = piece.astype(o_ref.dtype)
        else:
            acc[0][2 * p] = piece[:, :LANES]
            acc[0][2 * p + 1] = piece[:, LANES:]

    def plain():
        res = project()
        for p in range(tn // pair):
            emit(p, res[:, p * pair:(p + 1) * pair])

    def rotary(tab_ref, perm_ref, scale):
        res = project()
        c = jnp.concatenate([tab_ref[:, :LANES]] * 2, axis=1) * scale
        s = jnp.concatenate([tab_ref[:, LANES:]] * 2, axis=1) * scale
        for p in range(tn // pair):
            a = res[:, p * pair:(p + 1) * pair]
            partner = jnp.dot(a.astype(BF16), perm_ref[...], preferred_element_type=F32)
            emit(p, a * c + partner * s)

    is_att = (mode == MODE_ATT_Q) | (mode == MODE_ATT_K)
    is_ret = (mode == MODE_RET_Q) | (mode == MODE_RET_K)
    pl.when(mode == MODE_PLAIN)(plain)
    pl.when(is_att)(lambda: rotary(ta_ref, pa_ref, jnp.where(mode == MODE_ATT_Q, ATT_HEAD_DIM ** -0.5, 1.0)))
    if d == 1:
        pl.when(is_ret)(lambda: rotary(tr_ref, pr_ref, jnp.where(mode == MODE_RET_K, RET_QK_DIM ** -0.5, 1.0)))
        return

    acc_ref = acc[0]
    for h in range(tn // LANES):
        sl = slice(h * LANES, (h + 1) * LANES)
        if d == 4:
            P = ATT_BLOCK * d
            for sb in range(tm // P):
                for r in range(d):
                    o_ref[sb, r, :, sl] = acc_ref[h, pl.ds(sb * P + r, ATT_BLOCK, stride=d), :].astype(o_ref.dtype)
        else:
            for r in range(d):
                o_ref[r, :, sl] = acc_ref[h, pl.ds(r, tm // d, stride=d), :].astype(o_ref.dtype)


def _rotation_matrix(half, width):
    k = jnp.arange(2 * LANES, dtype=jnp.int32)[:, None]
    l = jnp.arange(2 * LANES, dtype=jnp.int32)[None, :]
    same_head = (k // LANES) == (l // LANES)
    kk, ll = k % LANES, l % LANES
    neg = same_head & (ll < half) & (kk == ll + half)
    pos = same_head & (ll >= half) & (ll < width) & (kk == ll - half)
    return (pos.astype(F32) - neg.astype(F32)).astype(BF16)


def _input_projection(xb, w_in, tabs_a, tabs_r, cols, modes, d):
    S, D = xb.shape
    tm, tn = PROJ_ROWS, COL_BLOCK
    n = len(cols)
    P = ATT_BLOCK * d
    if d == 1:
        out_shape = jax.ShapeDtypeStruct((S, n * tn), BF16)
        out_spec = pl.BlockSpec((tm, tn), lambda j, i, c, m: (i, j))
    elif d == 4:
        out_shape = jax.ShapeDtypeStruct((S // P, d, ATT_BLOCK, n * tn), BF16)
        out_spec = pl.BlockSpec((tm // P, d, ATT_BLOCK, tn), lambda j, i, c, m: (i, 0, 0, j))
    else:
        assert d == 16 and P == 2 * tm
        out_shape = jax.ShapeDtypeStruct((S // P, d, ATT_BLOCK, n * tn), BF16)
        out_spec = pl.BlockSpec((None, d, tm // d, tn), lambda j, i, c, m: (i // 2, 0, i % 2, j))

    def tab_rows(j, i, c, m):
        return (jnp.where(m[j] != MODE_PLAIN, i, 0), 0)

    grid_spec = pltpu.PrefetchScalarGridSpec(
        num_scalar_prefetch=2,
        grid=(n, S // tm),
        in_specs=[
            pl.BlockSpec((tm, D), lambda j, i, c, m: (i, 0)),
            pl.BlockSpec((D, tn), lambda j, i, c, m: (0, c[j])),
            pl.BlockSpec((tm, 2 * LANES), tab_rows),
            pl.BlockSpec((tm, 2 * LANES), tab_rows),
            pl.BlockSpec((2 * LANES, 2 * LANES), lambda j, i, c, m: (0, 0)),
            pl.BlockSpec((2 * LANES, 2 * LANES), lambda j, i, c, m: (0, 0)),
        ],
        out_specs=out_spec,
        scratch_shapes=[pltpu.VMEM((D, tn), BF16)]
        + ([] if d == 1 else [pltpu.VMEM((tn // LANES, tm, LANES), F32)]),
    )
    return pl.pallas_call(
        functools.partial(_inproj_kernel, d=d),
        grid_spec=grid_spec,
        out_shape=out_shape,
        compiler_params=_cparams(2),
        name=f"in_proj_d{d}",
    )(jnp.asarray(cols, jnp.int32), jnp.asarray(modes, jnp.int32), xb, w_in, tabs_a, tabs_r,
      _rotation_matrix(ROPE_DIM // 2, ROPE_DIM), _rotation_matrix(RET_QK_DIM // 2, RET_QK_DIM))


ATT_Q_BLOCKS = 2


def _attn_kernel(q_ref, k_ref, kp_ref, v_ref, vp_ref, o_ref, lse_ref):
    n = pl.program_id(1)
    qi = lax.broadcasted_iota(jnp.int32, (ATT_BLOCK, 2 * ATT_BLOCK), 0)
    kj = lax.broadcasted_iota(jnp.int32, (ATT_BLOCK, 2 * ATT_BLOCK), 1)
    dist = qi + ATT_BLOCK - kj
    band = (dist >= 0) & (dist <= ATT_STEPS)
    lane = lax.broadcasted_iota(jnp.int32, (ATT_BLOCK, LANES), 1)
    for sub in range(ATT_Q_BLOCKS):
        mask = band & ((kj >= ATT_BLOCK) | (n > 0)) if sub == 0 else band
        lse_slab = jnp.zeros((ATT_BLOCK, LANES), F32)
        for h in range(ATT_HEADS_PER_GROUP):
            sl = slice(h * ATT_HEAD_DIM, (h + 1) * ATT_HEAD_DIM)
            k_prev = kp_ref[:, sl] if sub == 0 else k_ref[sub - 1, :, sl]
            v_prev = vp_ref[:, sl] if sub == 0 else v_ref[sub - 1, :, sl]
            k2 = jnp.concatenate([k_prev, k_ref[sub, :, sl]], axis=0)
            v2 = jnp.concatenate([v_prev, v_ref[sub, :, sl]], axis=0)
            s = lax.dot_general(q_ref[sub, :, sl], k2, (((1,), (1,)), ((), ())),
                                preferred_element_type=F32)
            s = jnp.where(mask, s, NEG_INF)
            m = jnp.max(s, axis=1, keepdims=True)
            p = jnp.exp(s - m)
            l = jnp.sum(p, axis=1, keepdims=True)
            o = jnp.dot(p.astype(BF16), v2, preferred_element_type=F32)
            o_ref[sub, :, sl] = o / l
            lse_slab = jnp.where(lane == h, m + jnp.log(l), lse_slab)
        lse_ref[sub] = lse_slab


def _dilated_attention(qkv, g, cq, ck, cv):
    d = DILATIONS[g]
    W = ATT_GROUP_WIDTH
    nb = qkv.shape[0]
    nq = ATT_Q_BLOCKS
    assert qkv.shape[1] == d and nb % nq == 0
    blk = (nq, None, ATT_BLOCK, W)
    pblk = (None, None, ATT_BLOCK, W)
    cur = lambda c: (lambda r, n: (n, r, 0, c))
    prev = lambda c: (lambda r, n: (jnp.maximum(nq * n - 1, 0), r, 0, c))
    return pl.pallas_call(
        _attn_kernel,
        grid=(d, nb // nq),
        in_specs=[
            pl.BlockSpec(blk, cur(cq)),
            pl.BlockSpec(blk, cur(ck)),
            pl.BlockSpec(pblk, prev(ck)),
            pl.BlockSpec(blk, cur(cv)),
            pl.BlockSpec(pblk, prev(cv)),
        ],
        out_specs=[pl.BlockSpec(blk, lambda r, n: (n, r, 0, 0)),
                   pl.BlockSpec((nq, None, ATT_BLOCK, LANES), lambda r, n: (n, r, 0, 0))],
        out_shape=[jax.ShapeDtypeStruct((nb, d, ATT_BLOCK, W), F32),
                   jax.ShapeDtypeStruct((nb, d, ATT_BLOCK, LANES), F32)],
        compiler_params=_cparams(2),
        name=f"dilated_attn_g{g}",
    )(qkv, qkv, qkv, qkv, qkv)


def _retention_kernel(q_ref, k_ref, vlo_ref, vhi_ref, glo_ref, ghi_ref, o_ref, state_ref):
    @pl.when(pl.program_id(0) == 0)
    def _():
        state_ref[...] = jnp.zeros_like(state_ref)

    ri = lax.broadcasted_iota(jnp.int32, (RET_CHUNK, RET_CHUNK), 0)
    ci = lax.broadcasted_iota(jnp.int32, (RET_CHUNK, RET_CHUNK), 1)
    diff = (ri - ci).astype(F32)
    nrow = lax.broadcasted_iota(jnp.int32, (RET_CHUNK, 1), 0).astype(F32)
    half = RET_HEADS // 2
    for h in range(RET_HEADS):
        ld = RET_LOG_DECAY[h]
        sl = slice(h * RET_QK_DIM, (h + 1) * RET_QK_DIM)
        vsl = slice((h % half) * RET_V_DIM, (h % half + 1) * RET_V_DIM)
        v_ref, g_ref = (vlo_ref, glo_ref) if h < half else (vhi_ref, ghi_ref)
        q = q_ref[:, sl]
        k = k_ref[:, sl]
        v = v_ref[:, vsl]
        intra = jnp.where(diff >= 0, jnp.exp(jnp.maximum(diff, 0.0) * ld), 0.0)
        scores = lax.dot_general(q, k, (((1,), (1,)), ((), ())), preferred_element_type=F32) * intra
        inner = jnp.dot(scores.astype(BF16), v, preferred_element_type=F32)
        state = state_ref[h]
        q_dec = jnp.exp((nrow + 1.0) * ld)
        cross = jnp.dot((q.astype(F32) * q_dec).astype(BF16), state.astype(BF16),
                        preferred_element_type=F32)
        k_dec = jnp.exp((RET_CHUNK - 1.0 - nrow) * ld)
        kv = lax.dot_general((k.astype(F32) * k_dec).astype(BF16), v, (((0,), (0,)), ((), ())),
                             preferred_element_type=F32)
        state_ref[h] = math.exp(RET_CHUNK * ld) * state + kv
        o = inner + cross
        mu = jnp.mean(o, axis=1, keepdims=True)
        var = jnp.mean(jnp.square(o - mu), axis=1, keepdims=True)
        o = (o - mu) * lax.rsqrt(var + LN_EPS)
        gate = g_ref[:, vsl].astype(F32)
        gate = gate * (1.0 / (1.0 + jnp.exp(-gate)))
        o_ref[:, h * RET_V_DIM:(h + 1) * RET_V_DIM] = (gate * o).astype(o_ref.dtype)


def _retention(nat, S):
    nc = S // RET_CHUNK
    blk = (RET_CHUNK, COL_BLOCK)

    def col(cb):
        return pl.BlockSpec(blk, lambda c: (c, cb))

    return pl.pallas_call(
        _retention_kernel,
        grid=(nc,),
        in_specs=[col(NC_QR), col(NC_KR), col(NC_VR), col(NC_VR + 1), col(NC_GR), col(NC_GR + 1)],
        out_specs=pl.BlockSpec((RET_CHUNK, RET_HEADS * RET_V_DIM), lambda c: (c, 0)),
        out_shape=jax.ShapeDtypeStruct((S, RET_HEADS * RET_V_DIM), BF16),
        scratch_shapes=[pltpu.VMEM((RET_HEADS, RET_QK_DIM, RET_V_DIM), F32)],
        compiler_params=_cparams(1),
        name="retention",
    )(nat, nat, nat, nat, nat, nat)


MERGE_ROWS = 256
MERGE_CHAIN_ROWS = 256


def _layer_norm(z, scale, bias):
    mu = jnp.mean(z, axis=1, keepdims=True)
    var = jnp.mean(jnp.square(z - mu), axis=1, keepdims=True)
    return (z - mu) * lax.rsqrt(var + LN_EPS) * scale + bias


def _sigmoid(x):
    return 1.0 / (1.0 + jnp.exp(-x))


def _pack_bf16_pair(a, b):
    abits = lax.bitcast_convert_type(a.astype(BF16).astype(F32), jnp.uint32)
    bbits = lax.bitcast_convert_type(b.astype(BF16).astype(F32), jnp.uint32)
    return (abits >> 16) | (bbits & jnp.uint32(0xFFFF0000))


def _unpack_bf16_pair(w):
    a = lax.bitcast_convert_type(w << 16, F32).astype(BF16)
    b = lax.bitcast_convert_type(w & jnp.uint32(0xFFFF0000), F32).astype(BF16)
    return a, b


def _split_bf16(v):
    hi = v.astype(BF16)
    return hi, (v - hi.astype(F32)).astype(BF16)


def _merge_kernel(o0_ref, o1_ref, o2_ref, l0_ref, l1_ref, l2_ref, or_ref,
                  ga0_ref, ga1_ref, gb0_ref, gb1_ref, x_ref,
                  wa_ref, wr_ref, wo_ref, lns_ref, lnb_ref, wrt_ref, brt_ref, hexp_ref,
                  x1_ref, x1p_ref, route_ref, cnt_ref, s1_ref, s2_ref, ls1_ref, ls2_ref):
    tm = o0_ref.shape[0]
    for d, src, lsrc, dst, ldst in ((DILATIONS[1], o1_ref, l1_ref, s1_ref, ls1_ref),
                                    (DILATIONS[2], o2_ref, l2_ref, s2_ref, ls2_ref)):
        for r in range(d):
            for h in range(ATT_HEADS_PER_GROUP):
                dst[h, pl.ds(r, tm // d, stride=d), :] = src[r, :, h * ATT_HEAD_DIM:(h + 1) * ATT_HEAD_DIM]
            ldst[pl.ds(r, tm // d, stride=d), :] = lsrc[r]

    @pl.when(pl.program_id(0) == 0)
    def _():
        cnt_ref[...] = jnp.zeros_like(cnt_ref)

    for c in range(tm // MERGE_CHAIN_ROWS):
        _merge_rows(slice(c * MERGE_CHAIN_ROWS, (c + 1) * MERGE_CHAIN_ROWS),
                    o0_ref, l0_ref, or_ref, ga0_ref, ga1_ref, gb0_ref, gb1_ref, x_ref,
                    wa_ref, wr_ref, wo_ref, lns_ref, lnb_ref, wrt_ref, brt_ref, hexp_ref,
                    x1_ref, x1p_ref, route_ref, cnt_ref, s1_ref, s2_ref, ls1_ref, ls2_ref)


def _merge_rows(rs, o0_ref, l0_ref, or_ref, ga0_ref, ga1_ref, gb0_ref, gb1_ref, x_ref,
                wa_ref, wr_ref, wo_ref, lns_ref, lnb_ref, wrt_ref, brt_ref, hexp_ref,
                x1_ref, x1p_ref, route_ref, cnt_ref, s1_ref, s2_ref, ls1_ref, ls2_ref):
    rc = rs.stop - rs.start
    l0, l1, l2 = l0_ref[rs, :], ls1_ref[rs, :], ls2_ref[rs, :]
    lm = jnp.maximum(jnp.maximum(l0, l1), l2)
    e0, e1, e2 = jnp.exp(l0 - lm), jnp.exp(l1 - lm), jnp.exp(l2 - lm)
    den = e0 + e1 + e2

    def lanes_of_head(w):
        hi, lo = _split_bf16(w)
        return (jnp.dot(hi, hexp_ref[...], preferred_element_type=F32)
                + jnp.dot(lo, hexp_ref[...], preferred_element_type=F32))

    half = D_MODEL // 2
    halves = (slice(0, half), slice(half, D_MODEL))
    o_r = or_ref[rs, :]
    y_r = [jnp.dot(o_r, wr_ref[:, cs], preferred_element_type=F32) for cs in halves]
    heads = range(ATT_HEADS_PER_GROUP)
    o1 = jnp.concatenate([s1_ref[h, rs, :] for h in heads], axis=1)
    o2 = jnp.concatenate([s2_ref[h, rs, :] for h in heads], axis=1)
    o_a = (lanes_of_head(e0 / den) * o0_ref[rs, :] + lanes_of_head(e1 / den) * o1
           + lanes_of_head(e2 / den) * o2).astype(BF16)
    merged = []
    for n, cs in enumerate(halves):
        y_a = jnp.dot(o_a, wa_ref[:, cs], preferred_element_type=F32)
        gate_a = (ga0_ref, ga1_ref)[n][rs, :].astype(F32)
        gate_b = (gb0_ref, gb1_ref)[n][rs, :].astype(F32)
        merged.append((_sigmoid(gate_a) * y_a + _sigmoid(gate_b) * y_r[n]).astype(BF16))
    z = [ALPHA * x_ref[rs, cs]
         + jnp.dot(merged[0], wo_ref[:half, cs], preferred_element_type=F32)
         + jnp.dot(merged[1], wo_ref[half:, cs], preferred_element_type=F32) for cs in halves]
    mu = (jnp.sum(z[0], axis=1, keepdims=True) + jnp.sum(z[1], axis=1, keepdims=True)) / D_MODEL
    var = (jnp.sum(jnp.square(z[0] - mu), axis=1, keepdims=True)
           + jnp.sum(jnp.square(z[1] - mu), axis=1, keepdims=True)) / D_MODEL
    inv = lax.rsqrt(var + LN_EPS)
    x1 = [(z[n] - mu) * inv * lns_ref[:, cs] + lnb_ref[:, cs] for n, cs in enumerate(halves)]
    for n, cs in enumerate(halves):
        x1_ref[rs, cs] = x1[n]
    packed = _pack_bf16_pair(x1[0], x1[1])
    for c in range(PACKED_ROW_PIECES):
        x1p_ref[pl.ds(rs.start * PACKED_ROW_PIECES + c, rc, stride=PACKED_ROW_PIECES), :] = (
            packed[:, c * LANES:(c + 1) * LANES])

    logits = brt_ref[...]
    for n, cs in enumerate(halves):
        x_hi, x_lo = _split_bf16(x1[n])
        hl = jnp.dot(x_hi, wrt_ref[cs, :], preferred_element_type=F32)
        logits = (logits + hl[:, :LANES] + hl[:, LANES:]
                  + jnp.dot(x_lo, wrt_ref[cs, :LANES], preferred_element_type=F32))
    lane = lax.broadcasted_iota(jnp.int32, logits.shape, 1)
    is_g = lane < N_EXPERT_GROUPS
    gl = jnp.where(is_g, logits, NEG_INF)
    gmax = jnp.max(gl, axis=1, keepdims=True)
    g_sel = jnp.min(jnp.where(is_g & (gl == gmax), lane, LANES), axis=1, keepdims=True)
    p_group = 1.0 / jnp.sum(jnp.where(is_g, jnp.exp(gl - gmax), 0.0), axis=1, keepdims=True)
    lo = N_EXPERT_GROUPS + EXPERTS_PER_GROUP * g_sel
    in_grp = (lane >= lo) & (lane < lo + EXPERTS_PER_GROUP)
    el = jnp.where(in_grp, logits, NEG_INF)
    top1 = jnp.max(el, axis=1, keepdims=True)
    idx1 = jnp.min(jnp.where(in_grp & (el == top1), lane, LANES), axis=1, keepdims=True)
    el2 = jnp.where(lane == idx1, NEG_INF, el)
    top2 = jnp.max(el2, axis=1, keepdims=True)
    idx2 = jnp.min(jnp.where(in_grp & (lane != idx1) & (el2 == top2), lane, LANES), axis=1, keepdims=True)
    t = jnp.exp(top2 - top1)
    wt1 = p_group * (1.0 / (1.0 + t))
    wt2 = p_group * (t / (1.0 + t))
    slab = jnp.where(lane == 0, (idx1 - N_EXPERT_GROUPS).astype(F32), 0.0)
    slab = jnp.where(lane == 1, (idx2 - N_EXPERT_GROUPS).astype(F32), slab)
    slab = jnp.where(lane == 2, wt1, slab)
    slab = jnp.where(lane == 3, wt2, slab)

    hot = jnp.where((lane == idx1) | (lane == idx2), 1.0, 0.0)
    ri = lax.broadcasted_iota(jnp.int32, (rc, rc), 0)
    ci = lax.broadcasted_iota(jnp.int32, (rc, rc), 1)
    ltri = jnp.where(ri > ci, 1.0, 0.0).astype(BF16)
    prefix = jnp.dot(ltri, hot.astype(BF16), preferred_element_type=F32) + cnt_ref[...]
    rank1 = jnp.sum(jnp.where(lane == idx1, prefix, 0.0), axis=1, keepdims=True)
    rank2 = jnp.sum(jnp.where(lane == idx2, prefix, 0.0), axis=1, keepdims=True)
    cnt_ref[...] = cnt_ref[...] + jnp.sum(hot, axis=0, keepdims=True)
    slab = jnp.where(lane == 4, rank1, slab)
    slab = jnp.where(lane == 5, rank2, slab)
    route_ref[rs, :] = slab


def _merge(o_g, lse_g, o_r, nat, x2d, wa, wr, wo, lns, lnb, wrt, brt, S):
    tm = MERGE_ROWS
    D = D_MODEL
    W = ATT_GROUP_WIDTH
    head_expand = (jnp.arange(W, dtype=jnp.int32)[None, :] // ATT_HEAD_DIM
                   == jnp.arange(LANES, dtype=jnp.int32)[:, None]).astype(BF16)

    def rows(width):
        return pl.BlockSpec((tm, width), lambda i: (i, 0))

    def streams(g, width):
        d = DILATIONS[g]
        per = (ATT_BLOCK * d) // tm
        return pl.BlockSpec((None, d, tm // d, width), lambda i: (i // per, 0, i % per, 0))

    def ncol(cb):
        return pl.BlockSpec((tm, COL_BLOCK), lambda i: (i, cb))

    def const(shape):
        return pl.BlockSpec(shape, lambda i: (0, 0), pipeline_mode=pl.Buffered(1))

    return pl.pallas_call(
        _merge_kernel,
        grid=(S // tm,),
        in_specs=[
            rows(W), streams(1, W), streams(2, W),
            rows(LANES), streams(1, LANES), streams(2, LANES),
            rows(RET_HEADS * RET_V_DIM),
            ncol(NC_GATE_A), ncol(NC_GATE_A + 1), ncol(NC_GATE_B), ncol(NC_GATE_B + 1),
            rows(D),
            const((W, D)), const((RET_HEADS * RET_V_DIM, D)), const((D, D)),
            const((1, D)), const((1, D)), const((D, 2 * LANES)), const((1, LANES)), const((LANES, W)),
        ],
        out_specs=[rows(D), pl.BlockSpec((tm * PACKED_ROW_PIECES, LANES), lambda i: (i, 0)),
                   rows(LANES), pl.BlockSpec((1, LANES), lambda i: (0, 0))],
        out_shape=[jax.ShapeDtypeStruct((S, D), F32),
                   jax.ShapeDtypeStruct((S * PACKED_ROW_PIECES, LANES), jnp.uint32),
                   jax.ShapeDtypeStruct((S, LANES), F32),
                   jax.ShapeDtypeStruct((1, LANES), F32)],
        scratch_shapes=[pltpu.VMEM((ATT_HEADS_PER_GROUP, tm, ATT_HEAD_DIM), F32),
                        pltpu.VMEM((ATT_HEADS_PER_GROUP, tm, ATT_HEAD_DIM), F32),
                        pltpu.VMEM((tm, LANES), F32), pltpu.VMEM((tm, LANES), F32)],
        compiler_params=_cparams(1),
        name="merge_ln_router",
    )(*o_g, *lse_g, o_r, nat, nat, nat, nat, x2d, wa, wr, wo, lns, lnb, wrt, brt, head_expand)


META_N_USED, META_NEXT = 0, 1
WEIGHT_DMA_PRIORITY = 1


def _plan_kernel(dest_ref, cnt_ref, rt_ref, be_ref, meta_ref):
    n_blocks = be_ref.shape[0]

    def pad_row(r, carry):
        rt_ref[r] = 0
        return carry

    def per_expert(e, start):
        nb = lax.shift_right_logical(cnt_ref[e] + (MOE_BLOCK - 1), 7)

        def fill(b, carry):
            be_ref[lax.shift_right_logical(start, 7) + b] = e
            return carry
        lax.fori_loop(0, nb, fill, 0)
        lax.fori_loop(start + cnt_ref[e], start + nb * MOE_BLOCK, pad_row, 0)
        return start + nb * MOE_BLOCK
    total = lax.fori_loop(0, N_EXPERTS, per_expert, 0)
    n_used = lax.shift_right_logical(total, 7)
    meta_ref[META_N_USED] = n_used

    def tail(b, carry):
        be_ref[b] = be_ref[jnp.maximum(n_used - 1, 0)]
        return carry
    lax.fori_loop(n_used, n_blocks, tail, 0)
    lax.fori_loop(total, rt_ref.shape[0], pad_row, 0)

    def nxt(k, cur):
        e = N_EXPERTS - 1 - k
        meta_ref[META_NEXT + e] = cur
        return jnp.where(cnt_ref[e] > 0, e, cur)
    lax.fori_loop(0, N_EXPERTS, nxt, N_EXPERTS)

    def assign(a, carry):
        rt_ref[dest_ref[a]] = lax.shift_right_logical(a, 1)
        return carry
    lax.fori_loop(0, dest_ref.shape[0], assign, 0, unroll=8)


def _plan(dest, counts):
    A = dest.shape[0]
    n_blocks = A // MOE_BLOCK + N_EXPERTS
    smem = pl.BlockSpec(memory_space=pltpu.SMEM)
    return pl.pallas_call(
        _plan_kernel,
        in_specs=[smem, smem],
        out_specs=[smem, smem, smem],
        out_shape=[
            jax.ShapeDtypeStruct((n_blocks * MOE_BLOCK,), jnp.int32),
            jax.ShapeDtypeStruct((n_blocks,), jnp.int32),
            jax.ShapeDtypeStruct((META_NEXT + N_EXPERTS,), jnp.int32),
        ],
        name="moe_plan",
    )(dest, counts)


def _expert_kernel(be_ref, meta_ref, rt_ref, x_hbm, wg_hbm, wu_hbm, wd_hbm, y_ref,
                   xg_ref, wgf_ref, wuf_ref, wdf_ref, wgb_ref, wub_ref, wdb_ref, xsem, sem, slot_ref):
    b = pl.program_id(0)
    n_used = meta_ref[META_N_USED]
    e = be_ref[b]
    first = (b == 0) | (e != be_ref[jnp.maximum(b - 1, 0)])

    xp = PACKED_ROW_PIECES

    def gather(blk, slot):
        def body(i, carry):
            tok = rt_ref[blk * MOE_BLOCK + i]
            pltpu.make_async_copy(x_hbm.at[pl.ds(pl.multiple_of(tok * xp, xp), xp)],
                                  xg_ref.at[slot, pl.ds(pl.multiple_of(i * xp, xp), xp)],
                                  xsem.at[slot]).start()
            return carry
        lax.fori_loop(0, MOE_BLOCK, body, 0, unroll=8)

    @pl.when((b == 0) & (n_used > 0))
    def _():
        gather(0, 0)

    @pl.when(b + 1 < n_used)
    def _():
        gather(b + 1, (b + 1) % 2)

    def copies(ex, s):
        return (pltpu.make_async_copy(wg_hbm.at[ex], wgf_ref.at[s], sem.at[s]),
                pltpu.make_async_copy(wu_hbm.at[ex], wuf_ref.at[s], sem.at[s]),
                pltpu.make_async_copy(wd_hbm.at[ex], wdf_ref.at[s], sem.at[s]))

    @pl.when((b == 0) & (n_used > 0))
    def _():
        slot_ref[0] = 0
        for c in copies(e, 0):
            c.start(priority=WEIGHT_DMA_PRIORITY)

    @pl.when(first & (b < n_used))
    def _():
        s = slot_ref[0]
        nx = meta_ref[META_NEXT + e]

        @pl.when(nx < N_EXPERTS)
        def _():
            for c in copies(nx, 1 - s):
                c.start(priority=WEIGHT_DMA_PRIORITY)
        for c in copies(e, s):
            c.wait()
        wgb_ref[...] = wgf_ref[s].astype(BF16)
        wub_ref[...] = wuf_ref[s].astype(BF16)
        wdb_ref[...] = wdf_ref[s].astype(BF16)
        slot_ref[0] = 1 - s

    @pl.when(b < n_used)
    def _():
        slot = b % 2
        pltpu.make_async_copy(x_hbm.at[pl.ds(0, MOE_BLOCK * xp)], xg_ref.at[slot], xsem.at[slot]).wait()
        pieces = [_unpack_bf16_pair(xg_ref[slot, pl.ds(c, MOE_BLOCK, stride=xp), :]) for c in range(xp)]
        xb = jnp.concatenate([p[0] for p in pieces] + [p[1] for p in pieces], axis=1)
        hg = jnp.dot(xb, wgb_ref[...], preferred_element_type=F32)
        hu = jnp.dot(xb, wub_ref[...], preferred_element_type=F32)
        hid = (hg * _sigmoid(hg) * hu).astype(BF16)
        y = jnp.dot(hid, wdb_ref[...], preferred_element_type=F32)
        for c in range(ROW_PIECES):
            y_ref[pl.ds(c, MOE_BLOCK, stride=ROW_PIECES), :] = y[:, c * LANES:(c + 1) * LANES]

    @pl.when(b >= n_used)
    def _():
        y_ref[...] = jnp.zeros_like(y_ref)


def _experts(x1p, block_expert, meta, row_tok, w_gate, w_up, w_down):
    D = D_MODEL
    assert x1p.shape[1] == LANES and x1p.dtype == jnp.uint32
    n_blocks = block_expert.shape[0]
    R = n_blocks * MOE_BLOCK
    H = EXPERT_HIDDEN
    grid_spec = pltpu.PrefetchScalarGridSpec(
        num_scalar_prefetch=3,
        grid=(n_blocks,),
        in_specs=[
            pl.BlockSpec(memory_space=pl.ANY),
            pl.BlockSpec(memory_space=pl.ANY),
            pl.BlockSpec(memory_space=pl.ANY),
            pl.BlockSpec(memory_space=pl.ANY),
        ],
        out_specs=pl.BlockSpec((MOE_BLOCK * ROW_PIECES, LANES), lambda b, be, meta, rt: (b, 0)),
        scratch_shapes=[
            pltpu.VMEM((2, MOE_BLOCK * PACKED_ROW_PIECES, LANES), jnp.uint32),
            pltpu.VMEM((2, D, H), F32),
            pltpu.VMEM((2, D, H), F32),
            pltpu.VMEM((2, H, D), F32),
            pltpu.VMEM((D, H), BF16),
            pltpu.VMEM((D, H), BF16),
            pltpu.VMEM((H, D), BF16),
            pltpu.SemaphoreType.DMA((2,)),
            pltpu.SemaphoreType.DMA((2,)),
            pltpu.SMEM((1,), jnp.int32),
        ],
    )
    return pl.pallas_call(
        _expert_kernel,
        grid_spec=grid_spec,
        out_shape=jax.ShapeDtypeStruct((R * ROW_PIECES, LANES), F32),
        compiler_params=_cparams(1),
        name="moe_experts",
    )(block_expert, meta, row_tok, x1p, w_gate, w_up, w_down)


COMBINE_TOKENS = 64


def _combine_kernel(dest_ref, y_hbm, x1_ref, route_ref, lns_ref, lnb_ref, o_ref, yg_ref, sem):
    i = pl.program_id(0)
    tt = COMBINE_TOKENS
    n = dest_ref.shape[0] // (TOP_K * tt)

    rp = ROW_PIECES

    def issue(step, slot):
        def body(r, carry):
            for k in range(TOP_K):
                row = dest_ref[(step * tt + r) * TOP_K + k]
                pltpu.make_async_copy(y_hbm.at[pl.ds(pl.multiple_of(row * rp, rp), rp)],
                                      yg_ref.at[slot, k, pl.ds(pl.multiple_of(r * rp, rp), rp)],
                                      sem.at[slot]).start(priority=k)
            return carry
        lax.fori_loop(0, tt, body, 0, unroll=4)

    @pl.when(i == 0)
    def _():
        issue(0, 0)

    @pl.when(i + 1 < n)
    def _():
        issue(i + 1, (i + 1) % 2)

    slot = i % 2
    for k in range(TOP_K):
        pltpu.make_async_copy(y_hbm.at[pl.ds(0, tt * rp)], yg_ref.at[slot, k], sem.at[slot]).wait()
    route = route_ref[...]
    w1, w2 = route[:, 2:3], route[:, 3:4]
    moe = jnp.concatenate(
        [w1 * yg_ref[slot, 0, pl.ds(c, tt, stride=rp), :] + w2 * yg_ref[slot, 1, pl.ds(c, tt, stride=rp), :]
         for c in range(rp)], axis=1)
    o_ref[...] = _layer_norm(ALPHA * x1_ref[...] + moe, lns_ref[...], lnb_ref[...])


def _combine(dest, y, x1, route, lns, lnb):
    T, D = x1.shape
    tt = COMBINE_TOKENS
    grid_spec = pltpu.PrefetchScalarGridSpec(
        num_scalar_prefetch=1,
        grid=(T // tt,),
        in_specs=[
            pl.BlockSpec(memory_space=pl.ANY),
            pl.BlockSpec((tt, D), lambda i, d: (i, 0)),
            pl.BlockSpec((tt, LANES), lambda i, d: (i, 0)),
            pl.BlockSpec((1, D), lambda i, d: (0, 0)),
            pl.BlockSpec((1, D), lambda i, d: (0, 0)),
        ],
        out_specs=pl.BlockSpec((tt, D), lambda i, d: (i, 0)),
        scratch_shapes=[
            pltpu.VMEM((2, TOP_K, tt * ROW_PIECES, LANES), F32),
            pltpu.SemaphoreType.DMA((2,)),
        ],
    )
    return pl.pallas_call(
        _combine_kernel,
        grid_spec=grid_spec,
        out_shape=jax.ShapeDtypeStruct((T, D), F32),
        compiler_params=_cparams(1),
        name="moe_combine_ln",
    )(dest, y, x1, route, lns, lnb)


def _rotary_tables(positions):
    pos = positions.reshape(-1).astype(F32)
    S = pos.shape[0]
    inv_a = ROPE_THETA ** (-jnp.arange(0, ROPE_DIM, 2, dtype=F32) / ROPE_DIM)
    inv_r = RET_ROT_BASE ** (-jnp.linspace(0.0, 1.0, RET_QK_DIM // 2, dtype=F32))
    ang = jnp.concatenate([inv_a, inv_r])[:, None] * pos[None, :]
    cos_t, sin_t = lax.optimization_barrier((jnp.cos(ang), jnp.sin(ang)))
    na = ROPE_DIM // 2
    cos, sin, cr, sr = cos_t[:na].T, sin_t[:na].T, cos_t[na:].T, sin_t[na:].T
    pad1 = jnp.ones((S, LANES - ROPE_DIM), F32)
    pad0 = jnp.zeros((S, LANES - ROPE_DIM), F32)
    tabs_a = jnp.concatenate([cos, cos, pad1, sin, sin, pad0], axis=1)
    tabs_r = jnp.concatenate([cr, cr, sr, sr], axis=1)
    return tabs_a, tabs_r


def kernel(x, positions, w_in, w_branch_attn, w_branch_ret, w_out, ln1_scale, ln1_bias, w_group_router, b_group_router, w_expert_router, b_expert_router, w_expert_gate, w_expert_up, w_expert_down, ln2_scale, ln2_bias):
    B, S, D = x.shape
    assert B == 1 and D == D_MODEL and w_in.shape[0] == DEPTH == 1
    assert S % (ATT_BLOCK * DILATIONS[2]) == 0
    x2d = x.reshape(S, D)
    xb = x2d.astype(BF16)
    tabs_a, tabs_r = _rotary_tables(positions)

    nat = _input_projection(xb, w_in[0], tabs_a, tabs_r, NAT_COLS, NAT_MODES, 1)
    qkv_modes = (MODE_ATT_Q, MODE_ATT_K, MODE_PLAIN)
    qkv = [None] + [
        _input_projection(xb, w_in[0], tabs_a, tabs_r, (CB_QA + g, CB_KA + g, CB_VA + g), qkv_modes,
                          DILATIONS[g])
        for g in (1, 2)]
    o0, l0 = _dilated_attention(nat.reshape(S // ATT_BLOCK, 1, ATT_BLOCK, nat.shape[1]), 0,
                                NC_Q0, NC_K0, NC_V0)
    att = [(o0.reshape(S, ATT_GROUP_WIDTH), l0.reshape(S, LANES))]
    att += [_dilated_attention(qkv[g], g, 0, 1, 2) for g in (1, 2)]
    o_r = _retention(nat, S)

    w_route = jnp.concatenate(
        [w_group_router[0], w_expert_router[0].transpose(1, 0, 2).reshape(D, N_EXPERTS),
         jnp.zeros((D, LANES - N_EXPERT_GROUPS - N_EXPERTS), F32)], axis=1)
    b_route = jnp.concatenate(
        [b_group_router[0], b_expert_router[0].reshape(N_EXPERTS),
         jnp.zeros((LANES - N_EXPERT_GROUPS - N_EXPERTS,), F32)]).reshape(1, LANES)
    x1, x1p, route, cnt = _merge(
        [a[0] for a in att], [a[1] for a in att], o_r, nat, x2d,
        w_branch_attn[0].astype(BF16), w_branch_ret[0].astype(BF16), w_out[0].astype(BF16),
        ln1_scale[0].reshape(1, D), ln1_bias[0].reshape(1, D),
        jnp.concatenate(_split_bf16(w_route), axis=1), b_route, S)

    e_flat = route[:, 0:TOP_K].astype(jnp.int32).reshape(S * TOP_K)
    rank_flat = route[:, 4:4 + TOP_K].astype(jnp.int32).reshape(S * TOP_K)
    counts = cnt[0, N_EXPERT_GROUPS:N_EXPERT_GROUPS + N_EXPERTS].astype(jnp.int32)
    padded = (counts + (MOE_BLOCK - 1)) // MOE_BLOCK * MOE_BLOCK
    pstart = jnp.cumsum(padded) - padded
    hot = e_flat[:, None] == jnp.arange(N_EXPERTS, dtype=jnp.int32)[None, :]
    dest = jnp.sum(jnp.where(hot, pstart[None, :], 0), axis=1) + rank_flat
    row_tok, block_expert, meta = _plan(dest, counts)
    y = _experts(x1p, block_expert, meta, row_tok, w_expert_gate[0], w_expert_up[0], w_expert_down[0])
    out = _combine(dest, y, x1, route, ln2_scale[0].reshape(1, D), ln2_bias[0].reshape(1, D))
    return out.reshape(B, S, D)
```

```python
import functools
import math

import jax
import jax.numpy as jnp
from jax import lax
from jax.experimental import pallas as pl
from jax.experimental.pallas import tpu as pltpu

F32 = jnp.float32
BF16 = jnp.bfloat16

D_MODEL = 2048
ATT_HEAD_DIM = 128
ATT_HEADS_PER_GROUP = 8
DILATIONS = (1, 4, 16)
ATT_STEPS = 128
N_ATT_GROUPS = 3
ATT_GROUP_WIDTH = ATT_HEADS_PER_GROUP * ATT_HEAD_DIM
ROPE_DIM = ATT_HEAD_DIM // 4
ROPE_THETA = 500000.0
ATT_BLOCK = 128

RET_HEADS = 8
RET_QK_DIM = 128
RET_V_DIM = 256
RET_CHUNK = 128
RET_ROT_BASE = 10000.0
RET_LOG_DECAY = tuple(math.log1p(-(2.0 ** (-5.0 - h))) for h in range(RET_HEADS))

N_EXPERT_GROUPS = 4
EXPERTS_PER_GROUP = 8
N_EXPERTS = N_EXPERT_GROUPS * EXPERTS_PER_GROUP
TOP_K = 2
EXPERT_HIDDEN = 512
MOE_BLOCK = 128

DEPTH = 1
ALPHA = (2.0 * DEPTH) ** 0.25
LN_EPS = 1e-5
NEG_INF = -1e30

COL_BLOCK = 1024
CB_QA, CB_KA, CB_VA = 0, 3, 6
CB_QR, CB_KR, CB_VR, CB_GR = 9, 10, 11, 13
CB_GATE_A, CB_GATE_B = 15, 17

MODE_PLAIN, MODE_ATT_Q, MODE_ATT_K, MODE_RET_Q, MODE_RET_K = 0, 1, 2, 3, 4

NAT_COLS = (CB_QA, CB_KA, CB_VA, CB_QR, CB_KR, CB_VR, CB_VR + 1, CB_GR, CB_GR + 1,
            CB_GATE_A, CB_GATE_A + 1, CB_GATE_B, CB_GATE_B + 1)
NAT_MODES = (MODE_ATT_Q, MODE_ATT_K, MODE_PLAIN, MODE_RET_Q, MODE_RET_K) + (MODE_PLAIN,) * 8
NC_Q0, NC_K0, NC_V0, NC_QR, NC_KR, NC_VR, NC_GR, NC_GATE_A, NC_GATE_B = 0, 1, 2, 3, 4, 5, 7, 9, 11

LANES = 128
ROW_PIECES = D_MODEL // LANES
PACKED_ROW_PIECES = ROW_PIECES // 2
VMEM_LIMIT = 52 * 1024 * 1024
PROJ_ROWS = 1024


def _cparams(n_grid_dims):
    return pltpu.CompilerParams(
        dimension_semantics=("arbitrary",) * n_grid_dims,
        vmem_limit_bytes=VMEM_LIMIT,
    )


def _inproj_kernel(cols_ref, modes_ref, x_ref, w_ref, ta_ref, tr_ref, pa_ref, pr_ref, o_ref,
                   wbf_ref, *acc, d):
    del cols_ref
    j = pl.program_id(0)

    @pl.when(pl.program_id(1) == 0)
    def _():
        wbf_ref[...] = w_ref[...].astype(BF16)

    mode = modes_ref[j]
    tm, tn = x_ref.shape[0], wbf_ref.shape[1]
    pair = 2 * LANES

    def project():
        return jnp.dot(x_ref[...], wbf_ref[...], preferred_element_type=F32)

    def emit(p, piece):
        if d == 1:
            o_ref[:, p * pair:(p + 1) * pair] = piece.astype(o_ref.dtype)
        else:
            acc[0][2 * p] = piece[:, :LANES]
            acc[0][2 * p + 1] = piece[:, LANES:]

    def plain():
        res = project()
        for p in range(tn // pair):
            emit(p, res[:, p * pair:(p + 1) * pair])

    def rotary(tab_ref, perm_ref, scale):
        res = project()
        c = jnp.concatenate([tab_ref[:, :LANES]] * 2, axis=1) * scale
        s = jnp.concatenate([tab_ref[:, LANES:]] * 2, axis=1) * scale
        for p in range(tn // pair):
            a = res[:, p * pair:(p + 1) * pair]
            partner = jnp.dot(a.astype(BF16), perm_ref[...], preferred_element_type=F32)
            emit(p, a * c + partner * s)

    is_att = (mode == MODE_ATT_Q) | (mode == MODE_ATT_K)
    is_ret = (mode == MODE_RET_Q) | (mode == MODE_RET_K)
    pl.when(mode == MODE_PLAIN)(plain)
    pl.when(is_att)(lambda: rotary(ta_ref, pa_ref, jnp.where(mode == MODE_ATT_Q, ATT_HEAD_DIM ** -0.5, 1.0)))
    if d == 1:
        pl.when(is_ret)(lambda: rotary(tr_ref, pr_ref, jnp.where(mode == MODE_RET_K, RET_QK_DIM ** -0.5, 1.0)))
        return

    acc_ref = acc[0]
    for h in range(tn // LANES):
        sl = slice(h * LANES, (h + 1) * LANES)
        if d == 4:
            P = ATT_BLOCK * d
            for sb in range(tm // P):
                for r in range(d):
                    o_ref[sb, r, :, sl] = acc_ref[h, pl.ds(sb * P + r, ATT_BLOCK, stride=d), :].astype(o_ref.dtype)
        else:
            for r in range(d):
                o_ref[r, :, sl] = acc_ref[h, pl.ds(r, tm // d, stride=d), :].astype(o_ref.dtype)


def _rotation_matrix(half, width):
    k = jnp.arange(2 * LANES, dtype=jnp.int32)[:, None]
    l = jnp.arange(2 * LANES, dtype=jnp.int32)[None, :]
    same_head = (k // LANES) == (l // LANES)
    kk, ll = k % LANES, l % LANES
    neg = same_head & (ll < half) & (kk == ll + half)
    pos = same_head & (ll >= half) & (ll < width) & (kk == ll - half)
    return (pos.astype(F32) - neg.astype(F32)).astype(BF16)


def _input_projection(xb, w_in, tabs_a, tabs_r, cols, modes, d):
    S, D = xb.shape
    tm, tn = PROJ_ROWS, COL_BLOCK
    n = len(cols)
    P = ATT_BLOCK * d
    if d == 1:
        out_shape = jax.ShapeDtypeStruct((S, n * tn), BF16)
        out_spec = pl.BlockSpec((tm, tn), lambda j, i, c, m: (i, j))
    elif d == 4:
        out_shape = jax.ShapeDtypeStruct((S // P, d, ATT_BLOCK, n * tn), BF16)
        out_spec = pl.BlockSpec((tm // P, d, ATT_BLOCK, tn), lambda j, i, c, m: (i, 0, 0, j))
    else:
        assert d == 16 and P == 2 * tm
        out_shape = jax.ShapeDtypeStruct((S // P, d, ATT_BLOCK, n * tn), BF16)
        out_spec = pl.BlockSpec((None, d, tm // d, tn), lambda j, i, c, m: (i // 2, 0, i % 2, j))

    def tab_rows(j, i, c, m):
        return (jnp.where(m[j] != MODE_PLAIN, i, 0), 0)

    grid_spec = pltpu.PrefetchScalarGridSpec(
        num_scalar_prefetch=2,
        grid=(n, S // tm),
        in_specs=[
            pl.BlockSpec((tm, D), lambda j, i, c, m: (i, 0)),
            pl.BlockSpec((D, tn), lambda j, i, c, m: (0, c[j])),
            pl.BlockSpec((tm, 2 * LANES), tab_rows),
            pl.BlockSpec((tm, 2 * LANES), tab_rows),
            pl.BlockSpec((2 * LANES, 2 * LANES), lambda j, i, c, m: (0, 0)),
            pl.BlockSpec((2 * LANES, 2 * LANES), lambda j, i, c, m: (0, 0)),
        ],
        out_specs=out_spec,
        scratch_shapes=[pltpu.VMEM((D, tn), BF16)]
        + ([] if d == 1 else [pltpu.VMEM((tn // LANES, tm, LANES), F32)]),
    )
    return pl.pallas_call(
        functools.partial(_inproj_kernel, d=d),
        grid_spec=grid_spec,
        out_shape=out_shape,
        compiler_params=_cparams(2),
        name=f"in_proj_d{d}",
    )(jnp.asarray(cols, jnp.int32), jnp.asarray(modes, jnp.int32), xb, w_in, tabs_a, tabs_r,
      _rotation_matrix(ROPE_DIM // 2, ROPE_DIM), _rotation_matrix(RET_QK_DIM // 2, RET_QK_DIM))


ATT_Q_BLOCKS = 2


def _attn_kernel(q_ref, k_ref, kp_ref, v_ref, vp_ref, o_ref, lse_ref):
    n = pl.program_id(1)
    qi = lax.broadcasted_iota(jnp.int32, (ATT_BLOCK, 2 * ATT_BLOCK), 0)
    kj = lax.broadcasted_iota(jnp.int32, (ATT_BLOCK, 2 * ATT_BLOCK), 1)
    dist = qi + ATT_BLOCK - kj
    band = (dist >= 0) & (dist <= ATT_STEPS)
    lane = lax.broadcasted_iota(jnp.int32, (ATT_BLOCK, LANES), 1)
    for sub in range(ATT_Q_BLOCKS):
        mask = band & ((kj >= ATT_BLOCK) | (n > 0)) if sub == 0 else band
        lse_slab = jnp.zeros((ATT_BLOCK, LANES), F32)
        for h in range(ATT_HEADS_PER_GROUP):
            sl = slice(h * ATT_HEAD_DIM, (h + 1) * ATT_HEAD_DIM)
            k_prev = kp_ref[:, sl] if sub == 0 else k_ref[sub - 1, :, sl]
            v_prev = vp_ref[:, sl] if sub == 0 else v_ref[sub - 1, :, sl]
            k2 = jnp.concatenate([k_prev, k_ref[sub, :, sl]], axis=0)
            v2 = jnp.concatenate([v_prev, v_ref[sub, :, sl]], axis=0)
            s = lax.dot_general(q_ref[sub, :, sl], k2, (((1,), (1,)), ((), ())),
                                preferred_element_type=F32)
            s = jnp.where(mask, s, NEG_INF)
            m = jnp.max(s, axis=1, keepdims=True)
            p = jnp.exp(s - m)
            l = jnp.sum(p, axis=1, keepdims=True)
            o = jnp.dot(p.astype(BF16), v2, preferred_element_type=F32)
            o_ref[sub, :, sl] = o / l
            lse_slab = jnp.where(lane == h, m + jnp.log(l), lse_slab)
        lse_ref[sub] = lse_slab


def _dilated_attention(qkv, g, cq, ck, cv):
    d = DILATIONS[g]
    W = ATT_GROUP_WIDTH
    nb = qkv.shape[0]
    nq = ATT_Q_BLOCKS
    assert qkv.shape[1] == d and nb % nq == 0
    blk = (nq, None, ATT_BLOCK, W)
    pblk = (None, None, ATT_BLOCK, W)
    cur = lambda c: (lambda r, n: (n, r, 0, c))
    prev = lambda c: (lambda r, n: (jnp.maximum(nq * n - 1, 0), r, 0, c))
    return pl.pallas_call(
        _attn_kernel,
        grid=(d, nb // nq),
        in_specs=[
            pl.BlockSpec(blk, cur(cq)),
            pl.BlockSpec(blk, cur(ck)),
            pl.BlockSpec(pblk, prev(ck)),
            pl.BlockSpec(blk, cur(cv)),
            pl.BlockSpec(pblk, prev(cv)),
        ],
        out_specs=[pl.BlockSpec(blk, lambda r, n: (n, r, 0, 0)),
                   pl.BlockSpec((nq, None, ATT_BLOCK, LANES), lambda r, n: (n, r, 0, 0))],
        out_shape=[jax.ShapeDtypeStruct((nb, d, ATT_BLOCK, W), F32),
                   jax.ShapeDtypeStruct((nb, d, ATT_BLOCK, LANES), F32)],
        compiler_params=_cparams(2),
        name=f"dilated_attn_g{g}",
    )(qkv, qkv, qkv, qkv, qkv)


def _retention_kernel(q_ref, k_ref, vlo_ref, vhi_ref, glo_ref, ghi_ref, o_ref, state_ref):
    @pl.when(pl.program_id(0) == 0)
    def _():
        state_ref[...] = jnp.zeros_like(state_ref)

    ri = lax.broadcasted_iota(jnp.int32, (RET_CHUNK, RET_CHUNK), 0)
    ci = lax.broadcasted_iota(jnp.int32, (RET_CHUNK, RET_CHUNK), 1)
    diff = (ri - ci).astype(F32)
    nrow = lax.broadcasted_iota(jnp.int32, (RET_CHUNK, 1), 0).astype(F32)
    half = RET_HEADS // 2
    for h in range(RET_HEADS):
        ld = RET_LOG_DECAY[h]
        sl = slice(h * RET_QK_DIM, (h + 1) * RET_QK_DIM)
        vsl = slice((h % half) * RET_V_DIM, (h % half + 1) * RET_V_DIM)
        v_ref, g_ref = (vlo_ref, glo_ref) if h < half else (vhi_ref, ghi_ref)
        q = q_ref[:, sl]
        k = k_ref[:, sl]
        v = v_ref[:, vsl]
        intra = jnp.where(diff >= 0, jnp.exp(jnp.maximum(diff, 0.0) * ld), 0.0)
        scores = lax.dot_general(q, k, (((1,), (1,)), ((), ())), preferred_element_type=F32) * intra
        inner = jnp.dot(scores.astype(BF16), v, preferred_element_type=F32)
        state = state_ref[h]
        q_dec = jnp.exp((nrow + 1.0) * ld)
        cross = jnp.dot((q.astype(F32) * q_dec).astype(BF16), state.astype(BF16),
                        preferred_element_type=F32)
        k_dec = jnp.exp((RET_CHUNK - 1.0 - nrow) * ld)
        kv = lax.dot_general((k.astype(F32) * k_dec).astype(BF16), v, (((0,), (0,)), ((), ())),
                             preferred_element_type=F32)
        state_ref[h] = math.exp(RET_CHUNK * ld) * state + kv
        o = inner + cross
        mu = jnp.mean(o, axis=1, keepdims=True)
        var = jnp.mean(jnp.square(o - mu), axis=1, keepdims=True)
        o = (o - mu) * lax.rsqrt(var + LN_EPS)
        gate = g_ref[:, vsl].astype(F32)
        gate = gate * (1.0 / (1.0 + jnp.exp(-gate)))
        o_ref[:, h * RET_V_DIM:(h + 1) * RET_V_DIM] = (gate * o).astype(o_ref.dtype)


def _retention(nat, S):
    nc = S // RET_CHUNK
    blk = (RET_CHUNK, COL_BLOCK)

    def col(cb):
        return pl.BlockSpec(blk, lambda c: (c, cb))

    return pl.pallas_call(
        _retention_kernel,
        grid=(nc,),
        in_specs=[col(NC_QR), col(NC_KR), col(NC_VR), col(NC_VR + 1), col(NC_GR), col(NC_GR + 1)],
        out_specs=pl.BlockSpec((RET_CHUNK, RET_HEADS * RET_V_DIM), lambda c: (c, 0)),
        out_shape=jax.ShapeDtypeStruct((S, RET_HEADS * RET_V_DIM), BF16),
        scratch_shapes=[pltpu.VMEM((RET_HEADS, RET_QK_DIM, RET_V_DIM), F32)],
        compiler_params=_cparams(1),
        name="retention",
    )(nat, nat, nat, nat, nat, nat)


MERGE_ROWS = 256
MERGE_CHAIN_ROWS = 256


def _layer_norm(z, scale, bias):
    mu = jnp.mean(z, axis=1, keepdims=True)
    var = jnp.mean(jnp.square(z - mu), axis=1, keepdims=True)
    return (z - mu) * lax.rsqrt(var + LN_EPS) * scale + bias


def _sigmoid(x):
    return 1.0 / (1.0 + jnp.exp(-x))


def _pack_bf16_pair(a, b):
    abits = lax.bitcast_convert_type(a.astype(BF16).astype(F32), jnp.uint32)
    bbits = lax.bitcast_convert_type(b.astype(BF16).astype(F32), jnp.uint32)
    return (abits >> 16) | (bbits & jnp.uint32(0xFFFF0000))


def _unpack_bf16_pair(w):
    a = lax.bitcast_convert_type(w << 16, F32).astype(BF16)
    b = lax.bitcast_convert_type(w & jnp.uint32(0xFFFF0000), F32).astype(BF16)
    return a, b


def _split_bf16(v):
    hi = v.astype(BF16)
    return hi, (v - hi.astype(F32)).astype(BF16)


def _merge_kernel(o0_ref, o1_ref, o2_ref, l0_ref, l1_ref, l2_ref, or_ref,
                  ga0_ref, ga1_ref, gb0_ref, gb1_ref, x_ref,
                  wa_ref, wr_ref, wo_ref, lns_ref, lnb_ref, wrt_ref, brt_ref, hexp_ref,
                  x1_ref, x1p_ref, route_ref, cnt_ref, s1_ref, s2_ref, ls1_ref, ls2_ref):
    tm = o0_ref.shape[0]
    for d, src, lsrc, dst, ldst in ((DILATIONS[1], o1_ref, l1_ref, s1_ref, ls1_ref),
                                    (DILATIONS[2], o2_ref, l2_ref, s2_ref, ls2_ref)):
        for r in range(d):
            for h in range(ATT_HEADS_PER_GROUP):
                dst[h, pl.ds(r, tm // d, stride=d), :] = src[r, :, h * ATT_HEAD_DIM:(h + 1) * ATT_HEAD_DIM]
            ldst[pl.ds(r, tm // d, stride=d), :] = lsrc[r]

    @pl.when(pl.program_id(0) == 0)
    def _():
        cnt_ref[...] = jnp.zeros_like(cnt_ref)

    for c in range(tm // MERGE_CHAIN_ROWS):
        _merge_rows(slice(c * MERGE_CHAIN_ROWS, (c + 1) * MERGE_CHAIN_ROWS),
                    o0_ref, l0_ref, or_ref, ga0_ref, ga1_ref, gb0_ref, gb1_ref, x_ref,
                    wa_ref, wr_ref, wo_ref, lns_ref, lnb_ref, wrt_ref, brt_ref, hexp_ref,
                    x1_ref, x1p_ref, route_ref, cnt_ref, s1_ref, s2_ref, ls1_ref, ls2_ref)


def _merge_rows(rs, o0_ref, l0_ref, or_ref, ga0_ref, ga1_ref, gb0_ref, gb1_ref, x_ref,
                wa_ref, wr_ref, wo_ref, lns_ref, lnb_ref, wrt_ref, brt_ref, hexp_ref,
                x1_ref, x1p_ref, route_ref, cnt_ref, s1_ref, s2_ref, ls1_ref, ls2_ref):
    rc = rs.stop - rs.start
    l0, l1, l2 = l0_ref[rs, :], ls1_ref[rs, :], ls2_ref[rs, :]
    lm = jnp.maximum(jnp.maximum(l0, l1), l2)
    e0, e1, e2 = jnp.exp(l0 - lm), jnp.exp(l1 - lm), jnp.exp(l2 - lm)
    den = e0 + e1 + e2

    def lanes_of_head(w):
        hi, lo = _split_bf16(w)
        return (jnp.dot(hi, hexp_ref[...], preferred_element_type=F32)
                + jnp.dot(lo, hexp_ref[...], preferred_element_type=F32))

    half = D_MODEL // 2
    halves = (slice(0, half), slice(half, D_MODEL))
    o_r = or_ref[rs, :]
    y_r = [jnp.dot(o_r, wr_ref[:, cs], preferred_element_type=F32) for cs in halves]
    heads = range(ATT_HEADS_PER_GROUP)
    o1 = jnp.concatenate([s1_ref[h, rs, :] for h in heads], axis=1)
    o2 = jnp.concatenate([s2_ref[h, rs, :] for h in heads], axis=1)
    o_a = (lanes_of_head(e0 / den) * o0_ref[rs, :] + lanes_of_head(e1 / den) * o1
           + lanes_of_head(e2 / den) * o2).astype(BF16)
    merged = []
    for n, cs in enumerate(halves):
        y_a = jnp.dot(o_a, wa_ref[:, cs], preferred_element_type=F32)
        gate_a = (ga0_ref, ga1_ref)[n][rs, :].astype(F32)
        gate_b = (gb0_ref, gb1_ref)[n][rs, :].astype(F32)
        merged.append((_sigmoid(gate_a) * y_a + _sigmoid(gate_b) * y_r[n]).astype(BF16))
    z = [ALPHA * x_ref[rs, cs]
         + jnp.dot(merged[0], wo_ref[:half, cs], preferred_element_type=F32)
         + jnp.dot(merged[1], wo_ref[half:, cs], preferred_element_type=F32) for cs in halves]
    mu = (jnp.sum(z[0], axis=1, keepdims=True) + jnp.sum(z[1], axis=1, keepdims=True)) / D_MODEL
    var = (jnp.sum(jnp.square(z[0] - mu), axis=1, keepdims=True)
           + jnp.sum(jnp.square(z[1] - mu), axis=1, keepdims=True)) / D_MODEL
    inv = lax.rsqrt(var + LN_EPS)
    x1 = [(z[n] - mu) * inv * lns_ref[:, cs] + lnb_ref[:, cs] for n, cs in enumerate(halves)]
    for n, cs in enumerate(halves):
        x1_ref[rs, cs] = x1[n]
    packed = _pack_bf16_pair(x1[0], x1[1])
    for c in range(PACKED_ROW_PIECES):
        x1p_ref[pl.ds(rs.start * PACKED_ROW_PIECES + c, rc, stride=PACKED_ROW_PIECES), :] = (
            packed[:, c * LANES:(c + 1) * LANES])

    logits = brt_ref[...]
    for n, cs in enumerate(halves):
        x_hi, x_lo = _split_bf16(x1[n])
        hl = jnp.dot(x_hi, wrt_ref[cs, :], preferred_element_type=F32)
        logits = (logits + hl[:, :LANES] + hl[:, LANES:]
                  + jnp.dot(x_lo, wrt_ref[cs, :LANES], preferred_element_type=F32))
    lane = lax.broadcasted_iota(jnp.int32, logits.shape, 1)
    is_g = lane < N_EXPERT_GROUPS
    gl = jnp.where(is_g, logits, NEG_INF)
    gmax = jnp.max(gl, axis=1, keepdims=True)
    g_sel = jnp.min(jnp.where(is_g & (gl == gmax), lane, LANES), axis=1, keepdims=True)
    p_group = 1.0 / jnp.sum(jnp.where(is_g, jnp.exp(gl - gmax), 0.0), axis=1, keepdims=True)
    lo = N_EXPERT_GROUPS + EXPERTS_PER_GROUP * g_sel
    in_grp = (lane >= lo) & (lane < lo + EXPERTS_PER_GROUP)
    el = jnp.where(in_grp, logits, NEG_INF)
    top1 = jnp.max(el, axis=1, keepdims=True)
    idx1 = jnp.min(jnp.where(in_grp & (el == top1), lane, LANES), axis=1, keepdims=True)
    el2 = jnp.where(lane == idx1, NEG_INF, el)
    top2 = jnp.max(el2, axis=1, keepdims=True)
    idx2 = jnp.min(jnp.where(in_grp & (lane != idx1) & (el2 == top2), lane, LANES), axis=1, keepdims=True)
    t = jnp.exp(top2 - top1)
    wt1 = p_group * (1.0 / (1.0 + t))
    wt2 = p_group * (t / (1.0 + t))
    slab = jnp.where(lane == 0, (idx1 - N_EXPERT_GROUPS).astype(F32), 0.0)
    slab = jnp.where(lane == 1, (idx2 - N_EXPERT_GROUPS).astype(F32), slab)
    slab = jnp.where(lane == 2, wt1, slab)
    slab = jnp.where(lane == 3, wt2, slab)

    hot = jnp.where((lane == idx1) | (lane == idx2), 1.0, 0.0)
    ri = lax.broadcasted_iota(jnp.int32, (rc, rc), 0)
    ci = lax.broadcasted_iota(jnp.int32, (rc, rc), 1)
    ltri = jnp.where(ri > ci, 1.0, 0.0).astype(BF16)
    prefix = jnp.dot(ltri, hot.astype(BF16), preferred_element_type=F32) + cnt_ref[...]
    rank1 = jnp.sum(jnp.where(lane == idx1, prefix, 0.0), axis=1, keepdims=True)
    rank2 = jnp.sum(jnp.where(lane == idx2, prefix, 0.0), axis=1, keepdims=True)
    cnt_ref[...] = cnt_ref[...] + jnp.sum(hot, axis=0, keepdims=True)
    slab = jnp.where(lane == 4, rank1, slab)
    slab = jnp.where(lane == 5, rank2, slab)
    route_ref[rs, :] = slab


def _merge(o_g, lse_g, o_r, nat, x2d, wa, wr, wo, lns, lnb, wrt, brt, S):
    tm = MERGE_ROWS
    D = D_MODEL
    W = ATT_GROUP_WIDTH
    head_expand = (jnp.arange(W, dtype=jnp.int32)[None, :] // ATT_HEAD_DIM
                   == jnp.arange(LANES, dtype=jnp.int32)[:, None]).astype(BF16)

    def rows(width):
        return pl.BlockSpec((tm, width), lambda i: (i, 0))

    def streams(g, width):
        d = DILATIONS[g]
        per = (ATT_BLOCK * d) // tm
        return pl.BlockSpec((None, d, tm // d, width), lambda i: (i // per, 0, i % per, 0))

    def ncol(cb):
        return pl.BlockSpec((tm, COL_BLOCK), lambda i: (i, cb))

    def const(shape):
        return pl.BlockSpec(shape, lambda i: (0, 0), pipeline_mode=pl.Buffered(1))

    return pl.pallas_call(
        _merge_kernel,
        grid=(S // tm,),
        in_specs=[
            rows(W), streams(1, W), streams(2, W),
            rows(LANES), streams(1, LANES), streams(2, LANES),
            rows(RET_HEADS * RET_V_DIM),
            ncol(NC_GATE_A), ncol(NC_GATE_A + 1), ncol(NC_GATE_B), ncol(NC_GATE_B + 1),
            rows(D),
            const((W, D)), const((RET_HEADS * RET_V_DIM, D)), const((D, D)),
            const((1, D)), const((1, D)), const((D, 2 * LANES)), const((1, LANES)), const((LANES, W)),
        ],
        out_specs=[rows(D), pl.BlockSpec((tm * PACKED_ROW_PIECES, LANES), lambda i: (i, 0)),
                   rows(LANES), pl.BlockSpec((1, LANES), lambda i: (0, 0))],
        out_shape=[jax.ShapeDtypeStruct((S, D), F32),
                   jax.ShapeDtypeStruct((S * PACKED_ROW_PIECES, LANES), jnp.uint32),
                   jax.ShapeDtypeStruct((S, LANES), F32),
                   jax.ShapeDtypeStruct((1, LANES), F32)],
        scratch_shapes=[pltpu.VMEM((ATT_HEADS_PER_GROUP, tm, ATT_HEAD_DIM), F32),
                        pltpu.VMEM((ATT_HEADS_PER_GROUP, tm, ATT_HEAD_DIM), F32),
                        pltpu.VMEM((tm, LANES), F32), pltpu.VMEM((tm, LANES), F32)],
        compiler_params=_cparams(1),
        name="merge_ln_router",
    )(*o_g, *lse_g, o_r, nat, nat, nat, nat, x2d, wa, wr, wo, lns, lnb, wrt, brt, head_expand)


META_N_USED, META_NEXT = 0, 1
WEIGHT_DMA_PRIORITY = 1
GATHER_SLOTS = 3


def _plan_kernel(dest_ref, cnt_ref, rt_ref, be_ref, meta_ref):
    n_blocks = be_ref.shape[0]

    def pad_row(r, carry):
        rt_ref[r] = 0
        return carry

    def per_expert(e, start):
        nb = lax.shift_right_logical(cnt_ref[e] + (MOE_BLOCK - 1), 7)

        def fill(b, carry):
            be_ref[lax.shift_right_logical(start, 7) + b] = e
            return carry
        lax.fori_loop(0, nb, fill, 0)
        lax.fori_loop(start + cnt_ref[e], start + nb * MOE_BLOCK, pad_row, 0)
        return start + nb * MOE_BLOCK
    total = lax.fori_loop(0, N_EXPERTS, per_expert, 0)
    n_used = lax.shift_right_logical(total, 7)
    meta_ref[META_N_USED] = n_used

    def tail(b, carry):
        be_ref[b] = be_ref[jnp.maximum(n_used - 1, 0)]
        return carry
    lax.fori_loop(n_used, n_blocks, tail, 0)
    lax.fori_loop(total, rt_ref.shape[0], pad_row, 0)

    def nxt(k, cur):
        e = N_EXPERTS - 1 - k
        meta_ref[META_NEXT + e] = cur
        return jnp.where(cnt_ref[e] > 0, e, cur)
    lax.fori_loop(0, N_EXPERTS, nxt, N_EXPERTS)

    def assign(a, carry):
        rt_ref[dest_ref[a]] = lax.shift_right_logical(a, 1)
        return carry
    lax.fori_loop(0, dest_ref.shape[0], assign, 0, unroll=8)


def _plan(dest, counts):
    A = dest.shape[0]
    n_blocks = A // MOE_BLOCK + N_EXPERTS
    smem = pl.BlockSpec(memory_space=pltpu.SMEM)
    return pl.pallas_call(
        _plan_kernel,
        in_specs=[smem, smem],
        out_specs=[smem, smem, smem],
        out_shape=[
            jax.ShapeDtypeStruct((n_blocks * MOE_BLOCK,), jnp.int32),
            jax.ShapeDtypeStruct((n_blocks,), jnp.int32),
            jax.ShapeDtypeStruct((META_NEXT + N_EXPERTS,), jnp.int32),
        ],
        name="moe_plan",
    )(dest, counts)


def _expert_kernel(be_ref, meta_ref, rt_ref, x_hbm, wg_hbm, wu_hbm, wd_hbm, y_ref,
                   xg_ref, wgf_ref, wuf_ref, wdf_ref, wgb_ref, wub_ref, wdb_ref, xsem, sem, slot_ref):
    b = pl.program_id(0)
    n_used = meta_ref[META_N_USED]
    e = be_ref[b]
    first = (b == 0) | (e != be_ref[jnp.maximum(b - 1, 0)])

    xp = PACKED_ROW_PIECES

    last = jnp.maximum(n_used - 1, 0)

    def row_copy(blk, slot, i):
        tok = rt_ref[blk * MOE_BLOCK + i]
        return pltpu.make_async_copy(x_hbm.at[pl.ds(pl.multiple_of(tok * xp, xp), xp)],
                                     xg_ref.at[slot, pl.ds(pl.multiple_of(i * xp, xp), xp)],
                                     xsem.at[slot])

    def wait_rows(slot):
        pltpu.make_async_copy(x_hbm.at[pl.ds(0, MOE_BLOCK * xp)], xg_ref.at[slot], xsem.at[slot]).wait()

    @pl.when(b == 0)
    def _():
        for blk in range(GATHER_SLOTS - 1):
            def body(i, carry, blk=blk):
                row_copy(jnp.minimum(blk, last), blk, i).start()
                return carry
            lax.fori_loop(0, MOE_BLOCK, body, 0, unroll=8)

    def copies(ex, s):
        return (pltpu.make_async_copy(wg_hbm.at[ex], wgf_ref.at[s], sem.at[s]),
                pltpu.make_async_copy(wu_hbm.at[ex], wuf_ref.at[s], sem.at[s]),
                pltpu.make_async_copy(wd_hbm.at[ex], wdf_ref.at[s], sem.at[s]))

    @pl.when((b == 0) & (n_used > 0))
    def _():
        slot_ref[0] = 0
        for c in copies(e, 0):
            c.start(priority=WEIGHT_DMA_PRIORITY)

    @pl.when(first & (b < n_used))
    def _():
        s = slot_ref[0]
        nx = meta_ref[META_NEXT + e]

        @pl.when(nx < N_EXPERTS)
        def _():
            for c in copies(nx, 1 - s):
                c.start(priority=WEIGHT_DMA_PRIORITY)
        for c in copies(e, s):
            c.wait()
        wgb_ref[...] = wgf_ref[s].astype(BF16)
        wub_ref[...] = wuf_ref[s].astype(BF16)
        wdb_ref[...] = wdf_ref[s].astype(BF16)
        slot_ref[0] = 1 - s

    def step(slot):
        wait_rows(slot)
        nxt = jnp.minimum(b + (GATHER_SLOTS - 1), last)
        for i in range(MOE_BLOCK):
            row_copy(nxt, (slot + GATHER_SLOTS - 1) % GATHER_SLOTS, i).start()
        pieces = [_unpack_bf16_pair(xg_ref[slot, pl.ds(c, MOE_BLOCK, stride=xp), :]) for c in range(xp)]
        xb = jnp.concatenate([p[0] for p in pieces] + [p[1] for p in pieces], axis=1)
        hg = jnp.dot(xb, wgb_ref[...], preferred_element_type=F32)
        hu = jnp.dot(xb, wub_ref[...], preferred_element_type=F32)
        hid = (hg * _sigmoid(hg) * hu).astype(BF16)
        y = jnp.dot(hid, wdb_ref[...], preferred_element_type=F32)
        for c in range(ROW_PIECES):
            y_ref[pl.ds(c, MOE_BLOCK, stride=ROW_PIECES), :] = y[:, c * LANES:(c + 1) * LANES]

    for s in range(GATHER_SLOTS):
        pl.when((b < n_used) & (lax.rem(b, GATHER_SLOTS) == s))(functools.partial(step, s))

    @pl.when(b == last)
    def _():
        for ahead in range(1, GATHER_SLOTS):
            wait_rows(lax.rem(b + ahead, GATHER_SLOTS))

    @pl.when(b >= n_used)
    def _():
        y_ref[...] = jnp.zeros_like(y_ref)


def _experts(x1p, block_expert, meta, row_tok, w_gate, w_up, w_down):
    D = D_MODEL
    assert x1p.shape[1] == LANES and x1p.dtype == jnp.uint32
    n_blocks = block_expert.shape[0]
    R = n_blocks * MOE_BLOCK
    H = EXPERT_HIDDEN
    grid_spec = pltpu.PrefetchScalarGridSpec(
        num_scalar_prefetch=3,
        grid=(n_blocks,),
        in_specs=[
            pl.BlockSpec(memory_space=pl.ANY),
            pl.BlockSpec(memory_space=pl.ANY),
            pl.BlockSpec(memory_space=pl.ANY),
            pl.BlockSpec(memory_space=pl.ANY),
        ],
        out_specs=pl.BlockSpec((MOE_BLOCK * ROW_PIECES, LANES), lambda b, be, meta, rt: (b, 0)),
        scratch_shapes=[
            pltpu.VMEM((GATHER_SLOTS, MOE_BLOCK * PACKED_ROW_PIECES, LANES), jnp.uint32),
            pltpu.VMEM((2, D, H), F32),
            pltpu.VMEM((2, D, H), F32),
            pltpu.VMEM((2, H, D), F32),
            pltpu.VMEM((D, H), BF16),
            pltpu.VMEM((D, H), BF16),
            pltpu.VMEM((H, D), BF16),
            pltpu.SemaphoreType.DMA((GATHER_SLOTS,)),
            pltpu.SemaphoreType.DMA((2,)),
            pltpu.SMEM((1,), jnp.int32),
        ],
    )
    return pl.pallas_call(
        _expert_kernel,
        grid_spec=grid_spec,
        out_shape=jax.ShapeDtypeStruct((R * ROW_PIECES, LANES), F32),
        compiler_params=_cparams(1),
        name="moe_experts",
    )(block_expert, meta, row_tok, x1p, w_gate, w_up, w_down)


COMBINE_TOKENS = 64


def _combine_kernel(dest_ref, y_hbm, x1_ref, route_ref, lns_ref, lnb_ref, o_ref, yg_ref, sem):
    i = pl.program_id(0)
    tt = COMBINE_TOKENS
    n = dest_ref.shape[0] // (TOP_K * tt)

    rp = ROW_PIECES

    def issue(step, slot):
        def body(r, carry):
            for k in range(TOP_K):
                row = dest_ref[(step * tt + r) * TOP_K + k]
                pltpu.make_async_copy(y_hbm.at[pl.ds(pl.multiple_of(row * rp, rp), rp)],
                                      yg_ref.at[slot, k, pl.ds(pl.multiple_of(r * rp, rp), rp)],
                                      sem.at[slot]).start(priority=k)
            return carry
        lax.fori_loop(0, tt, body, 0, unroll=4)

    @pl.when(i == 0)
    def _():
        issue(0, 0)

    @pl.when(i + 1 < n)
    def _():
        issue(i + 1, (i + 1) % 2)

    slot = i % 2
    for k in range(TOP_K):
        pltpu.make_async_copy(y_hbm.at[pl.ds(0, tt * rp)], yg_ref.at[slot, k], sem.at[slot]).wait()
    route = route_ref[...]
    w1, w2 = route[:, 2:3], route[:, 3:4]
    moe = jnp.concatenate(
        [w1 * yg_ref[slot, 0, pl.ds(c, tt, stride=rp), :] + w2 * yg_ref[slot, 1, pl.ds(c, tt, stride=rp), :]
         for c in range(rp)], axis=1)
    o_ref[...] = _layer_norm(ALPHA * x1_ref[...] + moe, lns_ref[...], lnb_ref[...])


def _combine(dest, y, x1, route, lns, lnb):
    T, D = x1.shape
    tt = COMBINE_TOKENS
    grid_spec = pltpu.PrefetchScalarGridSpec(
        num_scalar_prefetch=1,
        grid=(T // tt,),
        in_specs=[
            pl.BlockSpec(memory_space=pl.ANY),
            pl.BlockSpec((tt, D), lambda i, d: (i, 0)),
            pl.BlockSpec((tt, LANES), lambda i, d: (i, 0)),
            pl.BlockSpec((1, D), lambda i, d: (0, 0)),
            pl.BlockSpec((1, D), lambda i, d: (0, 0)),
        ],
        out_specs=pl.BlockSpec((tt, D), lambda i, d: (i, 0)),
        scratch_shapes=[
            pltpu.VMEM((2, TOP_K, tt * ROW_PIECES, LANES), F32),
            pltpu.SemaphoreType.DMA((2,)),
        ],
    )
    return pl.pallas_call(
        _combine_kernel,
        grid_spec=grid_spec,
        out_shape=jax.ShapeDtypeStruct((T, D), F32),
        compiler_params=_cparams(1),
        name="moe_combine_ln",
    )(dest, y, x1, route, lns, lnb)


def _rotary_tables(positions):
    pos = positions.reshape(-1).astype(F32)
    S = pos.shape[0]
    inv_a = ROPE_THETA ** (-jnp.arange(0, ROPE_DIM, 2, dtype=F32) / ROPE_DIM)
    inv_r = RET_ROT_BASE ** (-jnp.linspace(0.0, 1.0, RET_QK_DIM // 2, dtype=F32))
    ang = jnp.concatenate([inv_a, inv_r])[:, None] * pos[None, :]
    cos_t, sin_t = lax.optimization_barrier((jnp.cos(ang), jnp.sin(ang)))
    na = ROPE_DIM // 2
    cos, sin, cr, sr = cos_t[:na].T, sin_t[:na].T, cos_t[na:].T, sin_t[na:].T
    pad1 = jnp.ones((S, LANES - ROPE_DIM), F32)
    pad0 = jnp.zeros((S, LANES - ROPE_DIM), F32)
    tabs_a = jnp.concatenate([cos, cos, pad1, sin, sin, pad0], axis=1)
    tabs_r = jnp.concatenate([cr, cr, sr, sr], axis=1)
    return tabs_a, tabs_r


def kernel(x, positions, w_in, w_branch_attn, w_branch_ret, w_out, ln1_scale, ln1_bias, w_group_router, b_group_router, w_expert_router, b_expert_router, w_expert_gate, w_expert_up, w_expert_down, ln2_scale, ln2_bias):
    B, S, D = x.shape
    assert B == 1 and D == D_MODEL and w_in.shape[0] == DEPTH == 1
    assert S % (ATT_BLOCK * DILATIONS[2]) == 0
    x2d = x.reshape(S, D)
    xb = x2d.astype(BF16)
    tabs_a, tabs_r = _rotary_tables(positions)

    nat = _input_projection(xb, w_in[0], tabs_a, tabs_r, NAT_COLS, NAT_MODES, 1)
    qkv_modes = (MODE_ATT_Q, MODE_ATT_K, MODE_PLAIN)
    qkv = [None] + [
        _input_projection(xb, w_in[0], tabs_a, tabs_r, (CB_QA + g, CB_KA + g, CB_VA + g), qkv_modes,
                          DILATIONS[g])
        for g in (1, 2)]
    o0, l0 = _dilated_attention(nat.reshape(S // ATT_BLOCK, 1, ATT_BLOCK, nat.shape[1]), 0,
                                NC_Q0, NC_K0, NC_V0)
    att = [(o0.reshape(S, ATT_GROUP_WIDTH), l0.reshape(S, LANES))]
    att += [_dilated_attention(qkv[g], g, 0, 1, 2) for g in (1, 2)]
    o_r = _retention(nat, S)

    w_route = jnp.concatenate(
        [w_group_router[0], w_expert_router[0].transpose(1, 0, 2).reshape(D, N_EXPERTS),
         jnp.zeros((D, LANES - N_EXPERT_GROUPS - N_EXPERTS), F32)], axis=1)
    b_route = jnp.concatenate(
        [b_group_router[0], b_expert_router[0].reshape(N_EXPERTS),
         jnp.zeros((LANES - N_EXPERT_GROUPS - N_EXPERTS,), F32)]).reshape(1, LANES)
    x1, x1p, route, cnt = _merge(
        [a[0] for a in att], [a[1] for a in att], o_r, nat, x2d,
        w_branch_attn[0].astype(BF16), w_branch_ret[0].astype(BF16), w_out[0].astype(BF16),
        ln1_scale[0].reshape(1, D), ln1_bias[0].reshape(1, D),
        jnp.concatenate(_split_bf16(w_route), axis=1), b_route, S)

    e_flat = route[:, 0:TOP_K].astype(jnp.int32).reshape(S * TOP_K)
    rank_flat = route[:, 4:4 + TOP_K].astype(jnp.int32).reshape(S * TOP_K)
    counts = cnt[0, N_EXPERT_GROUPS:N_EXPERT_GROUPS + N_EXPERTS].astype(jnp.int32)
    padded = (counts + (MOE_BLOCK - 1)) // MOE_BLOCK * MOE_BLOCK
    pstart = jnp.cumsum(padded) - padded
    hot = e_flat[:, None] == jnp.arange(N_EXPERTS, dtype=jnp.int32)[None, :]
    dest = jnp.sum(jnp.where(hot, pstart[None, :], 0), axis=1) + rank_flat
    row_tok, block_expert, meta = _plan(dest, counts)
    y = _experts(x1p, block_expert, meta, row_tok, w_expert_gate[0], w_expert_up[0], w_expert_down[0])
    out = _combine(dest, y, x1, route, ln2_scale[0].reshape(1, D), ln2_bias[0].reshape(1, D))
    return out.reshape(B, S, D)
```

```python
import functools
import math

import jax
import jax.numpy as jnp
from jax import lax
from jax.experimental import pallas as pl
from jax.experimental.pallas import tpu as pltpu

F32 = jnp.float32
BF16 = jnp.bfloat16

D_MODEL = 2048
ATT_HEAD_DIM = 128
ATT_HEADS_PER_GROUP = 8
DILATIONS = (1, 4, 16)
ATT_STEPS = 128
N_ATT_GROUPS = 3
ATT_GROUP_WIDTH = ATT_HEADS_PER_GROUP * ATT_HEAD_DIM
ROPE_DIM = ATT_HEAD_DIM // 4
ROPE_THETA = 500000.0
ATT_BLOCK = 128

RET_HEADS = 8
RET_QK_DIM = 128
RET_V_DIM = 256
RET_CHUNK = 128
RET_ROT_BASE = 10000.0
RET_LOG_DECAY = tuple(math.log1p(-(2.0 ** (-5.0 - h))) for h in range(RET_HEADS))

N_EXPERT_GROUPS = 4
EXPERTS_PER_GROUP = 8
N_EXPERTS = N_EXPERT_GROUPS * EXPERTS_PER_GROUP
TOP_K = 2
EXPERT_HIDDEN = 512
MOE_BLOCK = 128

DEPTH = 1
ALPHA = (2.0 * DEPTH) ** 0.25
LN_EPS = 1e-5
NEG_INF = -1e30

COL_BLOCK = 1024
CB_QA, CB_KA, CB_VA = 0, 3, 6
CB_QR, CB_KR, CB_VR, CB_GR = 9, 10, 11, 13
CB_GATE_A, CB_GATE_B = 15, 17

MODE_PLAIN, MODE_ATT_Q, MODE_ATT_K, MODE_RET_Q, MODE_RET_K = 0, 1, 2, 3, 4

NAT_COLS = (CB_QA, CB_KA, CB_VA, CB_QR, CB_KR, CB_VR, CB_VR + 1, CB_GR, CB_GR + 1,
            CB_GATE_A, CB_GATE_A + 1, CB_GATE_B, CB_GATE_B + 1)
NAT_MODES = (MODE_ATT_Q, MODE_ATT_K, MODE_PLAIN, MODE_RET_Q, MODE_RET_K) + (MODE_PLAIN,) * 8
NC_Q0, NC_K0, NC_V0, NC_QR, NC_KR, NC_VR, NC_GR, NC_GATE_A, NC_GATE_B = 0, 1, 2, 3, 4, 5, 7, 9, 11

LANES = 128
ROW_PIECES = D_MODEL // LANES
PACKED_ROW_PIECES = ROW_PIECES // 2
VMEM_LIMIT = 52 * 1024 * 1024
PROJ_ROWS = 1024


def _cparams(n_grid_dims):
    return pltpu.CompilerParams(
        dimension_semantics=("arbitrary",) * n_grid_dims,
        vmem_limit_bytes=VMEM_LIMIT,
    )


def _inproj_kernel(cols_ref, modes_ref, x_ref, w_ref, ta_ref, tr_ref, pa_ref, pr_ref, o_ref,
                   wbf_ref, *acc, d):
    del cols_ref
    j = pl.program_id(0)

    @pl.when(pl.program_id(1) == 0)
    def _():
        wbf_ref[...] = w_ref[...].astype(BF16)

    mode = modes_ref[j]
    tm, tn = x_ref.shape[0], wbf_ref.shape[1]
    pair = 2 * LANES

    def project():
        return jnp.dot(x_ref[...], wbf_ref[...], preferred_element_type=F32)

    def emit(p, piece):
        if d == 1:
            o_ref[:, p * pair:(p + 1) * pair] = piece.astype(o_ref.dtype)
        else:
            acc[0][2 * p] = piece[:, :LANES]
            acc[0][2 * p + 1] = piece[:, LANES:]

    def plain():
        res = project()
        for p in range(tn // pair):
            emit(p, res[:, p * pair:(p + 1) * pair])

    def rotary(tab_ref, perm_ref, scale):
        res = project()
        c = jnp.concatenate([tab_ref[:, :LANES]] * 2, axis=1) * scale
        s = jnp.concatenate([tab_ref[:, LANES:]] * 2, axis=1) * scale
        for p in range(tn // pair):
            a = res[:, p * pair:(p + 1) * pair]
            partner = jnp.dot(a.astype(BF16), perm_ref[...], preferred_element_type=F32)
            emit(p, a * c + partner * s)

    is_att = (mode == MODE_ATT_Q) | (mode == MODE_ATT_K)
    is_ret = (mode == MODE_RET_Q) | (mode == MODE_RET_K)
    pl.when(mode == MODE_PLAIN)(plain)
    pl.when(is_att)(lambda: rotary(ta_ref, pa_ref, jnp.where(mode == MODE_ATT_Q, ATT_HEAD_DIM ** -0.5, 1.0)))
    if d == 1:
        pl.when(is_ret)(lambda: rotary(tr_ref, pr_ref, jnp.where(mode == MODE_RET_K, RET_QK_DIM ** -0.5, 1.0)))
        return

    acc_ref = acc[0]
    for h in range(tn // LANES):
        sl = slice(h * LANES, (h + 1) * LANES)
        if d == 4:
            P = ATT_BLOCK * d
            for sb in range(tm // P):
                for r in range(d):
                    o_ref[sb, r, :, sl] = acc_ref[h, pl.ds(sb * P + r, ATT_BLOCK, stride=d), :].astype(o_ref.dtype)
        else:
            for r in range(d):
                o_ref[r, :, sl] = acc_ref[h, pl.ds(r, tm // d, stride=d), :].astype(o_ref.dtype)


def _rotation_matrix(half, width):
    k = jnp.arange(2 * LANES, dtype=jnp.int32)[:, None]
    l = jnp.arange(2 * LANES, dtype=jnp.int32)[None, :]
    same_head = (k // LANES) == (l // LANES)
    kk, ll = k % LANES, l % LANES
    neg = same_head & (ll < half) & (kk == ll + half)
    pos = same_head & (ll >= half) & (ll < width) & (kk == ll - half)
    return (pos.astype(F32) - neg.astype(F32)).astype(BF16)


def _input_projection(xb, w_in, tabs_a, tabs_r, cols, modes, d):
    S, D = xb.shape
    tm, tn = PROJ_ROWS, COL_BLOCK
    n = len(cols)
    P = ATT_BLOCK * d
    if d == 1:
        out_shape = jax.ShapeDtypeStruct((S, n * tn), BF16)
        out_spec = pl.BlockSpec((tm, tn), lambda j, i, c, m: (i, j))
    elif d == 4:
        out_shape = jax.ShapeDtypeStruct((S // P, d, ATT_BLOCK, n * tn), BF16)
        out_spec = pl.BlockSpec((tm // P, d, ATT_BLOCK, tn), lambda j, i, c, m: (i, 0, 0, j))
    else:
        assert d == 16 and P == 2 * tm
        out_shape = jax.ShapeDtypeStruct((S // P, d, ATT_BLOCK, n * tn), BF16)
        out_spec = pl.BlockSpec((None, d, tm // d, tn), lambda j, i, c, m: (i // 2, 0, i % 2, j))

    def tab_rows(j, i, c, m):
        return (jnp.where(m[j] != MODE_PLAIN, i, 0), 0)

    grid_spec = pltpu.PrefetchScalarGridSpec(
        num_scalar_prefetch=2,
        grid=(n, S // tm),
        in_specs=[
            pl.BlockSpec((tm, D), lambda j, i, c, m: (i, 0)),
            pl.BlockSpec((D, tn), lambda j, i, c, m: (0, c[j])),
            pl.BlockSpec((tm, 2 * LANES), tab_rows),
            pl.BlockSpec((tm, 2 * LANES), tab_rows),
            pl.BlockSpec((2 * LANES, 2 * LANES), lambda j, i, c, m: (0, 0)),
            pl.BlockSpec((2 * LANES, 2 * LANES), lambda j, i, c, m: (0, 0)),
        ],
        out_specs=out_spec,
        scratch_shapes=[pltpu.VMEM((D, tn), BF16)]
        + ([] if d == 1 else [pltpu.VMEM((tn // LANES, tm, LANES), F32)]),
    )
    return pl.pallas_call(
        functools.partial(_inproj_kernel, d=d),
        grid_spec=grid_spec,
        out_shape=out_shape,
        compiler_params=_cparams(2),
        name=f"in_proj_d{d}",
    )(jnp.asarray(cols, jnp.int32), jnp.asarray(modes, jnp.int32), xb, w_in, tabs_a, tabs_r,
      _rotation_matrix(ROPE_DIM // 2, ROPE_DIM), _rotation_matrix(RET_QK_DIM // 2, RET_QK_DIM))


ATT_Q_BLOCKS = 2


def _attn_kernel(q_ref, k_ref, kp_ref, v_ref, vp_ref, o_ref, lse_ref):
    n = pl.program_id(1)
    qi = lax.broadcasted_iota(jnp.int32, (ATT_BLOCK, 2 * ATT_BLOCK), 0)
    kj = lax.broadcasted_iota(jnp.int32, (ATT_BLOCK, 2 * ATT_BLOCK), 1)
    dist = qi + ATT_BLOCK - kj
    band = (dist >= 0) & (dist <= ATT_STEPS)
    lane = lax.broadcasted_iota(jnp.int32, (ATT_BLOCK, LANES), 1)
    for sub in range(ATT_Q_BLOCKS):
        mask = band & ((kj >= ATT_BLOCK) | (n > 0)) if sub == 0 else band
        lse_slab = jnp.zeros((ATT_BLOCK, LANES), F32)
        for h in range(ATT_HEADS_PER_GROUP):
            sl = slice(h * ATT_HEAD_DIM, (h + 1) * ATT_HEAD_DIM)
            k_prev = kp_ref[:, sl] if sub == 0 else k_ref[sub - 1, :, sl]
            v_prev = vp_ref[:, sl] if sub == 0 else v_ref[sub - 1, :, sl]
            k2 = jnp.concatenate([k_prev, k_ref[sub, :, sl]], axis=0)
            v2 = jnp.concatenate([v_prev, v_ref[sub, :, sl]], axis=0)
            s = lax.dot_general(q_ref[sub, :, sl], k2, (((1,), (1,)), ((), ())),
                                preferred_element_type=F32)
            s = jnp.where(mask, s, NEG_INF)
            m = jnp.max(s, axis=1, keepdims=True)
            p = jnp.exp(s - m)
            l = jnp.sum(p, axis=1, keepdims=True)
            o = jnp.dot(p.astype(BF16), v2, preferred_element_type=F32)
            o_ref[sub, :, sl] = o / l
            lse_slab = jnp.where(lane == h, m + jnp.log(l), lse_slab)
        lse_ref[sub] = lse_slab


def _dilated_attention(qkv, g, cq, ck, cv):
    d = DILATIONS[g]
    W = ATT_GROUP_WIDTH
    nb = qkv.shape[0]
    nq = ATT_Q_BLOCKS
    assert qkv.shape[1] == d and nb % nq == 0
    blk = (nq, None, ATT_BLOCK, W)
    pblk = (None, None, ATT_BLOCK, W)
    cur = lambda c: (lambda r, n: (n, r, 0, c))
    prev = lambda c: (lambda r, n: (jnp.maximum(nq * n - 1, 0), r, 0, c))
    return pl.pallas_call(
        _attn_kernel,
        grid=(d, nb // nq),
        in_specs=[
            pl.BlockSpec(blk, cur(cq)),
            pl.BlockSpec(blk, cur(ck)),
            pl.BlockSpec(pblk, prev(ck)),
            pl.BlockSpec(blk, cur(cv)),
            pl.BlockSpec(pblk, prev(cv)),
        ],
        out_specs=[pl.BlockSpec(blk, lambda r, n: (n, r, 0, 0)),
                   pl.BlockSpec((nq, None, ATT_BLOCK, LANES), lambda r, n: (n, r, 0, 0))],
        out_shape=[jax.ShapeDtypeStruct((nb, d, ATT_BLOCK, W), F32),
                   jax.ShapeDtypeStruct((nb, d, ATT_BLOCK, LANES), F32)],
        compiler_params=_cparams(2),
        name=f"dilated_attn_g{g}",
    )(qkv, qkv, qkv, qkv, qkv)


def _retention_kernel(q_ref, k_ref, vlo_ref, vhi_ref, glo_ref, ghi_ref, o_ref, state_ref):
    @pl.when(pl.program_id(0) == 0)
    def _():
        state_ref[...] = jnp.zeros_like(state_ref)

    ri = lax.broadcasted_iota(jnp.int32, (RET_CHUNK, RET_CHUNK), 0)
    ci = lax.broadcasted_iota(jnp.int32, (RET_CHUNK, RET_CHUNK), 1)
    diff = (ri - ci).astype(F32)
    nrow = lax.broadcasted_iota(jnp.int32, (RET_CHUNK, 1), 0).astype(F32)
    half = RET_HEADS // 2
    for h in range(RET_HEADS):
        ld = RET_LOG_DECAY[h]
        sl = slice(h * RET_QK_DIM, (h + 1) * RET_QK_DIM)
        vsl = slice((h % half) * RET_V_DIM, (h % half + 1) * RET_V_DIM)
        v_ref, g_ref = (vlo_ref, glo_ref) if h < half else (vhi_ref, ghi_ref)
        q = q_ref[:, sl]
        k = k_ref[:, sl]
        v = v_ref[:, vsl]
        intra = jnp.where(diff >= 0, jnp.exp(jnp.maximum(diff, 0.0) * ld), 0.0)
        scores = lax.dot_general(q, k, (((1,), (1,)), ((), ())), preferred_element_type=F32) * intra
        inner = jnp.dot(scores.astype(BF16), v, preferred_element_type=F32)
        state = state_ref[h]
        q_dec = jnp.exp((nrow + 1.0) * ld)
        cross = jnp.dot((q.astype(F32) * q_dec).astype(BF16), state.astype(BF16),
                        preferred_element_type=F32)
        k_dec = jnp.exp((RET_CHUNK - 1.0 - nrow) * ld)
        kv = lax.dot_general((k.astype(F32) * k_dec).astype(BF16), v, (((0,), (0,)), ((), ())),
                             preferred_element_type=F32)
        state_ref[h] = math.exp(RET_CHUNK * ld) * state + kv
        o = inner + cross
        mu = jnp.mean(o, axis=1, keepdims=True)
        var = jnp.mean(jnp.square(o - mu), axis=1, keepdims=True)
        o = (o - mu) * lax.rsqrt(var + LN_EPS)
        gate = g_ref[:, vsl].astype(F32)
        gate = gate * (1.0 / (1.0 + jnp.exp(-gate)))
        o_ref[:, h * RET_V_DIM:(h + 1) * RET_V_DIM] = (gate * o).astype(o_ref.dtype)


def _retention(nat, S):
    nc = S // RET_CHUNK
    blk = (RET_CHUNK, COL_BLOCK)

    def col(cb):
        return pl.BlockSpec(blk, lambda c: (c, cb))

    return pl.pallas_call(
        _retention_kernel,
        grid=(nc,),
        in_specs=[col(NC_QR), col(NC_KR), col(NC_VR), col(NC_VR + 1), col(NC_GR), col(NC_GR + 1)],
        out_specs=pl.BlockSpec((RET_CHUNK, RET_HEADS * RET_V_DIM), lambda c: (c, 0)),
        out_shape=jax.ShapeDtypeStruct((S, RET_HEADS * RET_V_DIM), BF16),
        scratch_shapes=[pltpu.VMEM((RET_HEADS, RET_QK_DIM, RET_V_DIM), F32)],
        compiler_params=_cparams(1),
        name="retention",
    )(nat, nat, nat, nat, nat, nat)


MERGE_ROWS = 256
MERGE_CHAIN_ROWS = 256


def _layer_norm(z, scale, bias):
    mu = jnp.mean(z, axis=1, keepdims=True)
    var = jnp.mean(jnp.square(z - mu), axis=1, keepdims=True)
    return (z - mu) * lax.rsqrt(var + LN_EPS) * scale + bias


def _sigmoid(x):
    return 1.0 / (1.0 + jnp.exp(-x))


def _pack_bf16_pair(a, b):
    abits = lax.bitcast_convert_type(a.astype(BF16).astype(F32), jnp.uint32)
    bbits = lax.bitcast_convert_type(b.astype(BF16).astype(F32), jnp.uint32)
    return (abits >> 16) | (bbits & jnp.uint32(0xFFFF0000))


def _unpack_bf16_pair(w):
    a = lax.bitcast_convert_type(w << 16, F32).astype(BF16)
    b = lax.bitcast_convert_type(w & jnp.uint32(0xFFFF0000), F32).astype(BF16)
    return a, b


def _split_bf16(v):
    hi = v.astype(BF16)
    return hi, (v - hi.astype(F32)).astype(BF16)


def _merge_kernel(o0_ref, o1_ref, o2_ref, l0_ref, l1_ref, l2_ref, or_ref,
                  ga0_ref, ga1_ref, gb0_ref, gb1_ref, x_ref,
                  wa_ref, wr_ref, wo_ref, lns_ref, lnb_ref, wrt_ref, brt_ref, hexp_ref,
                  x1_ref, x1p_ref, route_ref, cnt_ref, s1_ref, s2_ref, ls1_ref, ls2_ref):
    tm = o0_ref.shape[0]
    for d, src, lsrc, dst, ldst in ((DILATIONS[1], o1_ref, l1_ref, s1_ref, ls1_ref),
                                    (DILATIONS[2], o2_ref, l2_ref, s2_ref, ls2_ref)):
        for r in range(d):
            for h in range(ATT_HEADS_PER_GROUP):
                dst[h, pl.ds(r, tm // d, stride=d), :] = src[r, :, h * ATT_HEAD_DIM:(h + 1) * ATT_HEAD_DIM]
            ldst[pl.ds(r, tm // d, stride=d), :] = lsrc[r]

    @pl.when(pl.program_id(0) == 0)
    def _():
        cnt_ref[...] = jnp.zeros_like(cnt_ref)

    for c in range(tm // MERGE_CHAIN_ROWS):
        _merge_rows(slice(c * MERGE_CHAIN_ROWS, (c + 1) * MERGE_CHAIN_ROWS),
                    o0_ref, l0_ref, or_ref, ga0_ref, ga1_ref, gb0_ref, gb1_ref, x_ref,
                    wa_ref, wr_ref, wo_ref, lns_ref, lnb_ref, wrt_ref, brt_ref, hexp_ref,
                    x1_ref, x1p_ref, route_ref, cnt_ref, s1_ref, s2_ref, ls1_ref, ls2_ref)


def _merge_rows(rs, o0_ref, l0_ref, or_ref, ga0_ref, ga1_ref, gb0_ref, gb1_ref, x_ref,
                wa_ref, wr_ref, wo_ref, lns_ref, lnb_ref, wrt_ref, brt_ref, hexp_ref,
                x1_ref, x1p_ref, route_ref, cnt_ref, s1_ref, s2_ref, ls1_ref, ls2_ref):
    rc = rs.stop - rs.start
    l0, l1, l2 = l0_ref[rs, :], ls1_ref[rs, :], ls2_ref[rs, :]
    lm = jnp.maximum(jnp.maximum(l0, l1), l2)
    e0, e1, e2 = jnp.exp(l0 - lm), jnp.exp(l1 - lm), jnp.exp(l2 - lm)
    den = e0 + e1 + e2

    def lanes_of_head(w):
        hi, lo = _split_bf16(w)
        return (jnp.dot(hi, hexp_ref[...], preferred_element_type=F32)
                + jnp.dot(lo, hexp_ref[...], preferred_element_type=F32))

    half = D_MODEL // 2
    halves = (slice(0, half), slice(half, D_MODEL))
    o_r = or_ref[rs, :]
    y_r = [jnp.dot(o_r, wr_ref[:, cs], preferred_element_type=F32) for cs in halves]
    heads = range(ATT_HEADS_PER_GROUP)
    o1 = jnp.concatenate([s1_ref[h, rs, :] for h in heads], axis=1)
    o2 = jnp.concatenate([s2_ref[h, rs, :] for h in heads], axis=1)
    o_a = (lanes_of_head(e0 / den) * o0_ref[rs, :] + lanes_of_head(e1 / den) * o1
           + lanes_of_head(e2 / den) * o2).astype(BF16)
    merged = []
    for n, cs in enumerate(halves):
        y_a = jnp.dot(o_a, wa_ref[:, cs], preferred_element_type=F32)
        gate_a = (ga0_ref, ga1_ref)[n][rs, :].astype(F32)
        gate_b = (gb0_ref, gb1_ref)[n][rs, :].astype(F32)
        merged.append((_sigmoid(gate_a) * y_a + _sigmoid(gate_b) * y_r[n]).astype(BF16))
    z = [ALPHA * x_ref[rs, cs]
         + jnp.dot(merged[0], wo_ref[:half, cs], preferred_element_type=F32)
         + jnp.dot(merged[1], wo_ref[half:, cs], preferred_element_type=F32) for cs in halves]
    mu = (jnp.sum(z[0], axis=1, keepdims=True) + jnp.sum(z[1], axis=1, keepdims=True)) / D_MODEL
    var = (jnp.sum(jnp.square(z[0] - mu), axis=1, keepdims=True)
           + jnp.sum(jnp.square(z[1] - mu), axis=1, keepdims=True)) / D_MODEL
    inv = lax.rsqrt(var + LN_EPS)
    x1 = [(z[n] - mu) * inv * lns_ref[:, cs] + lnb_ref[:, cs] for n, cs in enumerate(halves)]
    for n, cs in enumerate(halves):
        x1_ref[rs, cs] = x1[n]
    packed = _pack_bf16_pair(x1[0], x1[1])
    for c in range(PACKED_ROW_PIECES):
        x1p_ref[pl.ds(rs.start * PACKED_ROW_PIECES + c, rc, stride=PACKED_ROW_PIECES), :] = (
            packed[:, c * LANES:(c + 1) * LANES])

    logits = brt_ref[...]
    for n, cs in enumerate(halves):
        x_hi, x_lo = _split_bf16(x1[n])
        hl = jnp.dot(x_hi, wrt_ref[cs, :], preferred_element_type=F32)
        logits = (logits + hl[:, :LANES] + hl[:, LANES:]
                  + jnp.dot(x_lo, wrt_ref[cs, :LANES], preferred_element_type=F32))
    lane = lax.broadcasted_iota(jnp.int32, logits.shape, 1)
    is_g = lane < N_EXPERT_GROUPS
    gl = jnp.where(is_g, logits, NEG_INF)
    gmax = jnp.max(gl, axis=1, keepdims=True)
    g_sel = jnp.min(jnp.where(is_g & (gl == gmax), lane, LANES), axis=1, keepdims=True)
    p_group = 1.0 / jnp.sum(jnp.where(is_g, jnp.exp(gl - gmax), 0.0), axis=1, keepdims=True)
    lo = N_EXPERT_GROUPS + EXPERTS_PER_GROUP * g_sel
    in_grp = (lane >= lo) & (lane < lo + EXPERTS_PER_GROUP)
    el = jnp.where(in_grp, logits, NEG_INF)
    top1 = jnp.max(el, axis=1, keepdims=True)
    idx1 = jnp.min(jnp.where(in_grp & (el == top1), lane, LANES), axis=1, keepdims=True)
    el2 = jnp.where(lane == idx1, NEG_INF, el)
    top2 = jnp.max(el2, axis=1, keepdims=True)
    idx2 = jnp.min(jnp.where(in_grp & (lane != idx1) & (el2 == top2), lane, LANES), axis=1, keepdims=True)
    t = jnp.exp(top2 - top1)
    wt1 = p_group * (1.0 / (1.0 + t))
    wt2 = p_group * (t / (1.0 + t))
    slab = jnp.where(lane == 0, (idx1 - N_EXPERT_GROUPS).astype(F32), 0.0)
    slab = jnp.where(lane == 1, (idx2 - N_EXPERT_GROUPS).astype(F32), slab)
    slab = jnp.where(lane == 2, wt1, slab)
    slab = jnp.where(lane == 3, wt2, slab)

    hot = jnp.where((lane == idx1) | (lane == idx2), 1.0, 0.0)
    ri = lax.broadcasted_iota(jnp.int32, (rc, rc), 0)
    ci = lax.broadcasted_iota(jnp.int32, (rc, rc), 1)
    ltri = jnp.where(ri > ci, 1.0, 0.0).astype(BF16)
    prefix = jnp.dot(ltri, hot.astype(BF16), preferred_element_type=F32) + cnt_ref[...]
    rank1 = jnp.sum(jnp.where(lane == idx1, prefix, 0.0), axis=1, keepdims=True)
    rank2 = jnp.sum(jnp.where(lane == idx2, prefix, 0.0), axis=1, keepdims=True)
    cnt_ref[...] = cnt_ref[...] + jnp.sum(hot, axis=0, keepdims=True)
    slab = jnp.where(lane == 4, rank1, slab)
    slab = jnp.where(lane == 5, rank2, slab)
    route_ref[rs, :] = slab


def _merge(o_g, lse_g, o_r, nat, x2d, wa, wr, wo, lns, lnb, wrt, brt, S):
    tm = MERGE_ROWS
    D = D_MODEL
    W = ATT_GROUP_WIDTH
    head_expand = (jnp.arange(W, dtype=jnp.int32)[None, :] // ATT_HEAD_DIM
                   == jnp.arange(LANES, dtype=jnp.int32)[:, None]).astype(BF16)

    def rows(width):
        return pl.BlockSpec((tm, width), lambda i: (i, 0))

    def streams(g, width):
        d = DILATIONS[g]
        per = (ATT_BLOCK * d) // tm
        return pl.BlockSpec((None, d, tm // d, width), lambda i: (i // per, 0, i % per, 0))

    def ncol(cb):
        return pl.BlockSpec((tm, COL_BLOCK), lambda i: (i, cb))

    def const(shape):
        return pl.BlockSpec(shape, lambda i: (0, 0), pipeline_mode=pl.Buffered(1))

    return pl.pallas_call(
        _merge_kernel,
        grid=(S // tm,),
        in_specs=[
            rows(W), streams(1, W), streams(2, W),
            rows(LANES), streams(1, LANES), streams(2, LANES),
            rows(RET_HEADS * RET_V_DIM),
            ncol(NC_GATE_A), ncol(NC_GATE_A + 1), ncol(NC_GATE_B), ncol(NC_GATE_B + 1),
            rows(D),
            const((W, D)), const((RET_HEADS * RET_V_DIM, D)), const((D, D)),
            const((1, D)), const((1, D)), const((D, 2 * LANES)), const((1, LANES)), const((LANES, W)),
        ],
        out_specs=[rows(D), pl.BlockSpec((tm * PACKED_ROW_PIECES, LANES), lambda i: (i, 0)),
                   rows(LANES), pl.BlockSpec((1, LANES), lambda i: (0, 0))],
        out_shape=[jax.ShapeDtypeStruct((S, D), F32),
                   jax.ShapeDtypeStruct((S * PACKED_ROW_PIECES, LANES), jnp.uint32),
                   jax.ShapeDtypeStruct((S, LANES), F32),
                   jax.ShapeDtypeStruct((1, LANES), F32)],
        scratch_shapes=[pltpu.VMEM((ATT_HEADS_PER_GROUP, tm, ATT_HEAD_DIM), F32),
                        pltpu.VMEM((ATT_HEADS_PER_GROUP, tm, ATT_HEAD_DIM), F32),
                        pltpu.VMEM((tm, LANES), F32), pltpu.VMEM((tm, LANES), F32)],
        compiler_params=_cparams(1),
        name="merge_ln_router",
    )(*o_g, *lse_g, o_r, nat, nat, nat, nat, x2d, wa, wr, wo, lns, lnb, wrt, brt, head_expand)


META_N_USED, META_NEXT = 0, 1
WEIGHT_DMA_PRIORITY = 1
GATHER_SLOTS = 3


def _plan_kernel(dest_ref, cnt_ref, rt_ref, be_ref, meta_ref):
    n_blocks = be_ref.shape[0]

    def pad_row(r, carry):
        rt_ref[r] = 0
        return carry

    def per_expert(e, start):
        nb = lax.shift_right_logical(cnt_ref[e] + (MOE_BLOCK - 1), 7)

        def fill(b, carry):
            be_ref[lax.shift_right_logical(start, 7) + b] = e
            return carry
        lax.fori_loop(0, nb, fill, 0)
        lax.fori_loop(start + cnt_ref[e], start + nb * MOE_BLOCK, pad_row, 0)
        return start + nb * MOE_BLOCK
    total = lax.fori_loop(0, N_EXPERTS, per_expert, 0)
    n_used = lax.shift_right_logical(total, 7)
    meta_ref[META_N_USED] = n_used

    def tail(b, carry):
        be_ref[b] = be_ref[jnp.maximum(n_used - 1, 0)]
        return carry
    lax.fori_loop(n_used, n_blocks, tail, 0)
    lax.fori_loop(total, rt_ref.shape[0], pad_row, 0)

    def nxt(k, cur):
        e = N_EXPERTS - 1 - k
        meta_ref[META_NEXT + e] = cur
        return jnp.where(cnt_ref[e] > 0, e, cur)
    lax.fori_loop(0, N_EXPERTS, nxt, N_EXPERTS)

    def assign(a, carry):
        rt_ref[dest_ref[a]] = lax.shift_right_logical(a, 1)
        return carry
    lax.fori_loop(0, dest_ref.shape[0], assign, 0, unroll=8)


def _plan(dest, counts):
    A = dest.shape[0]
    n_blocks = A // MOE_BLOCK + N_EXPERTS
    smem = pl.BlockSpec(memory_space=pltpu.SMEM)
    return pl.pallas_call(
        _plan_kernel,
        in_specs=[smem, smem],
        out_specs=[smem, smem, smem],
        out_shape=[
            jax.ShapeDtypeStruct((n_blocks * MOE_BLOCK,), jnp.int32),
            jax.ShapeDtypeStruct((n_blocks,), jnp.int32),
            jax.ShapeDtypeStruct((META_NEXT + N_EXPERTS,), jnp.int32),
        ],
        name="moe_plan",
    )(dest, counts)


def _expert_kernel(be_ref, meta_ref, rt_ref, x_hbm, wg_hbm, wu_hbm, wd_hbm, y_ref,
                   xg_ref, wgf_ref, wuf_ref, wdf_ref, wgb_ref, wub_ref, wdb_ref, xsem, sem, slot_ref):
    b = pl.program_id(0)
    n_used = meta_ref[META_N_USED]
    e = be_ref[b]
    first = (b == 0) | (e != be_ref[jnp.maximum(b - 1, 0)])

    xp = PACKED_ROW_PIECES

    last = jnp.maximum(n_used - 1, 0)

    def row_copy(blk, slot, i):
        tok = rt_ref[blk * MOE_BLOCK + i]
        return pltpu.make_async_copy(x_hbm.at[pl.ds(pl.multiple_of(tok * xp, xp), xp)],
                                     xg_ref.at[slot, pl.ds(pl.multiple_of(i * xp, xp), xp)],
                                     xsem.at[slot])

    def wait_rows(slot):
        pltpu.make_async_copy(x_hbm.at[pl.ds(0, MOE_BLOCK * xp)], xg_ref.at[slot], xsem.at[slot]).wait()

    @pl.when(b == 0)
    def _():
        for blk in range(GATHER_SLOTS - 1):
            def body(i, carry, blk=blk):
                row_copy(jnp.minimum(blk, last), blk, i).start()
                return carry
            lax.fori_loop(0, MOE_BLOCK, body, 0, unroll=8)

    def copies(ex, s):
        return (pltpu.make_async_copy(wg_hbm.at[ex], wgf_ref.at[s], sem.at[s]),
                pltpu.make_async_copy(wu_hbm.at[ex], wuf_ref.at[s], sem.at[s]),
                pltpu.make_async_copy(wd_hbm.at[ex], wdf_ref.at[s], sem.at[s]))

    @pl.when((b == 0) & (n_used > 0))
    def _():
        slot_ref[0] = 0
        for c in copies(e, 0):
            c.start(priority=WEIGHT_DMA_PRIORITY)

    @pl.when(first & (b < n_used))
    def _():
        s = slot_ref[0]
        nx = meta_ref[META_NEXT + e]

        @pl.when(nx < N_EXPERTS)
        def _():
            for c in copies(nx, 1 - s):
                c.start(priority=WEIGHT_DMA_PRIORITY)
        for c in copies(e, s):
            c.wait()
        wgb_ref[...] = wgf_ref[s].astype(BF16)
        wub_ref[...] = wuf_ref[s].astype(BF16)
        wdb_ref[...] = wdf_ref[s].astype(BF16)
        slot_ref[0] = 1 - s

    def step(slot):
        wait_rows(slot)
        nxt = jnp.minimum(b + (GATHER_SLOTS - 1), last)
        for i in range(MOE_BLOCK):
            row_copy(nxt, (slot + GATHER_SLOTS - 1) % GATHER_SLOTS, i).start()
        pieces = [_unpack_bf16_pair(xg_ref[slot, pl.ds(c, MOE_BLOCK, stride=xp), :]) for c in range(xp)]
        xb = jnp.concatenate([p[0] for p in pieces] + [p[1] for p in pieces], axis=1)
        hg = jnp.dot(xb, wgb_ref[...], preferred_element_type=F32)
        hu = jnp.dot(xb, wub_ref[...], preferred_element_type=F32)
        hid = (hg * _sigmoid(hg) * hu).astype(BF16)
        y = jnp.dot(hid, wdb_ref[...], preferred_element_type=F32)
        for c in range(ROW_PIECES):
            y_ref[pl.ds(c, MOE_BLOCK, stride=ROW_PIECES), :] = y[:, c * LANES:(c + 1) * LANES]

    for s in range(GATHER_SLOTS):
        pl.when((b < n_used) & (lax.rem(b, GATHER_SLOTS) == s))(functools.partial(step, s))

    @pl.when(b == last)
    def _():
        for ahead in range(1, GATHER_SLOTS):
            wait_rows(lax.rem(b + ahead, GATHER_SLOTS))

    @pl.when(b >= n_used)
    def _():
        y_ref[...] = jnp.zeros_like(y_ref)


def _experts(x1p, block_expert, meta, row_tok, w_gate, w_up, w_down):
    D = D_MODEL
    assert x1p.shape[1] == LANES and x1p.dtype == jnp.uint32
    n_blocks = block_expert.shape[0]
    R = n_blocks * MOE_BLOCK
    H = EXPERT_HIDDEN
    grid_spec = pltpu.PrefetchScalarGridSpec(
        num_scalar_prefetch=3,
        grid=(n_blocks,),
        in_specs=[
            pl.BlockSpec(memory_space=pl.ANY),
            pl.BlockSpec(memory_space=pl.ANY),
            pl.BlockSpec(memory_space=pl.ANY),
            pl.BlockSpec(memory_space=pl.ANY),
        ],
        out_specs=pl.BlockSpec((MOE_BLOCK * ROW_PIECES, LANES), lambda b, be, meta, rt: (b, 0)),
        scratch_shapes=[
            pltpu.VMEM((GATHER_SLOTS, MOE_BLOCK * PACKED_ROW_PIECES, LANES), jnp.uint32),
            pltpu.VMEM((2, D, H), F32),
            pltpu.VMEM((2, D, H), F32),
            pltpu.VMEM((2, H, D), F32),
            pltpu.VMEM((D, H), BF16),
            pltpu.VMEM((D, H), BF16),
            pltpu.VMEM((H, D), BF16),
            pltpu.SemaphoreType.DMA((GATHER_SLOTS,)),
            pltpu.SemaphoreType.DMA((2,)),
            pltpu.SMEM((1,), jnp.int32),
        ],
    )
    return pl.pallas_call(
        _expert_kernel,
        grid_spec=grid_spec,
        out_shape=jax.ShapeDtypeStruct((R * ROW_PIECES, LANES), F32),
        compiler_params=_cparams(1),
        name="moe_experts",
    )(block_expert, meta, row_tok, x1p, w_gate, w_up, w_down)


COMBINE_TOKENS = 64


def _combine_kernel(dest_ref, y_hbm, x1_ref, route_ref, lns_ref, lnb_ref, o_ref, yg_ref, sem):
    i = pl.program_id(0)
    tt = COMBINE_TOKENS
    n = dest_ref.shape[0] // (TOP_K * tt)

    rp = ROW_PIECES

    def start_token(step, slot, r):
        for k in range(TOP_K):
            row = dest_ref[(step * tt + r) * TOP_K + k]
            pltpu.make_async_copy(y_hbm.at[pl.ds(pl.multiple_of(row * rp, rp), rp)],
                                  yg_ref.at[slot, k, pl.ds(pl.multiple_of(r * rp, rp), rp)],
                                  sem.at[slot]).start(priority=k)

    def wait_slot(slot):
        for k in range(TOP_K):
            pltpu.make_async_copy(y_hbm.at[pl.ds(0, tt * rp)], yg_ref.at[slot, k], sem.at[slot]).wait()

    @pl.when(i == 0)
    def _():
        for ahead in range(GATHER_SLOTS - 1):
            def body(r, carry, ahead=ahead):
                start_token(min(ahead, n - 1), ahead, r)
                return carry
            lax.fori_loop(0, tt, body, 0, unroll=4)

    def step(slot):
        wait_slot(slot)
        nxt = jnp.minimum(i + (GATHER_SLOTS - 1), n - 1)
        for r in range(tt):
            start_token(nxt, (slot + GATHER_SLOTS - 1) % GATHER_SLOTS, r)
        route = route_ref[...]
        w1, w2 = route[:, 2:3], route[:, 3:4]
        moe = jnp.concatenate(
            [w1 * yg_ref[slot, 0, pl.ds(c, tt, stride=rp), :] + w2 * yg_ref[slot, 1, pl.ds(c, tt, stride=rp), :]
             for c in range(rp)], axis=1)
        o_ref[...] = _layer_norm(ALPHA * x1_ref[...] + moe, lns_ref[...], lnb_ref[...])

    for s in range(GATHER_SLOTS):
        pl.when(lax.rem(i, GATHER_SLOTS) == s)(functools.partial(step, s))

    @pl.when(i == n - 1)
    def _():
        for ahead in range(1, GATHER_SLOTS):
            wait_slot((n - 1 + ahead) % GATHER_SLOTS)


def _combine(dest, y, x1, route, lns, lnb):
    T, D = x1.shape
    tt = COMBINE_TOKENS
    grid_spec = pltpu.PrefetchScalarGridSpec(
        num_scalar_prefetch=1,
        grid=(T // tt,),
        in_specs=[
            pl.BlockSpec(memory_space=pl.ANY),
            pl.BlockSpec((tt, D), lambda i, d: (i, 0)),
            pl.BlockSpec((tt, LANES), lambda i, d: (i, 0)),
            pl.BlockSpec((1, D), lambda i, d: (0, 0)),
            pl.BlockSpec((1, D), lambda i, d: (0, 0)),
        ],
        out_specs=pl.BlockSpec((tt, D), lambda i, d: (i, 0)),
        scratch_shapes=[
            pltpu.VMEM((GATHER_SLOTS, TOP_K, tt * ROW_PIECES, LANES), F32),
            pltpu.SemaphoreType.DMA((GATHER_SLOTS,)),
        ],
    )
    return pl.pallas_call(
        _combine_kernel,
        grid_spec=grid_spec,
        out_shape=jax.ShapeDtypeStruct((T, D), F32),
        compiler_params=_cparams(1),
        name="moe_combine_ln",
    )(dest, y, x1, route, lns, lnb)


def _rotary_tables(positions):
    pos = positions.reshape(-1).astype(F32)
    S = pos.shape[0]
    inv_a = ROPE_THETA ** (-jnp.arange(0, ROPE_DIM, 2, dtype=F32) / ROPE_DIM)
    inv_r = RET_ROT_BASE ** (-jnp.linspace(0.0, 1.0, RET_QK_DIM // 2, dtype=F32))
    ang = jnp.concatenate([inv_a, inv_r])[:, None] * pos[None, :]
    cos_t, sin_t = lax.optimization_barrier((jnp.cos(ang), jnp.sin(ang)))
    na = ROPE_DIM // 2
    cos, sin, cr, sr = cos_t[:na].T, sin_t[:na].T, cos_t[na:].T, sin_t[na:].T
    pad1 = jnp.ones((S, LANES - ROPE_DIM), F32)
    pad0 = jnp.zeros((S, LANES - ROPE_DIM), F32)
    tabs_a = jnp.concatenate([cos, cos, pad1, sin, sin, pad0], axis=1)
    tabs_r = jnp.concatenate([cr, cr, sr, sr], axis=1)
    return tabs_a, tabs_r


def kernel(x, positions, w_in, w_branch_attn, w_branch_ret, w_out, ln1_scale, ln1_bias, w_group_router, b_group_router, w_expert_router, b_expert_router, w_expert_gate, w_expert_up, w_expert_down, ln2_scale, ln2_bias):
    B, S, D = x.shape
    assert B == 1 and D == D_MODEL and w_in.shape[0] == DEPTH == 1
    assert S % (ATT_BLOCK * DILATIONS[2]) == 0
    x2d = x.reshape(S, D)
    xb = x2d.astype(BF16)
    tabs_a, tabs_r = _rotary_tables(positions)

    nat = _input_projection(xb, w_in[0], tabs_a, tabs_r, NAT_COLS, NAT_MODES, 1)
    qkv_modes = (MODE_ATT_Q, MODE_ATT_K, MODE_PLAIN)
    qkv = [None] + [
        _input_projection(xb, w_in[0], tabs_a, tabs_r, (CB_QA + g, CB_KA + g, CB_VA + g), qkv_modes,
                          DILATIONS[g])
        for g in (1, 2)]
    o0, l0 = _dilated_attention(nat.reshape(S // ATT_BLOCK, 1, ATT_BLOCK, nat.shape[1]), 0,
                                NC_Q0, NC_K0, NC_V0)
    att = [(o0.reshape(S, ATT_GROUP_WIDTH), l0.reshape(S, LANES))]
    att += [_dilated_attention(qkv[g], g, 0, 1, 2) for g in (1, 2)]
    o_r = _retention(nat, S)

    w_route = jnp.concatenate(
        [w_group_router[0], w_expert_router[0].transpose(1, 0, 2).reshape(D, N_EXPERTS),
         jnp.zeros((D, LANES - N_EXPERT_GROUPS - N_EXPERTS), F32)], axis=1)
    b_route = jnp.concatenate(
        [b_group_router[0], b_expert_router[0].reshape(N_EXPERTS),
         jnp.zeros((LANES - N_EXPERT_GROUPS - N_EXPERTS,), F32)]).reshape(1, LANES)
    x1, x1p, route, cnt = _merge(
        [a[0] for a in att], [a[1] for a in att], o_r, nat, x2d,
        w_branch_attn[0].astype(BF16), w_branch_ret[0].astype(BF16), w_out[0].astype(BF16),
        ln1_scale[0].reshape(1, D), ln1_bias[0].reshape(1, D),
        jnp.concatenate(_split_bf16(w_route), axis=1), b_route, S)

    e_flat = route[:, 0:TOP_K].astype(jnp.int32).reshape(S * TOP_K)
    rank_flat = route[:, 4:4 + TOP_K].astype(jnp.int32).reshape(S * TOP_K)
    counts = cnt[0, N_EXPERT_GROUPS:N_EXPERT_GROUPS + N_EXPERTS].astype(jnp.int32)
    padded = (counts + (MOE_BLOCK - 1)) // MOE_BLOCK * MOE_BLOCK
    pstart = jnp.cumsum(padded) - padded
    hot = e_flat[:, None] == jnp.arange(N_EXPERTS, dtype=jnp.int32)[None, :]
    dest = jnp.sum(jnp.where(hot, pstart[None, :], 0), axis=1) + rank_flat
    row_tok, block_expert, meta = _plan(dest, counts)
    y = _experts(x1p, block_expert, meta, row_tok, w_expert_gate[0], w_expert_up[0], w_expert_down[0])
    out = _combine(dest, y, x1, route, ln2_scale[0].reshape(1, D), ln2_bias[0].reshape(1, D))
    return out.reshape(B, S, D)
```

```python
import functools
import math

import jax
import jax.numpy as jnp
from jax import lax
from jax.experimental import pallas as pl
from jax.experimental.pallas import tpu as pltpu

F32 = jnp.float32
BF16 = jnp.bfloat16

D_MODEL = 2048
ATT_HEAD_DIM = 128
ATT_HEADS_PER_GROUP = 8
DILATIONS = (1, 4, 16)
ATT_STEPS = 128
N_ATT_GROUPS = 3
ATT_GROUP_WIDTH = ATT_HEADS_PER_GROUP * ATT_HEAD_DIM
ROPE_DIM = ATT_HEAD_DIM // 4
ROPE_THETA = 500000.0
ATT_BLOCK = 128

RET_HEADS = 8
RET_QK_DIM = 128
RET_V_DIM = 256
RET_CHUNK = 128
RET_ROT_BASE = 10000.0
RET_LOG_DECAY = tuple(math.log1p(-(2.0 ** (-5.0 - h))) for h in range(RET_HEADS))

N_EXPERT_GROUPS = 4
EXPERTS_PER_GROUP = 8
N_EXPERTS = N_EXPERT_GROUPS * EXPERTS_PER_GROUP
TOP_K = 2
EXPERT_HIDDEN = 512
MOE_BLOCK = 128

DEPTH = 1
ALPHA = (2.0 * DEPTH) ** 0.25
LN_EPS = 1e-5
NEG_INF = -1e30

COL_BLOCK = 1024
CB_QA, CB_KA, CB_VA = 0, 3, 6
CB_QR, CB_KR, CB_VR, CB_GR = 9, 10, 11, 13
CB_GATE_A, CB_GATE_B = 15, 17

MODE_PLAIN, MODE_ATT_Q, MODE_ATT_K, MODE_RET_Q, MODE_RET_K = 0, 1, 2, 3, 4

NAT_COLS = (CB_QA, CB_KA, CB_VA, CB_QR, CB_KR, CB_VR, CB_VR + 1, CB_GR, CB_GR + 1,
            CB_GATE_A, CB_GATE_A + 1, CB_GATE_B, CB_GATE_B + 1)
NAT_MODES = (MODE_ATT_Q, MODE_ATT_K, MODE_PLAIN, MODE_RET_Q, MODE_RET_K) + (MODE_PLAIN,) * 8
NC_Q0, NC_K0, NC_V0, NC_QR, NC_KR, NC_VR, NC_GR, NC_GATE_A, NC_GATE_B = 0, 1, 2, 3, 4, 5, 7, 9, 11

LANES = 128
ROW_PIECES = D_MODEL // LANES
PACKED_ROW_PIECES = ROW_PIECES // 2
VMEM_LIMIT = 52 * 1024 * 1024
PROJ_ROWS = 1024
ACC_PITCH = 24


def _cparams(n_grid_dims):
    return pltpu.CompilerParams(
        dimension_semantics=("arbitrary",) * n_grid_dims,
        vmem_limit_bytes=VMEM_LIMIT,
    )


def _inproj_kernel(cols_ref, modes_ref, x_ref, w_ref, ta_ref, tr_ref, pa_ref, pr_ref, o_ref,
                   wbf_ref, *acc, d):
    del cols_ref
    j = pl.program_id(0)

    @pl.when(pl.program_id(1) == 0)
    def _():
        wbf_ref[...] = w_ref[...].astype(BF16)

    mode = modes_ref[j]
    tm, tn = x_ref.shape[0], wbf_ref.shape[1]
    pair = 2 * LANES

    def project():
        return jnp.dot(x_ref[...], wbf_ref[...], preferred_element_type=F32)

    def emit(p, piece):
        if d == 1:
            o_ref[:, p * pair:(p + 1) * pair] = piece.astype(o_ref.dtype)
        elif d == 4:
            acc[0][2 * p] = piece[:, :LANES]
            acc[0][2 * p + 1] = piece[:, LANES:]
        else:
            for g in range(tm // d):
                rows = slice(g * d, (g + 1) * d)
                dst = slice(g * ACC_PITCH, g * ACC_PITCH + d)
                acc[0][2 * p, dst, :] = piece[rows, :LANES]
                acc[0][2 * p + 1, dst, :] = piece[rows, LANES:]

    def plain():
        res = project()
        for p in range(tn // pair):
            emit(p, res[:, p * pair:(p + 1) * pair])

    def rotary(tab_ref, perm_ref, scale):
        res = project()
        c = jnp.concatenate([tab_ref[:, :LANES]] * 2, axis=1) * scale
        s = jnp.concatenate([tab_ref[:, LANES:]] * 2, axis=1) * scale
        for p in range(tn // pair):
            a = res[:, p * pair:(p + 1) * pair]
            partner = jnp.dot(a.astype(BF16), perm_ref[...], preferred_element_type=F32)
            emit(p, a * c + partner * s)

    is_att = (mode == MODE_ATT_Q) | (mode == MODE_ATT_K)
    is_ret = (mode == MODE_RET_Q) | (mode == MODE_RET_K)
    pl.when(mode == MODE_PLAIN)(plain)
    pl.when(is_att)(lambda: rotary(ta_ref, pa_ref, jnp.where(mode == MODE_ATT_Q, ATT_HEAD_DIM ** -0.5, 1.0)))
    if d == 1:
        pl.when(is_ret)(lambda: rotary(tr_ref, pr_ref, jnp.where(mode == MODE_RET_K, RET_QK_DIM ** -0.5, 1.0)))
        return

    acc_ref = acc[0]
    for h in range(tn // LANES):
        sl = slice(h * LANES, (h + 1) * LANES)
        if d == 4:
            P = ATT_BLOCK * d
            for sb in range(tm // P):
                for r in range(d):
                    o_ref[sb, r, :, sl] = acc_ref[h, pl.ds(sb * P + r, ATT_BLOCK, stride=d), :].astype(o_ref.dtype)
        else:
            for r in range(d):
                o_ref[r, :, sl] = acc_ref[h, pl.ds(r, tm // d, stride=ACC_PITCH), :].astype(o_ref.dtype)


def _rotation_matrix(half, width):
    k = jnp.arange(2 * LANES, dtype=jnp.int32)[:, None]
    l = jnp.arange(2 * LANES, dtype=jnp.int32)[None, :]
    same_head = (k // LANES) == (l // LANES)
    kk, ll = k % LANES, l % LANES
    neg = same_head & (ll < half) & (kk == ll + half)
    pos = same_head & (ll >= half) & (ll < width) & (kk == ll - half)
    return (pos.astype(F32) - neg.astype(F32)).astype(BF16)


def _input_projection(xb, w_in, tabs_a, tabs_r, cols, modes, d):
    S, D = xb.shape
    tm, tn = PROJ_ROWS, COL_BLOCK
    n = len(cols)
    P = ATT_BLOCK * d
    if d == 1:
        out_shape = jax.ShapeDtypeStruct((S, n * tn), BF16)
        out_spec = pl.BlockSpec((tm, tn), lambda j, i, c, m: (i, j))
    elif d == 4:
        out_shape = jax.ShapeDtypeStruct((S // P, d, ATT_BLOCK, n * tn), BF16)
        out_spec = pl.BlockSpec((tm // P, d, ATT_BLOCK, tn), lambda j, i, c, m: (i, 0, 0, j))
    else:
        assert d == 16 and P == 2 * tm
        out_shape = jax.ShapeDtypeStruct((S // P, d, ATT_BLOCK, n * tn), BF16)
        out_spec = pl.BlockSpec((None, d, tm // d, tn), lambda j, i, c, m: (i // 2, 0, i % 2, j))

    def tab_rows(j, i, c, m):
        return (jnp.where(m[j] != MODE_PLAIN, i, 0), 0)

    grid_spec = pltpu.PrefetchScalarGridSpec(
        num_scalar_prefetch=2,
        grid=(n, S // tm),
        in_specs=[
            pl.BlockSpec((tm, D), lambda j, i, c, m: (i, 0)),
            pl.BlockSpec((D, tn), lambda j, i, c, m: (0, c[j])),
            pl.BlockSpec((tm, 2 * LANES), tab_rows),
            pl.BlockSpec((tm, 2 * LANES), tab_rows),
            pl.BlockSpec((2 * LANES, 2 * LANES), lambda j, i, c, m: (0, 0)),
            pl.BlockSpec((2 * LANES, 2 * LANES), lambda j, i, c, m: (0, 0)),
        ],
        out_specs=out_spec,
        scratch_shapes=[pltpu.VMEM((D, tn), BF16)]
        + {1: [], 4: [pltpu.VMEM((tn // LANES, tm, LANES), F32)],
           16: [pltpu.VMEM((tn // LANES, tm // 16 * ACC_PITCH, LANES), F32)]}[d],
    )
    return pl.pallas_call(
        functools.partial(_inproj_kernel, d=d),
        grid_spec=grid_spec,
        out_shape=out_shape,
        compiler_params=_cparams(2),
        name=f"in_proj_d{d}",
    )(jnp.asarray(cols, jnp.int32), jnp.asarray(modes, jnp.int32), xb, w_in, tabs_a, tabs_r,
      _rotation_matrix(ROPE_DIM // 2, ROPE_DIM), _rotation_matrix(RET_QK_DIM // 2, RET_QK_DIM))


ATT_Q_BLOCKS = 2


def _attn_kernel(q_ref, k_ref, kp_ref, v_ref, vp_ref, o_ref, lse_ref):
    n = pl.program_id(1)
    qi = lax.broadcasted_iota(jnp.int32, (ATT_BLOCK, 2 * ATT_BLOCK), 0)
    kj = lax.broadcasted_iota(jnp.int32, (ATT_BLOCK, 2 * ATT_BLOCK), 1)
    dist = qi + ATT_BLOCK - kj
    band = (dist >= 0) & (dist <= ATT_STEPS)
    lane = lax.broadcasted_iota(jnp.int32, (ATT_BLOCK, LANES), 1)
    for sub in range(ATT_Q_BLOCKS):
        mask = band & ((kj >= ATT_BLOCK) | (n > 0)) if sub == 0 else band
        lse_slab = jnp.zeros((ATT_BLOCK, LANES), F32)
        for h in range(ATT_HEADS_PER_GROUP):
            sl = slice(h * ATT_HEAD_DIM, (h + 1) * ATT_HEAD_DIM)
            k_prev = kp_ref[:, sl] if sub == 0 else k_ref[sub - 1, :, sl]
            v_prev = vp_ref[:, sl] if sub == 0 else v_ref[sub - 1, :, sl]
            k2 = jnp.concatenate([k_prev, k_ref[sub, :, sl]], axis=0)
            v2 = jnp.concatenate([v_prev, v_ref[sub, :, sl]], axis=0)
            s = lax.dot_general(q_ref[sub, :, sl], k2, (((1,), (1,)), ((), ())),
                                preferred_element_type=F32)
            s = jnp.where(mask, s, NEG_INF)
            m = jnp.max(s, axis=1, keepdims=True)
            p = jnp.exp(s - m)
            l = jnp.sum(p, axis=1, keepdims=True)
            o = jnp.dot(p.astype(BF16), v2, preferred_element_type=F32)
            o_ref[sub, :, sl] = o / l
            lse_slab = jnp.where(lane == h, m + jnp.log(l), lse_slab)
        lse_ref[sub] = lse_slab


def _dilated_attention(qkv, g, cq, ck, cv):
    d = DILATIONS[g]
    W = ATT_GROUP_WIDTH
    nb = qkv.shape[0]
    nq = ATT_Q_BLOCKS
    assert qkv.shape[1] == d and nb % nq == 0
    blk = (nq, None, ATT_BLOCK, W)
    pblk = (None, None, ATT_BLOCK, W)
    cur = lambda c: (lambda r, n: (n, r, 0, c))
    prev = lambda c: (lambda r, n: (jnp.maximum(nq * n - 1, 0), r, 0, c))
    return pl.pallas_call(
        _attn_kernel,
        grid=(d, nb // nq),
        in_specs=[
            pl.BlockSpec(blk, cur(cq)),
            pl.BlockSpec(blk, cur(ck)),
            pl.BlockSpec(pblk, prev(ck)),
            pl.BlockSpec(blk, cur(cv)),
            pl.BlockSpec(pblk, prev(cv)),
        ],
        out_specs=[pl.BlockSpec(blk, lambda r, n: (n, r, 0, 0)),
                   pl.BlockSpec((nq, None, ATT_BLOCK, LANES), lambda r, n: (n, r, 0, 0))],
        out_shape=[jax.ShapeDtypeStruct((nb, d, ATT_BLOCK, W), F32),
                   jax.ShapeDtypeStruct((nb, d, ATT_BLOCK, LANES), F32)],
        compiler_params=_cparams(2),
        name=f"dilated_attn_g{g}",
    )(qkv, qkv, qkv, qkv, qkv)


def _retention_kernel(q_ref, k_ref, vlo_ref, vhi_ref, glo_ref, ghi_ref, o_ref,
                      state_ref, intra_ref, dec_ref):
    @pl.when(pl.program_id(0) == 0)
    def _():
        state_ref[...] = jnp.zeros_like(state_ref)
        ri = lax.broadcasted_iota(jnp.int32, (RET_CHUNK, RET_CHUNK), 0)
        ci = lax.broadcasted_iota(jnp.int32, (RET_CHUNK, RET_CHUNK), 1)
        diff = (ri - ci).astype(F32)
        nrow = ri.astype(F32)
        for h in range(RET_HEADS):
            ld = RET_LOG_DECAY[h]
            intra_ref[h] = jnp.where(diff >= 0, jnp.exp(jnp.maximum(diff, 0.0) * ld), 0.0)
            dec_ref[h, 0] = jnp.exp((nrow + 1.0) * ld)
            dec_ref[h, 1] = jnp.exp((RET_CHUNK - 1.0 - nrow) * ld)

    half = RET_HEADS // 2
    for h in range(RET_HEADS):
        ld = RET_LOG_DECAY[h]
        sl = slice(h * RET_QK_DIM, (h + 1) * RET_QK_DIM)
        vsl = slice((h % half) * RET_V_DIM, (h % half + 1) * RET_V_DIM)
        v_ref, g_ref = (vlo_ref, glo_ref) if h < half else (vhi_ref, ghi_ref)
        q = q_ref[:, sl]
        k = k_ref[:, sl]
        v = v_ref[:, vsl]
        scores = (lax.dot_general(q, k, (((1,), (1,)), ((), ())), preferred_element_type=F32)
                  * intra_ref[h])
        inner = jnp.dot(scores.astype(BF16), v, preferred_element_type=F32)
        state = state_ref[h]
        cross = jnp.dot((q.astype(F32) * dec_ref[h, 0]).astype(BF16), state.astype(BF16),
                        preferred_element_type=F32)
        kv = lax.dot_general((k.astype(F32) * dec_ref[h, 1]).astype(BF16), v, (((0,), (0,)), ((), ())),
                             preferred_element_type=F32)
        state_ref[h] = math.exp(RET_CHUNK * ld) * state + kv
        o = inner + cross
        mu = jnp.mean(o, axis=1, keepdims=True)
        var = jnp.mean(jnp.square(o - mu), axis=1, keepdims=True)
        o = (o - mu) * lax.rsqrt(var + LN_EPS)
        gate = g_ref[:, vsl].astype(F32)
        gate = gate * (1.0 / (1.0 + jnp.exp(-gate)))
        o_ref[:, h * RET_V_DIM:(h + 1) * RET_V_DIM] = (gate * o).astype(o_ref.dtype)


def _retention(nat, S):
    nc = S // RET_CHUNK
    blk = (RET_CHUNK, COL_BLOCK)

    def col(cb):
        return pl.BlockSpec(blk, lambda c: (c, cb))

    return pl.pallas_call(
        _retention_kernel,
        grid=(nc,),
        in_specs=[col(NC_QR), col(NC_KR), col(NC_VR), col(NC_VR + 1), col(NC_GR), col(NC_GR + 1)],
        out_specs=pl.BlockSpec((RET_CHUNK, RET_HEADS * RET_V_DIM), lambda c: (c, 0)),
        out_shape=jax.ShapeDtypeStruct((S, RET_HEADS * RET_V_DIM), BF16),
        scratch_shapes=[pltpu.VMEM((RET_HEADS, RET_QK_DIM, RET_V_DIM), F32),
                        pltpu.VMEM((RET_HEADS, RET_CHUNK, RET_CHUNK), F32),
                        pltpu.VMEM((RET_HEADS, 2, RET_CHUNK, RET_QK_DIM), F32)],
        compiler_params=_cparams(1),
        name="retention",
    )(nat, nat, nat, nat, nat, nat)


MERGE_ROWS = 256
MERGE_CHAIN_ROWS = 256


def _layer_norm(z, scale, bias):
    mu = jnp.mean(z, axis=1, keepdims=True)
    var = jnp.mean(jnp.square(z - mu), axis=1, keepdims=True)
    return (z - mu) * lax.rsqrt(var + LN_EPS) * scale + bias


def _sigmoid(x):
    return 1.0 / (1.0 + jnp.exp(-x))


def _pack_bf16_pair(a, b):
    abits = lax.bitcast_convert_type(a.astype(BF16).astype(F32), jnp.uint32)
    bbits = lax.bitcast_convert_type(b.astype(BF16).astype(F32), jnp.uint32)
    return (abits >> 16) | (bbits & jnp.uint32(0xFFFF0000))


def _unpack_bf16_pair(w):
    a = lax.bitcast_convert_type(w << 16, F32).astype(BF16)
    b = lax.bitcast_convert_type(w & jnp.uint32(0xFFFF0000), F32).astype(BF16)
    return a, b


def _split_bf16(v):
    hi = v.astype(BF16)
    return hi, (v - hi.astype(F32)).astype(BF16)


def _merge_kernel(o0_ref, o1_ref, o2_ref, l0_ref, l1_ref, l2_ref, or_ref,
                  ga0_ref, ga1_ref, gb0_ref, gb1_ref, x_ref,
                  wa_ref, wr_ref, wo_ref, lns_ref, lnb_ref, wrt_ref, brt_ref, hexp_ref,
                  x1_ref, x1p_ref, route_ref, cnt_ref, s1_ref, s2_ref, ls1_ref, ls2_ref):
    tm = o0_ref.shape[0]
    for d, src, lsrc, dst, ldst in ((DILATIONS[1], o1_ref, l1_ref, s1_ref, ls1_ref),
                                    (DILATIONS[2], o2_ref, l2_ref, s2_ref, ls2_ref)):
        for r in range(d):
            for h in range(ATT_HEADS_PER_GROUP):
                dst[h, pl.ds(r, tm // d, stride=d), :] = src[r, :, h * ATT_HEAD_DIM:(h + 1) * ATT_HEAD_DIM]
            ldst[pl.ds(r, tm // d, stride=d), :] = lsrc[r]

    @pl.when(pl.program_id(0) == 0)
    def _():
        cnt_ref[...] = jnp.zeros_like(cnt_ref)

    for c in range(tm // MERGE_CHAIN_ROWS):
        _merge_rows(slice(c * MERGE_CHAIN_ROWS, (c + 1) * MERGE_CHAIN_ROWS),
                    o0_ref, l0_ref, or_ref, ga0_ref, ga1_ref, gb0_ref, gb1_ref, x_ref,
                    wa_ref, wr_ref, wo_ref, lns_ref, lnb_ref, wrt_ref, brt_ref, hexp_ref,
                    x1_ref, x1p_ref, route_ref, cnt_ref, s1_ref, s2_ref, ls1_ref, ls2_ref)


def _merge_rows(rs, o0_ref, l0_ref, or_ref, ga0_ref, ga1_ref, gb0_ref, gb1_ref, x_ref,
                wa_ref, wr_ref, wo_ref, lns_ref, lnb_ref, wrt_ref, brt_ref, hexp_ref,
                x1_ref, x1p_ref, route_ref, cnt_ref, s1_ref, s2_ref, ls1_ref, ls2_ref):
    rc = rs.stop - rs.start
    l0, l1, l2 = l0_ref[rs, :], ls1_ref[rs, :], ls2_ref[rs, :]
    lm = jnp.maximum(jnp.maximum(l0, l1), l2)
    e0, e1, e2 = jnp.exp(l0 - lm), jnp.exp(l1 - lm), jnp.exp(l2 - lm)
    den = e0 + e1 + e2

    def lanes_of_head(w):
        hi, lo = _split_bf16(w)
        return (jnp.dot(hi, hexp_ref[...], preferred_element_type=F32)
                + jnp.dot(lo, hexp_ref[...], preferred_element_type=F32))

    half = D_MODEL // 2
    halves = (slice(0, half), slice(half, D_MODEL))
    o_r = or_ref[rs, :]
    y_r = [jnp.dot(o_r, wr_ref[:, cs], preferred_element_type=F32) for cs in halves]
    heads = range(ATT_HEADS_PER_GROUP)
    o1 = jnp.concatenate([s1_ref[h, rs, :] for h in heads], axis=1)
    o2 = jnp.concatenate([s2_ref[h, rs, :] for h in heads], axis=1)
    o_a = (lanes_of_head(e0 / den) * o0_ref[rs, :] + lanes_of_head(e1 / den) * o1
           + lanes_of_head(e2 / den) * o2).astype(BF16)
    merged = []
    for n, cs in enumerate(halves):
        y_a = jnp.dot(o_a, wa_ref[:, cs], preferred_element_type=F32)
        gate_a = (ga0_ref, ga1_ref)[n][rs, :].astype(F32)
        gate_b = (gb0_ref, gb1_ref)[n][rs, :].astype(F32)
        merged.append((_sigmoid(gate_a) * y_a + _sigmoid(gate_b) * y_r[n]).astype(BF16))
    z = [ALPHA * x_ref[rs, cs]
         + jnp.dot(merged[0], wo_ref[:half, cs], preferred_element_type=F32)
         + jnp.dot(merged[1], wo_ref[half:, cs], preferred_element_type=F32) for cs in halves]
    mu = (jnp.sum(z[0], axis=1, keepdims=True) + jnp.sum(z[1], axis=1, keepdims=True)) / D_MODEL
    var = (jnp.sum(jnp.square(z[0] - mu), axis=1, keepdims=True)
           + jnp.sum(jnp.square(z[1] - mu), axis=1, keepdims=True)) / D_MODEL
    inv = lax.rsqrt(var + LN_EPS)
    x1 = [(z[n] - mu) * inv * lns_ref[:, cs] + lnb_ref[:, cs] for n, cs in enumerate(halves)]
    for n, cs in enumerate(halves):
        x1_ref[rs, cs] = x1[n]
    packed = _pack_bf16_pair(x1[0], x1[1])
    for c in range(PACKED_ROW_PIECES):
        x1p_ref[pl.ds(rs.start * PACKED_ROW_PIECES + c, rc, stride=PACKED_ROW_PIECES), :] = (
            packed[:, c * LANES:(c + 1) * LANES])

    logits = brt_ref[...]
    for n, cs in enumerate(halves):
        x_hi, x_lo = _split_bf16(x1[n])
        hl = jnp.dot(x_hi, wrt_ref[cs, :], preferred_element_type=F32)
        logits = (logits + hl[:, :LANES] + hl[:, LANES:]
                  + jnp.dot(x_lo, wrt_ref[cs, :LANES], preferred_element_type=F32))
    lane = lax.broadcasted_iota(jnp.int32, logits.shape, 1)
    is_g = lane < N_EXPERT_GROUPS
    gl = jnp.where(is_g, logits, NEG_INF)
    gmax = jnp.max(gl, axis=1, keepdims=True)
    g_sel = jnp.min(jnp.where(is_g & (gl == gmax), lane, LANES), axis=1, keepdims=True)
    p_group = 1.0 / jnp.sum(jnp.where(is_g, jnp.exp(gl - gmax), 0.0), axis=1, keepdims=True)
    lo = N_EXPERT_GROUPS + EXPERTS_PER_GROUP * g_sel
    in_grp = (lane >= lo) & (lane < lo + EXPERTS_PER_GROUP)
    el = jnp.where(in_grp, logits, NEG_INF)
    top1 = jnp.max(el, axis=1, keepdims=True)
    idx1 = jnp.min(jnp.where(in_grp & (el == top1), lane, LANES), axis=1, keepdims=True)
    el2 = jnp.where(lane == idx1, NEG_INF, el)
    top2 = jnp.max(el2, axis=1, keepdims=True)
    idx2 = jnp.min(jnp.where(in_grp & (lane != idx1) & (el2 == top2), lane, LANES), axis=1, keepdims=True)
    t = jnp.exp(top2 - top1)
    wt1 = p_group * (1.0 / (1.0 + t))
    wt2 = p_group * (t / (1.0 + t))
    slab = jnp.where(lane == 0, (idx1 - N_EXPERT_GROUPS).astype(F32), 0.0)
    slab = jnp.where(lane == 1, (idx2 - N_EXPERT_GROUPS).astype(F32), slab)
    slab = jnp.where(lane == 2, wt1, slab)
    slab = jnp.where(lane == 3, wt2, slab)

    hot = jnp.where((lane == idx1) | (lane == idx2), 1.0, 0.0)
    ri = lax.broadcasted_iota(jnp.int32, (rc, rc), 0)
    ci = lax.broadcasted_iota(jnp.int32, (rc, rc), 1)
    ltri = jnp.where(ri > ci, 1.0, 0.0).astype(BF16)
    prefix = jnp.dot(ltri, hot.astype(BF16), preferred_element_type=F32) + cnt_ref[...]
    rank1 = jnp.sum(jnp.where(lane == idx1, prefix, 0.0), axis=1, keepdims=True)
    rank2 = jnp.sum(jnp.where(lane == idx2, prefix, 0.0), axis=1, keepdims=True)
    cnt_ref[...] = cnt_ref[...] + jnp.sum(hot, axis=0, keepdims=True)
    slab = jnp.where(lane == 4, rank1, slab)
    slab = jnp.where(lane == 5, rank2, slab)
    route_ref[rs, :] = slab


def _merge(o_g, lse_g, o_r, nat, x2d, wa, wr, wo, lns, lnb, wrt, brt, S):
    tm = MERGE_ROWS
    D = D_MODEL
    W = ATT_GROUP_WIDTH
    head_expand = (jnp.arange(W, dtype=jnp.int32)[None, :] // ATT_HEAD_DIM
                   == jnp.arange(LANES, dtype=jnp.int32)[:, None]).astype(BF16)

    def rows(width):
        return pl.BlockSpec((tm, width), lambda i: (i, 0))

    def streams(g, width):
        d = DILATIONS[g]
        per = (ATT_BLOCK * d) // tm
        return pl.BlockSpec((None, d, tm // d, width), lambda i: (i // per, 0, i % per, 0))

    def ncol(cb):
        return pl.BlockSpec((tm, COL_BLOCK), lambda i: (i, cb))

    def const(shape):
        return pl.BlockSpec(shape, lambda i: (0, 0), pipeline_mode=pl.Buffered(1))

    return pl.pallas_call(
        _merge_kernel,
        grid=(S // tm,),
        in_specs=[
            rows(W), streams(1, W), streams(2, W),
            rows(LANES), streams(1, LANES), streams(2, LANES),
            rows(RET_HEADS * RET_V_DIM),
            ncol(NC_GATE_A), ncol(NC_GATE_A + 1), ncol(NC_GATE_B), ncol(NC_GATE_B + 1),
            rows(D),
            const((W, D)), const((RET_HEADS * RET_V_DIM, D)), const((D, D)),
            const((1, D)), const((1, D)), const((D, 2 * LANES)), const((1, LANES)), const((LANES, W)),
        ],
        out_specs=[rows(D), pl.BlockSpec((tm * PACKED_ROW_PIECES, LANES), lambda i: (i, 0)),
                   rows(LANES), pl.BlockSpec((1, LANES), lambda i: (0, 0))],
        out_shape=[jax.ShapeDtypeStruct((S, D), F32),
                   jax.ShapeDtypeStruct((S * PACKED_ROW_PIECES, LANES), jnp.uint32),
                   jax.ShapeDtypeStruct((S, LANES), F32),
                   jax.ShapeDtypeStruct((1, LANES), F32)],
        scratch_shapes=[pltpu.VMEM((ATT_HEADS_PER_GROUP, tm, ATT_HEAD_DIM), F32),
                        pltpu.VMEM((ATT_HEADS_PER_GROUP, tm, ATT_HEAD_DIM), F32),
                        pltpu.VMEM((tm, LANES), F32), pltpu.VMEM((tm, LANES), F32)],
        compiler_params=_cparams(1),
        name="merge_ln_router",
    )(*o_g, *lse_g, o_r, nat, nat, nat, nat, x2d, wa, wr, wo, lns, lnb, wrt, brt, head_expand)


META_N_USED, META_NEXT = 0, 1
WEIGHT_DMA_PRIORITY = 1
GATHER_SLOTS = 3


def _plan_kernel(dest_ref, cnt_ref, rt_ref, be_ref, meta_ref):
    n_blocks = be_ref.shape[0]

    def pad_row(r, carry):
        rt_ref[r] = 0
        return carry

    def per_expert(e, start):
        nb = lax.shift_right_logical(cnt_ref[e] + (MOE_BLOCK - 1), 7)

        def fill(b, carry):
            be_ref[lax.shift_right_logical(start, 7) + b] = e
            return carry
        lax.fori_loop(0, nb, fill, 0)
        lax.fori_loop(start + cnt_ref[e], start + nb * MOE_BLOCK, pad_row, 0)
        return start + nb * MOE_BLOCK
    total = lax.fori_loop(0, N_EXPERTS, per_expert, 0)
    n_used = lax.shift_right_logical(total, 7)
    meta_ref[META_N_USED] = n_used

    def tail(b, carry):
        be_ref[b] = be_ref[jnp.maximum(n_used - 1, 0)]
        return carry
    lax.fori_loop(n_used, n_blocks, tail, 0)
    lax.fori_loop(total, rt_ref.shape[0], pad_row, 0)

    def nxt(k, cur):
        e = N_EXPERTS - 1 - k
        meta_ref[META_NEXT + e] = cur
        return jnp.where(cnt_ref[e] > 0, e, cur)
    lax.fori_loop(0, N_EXPERTS, nxt, N_EXPERTS)

    def assign(a, carry):
        rt_ref[dest_ref[a]] = lax.shift_right_logical(a, 1)
        return carry
    lax.fori_loop(0, dest_ref.shape[0], assign, 0, unroll=8)


def _plan(dest, counts):
    A = dest.shape[0]
    n_blocks = A // MOE_BLOCK + N_EXPERTS
    smem = pl.BlockSpec(memory_space=pltpu.SMEM)
    return pl.pallas_call(
        _plan_kernel,
        in_specs=[smem, smem],
        out_specs=[smem, smem, smem],
        out_shape=[
            jax.ShapeDtypeStruct((n_blocks * MOE_BLOCK,), jnp.int32),
            jax.ShapeDtypeStruct((n_blocks,), jnp.int32),
            jax.ShapeDtypeStruct((META_NEXT + N_EXPERTS,), jnp.int32),
        ],
        name="moe_plan",
    )(dest, counts)


def _expert_kernel(be_ref, meta_ref, rt_ref, x_hbm, wg_hbm, wu_hbm, wd_hbm, y_ref,
                   xg_ref, wgf_ref, wuf_ref, wdf_ref, wgb_ref, wub_ref, wdb_ref, xsem, sem, slot_ref):
    b = pl.program_id(0)
    n_used = meta_ref[META_N_USED]
    e = be_ref[b]
    first = (b == 0) | (e != be_ref[jnp.maximum(b - 1, 0)])

    xp = PACKED_ROW_PIECES

    last = jnp.maximum(n_used - 1, 0)

    def row_copy(blk, slot, i):
        tok = rt_ref[blk * MOE_BLOCK + i]
        return pltpu.make_async_copy(x_hbm.at[pl.ds(pl.multiple_of(tok * xp, xp), xp)],
                                     xg_ref.at[slot, pl.ds(pl.multiple_of(i * xp, xp), xp)],
                                     xsem.at[slot])

    def wait_rows(slot):
        pltpu.make_async_copy(x_hbm.at[pl.ds(0, MOE_BLOCK * xp)], xg_ref.at[slot], xsem.at[slot]).wait()

    @pl.when(b == 0)
    def _():
        for blk in range(GATHER_SLOTS - 1):
            def body(i, carry, blk=blk):
                row_copy(jnp.minimum(blk, last), blk, i).start()
                return carry
            lax.fori_loop(0, MOE_BLOCK, body, 0, unroll=8)

    def copies(ex, s):
        return (pltpu.make_async_copy(wg_hbm.at[ex], wgf_ref.at[s], sem.at[s]),
                pltpu.make_async_copy(wu_hbm.at[ex], wuf_ref.at[s], sem.at[s]),
                pltpu.make_async_copy(wd_hbm.at[ex], wdf_ref.at[s], sem.at[s]))

    @pl.when((b == 0) & (n_used > 0))
    def _():
        slot_ref[0] = 0
        for c in copies(e, 0):
            c.start(priority=WEIGHT_DMA_PRIORITY)

    @pl.when(first & (b < n_used))
    def _():
        s = slot_ref[0]
        nx = meta_ref[META_NEXT + e]

        @pl.when(nx < N_EXPERTS)
        def _():
            for c in copies(nx, 1 - s):
                c.start(priority=WEIGHT_DMA_PRIORITY)
        for c in copies(e, s):
            c.wait()
        wgb_ref[...] = wgf_ref[s].astype(BF16)
        wub_ref[...] = wuf_ref[s].astype(BF16)
        wdb_ref[...] = wdf_ref[s].astype(BF16)
        slot_ref[0] = 1 - s

    def step(slot):
        wait_rows(slot)
        nxt = jnp.minimum(b + (GATHER_SLOTS - 1), last)
        for i in range(MOE_BLOCK):
            row_copy(nxt, (slot + GATHER_SLOTS - 1) % GATHER_SLOTS, i).start()
        pieces = [_unpack_bf16_pair(xg_ref[slot, pl.ds(c, MOE_BLOCK, stride=xp), :]) for c in range(xp)]
        xb = jnp.concatenate([p[0] for p in pieces] + [p[1] for p in pieces], axis=1)
        hg = jnp.dot(xb, wgb_ref[...], preferred_element_type=F32)
        hu = jnp.dot(xb, wub_ref[...], preferred_element_type=F32)
        hid = (hg * _sigmoid(hg) * hu).astype(BF16)
        y = jnp.dot(hid, wdb_ref[...], preferred_element_type=F32)
        for c in range(ROW_PIECES):
            y_ref[pl.ds(c, MOE_BLOCK, stride=ROW_PIECES), :] = y[:, c * LANES:(c + 1) * LANES]

    for s in range(GATHER_SLOTS):
        pl.when((b < n_used) & (lax.rem(b, GATHER_SLOTS) == s))(functools.partial(step, s))

    @pl.when(b == last)
    def _():
        for ahead in range(1, GATHER_SLOTS):
            wait_rows(lax.rem(b + ahead, GATHER_SLOTS))

    @pl.when(b >= n_used)
    def _():
        y_ref[...] = jnp.zeros_like(y_ref)


def _experts(x1p, block_expert, meta, row_tok, w_gate, w_up, w_down):
    D = D_MODEL
    assert x1p.shape[1] == LANES and x1p.dtype == jnp.uint32
    n_blocks = block_expert.shape[0]
    R = n_blocks * MOE_BLOCK
    H = EXPERT_HIDDEN
    grid_spec = pltpu.PrefetchScalarGridSpec(
        num_scalar_prefetch=3,
        grid=(n_blocks,),
        in_specs=[
            pl.BlockSpec(memory_space=pl.ANY),
            pl.BlockSpec(memory_space=pl.ANY),
            pl.BlockSpec(memory_space=pl.ANY),
            pl.BlockSpec(memory_space=pl.ANY),
        ],
        out_specs=pl.BlockSpec((MOE_BLOCK * ROW_PIECES, LANES), lambda b, be, meta, rt: (b, 0)),
        scratch_shapes=[
            pltpu.VMEM((GATHER_SLOTS, MOE_BLOCK * PACKED_ROW_PIECES, LANES), jnp.uint32),
            pltpu.VMEM((2, D, H), F32),
            pltpu.VMEM((2, D, H), F32),
            pltpu.VMEM((2, H, D), F32),
            pltpu.VMEM((D, H), BF16),
            pltpu.VMEM((D, H), BF16),
            pltpu.VMEM((H, D), BF16),
            pltpu.SemaphoreType.DMA((GATHER_SLOTS,)),
            pltpu.SemaphoreType.DMA((2,)),
            pltpu.SMEM((1,), jnp.int32),
        ],
    )
    return pl.pallas_call(
        _expert_kernel,
        grid_spec=grid_spec,
        out_shape=jax.ShapeDtypeStruct((R * ROW_PIECES, LANES), F32),
        compiler_params=_cparams(1),
        name="moe_experts",
    )(block_expert, meta, row_tok, x1p, w_gate, w_up, w_down)


COMBINE_TOKENS = 64


def _combine_kernel(dest_ref, y_hbm, x1_ref, route_ref, lns_ref, lnb_ref, o_ref, yg_ref, sem):
    i = pl.program_id(0)
    tt = COMBINE_TOKENS
    n = dest_ref.shape[0] // (TOP_K * tt)

    rp = ROW_PIECES

    def start_token(step, slot, r):
        for k in range(TOP_K):
            row = dest_ref[(step * tt + r) * TOP_K + k]
            pltpu.make_async_copy(y_hbm.at[pl.ds(pl.multiple_of(row * rp, rp), rp)],
                                  yg_ref.at[slot, k, pl.ds(pl.multiple_of(r * rp, rp), rp)],
                                  sem.at[slot]).start(priority=k)

    def wait_slot(slot):
        for k in range(TOP_K):
            pltpu.make_async_copy(y_hbm.at[pl.ds(0, tt * rp)], yg_ref.at[slot, k], sem.at[slot]).wait()

    @pl.when(i == 0)
    def _():
        for ahead in range(GATHER_SLOTS - 1):
            def body(r, carry, ahead=ahead):
                start_token(min(ahead, n - 1), ahead, r)
                return carry
            lax.fori_loop(0, tt, body, 0, unroll=4)

    def step(slot):
        wait_slot(slot)
        nxt = jnp.minimum(i + (GATHER_SLOTS - 1), n - 1)
        for r in range(tt):
            start_token(nxt, (slot + GATHER_SLOTS - 1) % GATHER_SLOTS, r)
        route = route_ref[...]
        w1, w2 = route[:, 2:3], route[:, 3:4]
        moe = jnp.concatenate(
            [w1 * yg_ref[slot, 0, pl.ds(c, tt, stride=rp), :] + w2 * yg_ref[slot, 1, pl.ds(c, tt, stride=rp), :]
             for c in range(rp)], axis=1)
        o_ref[...] = _layer_norm(ALPHA * x1_ref[...] + moe, lns_ref[...], lnb_ref[...])

    for s in range(GATHER_SLOTS):
        pl.when(lax.rem(i, GATHER_SLOTS) == s)(functools.partial(step, s))

    @pl.when(i == n - 1)
    def _():
        for ahead in range(1, GATHER_SLOTS):
            wait_slot((n - 1 + ahead) % GATHER_SLOTS)


def _combine(dest, y, x1, route, lns, lnb):
    T, D = x1.shape
    tt = COMBINE_TOKENS
    grid_spec = pltpu.PrefetchScalarGridSpec(
        num_scalar_prefetch=1,
        grid=(T // tt,),
        in_specs=[
            pl.BlockSpec(memory_space=pl.ANY),
            pl.BlockSpec((tt, D), lambda i, d: (i, 0)),
            pl.BlockSpec((tt, LANES), lambda i, d: (i, 0)),
            pl.BlockSpec((1, D), lambda i, d: (0, 0)),
            pl.BlockSpec((1, D), lambda i, d: (0, 0)),
        ],
        out_specs=pl.BlockSpec((tt, D), lambda i, d: (i, 0)),
        scratch_shapes=[
            pltpu.VMEM((GATHER_SLOTS, TOP_K, tt * ROW_PIECES, LANES), F32),
            pltpu.SemaphoreType.DMA((GATHER_SLOTS,)),
        ],
    )
    return pl.pallas_call(
        _combine_kernel,
        grid_spec=grid_spec,
        out_shape=jax.ShapeDtypeStruct((T, D), F32),
        compiler_params=_cparams(1),
        name="moe_combine_ln",
    )(dest, y, x1, route, lns, lnb)


def _rotary_tables(positions):
    pos = positions.reshape(-1).astype(F32)
    S = pos.shape[0]
    inv_a = ROPE_THETA ** (-jnp.arange(0, ROPE_DIM, 2, dtype=F32) / ROPE_DIM)
    inv_r = RET_ROT_BASE ** (-jnp.linspace(0.0, 1.0, RET_QK_DIM // 2, dtype=F32))
    ang = jnp.concatenate([inv_a, inv_r])[:, None] * pos[None, :]
    cos_t, sin_t = lax.optimization_barrier((jnp.cos(ang), jnp.sin(ang)))
    na = ROPE_DIM // 2
    cos, sin, cr, sr = cos_t[:na].T, sin_t[:na].T, cos_t[na:].T, sin_t[na:].T
    pad1 = jnp.ones((S, LANES - ROPE_DIM), F32)
    pad0 = jnp.zeros((S, LANES - ROPE_DIM), F32)
    tabs_a = jnp.concatenate([cos, cos, pad1, sin, sin, pad0], axis=1)
    tabs_r = jnp.concatenate([cr, cr, sr, sr], axis=1)
    return tabs_a, tabs_r


def kernel(x, positions, w_in, w_branch_attn, w_branch_ret, w_out, ln1_scale, ln1_bias, w_group_router, b_group_router, w_expert_router, b_expert_router, w_expert_gate, w_expert_up, w_expert_down, ln2_scale, ln2_bias):
    B, S, D = x.shape
    assert B == 1 and D == D_MODEL and w_in.shape[0] == DEPTH == 1
    assert S % (ATT_BLOCK * DILATIONS[2]) == 0
    x2d = x.reshape(S, D)
    xb = x2d.astype(BF16)
    tabs_a, tabs_r = _rotary_tables(positions)

    nat = _input_projection(xb, w_in[0], tabs_a, tabs_r, NAT_COLS, NAT_MODES, 1)
    qkv_modes = (MODE_ATT_Q, MODE_ATT_K, MODE_PLAIN)
    qkv = [None] + [
        _input_projection(xb, w_in[0], tabs_a, tabs_r, (CB_QA + g, CB_KA + g, CB_VA + g), qkv_modes,
                          DILATIONS[g])
        for g in (1, 2)]
    o0, l0 = _dilated_attention(nat.reshape(S // ATT_BLOCK, 1, ATT_BLOCK, nat.shape[1]), 0,
                                NC_Q0, NC_K0, NC_V0)
    att = [(o0.reshape(S, ATT_GROUP_WIDTH), l0.reshape(S, LANES))]
    att += [_dilated_attention(qkv[g], g, 0, 1, 2) for g in (1, 2)]
    o_r = _retention(nat, S)

    w_route = jnp.concatenate(
        [w_group_router[0], w_expert_router[0].transpose(1, 0, 2).reshape(D, N_EXPERTS),
         jnp.zeros((D, LANES - N_EXPERT_GROUPS - N_EXPERTS), F32)], axis=1)
    b_route = jnp.concatenate(
        [b_group_router[0], b_expert_router[0].reshape(N_EXPERTS),
         jnp.zeros((LANES - N_EXPERT_GROUPS - N_EXPERTS,), F32)]).reshape(1, LANES)
    x1, x1p, route, cnt = _merge(
        [a[0] for a in att], [a[1] for a in att], o_r, nat, x2d,
        w_branch_attn[0].astype(BF16), w_branch_ret[0].astype(BF16), w_out[0].astype(BF16),
        ln1_scale[0].reshape(1, D), ln1_bias[0].reshape(1, D),
        jnp.concatenate(_split_bf16(w_route), axis=1), b_route, S)

    e_flat = route[:, 0:TOP_K].astype(jnp.int32).reshape(S * TOP_K)
    rank_flat = route[:, 4:4 + TOP_K].astype(jnp.int32).reshape(S * TOP_K)
    counts = cnt[0, N_EXPERT_GROUPS:N_EXPERT_GROUPS + N_EXPERTS].astype(jnp.int32)
    padded = (counts + (MOE_BLOCK - 1)) // MOE_BLOCK * MOE_BLOCK
    pstart = jnp.cumsum(padded) - padded
    hot = e_flat[:, None] == jnp.arange(N_EXPERTS, dtype=jnp.int32)[None, :]
    dest = jnp.sum(jnp.where(hot, pstart[None, :], 0), axis=1) + rank_flat
    row_tok, block_expert, meta = _plan(dest, counts)
    y = _experts(x1p, block_expert, meta, row_tok, w_expert_gate[0], w_expert_up[0], w_expert_down[0])
    out = _combine(dest, y, x1, route, ln2_scale[0].reshape(1, D), ln2_bias[0].reshape(1, D))
    return out.reshape(B, S, D)
```

```python
import functools
import math

import jax
import jax.numpy as jnp
from jax import lax
from jax.experimental import pallas as pl
from jax.experimental.pallas import tpu as pltpu

F32 = jnp.float32
BF16 = jnp.bfloat16

D_MODEL = 2048
ATT_HEAD_DIM = 128
ATT_HEADS_PER_GROUP = 8
DILATIONS = (1, 4, 16)
ATT_STEPS = 128
N_ATT_GROUPS = 3
ATT_GROUP_WIDTH = ATT_HEADS_PER_GROUP * ATT_HEAD_DIM
ROPE_DIM = ATT_HEAD_DIM // 4
ROPE_THETA = 500000.0
ATT_BLOCK = 128

RET_HEADS = 8
RET_QK_DIM = 128
RET_V_DIM = 256
RET_CHUNK = 128
RET_ROT_BASE = 10000.0
RET_LOG_DECAY = tuple(math.log1p(-(2.0 ** (-5.0 - h))) for h in range(RET_HEADS))

N_EXPERT_GROUPS = 4
EXPERTS_PER_GROUP = 8
N_EXPERTS = N_EXPERT_GROUPS * EXPERTS_PER_GROUP
TOP_K = 2
EXPERT_HIDDEN = 512
MOE_BLOCK = 128
MOE_BLOCK_SHIFT = MOE_BLOCK.bit_length() - 1
assert MOE_BLOCK == 1 << MOE_BLOCK_SHIFT

DEPTH = 1
ALPHA = (2.0 * DEPTH) ** 0.25
LN_EPS = 1e-5
NEG_INF = -1e30

COL_BLOCK = 1024
CB_QA, CB_KA, CB_VA = 0, 3, 6
CB_QR, CB_KR, CB_VR, CB_GR = 9, 10, 11, 13
CB_GATE_A, CB_GATE_B = 15, 17

MODE_PLAIN, MODE_ATT_Q, MODE_ATT_K, MODE_RET_Q, MODE_RET_K = 0, 1, 2, 3, 4

NAT_COLS = (CB_QA, CB_KA, CB_VA, CB_QR, CB_KR, CB_VR, CB_VR + 1, CB_GR, CB_GR + 1,
            CB_GATE_A, CB_GATE_A + 1, CB_GATE_B, CB_GATE_B + 1)
NAT_MODES = (MODE_ATT_Q, MODE_ATT_K, MODE_PLAIN, MODE_RET_Q, MODE_RET_K) + (MODE_PLAIN,) * 8
NC_Q0, NC_K0, NC_V0, NC_QR, NC_KR, NC_VR, NC_GR, NC_GATE_A, NC_GATE_B = 0, 1, 2, 3, 4, 5, 7, 9, 11

LANES = 128
ROW_PIECES = D_MODEL // LANES
PACKED_ROW_PIECES = ROW_PIECES // 2
VMEM_LIMIT = 52 * 1024 * 1024
PROJ_ROWS = 1024
ACC_PITCH = 24


def _cparams(n_grid_dims):
    return pltpu.CompilerParams(
        dimension_semantics=("arbitrary",) * n_grid_dims,
        vmem_limit_bytes=VMEM_LIMIT,
    )


def _inproj_kernel(cols_ref, modes_ref, x_ref, w_ref, ta_ref, tr_ref, pa_ref, pr_ref, o_ref,
                   wbf_ref, *acc, d):
    del cols_ref
    j = pl.program_id(0)

    @pl.when(pl.program_id(1) == 0)
    def _():
        wbf_ref[...] = w_ref[...].astype(BF16)

    mode = modes_ref[j]
    tm, tn = x_ref.shape[0], wbf_ref.shape[1]
    pair = 2 * LANES

    def project():
        return jnp.dot(x_ref[...], wbf_ref[...], preferred_element_type=F32)

    def emit(p, piece):
        if d == 1:
            o_ref[:, p * pair:(p + 1) * pair] = piece.astype(o_ref.dtype)
        elif d == 4:
            acc[0][2 * p] = piece[:, :LANES]
            acc[0][2 * p + 1] = piece[:, LANES:]
        else:
            for g in range(tm // d):
                rows = slice(g * d, (g + 1) * d)
                dst = slice(g * ACC_PITCH, g * ACC_PITCH + d)
                acc[0][2 * p, dst, :] = piece[rows, :LANES]
                acc[0][2 * p + 1, dst, :] = piece[rows, LANES:]

    def plain():
        res = project()
        for p in range(tn // pair):
            emit(p, res[:, p * pair:(p + 1) * pair])

    def rotary(tab_ref, perm_ref, scale):
        res = project()
        c = jnp.concatenate([tab_ref[:, :LANES]] * 2, axis=1) * scale
        s = jnp.concatenate([tab_ref[:, LANES:]] * 2, axis=1) * scale
        for p in range(tn // pair):
            a = res[:, p * pair:(p + 1) * pair]
            partner = jnp.dot(a.astype(BF16), perm_ref[...], preferred_element_type=F32)
            emit(p, a * c + partner * s)

    is_att = (mode == MODE_ATT_Q) | (mode == MODE_ATT_K)
    is_ret = (mode == MODE_RET_Q) | (mode == MODE_RET_K)
    pl.when(mode == MODE_PLAIN)(plain)
    pl.when(is_att)(lambda: rotary(ta_ref, pa_ref, jnp.where(mode == MODE_ATT_Q, ATT_HEAD_DIM ** -0.5, 1.0)))
    if d == 1:
        pl.when(is_ret)(lambda: rotary(tr_ref, pr_ref, jnp.where(mode == MODE_RET_K, RET_QK_DIM ** -0.5, 1.0)))
        return

    acc_ref = acc[0]
    for h in range(tn // LANES):
        sl = slice(h * LANES, (h + 1) * LANES)
        if d == 4:
            P = ATT_BLOCK * d
            for sb in range(tm // P):
                for r in range(d):
                    o_ref[sb, r, :, sl] = acc_ref[h, pl.ds(sb * P + r, ATT_BLOCK, stride=d), :].astype(o_ref.dtype)
        else:
            for r in range(d):
                o_ref[r, :, sl] = acc_ref[h, pl.ds(r, tm // d, stride=ACC_PITCH), :].astype(o_ref.dtype)


def _rotation_matrix(half, width):
    k = jnp.arange(2 * LANES, dtype=jnp.int32)[:, None]
    l = jnp.arange(2 * LANES, dtype=jnp.int32)[None, :]
    same_head = (k // LANES) == (l // LANES)
    kk, ll = k % LANES, l % LANES
    neg = same_head & (ll < half) & (kk == ll + half)
    pos = same_head & (ll >= half) & (ll < width) & (kk == ll - half)
    return (pos.astype(F32) - neg.astype(F32)).astype(BF16)


def _input_projection(xb, w_in, tabs_a, tabs_r, cols, modes, d):
    S, D = xb.shape
    tm, tn = PROJ_ROWS, COL_BLOCK
    n = len(cols)
    P = ATT_BLOCK * d
    if d == 1:
        out_shape = jax.ShapeDtypeStruct((S, n * tn), BF16)
        out_spec = pl.BlockSpec((tm, tn), lambda j, i, c, m: (i, j))
    elif d == 4:
        out_shape = jax.ShapeDtypeStruct((S // P, d, ATT_BLOCK, n * tn), BF16)
        out_spec = pl.BlockSpec((tm // P, d, ATT_BLOCK, tn), lambda j, i, c, m: (i, 0, 0, j))
    else:
        assert d == 16 and P == 2 * tm
        out_shape = jax.ShapeDtypeStruct((S // P, d, ATT_BLOCK, n * tn), BF16)
        out_spec = pl.BlockSpec((None, d, tm // d, tn), lambda j, i, c, m: (i // 2, 0, i % 2, j))

    def tab_rows(j, i, c, m):
        return (jnp.where(m[j] != MODE_PLAIN, i, 0), 0)

    grid_spec = pltpu.PrefetchScalarGridSpec(
        num_scalar_prefetch=2,
        grid=(n, S // tm),
        in_specs=[
            pl.BlockSpec((tm, D), lambda j, i, c, m: (i, 0)),
            pl.BlockSpec((D, tn), lambda j, i, c, m: (0, c[j])),
            pl.BlockSpec((tm, 2 * LANES), tab_rows),
            pl.BlockSpec((tm, 2 * LANES), tab_rows),
            pl.BlockSpec((2 * LANES, 2 * LANES), lambda j, i, c, m: (0, 0)),
            pl.BlockSpec((2 * LANES, 2 * LANES), lambda j, i, c, m: (0, 0)),
        ],
        out_specs=out_spec,
        scratch_shapes=[pltpu.VMEM((D, tn), BF16)]
        + {1: [], 4: [pltpu.VMEM((tn // LANES, tm, LANES), F32)],
           16: [pltpu.VMEM((tn // LANES, tm // 16 * ACC_PITCH, LANES), F32)]}[d],
    )
    return pl.pallas_call(
        functools.partial(_inproj_kernel, d=d),
        grid_spec=grid_spec,
        out_shape=out_shape,
        compiler_params=_cparams(2),
        name=f"in_proj_d{d}",
    )(jnp.asarray(cols, jnp.int32), jnp.asarray(modes, jnp.int32), xb, w_in, tabs_a, tabs_r,
      _rotation_matrix(ROPE_DIM // 2, ROPE_DIM), _rotation_matrix(RET_QK_DIM // 2, RET_QK_DIM))


ATT_Q_BLOCKS = 8


def _attn_kernel(q_ref, k_ref, kp_ref, v_ref, vp_ref, o_ref, lse_ref):
    n = pl.program_id(1)
    qi = lax.broadcasted_iota(jnp.int32, (ATT_BLOCK, 2 * ATT_BLOCK), 0)
    kj = lax.broadcasted_iota(jnp.int32, (ATT_BLOCK, 2 * ATT_BLOCK), 1)
    dist = qi + ATT_BLOCK - kj
    band = (dist >= 0) & (dist <= ATT_STEPS)
    lane = lax.broadcasted_iota(jnp.int32, (ATT_BLOCK, LANES), 1)
    for sub in range(q_ref.shape[0]):
        mask = band & ((kj >= ATT_BLOCK) | (n > 0)) if sub == 0 else band
        lse_slab = jnp.zeros((ATT_BLOCK, LANES), F32)
        for h in range(ATT_HEADS_PER_GROUP):
            sl = slice(h * ATT_HEAD_DIM, (h + 1) * ATT_HEAD_DIM)
            k_prev = kp_ref[:, sl] if sub == 0 else k_ref[sub - 1, :, sl]
            v_prev = vp_ref[:, sl] if sub == 0 else v_ref[sub - 1, :, sl]
            k2 = jnp.concatenate([k_prev, k_ref[sub, :, sl]], axis=0)
            v2 = jnp.concatenate([v_prev, v_ref[sub, :, sl]], axis=0)
            s = lax.dot_general(q_ref[sub, :, sl], k2, (((1,), (1,)), ((), ())),
                                preferred_element_type=F32)
            s = jnp.where(mask, s, NEG_INF)
            m = jnp.max(s, axis=1, keepdims=True)
            p = jnp.exp(s - m)
            l = jnp.sum(p, axis=1, keepdims=True)
            o = jnp.dot(p.astype(BF16), v2, preferred_element_type=F32)
            o_ref[sub, :, sl] = o / l
            lse_slab = jnp.where(lane == h, m + jnp.log(l), lse_slab)
        lse_ref[sub] = lse_slab


def _dilated_attention(qkv, g, cq, ck, cv):
    d = DILATIONS[g]
    W = ATT_GROUP_WIDTH
    nb = qkv.shape[0]
    nq = min(ATT_Q_BLOCKS, nb)
    assert qkv.shape[1] == d and nb % nq == 0
    blk = (nq, None, ATT_BLOCK, W)
    pblk = (None, None, ATT_BLOCK, W)
    cur = lambda c: (lambda r, n: (n, r, 0, c))
    prev = lambda c: (lambda r, n: (jnp.maximum(nq * n - 1, 0), r, 0, c))
    return pl.pallas_call(
        _attn_kernel,
        grid=(d, nb // nq),
        in_specs=[
            pl.BlockSpec(blk, cur(cq)),
            pl.BlockSpec(blk, cur(ck)),
            pl.BlockSpec(pblk, prev(ck)),
            pl.BlockSpec(blk, cur(cv)),
            pl.BlockSpec(pblk, prev(cv)),
        ],
        out_specs=[pl.BlockSpec(blk, lambda r, n: (n, r, 0, 0)),
                   pl.BlockSpec((nq, None, ATT_BLOCK, LANES), lambda r, n: (n, r, 0, 0))],
        out_shape=[jax.ShapeDtypeStruct((nb, d, ATT_BLOCK, W), F32),
                   jax.ShapeDtypeStruct((nb, d, ATT_BLOCK, LANES), F32)],
        compiler_params=_cparams(2),
        name=f"dilated_attn_g{g}",
    )(qkv, qkv, qkv, qkv, qkv)


RET_CHUNKS_PER_STEP = 4


def _retention_kernel(q_ref, k_ref, vlo_ref, vhi_ref, glo_ref, ghi_ref, o_ref,
                      state_ref, intra_ref, dec_ref):
    @pl.when(pl.program_id(0) == 0)
    def _():
        state_ref[...] = jnp.zeros_like(state_ref)
        ri = lax.broadcasted_iota(jnp.int32, (RET_CHUNK, RET_CHUNK), 0)
        ci = lax.broadcasted_iota(jnp.int32, (RET_CHUNK, RET_CHUNK), 1)
        diff = (ri - ci).astype(F32)
        nrow = ri.astype(F32)
        for h in range(RET_HEADS):
            ld = RET_LOG_DECAY[h]
            intra_ref[h] = jnp.where(diff >= 0, jnp.exp(jnp.maximum(diff, 0.0) * ld), 0.0)
            dec_ref[h, 0] = jnp.exp((nrow + 1.0) * ld)
            dec_ref[h, 1] = jnp.exp((RET_CHUNK - 1.0 - nrow) * ld)

    half = RET_HEADS // 2
    for h in range(RET_HEADS):
        ld = RET_LOG_DECAY[h]
        sl = slice(h * RET_QK_DIM, (h + 1) * RET_QK_DIM)
        vsl = slice((h % half) * RET_V_DIM, (h % half + 1) * RET_V_DIM)
        v_ref, g_ref = (vlo_ref, glo_ref) if h < half else (vhi_ref, ghi_ref)
        state = state_ref[h]
        for c in range(q_ref.shape[0] // RET_CHUNK):
            rs = slice(c * RET_CHUNK, (c + 1) * RET_CHUNK)
            q = q_ref[rs, sl]
            k = k_ref[rs, sl]
            v = v_ref[rs, vsl]
            scores = (lax.dot_general(q, k, (((1,), (1,)), ((), ())), preferred_element_type=F32)
                      * intra_ref[h])
            inner = jnp.dot(scores.astype(BF16), v, preferred_element_type=F32)
            cross = jnp.dot((q.astype(F32) * dec_ref[h, 0]).astype(BF16), state.astype(BF16),
                            preferred_element_type=F32)
            kv = lax.dot_general((k.astype(F32) * dec_ref[h, 1]).astype(BF16), v,
                                 (((0,), (0,)), ((), ())), preferred_element_type=F32)
            state = math.exp(RET_CHUNK * ld) * state + kv
            o = inner + cross
            mu = jnp.mean(o, axis=1, keepdims=True)
            var = jnp.mean(jnp.square(o - mu), axis=1, keepdims=True)
            o = (o - mu) * lax.rsqrt(var + LN_EPS)
            gate = g_ref[rs, vsl].astype(F32)
            gate = gate * (1.0 / (1.0 + jnp.exp(-gate)))
            o_ref[rs, h * RET_V_DIM:(h + 1) * RET_V_DIM] = (gate * o).astype(o_ref.dtype)
        state_ref[h] = state


def _retention(nat, S):
    rows = RET_CHUNK * RET_CHUNKS_PER_STEP
    blk = (rows, COL_BLOCK)

    def col(cb):
        return pl.BlockSpec(blk, lambda c: (c, cb))

    return pl.pallas_call(
        _retention_kernel,
        grid=(S // rows,),
        in_specs=[col(NC_QR), col(NC_KR), col(NC_VR), col(NC_VR + 1), col(NC_GR), col(NC_GR + 1)],
        out_specs=pl.BlockSpec((rows, RET_HEADS * RET_V_DIM), lambda c: (c, 0)),
        out_shape=jax.ShapeDtypeStruct((S, RET_HEADS * RET_V_DIM), BF16),
        scratch_shapes=[pltpu.VMEM((RET_HEADS, RET_QK_DIM, RET_V_DIM), F32),
                        pltpu.VMEM((RET_HEADS, RET_CHUNK, RET_CHUNK), F32),
                        pltpu.VMEM((RET_HEADS, 2, RET_CHUNK, RET_QK_DIM), F32)],
        compiler_params=_cparams(1),
        name="retention",
    )(nat, nat, nat, nat, nat, nat)


MERGE_ROWS = 256
MERGE_CHAIN_ROWS = 256


def _layer_norm(z, scale, bias):
    mu = jnp.mean(z, axis=1, keepdims=True)
    var = jnp.mean(jnp.square(z - mu), axis=1, keepdims=True)
    return (z - mu) * lax.rsqrt(var + LN_EPS) * scale + bias


def _sigmoid(x):
    return 1.0 / (1.0 + jnp.exp(-x))


def _pack_bf16_pair(a, b):
    abits = lax.bitcast_convert_type(a.astype(BF16).astype(F32), jnp.uint32)
    bbits = lax.bitcast_convert_type(b.astype(BF16).astype(F32), jnp.uint32)
    return (abits >> 16) | (bbits & jnp.uint32(0xFFFF0000))


def _unpack_bf16_pair(w):
    a = lax.bitcast_convert_type(w << 16, F32).astype(BF16)
    b = lax.bitcast_convert_type(w & jnp.uint32(0xFFFF0000), F32).astype(BF16)
    return a, b


def _split_bf16(v):
    hi = v.astype(BF16)
    return hi, (v - hi.astype(F32)).astype(BF16)


def _merge_kernel(o0_ref, o1_ref, o2_ref, l0_ref, l1_ref, l2_ref, or_ref,
                  ga0_ref, ga1_ref, gb0_ref, gb1_ref, x_ref,
                  wa_ref, wr_ref, wo_ref, lns_ref, lnb_ref, wrt_ref, brt_ref, hexp_ref,
                  x1_ref, x1p_ref, route_ref, cnt_ref, s1_ref, s2_ref, ls1_ref, ls2_ref):
    tm = o0_ref.shape[0]
    for d, src, lsrc, dst, ldst in ((DILATIONS[1], o1_ref, l1_ref, s1_ref, ls1_ref),
                                    (DILATIONS[2], o2_ref, l2_ref, s2_ref, ls2_ref)):
        for r in range(d):
            for h in range(ATT_HEADS_PER_GROUP):
                dst[h, pl.ds(r, tm // d, stride=d), :] = src[r, :, h * ATT_HEAD_DIM:(h + 1) * ATT_HEAD_DIM]
            ldst[pl.ds(r, tm // d, stride=d), :] = lsrc[r]

    @pl.when(pl.program_id(0) == 0)
    def _():
        cnt_ref[...] = jnp.zeros_like(cnt_ref)

    for c in range(tm // MERGE_CHAIN_ROWS):
        _merge_rows(slice(c * MERGE_CHAIN_ROWS, (c + 1) * MERGE_CHAIN_ROWS),
                    o0_ref, l0_ref, or_ref, ga0_ref, ga1_ref, gb0_ref, gb1_ref, x_ref,
                    wa_ref, wr_ref, wo_ref, lns_ref, lnb_ref, wrt_ref, brt_ref, hexp_ref,
                    x1_ref, x1p_ref, route_ref, cnt_ref, s1_ref, s2_ref, ls1_ref, ls2_ref)


def _merge_rows(rs, o0_ref, l0_ref, or_ref, ga0_ref, ga1_ref, gb0_ref, gb1_ref, x_ref,
                wa_ref, wr_ref, wo_ref, lns_ref, lnb_ref, wrt_ref, brt_ref, hexp_ref,
                x1_ref, x1p_ref, route_ref, cnt_ref, s1_ref, s2_ref, ls1_ref, ls2_ref):
    rc = rs.stop - rs.start
    l0, l1, l2 = l0_ref[rs, :], ls1_ref[rs, :], ls2_ref[rs, :]
    lm = jnp.maximum(jnp.maximum(l0, l1), l2)
    e0, e1, e2 = jnp.exp(l0 - lm), jnp.exp(l1 - lm), jnp.exp(l2 - lm)
    den = e0 + e1 + e2

    def lanes_of_head(w):
        hi, lo = _split_bf16(w)
        return (jnp.dot(hi, hexp_ref[...], preferred_element_type=F32)
                + jnp.dot(lo, hexp_ref[...], preferred_element_type=F32))

    half = D_MODEL // 2
    halves = (slice(0, half), slice(half, D_MODEL))
    o_r = or_ref[rs, :]
    y_r = [jnp.dot(o_r, wr_ref[:, cs], preferred_element_type=F32) for cs in halves]
    heads = range(ATT_HEADS_PER_GROUP)
    o1 = jnp.concatenate([s1_ref[h, rs, :] for h in heads], axis=1)
    o2 = jnp.concatenate([s2_ref[h, rs, :] for h in heads], axis=1)
    o_a = (lanes_of_head(e0 / den) * o0_ref[rs, :] + lanes_of_head(e1 / den) * o1
           + lanes_of_head(e2 / den) * o2).astype(BF16)
    merged = []
    for n, cs in enumerate(halves):
        y_a = jnp.dot(o_a, wa_ref[:, cs], preferred_element_type=F32)
        gate_a = (ga0_ref, ga1_ref)[n][rs, :].astype(F32)
        gate_b = (gb0_ref, gb1_ref)[n][rs, :].astype(F32)
        merged.append((_sigmoid(gate_a) * y_a + _sigmoid(gate_b) * y_r[n]).astype(BF16))
    z = [ALPHA * x_ref[rs, cs]
         + jnp.dot(merged[0], wo_ref[:half, cs], preferred_element_type=F32)
         + jnp.dot(merged[1], wo_ref[half:, cs], preferred_element_type=F32) for cs in halves]
    mu = (jnp.sum(z[0], axis=1, keepdims=True) + jnp.sum(z[1], axis=1, keepdims=True)) / D_MODEL
    var = (jnp.sum(jnp.square(z[0] - mu), axis=1, keepdims=True)
           + jnp.sum(jnp.square(z[1] - mu), axis=1, keepdims=True)) / D_MODEL
    inv = lax.rsqrt(var + LN_EPS)
    x1 = [(z[n] - mu) * inv * lns_ref[:, cs] + lnb_ref[:, cs] for n, cs in enumerate(halves)]
    for n, cs in enumerate(halves):
        x1_ref[rs, cs] = x1[n]
    packed = _pack_bf16_pair(x1[0], x1[1])
    for c in range(PACKED_ROW_PIECES):
        x1p_ref[pl.ds(rs.start * PACKED_ROW_PIECES + c, rc, stride=PACKED_ROW_PIECES), :] = (
            packed[:, c * LANES:(c + 1) * LANES])

    logits = brt_ref[...]
    for n, cs in enumerate(halves):
        x_hi, x_lo = _split_bf16(x1[n])
        hl = jnp.dot(x_hi, wrt_ref[cs, :], preferred_element_type=F32)
        logits = (logits + hl[:, :LANES] + hl[:, LANES:]
                  + jnp.dot(x_lo, wrt_ref[cs, :LANES], preferred_element_type=F32))
    lane = lax.broadcasted_iota(jnp.int32, logits.shape, 1)
    is_g = lane < N_EXPERT_GROUPS
    gl = jnp.where(is_g, logits, NEG_INF)
    gmax = jnp.max(gl, axis=1, keepdims=True)
    g_sel = jnp.min(jnp.where(is_g & (gl == gmax), lane, LANES), axis=1, keepdims=True)
    p_group = 1.0 / jnp.sum(jnp.where(is_g, jnp.exp(gl - gmax), 0.0), axis=1, keepdims=True)
    lo = N_EXPERT_GROUPS + EXPERTS_PER_GROUP * g_sel
    in_grp = (lane >= lo) & (lane < lo + EXPERTS_PER_GROUP)
    el = jnp.where(in_grp, logits, NEG_INF)
    top1 = jnp.max(el, axis=1, keepdims=True)
    idx1 = jnp.min(jnp.where(in_grp & (el == top1), lane, LANES), axis=1, keepdims=True)
    el2 = jnp.where(lane == idx1, NEG_INF, el)
    top2 = jnp.max(el2, axis=1, keepdims=True)
    idx2 = jnp.min(jnp.where(in_grp & (lane != idx1) & (el2 == top2), lane, LANES), axis=1, keepdims=True)
    t = jnp.exp(top2 - top1)
    wt1 = p_group * (1.0 / (1.0 + t))
    wt2 = p_group * (t / (1.0 + t))
    slab = jnp.where(lane == 0, (idx1 - N_EXPERT_GROUPS).astype(F32), 0.0)
    slab = jnp.where(lane == 1, (idx2 - N_EXPERT_GROUPS).astype(F32), slab)
    slab = jnp.where(lane == 2, wt1, slab)
    slab = jnp.where(lane == 3, wt2, slab)

    hot = jnp.where((lane == idx1) | (lane == idx2), 1.0, 0.0)
    ri = lax.broadcasted_iota(jnp.int32, (rc, rc), 0)
    ci = lax.broadcasted_iota(jnp.int32, (rc, rc), 1)
    ltri = jnp.where(ri > ci, 1.0, 0.0).astype(BF16)
    prefix = jnp.dot(ltri, hot.astype(BF16), preferred_element_type=F32) + cnt_ref[...]
    rank1 = jnp.sum(jnp.where(lane == idx1, prefix, 0.0), axis=1, keepdims=True)
    rank2 = jnp.sum(jnp.where(lane == idx2, prefix, 0.0), axis=1, keepdims=True)
    cnt_ref[...] = cnt_ref[...] + jnp.sum(hot, axis=0, keepdims=True)
    slab = jnp.where(lane == 4, rank1, slab)
    slab = jnp.where(lane == 5, rank2, slab)
    route_ref[rs, :] = slab


def _merge(o_g, lse_g, o_r, nat, x2d, wa, wr, wo, lns, lnb, wrt, brt, S):
    tm = MERGE_ROWS
    D = D_MODEL
    W = ATT_GROUP_WIDTH
    head_expand = (jnp.arange(W, dtype=jnp.int32)[None, :] // ATT_HEAD_DIM
                   == jnp.arange(LANES, dtype=jnp.int32)[:, None]).astype(BF16)

    def rows(width):
        return pl.BlockSpec((tm, width), lambda i: (i, 0))

    def streams(g, width):
        d = DILATIONS[g]
        per = (ATT_BLOCK * d) // tm
        return pl.BlockSpec((None, d, tm // d, width), lambda i: (i // per, 0, i % per, 0))

    def ncol(cb):
        return pl.BlockSpec((tm, COL_BLOCK), lambda i: (i, cb))

    def const(shape):
        return pl.BlockSpec(shape, lambda i: (0, 0), pipeline_mode=pl.Buffered(1))

    return pl.pallas_call(
        _merge_kernel,
        grid=(S // tm,),
        in_specs=[
            rows(W), streams(1, W), streams(2, W),
            rows(LANES), streams(1, LANES), streams(2, LANES),
            rows(RET_HEADS * RET_V_DIM),
            ncol(NC_GATE_A), ncol(NC_GATE_A + 1), ncol(NC_GATE_B), ncol(NC_GATE_B + 1),
            rows(D),
            const((W, D)), const((RET_HEADS * RET_V_DIM, D)), const((D, D)),
            const((1, D)), const((1, D)), const((D, 2 * LANES)), const((1, LANES)), const((LANES, W)),
        ],
        out_specs=[rows(D), pl.BlockSpec((tm * PACKED_ROW_PIECES, LANES), lambda i: (i, 0)),
                   rows(LANES), pl.BlockSpec((1, LANES), lambda i: (0, 0))],
        out_shape=[jax.ShapeDtypeStruct((S, D), F32),
                   jax.ShapeDtypeStruct((S * PACKED_ROW_PIECES, LANES), jnp.uint32),
                   jax.ShapeDtypeStruct((S, LANES), F32),
                   jax.ShapeDtypeStruct((1, LANES), F32)],
        scratch_shapes=[pltpu.VMEM((ATT_HEADS_PER_GROUP, tm, ATT_HEAD_DIM), F32),
                        pltpu.VMEM((ATT_HEADS_PER_GROUP, tm, ATT_HEAD_DIM), F32),
                        pltpu.VMEM((tm, LANES), F32), pltpu.VMEM((tm, LANES), F32)],
        compiler_params=_cparams(1),
        name="merge_ln_router",
    )(*o_g, *lse_g, o_r, nat, nat, nat, nat, x2d, wa, wr, wo, lns, lnb, wrt, brt, head_expand)


META_N_USED, META_NEXT = 0, 1
WEIGHT_DMA_PRIORITY = 1
GATHER_SLOTS = 3


def _plan_kernel(dest_ref, cnt_ref, rt_ref, be_ref, meta_ref):
    n_blocks = be_ref.shape[0]

    def pad_row(r, carry):
        rt_ref[r] = 0
        return carry

    def per_expert(e, start):
        nb = lax.shift_right_logical(cnt_ref[e] + (MOE_BLOCK - 1), MOE_BLOCK_SHIFT)

        def fill(b, carry):
            be_ref[lax.shift_right_logical(start, MOE_BLOCK_SHIFT) + b] = e
            return carry
        lax.fori_loop(0, nb, fill, 0)
        lax.fori_loop(start + cnt_ref[e], start + nb * MOE_BLOCK, pad_row, 0)
        return start + nb * MOE_BLOCK
    total = lax.fori_loop(0, N_EXPERTS, per_expert, 0)
    n_used = lax.shift_right_logical(total, MOE_BLOCK_SHIFT)
    meta_ref[META_N_USED] = n_used

    def tail(b, carry):
        be_ref[b] = be_ref[jnp.maximum(n_used - 1, 0)]
        return carry
    lax.fori_loop(n_used, n_blocks, tail, 0)
    lax.fori_loop(total, rt_ref.shape[0], pad_row, 0)

    def nxt(k, cur):
        e = N_EXPERTS - 1 - k
        meta_ref[META_NEXT + e] = cur
        return jnp.where(cnt_ref[e] > 0, e, cur)
    lax.fori_loop(0, N_EXPERTS, nxt, N_EXPERTS)

    def assign(a, carry):
        rt_ref[dest_ref[a]] = lax.shift_right_logical(a, 1)
        return carry
    lax.fori_loop(0, dest_ref.shape[0], assign, 0, unroll=8)


def _plan(dest, counts):
    A = dest.shape[0]
    n_blocks = A // MOE_BLOCK + N_EXPERTS
    smem = pl.BlockSpec(memory_space=pltpu.SMEM)
    return pl.pallas_call(
        _plan_kernel,
        in_specs=[smem, smem],
        out_specs=[smem, smem, smem],
        out_shape=[
            jax.ShapeDtypeStruct((n_blocks * MOE_BLOCK,), jnp.int32),
            jax.ShapeDtypeStruct((n_blocks,), jnp.int32),
            jax.ShapeDtypeStruct((META_NEXT + N_EXPERTS,), jnp.int32),
        ],
        name="moe_plan",
    )(dest, counts)


def _expert_kernel(be_ref, meta_ref, rt_ref, x_hbm, wg_hbm, wu_hbm, wd_hbm, y_ref,
                   xg_ref, wgf_ref, wuf_ref, wdf_ref, wgb_ref, wub_ref, wdb_ref, xsem, sem, slot_ref):
    b = pl.program_id(0)
    n_used = meta_ref[META_N_USED]
    e = be_ref[b]
    first = (b == 0) | (e != be_ref[jnp.maximum(b - 1, 0)])

    xp = PACKED_ROW_PIECES

    last = jnp.maximum(n_used - 1, 0)

    def row_copy(blk, slot, i):
        tok = rt_ref[blk * MOE_BLOCK + i]
        return pltpu.make_async_copy(x_hbm.at[pl.ds(pl.multiple_of(tok * xp, xp), xp)],
                                     xg_ref.at[slot, pl.ds(pl.multiple_of(i * xp, xp), xp)],
                                     xsem.at[slot])

    def wait_rows(slot):
        pltpu.make_async_copy(x_hbm.at[pl.ds(0, MOE_BLOCK * xp)], xg_ref.at[slot], xsem.at[slot]).wait()

    @pl.when(b == 0)
    def _():
        for blk in range(GATHER_SLOTS - 1):
            def body(i, carry, blk=blk):
                row_copy(jnp.minimum(blk, last), blk, i).start()
                return carry
            lax.fori_loop(0, MOE_BLOCK, body, 0, unroll=8)

    def copies(ex, s):
        return (pltpu.make_async_copy(wg_hbm.at[ex], wgf_ref.at[s], sem.at[s]),
                pltpu.make_async_copy(wu_hbm.at[ex], wuf_ref.at[s], sem.at[s]),
                pltpu.make_async_copy(wd_hbm.at[ex], wdf_ref.at[s], sem.at[s]))

    @pl.when((b == 0) & (n_used > 0))
    def _():
        slot_ref[0] = 0
        for c in copies(e, 0):
            c.start(priority=WEIGHT_DMA_PRIORITY)

    @pl.when(first & (b < n_used))
    def _():
        s = slot_ref[0]
        nx = meta_ref[META_NEXT + e]

        @pl.when(nx < N_EXPERTS)
        def _():
            for c in copies(nx, 1 - s):
                c.start(priority=WEIGHT_DMA_PRIORITY)
        for c in copies(e, s):
            c.wait()
        wgb_ref[...] = wgf_ref[s].astype(BF16)
        wub_ref[...] = wuf_ref[s].astype(BF16)
        wdb_ref[...] = wdf_ref[s].astype(BF16)
        slot_ref[0] = 1 - s

    def step(slot):
        wait_rows(slot)
        nxt = jnp.minimum(b + (GATHER_SLOTS - 1), last)
        for i in range(MOE_BLOCK):
            row_copy(nxt, (slot + GATHER_SLOTS - 1) % GATHER_SLOTS, i).start()
        pieces = [_unpack_bf16_pair(xg_ref[slot, pl.ds(c, MOE_BLOCK, stride=xp), :]) for c in range(xp)]
        xb = jnp.concatenate([p[0] for p in pieces] + [p[1] for p in pieces], axis=1)
        hg = jnp.dot(xb, wgb_ref[...], preferred_element_type=F32)
        hu = jnp.dot(xb, wub_ref[...], preferred_element_type=F32)
        hid = (hg * _sigmoid(hg) * hu).astype(BF16)
        y = jnp.dot(hid, wdb_ref[...], preferred_element_type=F32)
        for c in range(ROW_PIECES):
            y_ref[pl.ds(c, MOE_BLOCK, stride=ROW_PIECES), :] = y[:, c * LANES:(c + 1) * LANES]

    for s in range(GATHER_SLOTS):
        pl.when((b < n_used) & (lax.rem(b, GATHER_SLOTS) == s))(functools.partial(step, s))

    @pl.when(b == last)
    def _():
        for ahead in range(1, GATHER_SLOTS):
            wait_rows(lax.rem(b + ahead, GATHER_SLOTS))

    @pl.when(b >= n_used)
    def _():
        y_ref[...] = jnp.zeros_like(y_ref)


def _experts(x1p, block_expert, meta, row_tok, w_gate, w_up, w_down):
    D = D_MODEL
    assert x1p.shape[1] == LANES and x1p.dtype == jnp.uint32
    n_blocks = block_expert.shape[0]
    R = n_blocks * MOE_BLOCK
    H = EXPERT_HIDDEN
    grid_spec = pltpu.PrefetchScalarGridSpec(
        num_scalar_prefetch=3,
        grid=(n_blocks,),
        in_specs=[
            pl.BlockSpec(memory_space=pl.ANY),
            pl.BlockSpec(memory_space=pl.ANY),
            pl.BlockSpec(memory_space=pl.ANY),
            pl.BlockSpec(memory_space=pl.ANY),
        ],
        out_specs=pl.BlockSpec((MOE_BLOCK * ROW_PIECES, LANES), lambda b, be, meta, rt: (b, 0)),
        scratch_shapes=[
            pltpu.VMEM((GATHER_SLOTS, MOE_BLOCK * PACKED_ROW_PIECES, LANES), jnp.uint32),
            pltpu.VMEM((2, D, H), F32),
            pltpu.VMEM((2, D, H), F32),
            pltpu.VMEM((2, H, D), F32),
            pltpu.VMEM((D, H), BF16),
            pltpu.VMEM((D, H), BF16),
            pltpu.VMEM((H, D), BF16),
            pltpu.SemaphoreType.DMA((GATHER_SLOTS,)),
            pltpu.SemaphoreType.DMA((2,)),
            pltpu.SMEM((1,), jnp.int32),
        ],
    )
    return pl.pallas_call(
        _expert_kernel,
        grid_spec=grid_spec,
        out_shape=jax.ShapeDtypeStruct((R * ROW_PIECES, LANES), F32),
        compiler_params=_cparams(1),
        name="moe_experts",
    )(block_expert, meta, row_tok, x1p, w_gate, w_up, w_down)


COMBINE_TOKENS = 128


def _combine_kernel(dest_ref, y_hbm, x1_ref, route_ref, lns_ref, lnb_ref, o_ref, yg_ref, sem):
    i = pl.program_id(0)
    tt = COMBINE_TOKENS
    n = dest_ref.shape[0] // (TOP_K * tt)

    rp = ROW_PIECES

    def start_token(step, slot, r):
        for k in range(TOP_K):
            row = dest_ref[(step * tt + r) * TOP_K + k]
            pltpu.make_async_copy(y_hbm.at[pl.ds(pl.multiple_of(row * rp, rp), rp)],
                                  yg_ref.at[slot, k, pl.ds(pl.multiple_of(r * rp, rp), rp)],
                                  sem.at[slot]).start(priority=k)

    def wait_slot(slot):
        for k in range(TOP_K):
            pltpu.make_async_copy(y_hbm.at[pl.ds(0, tt * rp)], yg_ref.at[slot, k], sem.at[slot]).wait()

    @pl.when(i == 0)
    def _():
        for ahead in range(GATHER_SLOTS - 1):
            def body(r, carry, ahead=ahead):
                start_token(min(ahead, n - 1), ahead, r)
                return carry
            lax.fori_loop(0, tt, body, 0, unroll=4)

    def step(slot):
        wait_slot(slot)
        nxt = jnp.minimum(i + (GATHER_SLOTS - 1), n - 1)
        for r in range(tt):
            start_token(nxt, (slot + GATHER_SLOTS - 1) % GATHER_SLOTS, r)
        route = route_ref[...]
        w1, w2 = route[:, 2:3], route[:, 3:4]
        moe = jnp.concatenate(
            [w1 * yg_ref[slot, 0, pl.ds(c, tt, stride=rp), :] + w2 * yg_ref[slot, 1, pl.ds(c, tt, stride=rp), :]
             for c in range(rp)], axis=1)
        o_ref[...] = _layer_norm(ALPHA * x1_ref[...] + moe, lns_ref[...], lnb_ref[...])

    for s in range(GATHER_SLOTS):
        pl.when(lax.rem(i, GATHER_SLOTS) == s)(functools.partial(step, s))

    @pl.when(i == n - 1)
    def _():
        for ahead in range(1, GATHER_SLOTS):
            wait_slot((n - 1 + ahead) % GATHER_SLOTS)


def _combine(dest, y, x1, route, lns, lnb):
    T, D = x1.shape
    tt = COMBINE_TOKENS
    grid_spec = pltpu.PrefetchScalarGridSpec(
        num_scalar_prefetch=1,
        grid=(T // tt,),
        in_specs=[
            pl.BlockSpec(memory_space=pl.ANY),
            pl.BlockSpec((tt, D), lambda i, d: (i, 0)),
            pl.BlockSpec((tt, LANES), lambda i, d: (i, 0)),
            pl.BlockSpec((1, D), lambda i, d: (0, 0)),
            pl.BlockSpec((1, D), lambda i, d: (0, 0)),
        ],
        out_specs=pl.BlockSpec((tt, D), lambda i, d: (i, 0)),
        scratch_shapes=[
            pltpu.VMEM((GATHER_SLOTS, TOP_K, tt * ROW_PIECES, LANES), F32),
            pltpu.SemaphoreType.DMA((GATHER_SLOTS,)),
        ],
    )
    return pl.pallas_call(
        _combine_kernel,
        grid_spec=grid_spec,
        out_shape=jax.ShapeDtypeStruct((T, D), F32),
        compiler_params=_cparams(1),
        name="moe_combine_ln",
    )(dest, y, x1, route, lns, lnb)


def _rotary_tables(positions):
    pos = positions.reshape(-1).astype(F32)
    S = pos.shape[0]
    inv_a = ROPE_THETA ** (-jnp.arange(0, ROPE_DIM, 2, dtype=F32) / ROPE_DIM)
    inv_r = RET_ROT_BASE ** (-jnp.linspace(0.0, 1.0, RET_QK_DIM // 2, dtype=F32))
    ang = jnp.concatenate([inv_a, inv_r])[:, None] * pos[None, :]
    cos_t, sin_t = lax.optimization_barrier((jnp.cos(ang), jnp.sin(ang)))
    na = ROPE_DIM // 2
    cos, sin, cr, sr = cos_t[:na].T, sin_t[:na].T, cos_t[na:].T, sin_t[na:].T
    pad1 = jnp.ones((S, LANES - ROPE_DIM), F32)
    pad0 = jnp.zeros((S, LANES - ROPE_DIM), F32)
    tabs_a = jnp.concatenate([cos, cos, pad1, sin, sin, pad0], axis=1)
    tabs_r = jnp.concatenate([cr, cr, sr, sr], axis=1)
    return tabs_a, tabs_r


def kernel(x, positions, w_in, w_branch_attn, w_branch_ret, w_out, ln1_scale, ln1_bias, w_group_router, b_group_router, w_expert_router, b_expert_router, w_expert_gate, w_expert_up, w_expert_down, ln2_scale, ln2_bias):
    B, S, D = x.shape
    assert B == 1 and D == D_MODEL and w_in.shape[0] == DEPTH == 1
    assert S % (ATT_BLOCK * DILATIONS[2]) == 0
    x2d = x.reshape(S, D)
    xb = x2d.astype(BF16)
    tabs_a, tabs_r = _rotary_tables(positions)

    nat = _input_projection(xb, w_in[0], tabs_a, tabs_r, NAT_COLS, NAT_MODES, 1)
    qkv_modes = (MODE_ATT_Q, MODE_ATT_K, MODE_PLAIN)
    qkv = [None] + [
        _input_projection(xb, w_in[0], tabs_a, tabs_r, (CB_QA + g, CB_KA + g, CB_VA + g), qkv_modes,
                          DILATIONS[g])
        for g in (1, 2)]
    o0, l0 = _dilated_attention(nat.reshape(S // ATT_BLOCK, 1, ATT_BLOCK, nat.shape[1]), 0,
                                NC_Q0, NC_K0, NC_V0)
    att = [(o0.reshape(S, ATT_GROUP_WIDTH), l0.reshape(S, LANES))]
    att += [_dilated_attention(qkv[g], g, 0, 1, 2) for g in (1, 2)]
    o_r = _retention(nat, S)

    w_route = jnp.concatenate(
        [w_group_router[0], w_expert_router[0].transpose(1, 0, 2).reshape(D, N_EXPERTS),
         jnp.zeros((D, LANES - N_EXPERT_GROUPS - N_EXPERTS), F32)], axis=1)
    b_route = jnp.concatenate(
        [b_group_router[0], b_expert_router[0].reshape(N_EXPERTS),
         jnp.zeros((LANES - N_EXPERT_GROUPS - N_EXPERTS,), F32)]).reshape(1, LANES)
    x1, x1p, route, cnt = _merge(
        [a[0] for a in att], [a[1] for a in att], o_r, nat, x2d,
        w_branch_attn[0].astype(BF16), w_branch_ret[0].astype(BF16), w_out[0].astype(BF16),
        ln1_scale[0].reshape(1, D), ln1_bias[0].reshape(1, D),
        jnp.concatenate(_split_bf16(w_route), axis=1), b_route, S)

    e_flat = route[:, 0:TOP_K].astype(jnp.int32).reshape(S * TOP_K)
    rank_flat = route[:, 4:4 + TOP_K].astype(jnp.int32).reshape(S * TOP_K)
    counts = cnt[0, N_EXPERT_GROUPS:N_EXPERT_GROUPS + N_EXPERTS].astype(jnp.int32)
    padded = (counts + (MOE_BLOCK - 1)) // MOE_BLOCK * MOE_BLOCK
    pstart = jnp.cumsum(padded) - padded
    hot = e_flat[:, None] == jnp.arange(N_EXPERTS, dtype=jnp.int32)[None, :]
    dest = jnp.sum(jnp.where(hot, pstart[None, :], 0), axis=1) + rank_flat
    row_tok, block_expert, meta = _plan(dest, counts)
    y = _experts(x1p, block_expert, meta, row_tok, w_expert_gate[0], w_expert_up[0], w_expert_down[0])
    out = _combine(dest, y, x1, route, ln2_scale[0].reshape(1, D), ln2_bias[0].reshape(1, D))
    return out.reshape(B, S, D)
```

```python
import functools
import math

import jax
import jax.numpy as jnp
from jax import lax
from jax.experimental import pallas as pl
from jax.experimental.pallas import tpu as pltpu

F32 = jnp.float32
BF16 = jnp.bfloat16

D_MODEL = 2048
ATT_HEAD_DIM = 128
ATT_HEADS_PER_GROUP = 8
DILATIONS = (1, 4, 16)
ATT_STEPS = 128
ATT_GROUP_WIDTH = ATT_HEADS_PER_GROUP * ATT_HEAD_DIM
ROPE_DIM = ATT_HEAD_DIM // 4
ROPE_THETA = 500000.0
ATT_BLOCK = 128

RET_HEADS = 8
RET_QK_DIM = 128
RET_V_DIM = 256
RET_CHUNK = 128
RET_ROT_BASE = 10000.0
RET_LOG_DECAY = tuple(math.log1p(-(2.0 ** (-5.0 - h))) for h in range(RET_HEADS))

N_EXPERT_GROUPS = 4
EXPERTS_PER_GROUP = 8
N_EXPERTS = N_EXPERT_GROUPS * EXPERTS_PER_GROUP
TOP_K = 2
EXPERT_HIDDEN = 512
MOE_BLOCK = 128
MOE_BLOCK_SHIFT = MOE_BLOCK.bit_length() - 1
assert MOE_BLOCK == 1 << MOE_BLOCK_SHIFT

DEPTH = 1
ALPHA = (2.0 * DEPTH) ** 0.25
LN_EPS = 1e-5
NEG_INF = -1e30

COL_BLOCK = 1024
CB_QA, CB_KA, CB_VA = 0, 3, 6
CB_QR, CB_KR, CB_VR, CB_GR = 9, 10, 11, 13
CB_GATE_A, CB_GATE_B = 15, 17

MODE_PLAIN, MODE_ATT_Q, MODE_ATT_K, MODE_RET_Q, MODE_RET_K = 0, 1, 2, 3, 4

NAT_COLS = (CB_QA, CB_KA, CB_VA, CB_QR, CB_KR, CB_VR, CB_VR + 1, CB_GR, CB_GR + 1,
            CB_GATE_A, CB_GATE_A + 1, CB_GATE_B, CB_GATE_B + 1)
NAT_MODES = (MODE_ATT_Q, MODE_ATT_K, MODE_PLAIN, MODE_RET_Q, MODE_RET_K) + (MODE_PLAIN,) * 8
NC_Q0, NC_K0, NC_V0, NC_QR, NC_KR, NC_VR, NC_GR, NC_GATE_A, NC_GATE_B = 0, 1, 2, 3, 4, 5, 7, 9, 11

LANES = 128
ROW_PIECES = D_MODEL // LANES
PACKED_ROW_PIECES = ROW_PIECES // 2
VMEM_LIMIT = 56 * 1024 * 1024
PROJ_ROWS = 1024
ACC_PITCH = 24


def _cparams(n_grid_dims):
    return pltpu.CompilerParams(
        dimension_semantics=("arbitrary",) * n_grid_dims,
        vmem_limit_bytes=VMEM_LIMIT,
    )


def _inproj_kernel(cols_ref, modes_ref, x_ref, w_ref, ta_ref, tr_ref, pa_ref, pr_ref, o_ref,
                   wbf_ref, *acc, d):
    del cols_ref
    j = pl.program_id(0)

    @pl.when(pl.program_id(1) == 0)
    def _():
        wbf_ref[...] = w_ref[...].astype(BF16)

    mode = modes_ref[j]
    tm, tn = x_ref.shape[0], wbf_ref.shape[1]
    pair = 2 * LANES

    def project():
        return jnp.dot(x_ref[...], wbf_ref[...], preferred_element_type=F32)

    def emit(p, piece):
        if d == 1:
            o_ref[:, p * pair:(p + 1) * pair] = piece.astype(o_ref.dtype)
        elif d == 4:
            acc[0][2 * p] = piece[:, :LANES]
            acc[0][2 * p + 1] = piece[:, LANES:]
        else:
            for g in range(tm // d):
                rows = slice(g * d, (g + 1) * d)
                dst = slice(g * ACC_PITCH, g * ACC_PITCH + d)
                acc[0][2 * p, dst, :] = piece[rows, :LANES]
                acc[0][2 * p + 1, dst, :] = piece[rows, LANES:]

    def plain():
        res = project()
        for p in range(tn // pair):
            emit(p, res[:, p * pair:(p + 1) * pair])

    def rotary(tab_ref, perm_ref, scale, half):
        res = project()
        tab = tab_ref[...]
        lane = lax.broadcasted_iota(jnp.int32, tab.shape, 1)
        c = jnp.where(lane < half, tab, pltpu.roll(tab, half, 1))
        s = jnp.where(lane < half, pltpu.roll(tab, LANES - half, 1), tab)
        c = jnp.concatenate([jnp.where(lane < 2 * half, c, 1.0)] * 2, axis=1) * scale
        s = jnp.concatenate([jnp.where(lane < 2 * half, s, 0.0)] * 2, axis=1) * scale
        for p in range(tn // pair):
            a = res[:, p * pair:(p + 1) * pair]
            partner = jnp.dot(a.astype(BF16), perm_ref[...], preferred_element_type=F32)
            emit(p, a * c + partner * s)

    is_att = (mode == MODE_ATT_Q) | (mode == MODE_ATT_K)
    is_ret = (mode == MODE_RET_Q) | (mode == MODE_RET_K)
    pl.when(mode == MODE_PLAIN)(plain)
    pl.when(is_att)(lambda: rotary(ta_ref, pa_ref, jnp.where(mode == MODE_ATT_Q, ATT_HEAD_DIM ** -0.5, 1.0),
                                   ROPE_DIM // 2))
    if d == 1:
        pl.when(is_ret)(lambda: rotary(tr_ref, pr_ref, jnp.where(mode == MODE_RET_K, RET_QK_DIM ** -0.5, 1.0),
                                       RET_QK_DIM // 2))
        return

    acc_ref = acc[0]
    for h in range(tn // LANES):
        sl = slice(h * LANES, (h + 1) * LANES)
        if d == 4:
            P = ATT_BLOCK * d
            for sb in range(tm // P):
                for r in range(d):
                    o_ref[sb, r, :, sl] = acc_ref[h, pl.ds(sb * P + r, ATT_BLOCK, stride=d), :].astype(o_ref.dtype)
        else:
            for r in range(d):
                o_ref[r, :, sl] = acc_ref[h, pl.ds(r, tm // d, stride=ACC_PITCH), :].astype(o_ref.dtype)


def _rotation_matrix(half, width):
    k = jnp.arange(2 * LANES, dtype=jnp.int32)[:, None]
    l = jnp.arange(2 * LANES, dtype=jnp.int32)[None, :]
    same_head = (k // LANES) == (l // LANES)
    kk, ll = k % LANES, l % LANES
    neg = same_head & (ll < half) & (kk == ll + half)
    pos = same_head & (ll >= half) & (ll < width) & (kk == ll - half)
    return (pos.astype(F32) - neg.astype(F32)).astype(BF16)


def _input_projection(xb, w_in, tabs_a, tabs_r, cols, modes, d):
    S, D = xb.shape
    tm, tn = PROJ_ROWS, COL_BLOCK
    n = len(cols)
    P = ATT_BLOCK * d
    if d == 1:
        out_shape = jax.ShapeDtypeStruct((S, n * tn), BF16)
        out_spec = pl.BlockSpec((tm, tn), lambda j, i, c, m: (i, j))
    elif d == 4:
        out_shape = jax.ShapeDtypeStruct((S // P, d, ATT_BLOCK, n * tn), BF16)
        out_spec = pl.BlockSpec((tm // P, d, ATT_BLOCK, tn), lambda j, i, c, m: (i, 0, 0, j))
    else:
        assert d == 16 and P == 2 * tm
        out_shape = jax.ShapeDtypeStruct((S // P, d, ATT_BLOCK, n * tn), BF16)
        out_spec = pl.BlockSpec((None, d, tm // d, tn), lambda j, i, c, m: (i // 2, 0, i % 2, j))

    def tab_rows(j, i, c, m):
        return (jnp.where(m[j] != MODE_PLAIN, i, 0), 0)

    grid_spec = pltpu.PrefetchScalarGridSpec(
        num_scalar_prefetch=2,
        grid=(n, S // tm),
        in_specs=[
            pl.BlockSpec((tm, D), lambda j, i, c, m: (i, 0)),
            pl.BlockSpec((D, tn), lambda j, i, c, m: (0, c[j])),
            pl.BlockSpec((tm, LANES), tab_rows),
            pl.BlockSpec((tm, LANES), tab_rows),
            pl.BlockSpec((2 * LANES, 2 * LANES), lambda j, i, c, m: (0, 0)),
            pl.BlockSpec((2 * LANES, 2 * LANES), lambda j, i, c, m: (0, 0)),
        ],
        out_specs=out_spec,
        scratch_shapes=[pltpu.VMEM((D, tn), BF16)]
        + {1: [], 4: [pltpu.VMEM((tn // LANES, tm, LANES), F32)],
           16: [pltpu.VMEM((tn // LANES, tm // 16 * ACC_PITCH, LANES), F32)]}[d],
    )
    return pl.pallas_call(
        functools.partial(_inproj_kernel, d=d),
        grid_spec=grid_spec,
        out_shape=out_shape,
        compiler_params=_cparams(2),
        name=f"in_proj_d{d}",
    )(jnp.asarray(cols, jnp.int32), jnp.asarray(modes, jnp.int32), xb, w_in, tabs_a, tabs_r,
      _rotation_matrix(ROPE_DIM // 2, ROPE_DIM), _rotation_matrix(RET_QK_DIM // 2, RET_QK_DIM))


ATT_Q_BLOCKS = 8


def _attn_kernel(q_ref, k_ref, kp_ref, v_ref, vp_ref, o_ref, lse_ref):
    n = pl.program_id(1)
    qi = lax.broadcasted_iota(jnp.int32, (ATT_BLOCK, 2 * ATT_BLOCK), 0)
    kj = lax.broadcasted_iota(jnp.int32, (ATT_BLOCK, 2 * ATT_BLOCK), 1)
    dist = qi + ATT_BLOCK - kj
    band = (dist >= 0) & (dist <= ATT_STEPS)
    lane = lax.broadcasted_iota(jnp.int32, (ATT_BLOCK, LANES), 1)
    for sub in range(q_ref.shape[0]):
        mask = band & ((kj >= ATT_BLOCK) | (n > 0)) if sub == 0 else band
        lse_slab = jnp.zeros((ATT_BLOCK, LANES), F32)
        for h in range(ATT_HEADS_PER_GROUP):
            sl = slice(h * ATT_HEAD_DIM, (h + 1) * ATT_HEAD_DIM)
            k_prev = kp_ref[:, sl] if sub == 0 else k_ref[sub - 1, :, sl]
            v_prev = vp_ref[:, sl] if sub == 0 else v_ref[sub - 1, :, sl]
            k2 = jnp.concatenate([k_prev, k_ref[sub, :, sl]], axis=0)
            v2 = jnp.concatenate([v_prev, v_ref[sub, :, sl]], axis=0)
            s = lax.dot_general(q_ref[sub, :, sl], k2, (((1,), (1,)), ((), ())),
                                preferred_element_type=F32)
            s = jnp.where(mask, s, NEG_INF)
            m = jnp.max(s, axis=1, keepdims=True)
            p = jnp.exp(s - m)
            l = jnp.sum(p, axis=1, keepdims=True)
            o = jnp.dot(p.astype(BF16), v2, preferred_element_type=F32)
            o_ref[sub, :, sl] = o / l
            lse_slab = jnp.where(lane == h, m + jnp.log(l), lse_slab)
        lse_ref[sub] = lse_slab


def _dilated_attention(qkv, g, cq, ck, cv):
    d = DILATIONS[g]
    W = ATT_GROUP_WIDTH
    nb = qkv.shape[0]
    nq = min(ATT_Q_BLOCKS, nb)
    assert qkv.shape[1] == d and nb % nq == 0
    blk = (nq, None, ATT_BLOCK, W)
    pblk = (None, None, ATT_BLOCK, W)
    cur = lambda c: (lambda r, n: (n, r, 0, c))
    prev = lambda c: (lambda r, n: (jnp.maximum(nq * n - 1, 0), r, 0, c))
    return pl.pallas_call(
        _attn_kernel,
        grid=(d, nb // nq),
        in_specs=[
            pl.BlockSpec(blk, cur(cq)),
            pl.BlockSpec(blk, cur(ck)),
            pl.BlockSpec(pblk, prev(ck)),
            pl.BlockSpec(blk, cur(cv)),
            pl.BlockSpec(pblk, prev(cv)),
        ],
        out_specs=[pl.BlockSpec(blk, lambda r, n: (n, r, 0, 0)),
                   pl.BlockSpec((nq, None, ATT_BLOCK, LANES), lambda r, n: (n, r, 0, 0))],
        out_shape=[jax.ShapeDtypeStruct((nb, d, ATT_BLOCK, W), F32),
                   jax.ShapeDtypeStruct((nb, d, ATT_BLOCK, LANES), F32)],
        compiler_params=_cparams(2),
        name=f"dilated_attn_g{g}",
    )(qkv, qkv, qkv, qkv, qkv)


RET_CHUNKS_PER_STEP = 4


def _retention_kernel(q_ref, k_ref, vlo_ref, vhi_ref, glo_ref, ghi_ref, o_ref,
                      state_ref, intra_ref, dec_ref):
    @pl.when(pl.program_id(0) == 0)
    def _():
        state_ref[...] = jnp.zeros_like(state_ref)
        ri = lax.broadcasted_iota(jnp.int32, (RET_CHUNK, RET_CHUNK), 0)
        ci = lax.broadcasted_iota(jnp.int32, (RET_CHUNK, RET_CHUNK), 1)
        diff = (ri - ci).astype(F32)
        nrow = ri.astype(F32)
        for h in range(RET_HEADS):
            ld = RET_LOG_DECAY[h]
            intra_ref[h] = jnp.where(diff >= 0, jnp.exp(jnp.maximum(diff, 0.0) * ld), 0.0)
            dec_ref[h, 0] = jnp.exp((nrow + 1.0) * ld)
            dec_ref[h, 1] = jnp.exp((RET_CHUNK - 1.0 - nrow) * ld)

    half = RET_HEADS // 2
    for h in range(RET_HEADS):
        ld = RET_LOG_DECAY[h]
        sl = slice(h * RET_QK_DIM, (h + 1) * RET_QK_DIM)
        vsl = slice((h % half) * RET_V_DIM, (h % half + 1) * RET_V_DIM)
        v_ref, g_ref = (vlo_ref, glo_ref) if h < half else (vhi_ref, ghi_ref)
        state = state_ref[h]
        for c in range(q_ref.shape[0] // RET_CHUNK):
            rs = slice(c * RET_CHUNK, (c + 1) * RET_CHUNK)
            q = q_ref[rs, sl]
            k = k_ref[rs, sl]
            v = v_ref[rs, vsl]
            scores = (lax.dot_general(q, k, (((1,), (1,)), ((), ())), preferred_element_type=F32)
                      * intra_ref[h])
            inner = jnp.dot(scores.astype(BF16), v, preferred_element_type=F32)
            cross = jnp.dot((q.astype(F32) * dec_ref[h, 0]).astype(BF16), state.astype(BF16),
                            preferred_element_type=F32)
            kv = lax.dot_general((k.astype(F32) * dec_ref[h, 1]).astype(BF16), v,
                                 (((0,), (0,)), ((), ())), preferred_element_type=F32)
            state = math.exp(RET_CHUNK * ld) * state + kv
            o = inner + cross
            mu = jnp.mean(o, axis=1, keepdims=True)
            var = jnp.mean(jnp.square(o - mu), axis=1, keepdims=True)
            o = (o - mu) * lax.rsqrt(var + LN_EPS)
            gate = g_ref[rs, vsl].astype(F32)
            gate = gate * (1.0 / (1.0 + jnp.exp(-gate)))
            o_ref[rs, h * RET_V_DIM:(h + 1) * RET_V_DIM] = (gate * o).astype(o_ref.dtype)
        state_ref[h] = state


def _retention(nat, S):
    rows = RET_CHUNK * RET_CHUNKS_PER_STEP
    blk = (rows, COL_BLOCK)

    def col(cb):
        return pl.BlockSpec(blk, lambda c: (c, cb))

    return pl.pallas_call(
        _retention_kernel,
        grid=(S // rows,),
        in_specs=[col(NC_QR), col(NC_KR), col(NC_VR), col(NC_VR + 1), col(NC_GR), col(NC_GR + 1)],
        out_specs=pl.BlockSpec((rows, RET_HEADS * RET_V_DIM), lambda c: (c, 0)),
        out_shape=jax.ShapeDtypeStruct((S, RET_HEADS * RET_V_DIM), BF16),
        scratch_shapes=[pltpu.VMEM((RET_HEADS, RET_QK_DIM, RET_V_DIM), F32),
                        pltpu.VMEM((RET_HEADS, RET_CHUNK, RET_CHUNK), F32),
                        pltpu.VMEM((RET_HEADS, 2, RET_CHUNK, RET_QK_DIM), F32)],
        compiler_params=_cparams(1),
        name="retention",
    )(nat, nat, nat, nat, nat, nat)


MERGE_ROWS = 512
COL_HALVES = (slice(0, D_MODEL // 2), slice(D_MODEL // 2, D_MODEL))


def _layer_norm(z, scale, bias):
    mu = jnp.mean(z, axis=1, keepdims=True)
    var = jnp.mean(jnp.square(z - mu), axis=1, keepdims=True)
    return (z - mu) * lax.rsqrt(var + LN_EPS) * scale + bias


def _sigmoid(x):
    return 1.0 / (1.0 + jnp.exp(-x))


def _pack_bf16_pair(a, b):
    abits = lax.bitcast_convert_type(a.astype(BF16).astype(F32), jnp.uint32)
    bbits = lax.bitcast_convert_type(b.astype(BF16).astype(F32), jnp.uint32)
    return (abits >> 16) | (bbits & jnp.uint32(0xFFFF0000))


def _unpack_bf16_pair(w):
    a = lax.bitcast_convert_type(w << 16, F32).astype(BF16)
    b = lax.bitcast_convert_type(w & jnp.uint32(0xFFFF0000), F32).astype(BF16)
    return a, b


def _split_bf16(v):
    hi = v.astype(BF16)
    return hi, (v - hi.astype(F32)).astype(BF16)


def _mix_kernel(o0_ref, o1_ref, o2_ref, l0_ref, l1_ref, l2_ref, or_ref,
                ga0_ref, ga1_ref, gb0_ref, gb1_ref, wa_ref, wr_ref, hexp_ref,
                m_ref, s1_ref, s2_ref, ls1_ref, ls2_ref):
    tm = o0_ref.shape[0]
    for d, src, lsrc, dst, ldst in ((DILATIONS[1], o1_ref, l1_ref, s1_ref, ls1_ref),
                                    (DILATIONS[2], o2_ref, l2_ref, s2_ref, ls2_ref)):
        for r in range(d):
            for h in range(ATT_HEADS_PER_GROUP):
                dst[h, pl.ds(r, tm // d, stride=d), :] = src[r, :, h * ATT_HEAD_DIM:(h + 1) * ATT_HEAD_DIM]
            ldst[pl.ds(r, tm // d, stride=d), :] = lsrc[r]

    l0, l1, l2 = l0_ref[...], ls1_ref[...], ls2_ref[...]
    lm = jnp.maximum(jnp.maximum(l0, l1), l2)
    e0, e1, e2 = jnp.exp(l0 - lm), jnp.exp(l1 - lm), jnp.exp(l2 - lm)
    den = e0 + e1 + e2

    def lanes_of_head(w):
        hi, lo = _split_bf16(w)
        return (jnp.dot(hi, hexp_ref[...], preferred_element_type=F32)
                + jnp.dot(lo, hexp_ref[...], preferred_element_type=F32))

    o_r = or_ref[...]
    y_r = [jnp.dot(o_r, wr_ref[:, cs], preferred_element_type=F32) for cs in COL_HALVES]
    heads = range(ATT_HEADS_PER_GROUP)
    o1 = jnp.concatenate([s1_ref[h] for h in heads], axis=1)
    o2 = jnp.concatenate([s2_ref[h] for h in heads], axis=1)
    o_a = (lanes_of_head(e0 / den) * o0_ref[...] + lanes_of_head(e1 / den) * o1
           + lanes_of_head(e2 / den) * o2).astype(BF16)
    for n, cs in enumerate(COL_HALVES):
        y_a = jnp.dot(o_a, wa_ref[:, cs], preferred_element_type=F32)
        gate_a = (ga0_ref, ga1_ref)[n][...].astype(F32)
        gate_b = (gb0_ref, gb1_ref)[n][...].astype(F32)
        m_ref[:, cs] = (_sigmoid(gate_a) * y_a + _sigmoid(gate_b) * y_r[n]).astype(m_ref.dtype)


def _finish_kernel(m_ref, x_ref, wo_ref, lns_ref, lnb_ref, wrt_ref, brt_ref,
                   x1_ref, x1p_ref, route_ref, cnt_ref):
    rc = x_ref.shape[0]
    half = D_MODEL // 2
    halves = COL_HALVES

    @pl.when(pl.program_id(0) == 0)
    def _():
        cnt_ref[...] = jnp.zeros_like(cnt_ref)

    z = [ALPHA * x_ref[:, cs]
         + jnp.dot(m_ref[:, :half], wo_ref[:half, cs], preferred_element_type=F32)
         + jnp.dot(m_ref[:, half:], wo_ref[half:, cs], preferred_element_type=F32) for cs in halves]
    mu = (jnp.sum(z[0], axis=1, keepdims=True) + jnp.sum(z[1], axis=1, keepdims=True)) / D_MODEL
    var = (jnp.sum(jnp.square(z[0] - mu), axis=1, keepdims=True)
           + jnp.sum(jnp.square(z[1] - mu), axis=1, keepdims=True)) / D_MODEL
    inv = lax.rsqrt(var + LN_EPS)
    x1 = [(z[n] - mu) * inv * lns_ref[:, cs] + lnb_ref[:, cs] for n, cs in enumerate(halves)]
    for n, cs in enumerate(halves):
        x1_ref[:, cs] = x1[n]
    packed = _pack_bf16_pair(x1[0], x1[1])
    for c in range(PACKED_ROW_PIECES):
        x1p_ref[pl.ds(c, rc, stride=PACKED_ROW_PIECES), :] = (
            packed[:, c * LANES:(c + 1) * LANES])

    logits = brt_ref[...]
    for n, cs in enumerate(halves):
        x_hi, x_lo = _split_bf16(x1[n])
        hl = jnp.dot(x_hi, wrt_ref[cs, :], preferred_element_type=F32)
        logits = (logits + hl[:, :LANES] + hl[:, LANES:]
                  + jnp.dot(x_lo, wrt_ref[cs, :LANES], preferred_element_type=F32))
    lane = lax.broadcasted_iota(jnp.int32, logits.shape, 1)
    is_g = lane < N_EXPERT_GROUPS
    gl = jnp.where(is_g, logits, NEG_INF)
    gmax = jnp.max(gl, axis=1, keepdims=True)
    g_sel = jnp.min(jnp.where(is_g & (gl == gmax), lane, LANES), axis=1, keepdims=True)
    p_group = 1.0 / jnp.sum(jnp.where(is_g, jnp.exp(gl - gmax), 0.0), axis=1, keepdims=True)
    lo = N_EXPERT_GROUPS + EXPERTS_PER_GROUP * g_sel
    in_grp = (lane >= lo) & (lane < lo + EXPERTS_PER_GROUP)
    el = jnp.where(in_grp, logits, NEG_INF)
    top1 = jnp.max(el, axis=1, keepdims=True)
    idx1 = jnp.min(jnp.where(in_grp & (el == top1), lane, LANES), axis=1, keepdims=True)
    el2 = jnp.where(lane == idx1, NEG_INF, el)
    top2 = jnp.max(el2, axis=1, keepdims=True)
    idx2 = jnp.min(jnp.where(in_grp & (lane != idx1) & (el2 == top2), lane, LANES), axis=1, keepdims=True)
    t = jnp.exp(top2 - top1)
    wt1 = p_group * (1.0 / (1.0 + t))
    wt2 = p_group * (t / (1.0 + t))
    slab = jnp.where(lane == 0, (idx1 - N_EXPERT_GROUPS).astype(F32), 0.0)
    slab = jnp.where(lane == 1, (idx2 - N_EXPERT_GROUPS).astype(F32), slab)
    slab = jnp.where(lane == 2, wt1, slab)
    slab = jnp.where(lane == 3, wt2, slab)

    hot = jnp.where((lane == idx1) | (lane == idx2), 1.0, 0.0)
    ri = lax.broadcasted_iota(jnp.int32, (rc, rc), 0)
    ci = lax.broadcasted_iota(jnp.int32, (rc, rc), 1)
    ltri = jnp.where(ri > ci, 1.0, 0.0).astype(BF16)
    prefix = jnp.dot(ltri, hot.astype(BF16), preferred_element_type=F32) + cnt_ref[...]
    rank1 = jnp.sum(jnp.where(lane == idx1, prefix, 0.0), axis=1, keepdims=True)
    rank2 = jnp.sum(jnp.where(lane == idx2, prefix, 0.0), axis=1, keepdims=True)
    cnt_ref[...] = cnt_ref[...] + jnp.sum(hot, axis=0, keepdims=True)
    slab = jnp.where(lane == 4, rank1, slab)
    slab = jnp.where(lane == 5, rank2, slab)
    route_ref[...] = slab


def _row_spec(tm, width):
    return pl.BlockSpec((tm, width), lambda i: (i, 0))


def _const_spec(shape):
    return pl.BlockSpec(shape, lambda i: (0, 0), pipeline_mode=pl.Buffered(1))


def _mix(o_g, lse_g, o_r, nat, wa, wr, S):
    tm = MERGE_ROWS
    D = D_MODEL
    W = ATT_GROUP_WIDTH
    head_expand = (jnp.arange(W, dtype=jnp.int32)[None, :] // ATT_HEAD_DIM
                   == jnp.arange(LANES, dtype=jnp.int32)[:, None]).astype(BF16)

    def streams(g, width):
        d = DILATIONS[g]
        per = (ATT_BLOCK * d) // tm
        return pl.BlockSpec((None, d, tm // d, width), lambda i: (i // per, 0, i % per, 0))

    def ncol(cb):
        return pl.BlockSpec((tm, COL_BLOCK), lambda i: (i, cb))

    return pl.pallas_call(
        _mix_kernel,
        grid=(S // tm,),
        in_specs=[
            _row_spec(tm, W), streams(1, W), streams(2, W),
            _row_spec(tm, LANES), streams(1, LANES), streams(2, LANES),
            _row_spec(tm, RET_HEADS * RET_V_DIM),
            ncol(NC_GATE_A), ncol(NC_GATE_A + 1), ncol(NC_GATE_B), ncol(NC_GATE_B + 1),
            _const_spec((W, D)), _const_spec((RET_HEADS * RET_V_DIM, D)), _const_spec((LANES, W)),
        ],
        out_specs=_row_spec(tm, D),
        out_shape=jax.ShapeDtypeStruct((S, D), BF16),
        scratch_shapes=[pltpu.VMEM((ATT_HEADS_PER_GROUP, tm, ATT_HEAD_DIM), F32),
                        pltpu.VMEM((ATT_HEADS_PER_GROUP, tm, ATT_HEAD_DIM), F32),
                        pltpu.VMEM((tm, LANES), F32), pltpu.VMEM((tm, LANES), F32)],
        compiler_params=_cparams(1),
        name="mix_branches",
    )(*o_g, *lse_g, o_r, nat, nat, nat, nat, wa, wr, head_expand)


def _finish(merged, x2d, wo, lns, lnb, wrt, brt):
    S, D = x2d.shape
    tm = MERGE_ROWS
    return pl.pallas_call(
        _finish_kernel,
        grid=(S // tm,),
        in_specs=[
            _row_spec(tm, D), _row_spec(tm, D),
            _const_spec((D, D)), _const_spec((1, D)), _const_spec((1, D)),
            _const_spec((D, 2 * LANES)), _const_spec((1, LANES)),
        ],
        out_specs=[_row_spec(tm, D), pl.BlockSpec((tm * PACKED_ROW_PIECES, LANES), lambda i: (i, 0)),
                   _row_spec(tm, LANES), pl.BlockSpec((1, LANES), lambda i: (0, 0))],
        out_shape=[jax.ShapeDtypeStruct((S, D), F32),
                   jax.ShapeDtypeStruct((S * PACKED_ROW_PIECES, LANES), jnp.uint32),
                   jax.ShapeDtypeStruct((S, LANES), F32),
                   jax.ShapeDtypeStruct((1, LANES), F32)],
        compiler_params=_cparams(1),
        name="out_proj_ln_router",
    )(merged, x2d, wo, lns, lnb, wrt, brt)


META_N_USED, META_NEXT = 0, 1
META_SIZE = META_NEXT + N_EXPERTS
WEIGHT_DMA_PRIORITY = 1
GATHER_SLOTS = 3
GATHER_GROUP = 2
WEIGHT_SLOTS = 2


def _plan_kernel(dest_ref, cnt_ref, rt_ref, be_ref, meta_ref):
    n_blocks = be_ref.shape[0]

    def pad_row(r, carry):
        rt_ref[r] = 0
        return carry

    def per_expert(e, start):
        nb = lax.shift_right_logical(cnt_ref[e] + (MOE_BLOCK - 1), MOE_BLOCK_SHIFT)

        def fill(b, carry):
            be_ref[lax.shift_right_logical(start, MOE_BLOCK_SHIFT) + b] = e
            return carry
        lax.fori_loop(0, nb, fill, 0)
        lax.fori_loop(start + cnt_ref[e], start + nb * MOE_BLOCK, pad_row, 0)
        return start + nb * MOE_BLOCK
    total = lax.fori_loop(0, N_EXPERTS, per_expert, 0)
    n_used = lax.shift_right_logical(total, MOE_BLOCK_SHIFT)
    meta_ref[META_N_USED] = n_used

    def tail(b, carry):
        be_ref[b] = be_ref[jnp.maximum(n_used - 1, 0)]
        return carry
    lax.fori_loop(n_used, n_blocks, tail, 0)
    lax.fori_loop(total, rt_ref.shape[0], pad_row, 0)

    def nxt(k, cur):
        e = N_EXPERTS - 1 - k
        meta_ref[META_NEXT + e] = cur
        return jnp.where(cnt_ref[e] > 0, e, cur)
    lax.fori_loop(0, N_EXPERTS, nxt, N_EXPERTS)

    def assign(a, carry):
        rt_ref[dest_ref[a]] = lax.shift_right_logical(a, 1)
        return carry
    lax.fori_loop(0, dest_ref.shape[0], assign, 0, unroll=8)


def _plan(dest, counts):
    A = dest.shape[0]
    n_blocks = A // MOE_BLOCK + N_EXPERTS
    smem = pl.BlockSpec(memory_space=pltpu.SMEM)
    return pl.pallas_call(
        _plan_kernel,
        in_specs=[smem, smem],
        out_specs=[smem, smem, smem],
        out_shape=[
            jax.ShapeDtypeStruct((n_blocks * MOE_BLOCK,), jnp.int32),
            jax.ShapeDtypeStruct((n_blocks,), jnp.int32),
            jax.ShapeDtypeStruct((META_SIZE,), jnp.int32),
        ],
        name="moe_plan",
    )(dest, counts)


def _expert_kernel(be_ref, meta_ref, rt_ref, x_hbm, wg_hbm, wu_hbm, wd_hbm, y_ref,
                   xg_ref, wgf_ref, wuf_ref, wdf_ref, wgb_ref, wub_ref, wdb_ref, xsem, sem, slot_ref):
    b = pl.program_id(0)
    n_used = meta_ref[META_N_USED]
    e = be_ref[b]
    first = (b == 0) | (e != be_ref[jnp.maximum(b - 1, 0)])

    xp = PACKED_ROW_PIECES

    last = jnp.maximum(n_used - 1, 0)
    group_rows = GATHER_GROUP * MOE_BLOCK
    group = b // GATHER_GROUP
    last_group = last // GATHER_GROUP

    def row_copy(grp, slot, i):
        tok = rt_ref[grp * group_rows + i]
        return pltpu.make_async_copy(x_hbm.at[pl.ds(pl.multiple_of(tok * xp, xp), xp)],
                                     xg_ref.at[slot, pl.ds(pl.multiple_of(i * xp, xp), xp)],
                                     xsem.at[slot])

    def wait_rows(slot):
        pltpu.make_async_copy(x_hbm.at[pl.ds(0, group_rows * xp)], xg_ref.at[slot], xsem.at[slot]).wait()

    @pl.when(b == 0)
    def _():
        for ahead in range(GATHER_SLOTS - 1):
            def body(i, carry, ahead=ahead):
                row_copy(jnp.minimum(ahead, last_group), ahead, i).start()
                return carry
            lax.fori_loop(0, group_rows, body, 0, unroll=8)

    def copies(ex, s):
        return (pltpu.make_async_copy(wg_hbm.at[ex], wgf_ref.at[s], sem.at[s]),
                pltpu.make_async_copy(wu_hbm.at[ex], wuf_ref.at[s], sem.at[s]),
                pltpu.make_async_copy(wd_hbm.at[ex], wdf_ref.at[s], sem.at[s]))

    def next_expert(ex):
        return jnp.where(ex < N_EXPERTS, meta_ref[META_NEXT + jnp.minimum(ex, N_EXPERTS - 1)], N_EXPERTS)

    def fetch(ex, s):
        @pl.when(ex < N_EXPERTS)
        def _():
            for c in copies(ex, s):
                c.start(priority=WEIGHT_DMA_PRIORITY)

    @pl.when((b == 0) & (n_used > 0))
    def _():
        slot_ref[0] = 0
        ex = e
        for s in range(WEIGHT_SLOTS - 1):
            fetch(ex, s)
            ex = next_expert(ex)

    @pl.when(first & (b < n_used))
    def _():
        s = slot_ref[0]
        ex = e
        for _ in range(WEIGHT_SLOTS - 1):
            ex = next_expert(ex)
        fetch(ex, lax.rem(s + WEIGHT_SLOTS - 1, WEIGHT_SLOTS))
        for c in copies(e, s):
            c.wait()
        wgb_ref[...] = wgf_ref[s].astype(BF16)
        wub_ref[...] = wuf_ref[s].astype(BF16)
        wdb_ref[...] = wdf_ref[s].astype(BF16)
        slot_ref[0] = lax.rem(s + 1, WEIGHT_SLOTS)

    def step(slot, sub):
        if sub == 0:
            wait_rows(slot)
            nxt = jnp.minimum(group + (GATHER_SLOTS - 1), last_group)
            for i in range(group_rows):
                row_copy(nxt, (slot + GATHER_SLOTS - 1) % GATHER_SLOTS, i).start()
        base = sub * MOE_BLOCK * xp
        pieces = [_unpack_bf16_pair(xg_ref[slot, pl.ds(base + c, MOE_BLOCK, stride=xp), :])
                  for c in range(xp)]
        xb = jnp.concatenate([p[0] for p in pieces] + [p[1] for p in pieces], axis=1)
        hg = jnp.dot(xb, wgb_ref[...], preferred_element_type=F32)
        hu = jnp.dot(xb, wub_ref[...], preferred_element_type=F32)
        hid = (hg * _sigmoid(hg) * hu).astype(BF16)
        y = jnp.dot(hid, wdb_ref[...], preferred_element_type=F32)
        for c in range(ROW_PIECES):
            y_ref[pl.ds(c, MOE_BLOCK, stride=ROW_PIECES), :] = y[:, c * LANES:(c + 1) * LANES]

    for s in range(GATHER_SLOTS):
        for sub in range(GATHER_GROUP):
            pl.when((b < n_used) & (lax.rem(group, GATHER_SLOTS) == s)
                    & (lax.rem(b, GATHER_GROUP) == sub))(functools.partial(step, s, sub))

    @pl.when(b == last)
    def _():
        for ahead in range(1, GATHER_SLOTS):
            wait_rows(lax.rem(group + ahead, GATHER_SLOTS))

    @pl.when(b >= n_used)
    def _():
        y_ref[...] = jnp.zeros_like(y_ref)


def _experts(x1p, block_expert, meta, row_tok, w_gate, w_up, w_down):
    D = D_MODEL
    assert x1p.shape[1] == LANES and x1p.dtype == jnp.uint32
    n_blocks = block_expert.shape[0]
    R = n_blocks * MOE_BLOCK
    H = EXPERT_HIDDEN
    grid_spec = pltpu.PrefetchScalarGridSpec(
        num_scalar_prefetch=3,
        grid=(n_blocks,),
        in_specs=[
            pl.BlockSpec(memory_space=pl.ANY),
            pl.BlockSpec(memory_space=pl.ANY),
            pl.BlockSpec(memory_space=pl.ANY),
            pl.BlockSpec(memory_space=pl.ANY),
        ],
        out_specs=pl.BlockSpec((MOE_BLOCK * ROW_PIECES, LANES), lambda b, be, meta, rt: (b, 0)),
        scratch_shapes=[
            pltpu.VMEM((GATHER_SLOTS, GATHER_GROUP * MOE_BLOCK * PACKED_ROW_PIECES, LANES), jnp.uint32),
            pltpu.VMEM((WEIGHT_SLOTS, D, H), F32),
            pltpu.VMEM((WEIGHT_SLOTS, D, H), F32),
            pltpu.VMEM((WEIGHT_SLOTS, H, D), F32),
            pltpu.VMEM((D, H), BF16),
            pltpu.VMEM((D, H), BF16),
            pltpu.VMEM((H, D), BF16),
            pltpu.SemaphoreType.DMA((GATHER_SLOTS,)),
            pltpu.SemaphoreType.DMA((WEIGHT_SLOTS,)),
            pltpu.SMEM((1,), jnp.int32),
        ],
    )
    return pl.pallas_call(
        _expert_kernel,
        grid_spec=grid_spec,
        out_shape=jax.ShapeDtypeStruct((R * ROW_PIECES, LANES), F32),
        compiler_params=_cparams(1),
        name="moe_experts",
    )(block_expert, meta, row_tok, x1p, w_gate, w_up, w_down)


COMBINE_TOKENS = 128


def _combine_kernel(dest_ref, y_hbm, x1_ref, route_ref, lns_ref, lnb_ref, o_ref, yg_ref, sem):
    i = pl.program_id(0)
    tt = COMBINE_TOKENS
    n = dest_ref.shape[0] // (TOP_K * tt)

    rp = ROW_PIECES

    def start_token(step, slot, r):
        for k in range(TOP_K):
            row = dest_ref[(step * tt + r) * TOP_K + k]
            pltpu.make_async_copy(y_hbm.at[pl.ds(pl.multiple_of(row * rp, rp), rp)],
                                  yg_ref.at[slot, k, pl.ds(pl.multiple_of(r * rp, rp), rp)],
                                  sem.at[slot]).start(priority=k)

    def wait_slot(slot):
        for k in range(TOP_K):
            pltpu.make_async_copy(y_hbm.at[pl.ds(0, tt * rp)], yg_ref.at[slot, k], sem.at[slot]).wait()

    @pl.when(i == 0)
    def _():
        for ahead in range(GATHER_SLOTS - 1):
            def body(r, carry, ahead=ahead):
                start_token(min(ahead, n - 1), ahead, r)
                return carry
            lax.fori_loop(0, tt, body, 0, unroll=4)

    def step(slot):
        wait_slot(slot)
        nxt = jnp.minimum(i + (GATHER_SLOTS - 1), n - 1)
        for r in range(tt):
            start_token(nxt, (slot + GATHER_SLOTS - 1) % GATHER_SLOTS, r)
        route = route_ref[...]
        w1, w2 = route[:, 2:3], route[:, 3:4]
        moe = jnp.concatenate(
            [w1 * yg_ref[slot, 0, pl.ds(c, tt, stride=rp), :] + w2 * yg_ref[slot, 1, pl.ds(c, tt, stride=rp), :]
             for c in range(rp)], axis=1)
        o_ref[...] = _layer_norm(ALPHA * x1_ref[...] + moe, lns_ref[...], lnb_ref[...])

    for s in range(GATHER_SLOTS):
        pl.when(lax.rem(i, GATHER_SLOTS) == s)(functools.partial(step, s))

    @pl.when(i == n - 1)
    def _():
        for ahead in range(1, GATHER_SLOTS):
            wait_slot((n - 1 + ahead) % GATHER_SLOTS)


def _combine(dest, y, x1, route, lns, lnb):
    T, D = x1.shape
    tt = COMBINE_TOKENS
    grid_spec = pltpu.PrefetchScalarGridSpec(
        num_scalar_prefetch=1,
        grid=(T // tt,),
        in_specs=[
            pl.BlockSpec(memory_space=pl.ANY),
            pl.BlockSpec((tt, D), lambda i, d: (i, 0)),
            pl.BlockSpec((tt, LANES), lambda i, d: (i, 0)),
            pl.BlockSpec((1, D), lambda i, d: (0, 0)),
            pl.BlockSpec((1, D), lambda i, d: (0, 0)),
        ],
        out_specs=pl.BlockSpec((tt, D), lambda i, d: (i, 0)),
        scratch_shapes=[
            pltpu.VMEM((GATHER_SLOTS, TOP_K, tt * ROW_PIECES, LANES), F32),
            pltpu.SemaphoreType.DMA((GATHER_SLOTS,)),
        ],
    )
    return pl.pallas_call(
        _combine_kernel,
        grid_spec=grid_spec,
        out_shape=jax.ShapeDtypeStruct((T, D), F32),
        compiler_params=_cparams(1),
        name="moe_combine_ln",
    )(dest, y, x1, route, lns, lnb)


def _rotary_tables(positions):
    pos = positions.reshape(-1).astype(F32)
    S = pos.shape[0]
    inv_a = ROPE_THETA ** (-jnp.arange(0, ROPE_DIM, 2, dtype=F32) / ROPE_DIM)
    inv_r = RET_ROT_BASE ** (-jnp.linspace(0.0, 1.0, RET_QK_DIM // 2, dtype=F32))
    ang = jnp.concatenate([inv_a, inv_r])[:, None] * pos[None, :]
    cos_t, sin_t = lax.optimization_barrier((jnp.cos(ang), jnp.sin(ang)))
    na = ROPE_DIM // 2
    cos, sin, cr, sr = cos_t[:na].T, sin_t[:na].T, cos_t[na:].T, sin_t[na:].T
    tabs_a = jnp.concatenate([cos, sin, jnp.zeros((S, LANES - ROPE_DIM), F32)], axis=1)
    tabs_r = jnp.concatenate([cr, sr], axis=1)
    return tabs_a, tabs_r


def kernel(x, positions, w_in, w_branch_attn, w_branch_ret, w_out, ln1_scale, ln1_bias, w_group_router, b_group_router, w_expert_router, b_expert_router, w_expert_gate, w_expert_up, w_expert_down, ln2_scale, ln2_bias):
    B, S, D = x.shape
    assert B == 1 and D == D_MODEL and w_in.shape[0] == DEPTH == 1
    assert S % (ATT_BLOCK * DILATIONS[2]) == 0
    x2d = x.reshape(S, D)
    xb = x2d.astype(BF16)
    tabs_a, tabs_r = _rotary_tables(positions)

    nat = _input_projection(xb, w_in[0], tabs_a, tabs_r, NAT_COLS, NAT_MODES, 1)
    qkv_modes = (MODE_ATT_Q, MODE_ATT_K, MODE_PLAIN)
    qkv = [None] + [
        _input_projection(xb, w_in[0], tabs_a, tabs_r, (CB_QA + g, CB_KA + g, CB_VA + g), qkv_modes,
                          DILATIONS[g])
        for g in (1, 2)]
    o0, l0 = _dilated_attention(nat.reshape(S // ATT_BLOCK, 1, ATT_BLOCK, nat.shape[1]), 0,
                                NC_Q0, NC_K0, NC_V0)
    att = [(o0.reshape(S, ATT_GROUP_WIDTH), l0.reshape(S, LANES))]
    att += [_dilated_attention(qkv[g], g, 0, 1, 2) for g in (1, 2)]
    o_r = _retention(nat, S)

    w_route = jnp.concatenate(
        [w_group_router[0], w_expert_router[0].transpose(1, 0, 2).reshape(D, N_EXPERTS),
         jnp.zeros((D, LANES - N_EXPERT_GROUPS - N_EXPERTS), F32)], axis=1)
    b_route = jnp.concatenate(
        [b_group_router[0], b_expert_router[0].reshape(N_EXPERTS),
         jnp.zeros((LANES - N_EXPERT_GROUPS - N_EXPERTS,), F32)]).reshape(1, LANES)
    merged = _mix([a[0] for a in att], [a[1] for a in att], o_r, nat,
                  w_branch_attn[0].astype(BF16), w_branch_ret[0].astype(BF16), S)
    x1, x1p, route, cnt = _finish(
        merged, x2d, w_out[0].astype(BF16), ln1_scale[0].reshape(1, D), ln1_bias[0].reshape(1, D),
        jnp.concatenate(_split_bf16(w_route), axis=1), b_route)

    e_flat = route[:, 0:TOP_K].astype(jnp.int32).reshape(S * TOP_K)
    rank_flat = route[:, 4:4 + TOP_K].astype(jnp.int32).reshape(S * TOP_K)
    counts = cnt[0, N_EXPERT_GROUPS:N_EXPERT_GROUPS + N_EXPERTS].astype(jnp.int32)
    padded = (counts + (MOE_BLOCK - 1)) // MOE_BLOCK * MOE_BLOCK
    pstart = jnp.cumsum(padded) - padded
    hot = e_flat[:, None] == jnp.arange(N_EXPERTS, dtype=jnp.int32)[None, :]
    dest = jnp.sum(jnp.where(hot, pstart[None, :], 0), axis=1) + rank_flat
    row_tok, block_expert, meta = _plan(dest, counts)
    y = _experts(x1p, block_expert, meta, row_tok, w_expert_gate[0], w_expert_up[0], w_expert_down[0])
    out = _combine(dest, y, x1, route, ln2_scale[0].reshape(1, D), ln2_bias[0].reshape(1, D))
    return out.reshape(B, S, D)
```

```python
import functools
import math

import jax
import jax.numpy as jnp
from jax import lax
from jax.experimental import pallas as pl
from jax.experimental.pallas import tpu as pltpu

F32 = jnp.float32
BF16 = jnp.bfloat16

D_MODEL = 2048
ATT_HEAD_DIM = 128
ATT_HEADS_PER_GROUP = 8
DILATIONS = (1, 4, 16)
ATT_STEPS = 128
ATT_GROUP_WIDTH = ATT_HEADS_PER_GROUP * ATT_HEAD_DIM
ROPE_DIM = ATT_HEAD_DIM // 4
ROPE_THETA = 500000.0
ATT_BLOCK = 128

RET_HEADS = 8
RET_QK_DIM = 128
RET_V_DIM = 256
RET_CHUNK = 128
RET_ROT_BASE = 10000.0
RET_LOG_DECAY = tuple(math.log1p(-(2.0 ** (-5.0 - h))) for h in range(RET_HEADS))

N_EXPERT_GROUPS = 4
EXPERTS_PER_GROUP = 8
N_EXPERTS = N_EXPERT_GROUPS * EXPERTS_PER_GROUP
TOP_K = 2
EXPERT_HIDDEN = 512
MOE_BLOCK = 128
MOE_BLOCK_SHIFT = MOE_BLOCK.bit_length() - 1
assert MOE_BLOCK == 1 << MOE_BLOCK_SHIFT

DEPTH = 1
ALPHA = (2.0 * DEPTH) ** 0.25
LN_EPS = 1e-5
NEG_INF = -1e30

COL_BLOCK = 1024
CB_QA, CB_KA, CB_VA = 0, 3, 6
CB_QR, CB_KR, CB_VR, CB_GR = 9, 10, 11, 13
CB_GATE_A, CB_GATE_B = 15, 17

MODE_PLAIN, MODE_ATT_Q, MODE_ATT_K, MODE_RET_Q, MODE_RET_K = 0, 1, 2, 3, 4

NAT_COLS = (CB_QA, CB_KA, CB_VA, CB_QR, CB_KR, CB_VR, CB_VR + 1, CB_GR, CB_GR + 1,
            CB_GATE_A, CB_GATE_A + 1, CB_GATE_B, CB_GATE_B + 1)
NAT_MODES = (MODE_ATT_Q, MODE_ATT_K, MODE_PLAIN, MODE_RET_Q, MODE_RET_K) + (MODE_PLAIN,) * 8
NC_Q0, NC_K0, NC_V0, NC_QR, NC_KR, NC_VR, NC_GR, NC_GATE_A, NC_GATE_B = 0, 1, 2, 3, 4, 5, 7, 9, 11

LANES = 128
ROW_PIECES = D_MODEL // LANES
PACKED_ROW_PIECES = ROW_PIECES // 2
VMEM_LIMIT = 56 * 1024 * 1024
PROJ_ROWS = 1024
ACC_PITCH = 24


def _cparams(n_grid_dims):
    return pltpu.CompilerParams(
        dimension_semantics=("arbitrary",) * n_grid_dims,
        vmem_limit_bytes=VMEM_LIMIT,
    )


def _inproj_kernel(cols_ref, modes_ref, x_ref, w_ref, ta_ref, tr_ref, pa_ref, pr_ref, o_ref,
                   wbf_ref, *acc, d):
    del cols_ref
    j = pl.program_id(0)

    @pl.when(pl.program_id(1) == 0)
    def _():
        wbf_ref[...] = w_ref[...].astype(BF16)

    mode = modes_ref[j]
    tm, tn = x_ref.shape[0], wbf_ref.shape[1]
    pair = 2 * LANES

    def project():
        return jnp.dot(x_ref[...], wbf_ref[...], preferred_element_type=F32)

    def emit(p, piece):
        if d == 1:
            o_ref[:, p * pair:(p + 1) * pair] = piece.astype(o_ref.dtype)
        elif d == 4:
            acc[0][2 * p] = piece[:, :LANES]
            acc[0][2 * p + 1] = piece[:, LANES:]
        else:
            for g in range(tm // d):
                rows = slice(g * d, (g + 1) * d)
                dst = slice(g * ACC_PITCH, g * ACC_PITCH + d)
                acc[0][2 * p, dst, :] = piece[rows, :LANES]
                acc[0][2 * p + 1, dst, :] = piece[rows, LANES:]

    def plain():
        res = project()
        for p in range(tn // pair):
            emit(p, res[:, p * pair:(p + 1) * pair])

    def rotary(tab_ref, perm_ref, scale, half):
        res = project()
        tab = tab_ref[...]
        lane = lax.broadcasted_iota(jnp.int32, tab.shape, 1)
        c = jnp.where(lane < half, tab, pltpu.roll(tab, half, 1))
        s = jnp.where(lane < half, pltpu.roll(tab, LANES - half, 1), tab)
        c = jnp.concatenate([jnp.where(lane < 2 * half, c, 1.0)] * 2, axis=1) * scale
        s = jnp.concatenate([jnp.where(lane < 2 * half, s, 0.0)] * 2, axis=1) * scale
        for p in range(tn // pair):
            a = res[:, p * pair:(p + 1) * pair]
            partner = jnp.dot(a.astype(BF16), perm_ref[...], preferred_element_type=F32)
            emit(p, a * c + partner * s)

    is_att = (mode == MODE_ATT_Q) | (mode == MODE_ATT_K)
    is_ret = (mode == MODE_RET_Q) | (mode == MODE_RET_K)
    pl.when(mode == MODE_PLAIN)(plain)
    pl.when(is_att)(lambda: rotary(ta_ref, pa_ref, jnp.where(mode == MODE_ATT_Q, ATT_HEAD_DIM ** -0.5, 1.0),
                                   ROPE_DIM // 2))
    if d == 1:
        pl.when(is_ret)(lambda: rotary(tr_ref, pr_ref, jnp.where(mode == MODE_RET_K, RET_QK_DIM ** -0.5, 1.0),
                                       RET_QK_DIM // 2))
        return

    acc_ref = acc[0]
    for h in range(tn // LANES):
        sl = slice(h * LANES, (h + 1) * LANES)
        if d == 4:
            P = ATT_BLOCK * d
            for sb in range(tm // P):
                for r in range(d):
                    o_ref[sb, r, :, sl] = acc_ref[h, pl.ds(sb * P + r, ATT_BLOCK, stride=d), :].astype(o_ref.dtype)
        else:
            for r in range(d):
                o_ref[r, :, sl] = acc_ref[h, pl.ds(r, tm // d, stride=ACC_PITCH), :].astype(o_ref.dtype)


def _rotation_matrix(half, width):
    k = jnp.arange(2 * LANES, dtype=jnp.int32)[:, None]
    l = jnp.arange(2 * LANES, dtype=jnp.int32)[None, :]
    same_head = (k // LANES) == (l // LANES)
    kk, ll = k % LANES, l % LANES
    neg = same_head & (ll < half) & (kk == ll + half)
    pos = same_head & (ll >= half) & (ll < width) & (kk == ll - half)
    return (pos.astype(F32) - neg.astype(F32)).astype(BF16)


def _input_projection(xb, w_in, tabs_a, tabs_r, cols, modes, d):
    S, D = xb.shape
    tm, tn = PROJ_ROWS, COL_BLOCK
    n = len(cols)
    P = ATT_BLOCK * d
    if d == 1:
        out_shape = jax.ShapeDtypeStruct((S, n * tn), BF16)
        out_spec = pl.BlockSpec((tm, tn), lambda j, i, c, m: (i, j))
    elif d == 4:
        out_shape = jax.ShapeDtypeStruct((S // P, d, ATT_BLOCK, n * tn), BF16)
        out_spec = pl.BlockSpec((tm // P, d, ATT_BLOCK, tn), lambda j, i, c, m: (i, 0, 0, j))
    else:
        assert d == 16 and P == 2 * tm
        out_shape = jax.ShapeDtypeStruct((S // P, d, ATT_BLOCK, n * tn), BF16)
        out_spec = pl.BlockSpec((None, d, tm // d, tn), lambda j, i, c, m: (i // 2, 0, i % 2, j))

    def tab_rows(j, i, c, m):
        return (jnp.where(m[j] != MODE_PLAIN, i, 0), 0)

    grid_spec = pltpu.PrefetchScalarGridSpec(
        num_scalar_prefetch=2,
        grid=(n, S // tm),
        in_specs=[
            pl.BlockSpec((tm, D), lambda j, i, c, m: (i, 0)),
            pl.BlockSpec((D, tn), lambda j, i, c, m: (0, c[j])),
            pl.BlockSpec((tm, LANES), tab_rows),
            pl.BlockSpec((tm, LANES), tab_rows),
            pl.BlockSpec((2 * LANES, 2 * LANES), lambda j, i, c, m: (0, 0)),
            pl.BlockSpec((2 * LANES, 2 * LANES), lambda j, i, c, m: (0, 0)),
        ],
        out_specs=out_spec,
        scratch_shapes=[pltpu.VMEM((D, tn), BF16)]
        + {1: [], 4: [pltpu.VMEM((tn // LANES, tm, LANES), F32)],
           16: [pltpu.VMEM((tn // LANES, tm // 16 * ACC_PITCH, LANES), F32)]}[d],
    )
    return pl.pallas_call(
        functools.partial(_inproj_kernel, d=d),
        grid_spec=grid_spec,
        out_shape=out_shape,
        compiler_params=_cparams(2),
        name=f"in_proj_d{d}",
    )(jnp.asarray(cols, jnp.int32), jnp.asarray(modes, jnp.int32), xb, w_in, tabs_a, tabs_r,
      _rotation_matrix(ROPE_DIM // 2, ROPE_DIM), _rotation_matrix(RET_QK_DIM // 2, RET_QK_DIM))


ATT_Q_BLOCKS = 8


def _attn_kernel(q_ref, k_ref, kp_ref, v_ref, vp_ref, o_ref, lse_ref):
    n = pl.program_id(1)
    qi = lax.broadcasted_iota(jnp.int32, (ATT_BLOCK, 2 * ATT_BLOCK), 0)
    kj = lax.broadcasted_iota(jnp.int32, (ATT_BLOCK, 2 * ATT_BLOCK), 1)
    dist = qi + ATT_BLOCK - kj
    band = (dist >= 0) & (dist <= ATT_STEPS)
    lane = lax.broadcasted_iota(jnp.int32, (ATT_BLOCK, LANES), 1)
    for sub in range(q_ref.shape[0]):
        mask = band & ((kj >= ATT_BLOCK) | (n > 0)) if sub == 0 else band
        lse_slab = jnp.zeros((ATT_BLOCK, LANES), F32)
        for h in range(ATT_HEADS_PER_GROUP):
            sl = slice(h * ATT_HEAD_DIM, (h + 1) * ATT_HEAD_DIM)
            k_prev = kp_ref[:, sl] if sub == 0 else k_ref[sub - 1, :, sl]
            v_prev = vp_ref[:, sl] if sub == 0 else v_ref[sub - 1, :, sl]
            k2 = jnp.concatenate([k_prev, k_ref[sub, :, sl]], axis=0)
            v2 = jnp.concatenate([v_prev, v_ref[sub, :, sl]], axis=0)
            s = lax.dot_general(q_ref[sub, :, sl], k2, (((1,), (1,)), ((), ())),
                                preferred_element_type=F32)
            s = jnp.where(mask, s, NEG_INF)
            m = jnp.max(s, axis=1, keepdims=True)
            p = jnp.exp(s - m)
            l = jnp.sum(p, axis=1, keepdims=True)
            o = jnp.dot(p.astype(BF16), v2, preferred_element_type=F32)
            o_ref[sub, :, sl] = o / l
            lse_slab = jnp.where(lane == h, m + jnp.log(l), lse_slab)
        lse_ref[sub] = lse_slab


def _dilated_attention(qkv, g, cq, ck, cv):
    d = DILATIONS[g]
    W = ATT_GROUP_WIDTH
    nb = qkv.shape[0]
    nq = min(ATT_Q_BLOCKS, nb)
    assert qkv.shape[1] == d and nb % nq == 0
    blk = (nq, None, ATT_BLOCK, W)
    pblk = (None, None, ATT_BLOCK, W)
    cur = lambda c: (lambda r, n: (n, r, 0, c))
    prev = lambda c: (lambda r, n: (jnp.maximum(nq * n - 1, 0), r, 0, c))
    return pl.pallas_call(
        _attn_kernel,
        grid=(d, nb // nq),
        in_specs=[
            pl.BlockSpec(blk, cur(cq)),
            pl.BlockSpec(blk, cur(ck)),
            pl.BlockSpec(pblk, prev(ck)),
            pl.BlockSpec(blk, cur(cv)),
            pl.BlockSpec(pblk, prev(cv)),
        ],
        out_specs=[pl.BlockSpec(blk, lambda r, n: (n, r, 0, 0)),
                   pl.BlockSpec((nq, None, ATT_BLOCK, LANES), lambda r, n: (n, r, 0, 0))],
        out_shape=[jax.ShapeDtypeStruct((nb, d, ATT_BLOCK, W), F32),
                   jax.ShapeDtypeStruct((nb, d, ATT_BLOCK, LANES), F32)],
        compiler_params=_cparams(2),
        name=f"dilated_attn_g{g}",
    )(qkv, qkv, qkv, qkv, qkv)


RET_CHUNKS_PER_STEP = 4


def _retention_kernel(q_ref, k_ref, vlo_ref, vhi_ref, glo_ref, ghi_ref, o_ref,
                      state_ref, intra_ref, dec_ref):
    @pl.when(pl.program_id(0) == 0)
    def _():
        state_ref[...] = jnp.zeros_like(state_ref)
        ri = lax.broadcasted_iota(jnp.int32, (RET_CHUNK, RET_CHUNK), 0)
        ci = lax.broadcasted_iota(jnp.int32, (RET_CHUNK, RET_CHUNK), 1)
        diff = (ri - ci).astype(F32)
        nrow = ri.astype(F32)
        for h in range(RET_HEADS):
            ld = RET_LOG_DECAY[h]
            intra_ref[h] = jnp.where(diff >= 0, jnp.exp(jnp.maximum(diff, 0.0) * ld), 0.0)
            dec_ref[h, 0] = jnp.exp((nrow + 1.0) * ld)
            dec_ref[h, 1] = jnp.exp((RET_CHUNK - 1.0 - nrow) * ld)

    half = RET_HEADS // 2
    for h in range(RET_HEADS):
        ld = RET_LOG_DECAY[h]
        sl = slice(h * RET_QK_DIM, (h + 1) * RET_QK_DIM)
        vsl = slice((h % half) * RET_V_DIM, (h % half + 1) * RET_V_DIM)
        v_ref, g_ref = (vlo_ref, glo_ref) if h < half else (vhi_ref, ghi_ref)
        state = state_ref[h]
        for c in range(q_ref.shape[0] // RET_CHUNK):
            rs = slice(c * RET_CHUNK, (c + 1) * RET_CHUNK)
            q = q_ref[rs, sl]
            k = k_ref[rs, sl]
            v = v_ref[rs, vsl]
            scores = (lax.dot_general(q, k, (((1,), (1,)), ((), ())), preferred_element_type=F32)
                      * intra_ref[h])
            inner = jnp.dot(scores.astype(BF16), v, preferred_element_type=F32)
            cross = jnp.dot((q.astype(F32) * dec_ref[h, 0]).astype(BF16), state.astype(BF16),
                            preferred_element_type=F32)
            kv = lax.dot_general((k.astype(F32) * dec_ref[h, 1]).astype(BF16), v,
                                 (((0,), (0,)), ((), ())), preferred_element_type=F32)
            state = math.exp(RET_CHUNK * ld) * state + kv
            o = inner + cross
            mu = jnp.mean(o, axis=1, keepdims=True)
            var = jnp.mean(jnp.square(o - mu), axis=1, keepdims=True)
            o = (o - mu) * lax.rsqrt(var + LN_EPS)
            gate = g_ref[rs, vsl].astype(F32)
            gate = gate * (1.0 / (1.0 + jnp.exp(-gate)))
            o_ref[rs, h * RET_V_DIM:(h + 1) * RET_V_DIM] = (gate * o).astype(o_ref.dtype)
        state_ref[h] = state


def _retention(nat, S):
    rows = RET_CHUNK * RET_CHUNKS_PER_STEP
    blk = (rows, COL_BLOCK)

    def col(cb):
        return pl.BlockSpec(blk, lambda c: (c, cb))

    return pl.pallas_call(
        _retention_kernel,
        grid=(S // rows,),
        in_specs=[col(NC_QR), col(NC_KR), col(NC_VR), col(NC_VR + 1), col(NC_GR), col(NC_GR + 1)],
        out_specs=pl.BlockSpec((rows, RET_HEADS * RET_V_DIM), lambda c: (c, 0)),
        out_shape=jax.ShapeDtypeStruct((S, RET_HEADS * RET_V_DIM), BF16),
        scratch_shapes=[pltpu.VMEM((RET_HEADS, RET_QK_DIM, RET_V_DIM), F32),
                        pltpu.VMEM((RET_HEADS, RET_CHUNK, RET_CHUNK), F32),
                        pltpu.VMEM((RET_HEADS, 2, RET_CHUNK, RET_QK_DIM), F32)],
        compiler_params=_cparams(1),
        name="retention",
    )(nat, nat, nat, nat, nat, nat)


MERGE_ROWS = 512
COL_HALVES = (slice(0, D_MODEL // 2), slice(D_MODEL // 2, D_MODEL))


def _layer_norm(z, scale, bias):
    mu = jnp.mean(z, axis=1, keepdims=True)
    var = jnp.mean(jnp.square(z - mu), axis=1, keepdims=True)
    return (z - mu) * lax.rsqrt(var + LN_EPS) * scale + bias


def _sigmoid(x):
    return 1.0 / (1.0 + jnp.exp(-x))


def _pack_bf16_pair(a, b):
    abits = lax.bitcast_convert_type(a.astype(BF16).astype(F32), jnp.uint32)
    bbits = lax.bitcast_convert_type(b.astype(BF16).astype(F32), jnp.uint32)
    return (abits >> 16) | (bbits & jnp.uint32(0xFFFF0000))


def _unpack_bf16_pair(w):
    a = lax.bitcast_convert_type(w << 16, F32).astype(BF16)
    b = lax.bitcast_convert_type(w & jnp.uint32(0xFFFF0000), F32).astype(BF16)
    return a, b


def _split_bf16(v):
    hi = v.astype(BF16)
    return hi, (v - hi.astype(F32)).astype(BF16)


def _mix_kernel(o0_ref, o1_ref, o2_ref, l0_ref, l1_ref, l2_ref, or_ref,
                ga0_ref, ga1_ref, gb0_ref, gb1_ref, wa_ref, wr_ref, hexp_ref,
                m_ref, s1_ref, s2_ref, ls1_ref, ls2_ref):
    tm = o0_ref.shape[0]
    for d, src, lsrc, dst, ldst in ((DILATIONS[1], o1_ref, l1_ref, s1_ref, ls1_ref),
                                    (DILATIONS[2], o2_ref, l2_ref, s2_ref, ls2_ref)):
        for r in range(d):
            for h in range(ATT_HEADS_PER_GROUP):
                dst[h, pl.ds(r, tm // d, stride=d), :] = src[r, :, h * ATT_HEAD_DIM:(h + 1) * ATT_HEAD_DIM]
            ldst[pl.ds(r, tm // d, stride=d), :] = lsrc[r]

    l0, l1, l2 = l0_ref[...], ls1_ref[...], ls2_ref[...]
    lm = jnp.maximum(jnp.maximum(l0, l1), l2)
    e0, e1, e2 = jnp.exp(l0 - lm), jnp.exp(l1 - lm), jnp.exp(l2 - lm)
    den = e0 + e1 + e2

    def lanes_of_head(w):
        hi, lo = _split_bf16(w)
        return (jnp.dot(hi, hexp_ref[...], preferred_element_type=F32)
                + jnp.dot(lo, hexp_ref[...], preferred_element_type=F32))

    o_r = or_ref[...]
    y_r = [jnp.dot(o_r, wr_ref[:, cs], preferred_element_type=F32) for cs in COL_HALVES]
    heads = range(ATT_HEADS_PER_GROUP)
    o1 = jnp.concatenate([s1_ref[h] for h in heads], axis=1)
    o2 = jnp.concatenate([s2_ref[h] for h in heads], axis=1)
    o_a = (lanes_of_head(e0 / den) * o0_ref[...] + lanes_of_head(e1 / den) * o1
           + lanes_of_head(e2 / den) * o2).astype(BF16)
    for n, cs in enumerate(COL_HALVES):
        y_a = jnp.dot(o_a, wa_ref[:, cs], preferred_element_type=F32)
        gate_a = (ga0_ref, ga1_ref)[n][...].astype(F32)
        gate_b = (gb0_ref, gb1_ref)[n][...].astype(F32)
        m_ref[:, cs] = (_sigmoid(gate_a) * y_a + _sigmoid(gate_b) * y_r[n]).astype(m_ref.dtype)


def _finish_kernel(m_ref, x_ref, wo_ref, lns_ref, lnb_ref, wrt_ref, brt_ref,
                   x1_ref, x1p_ref, route_ref, cnt_ref):
    rc = x_ref.shape[0]
    half = D_MODEL // 2
    halves = COL_HALVES

    @pl.when(pl.program_id(0) == 0)
    def _():
        cnt_ref[...] = jnp.zeros_like(cnt_ref)

    z = [ALPHA * x_ref[:, cs]
         + jnp.dot(m_ref[:, :half], wo_ref[:half, cs], preferred_element_type=F32)
         + jnp.dot(m_ref[:, half:], wo_ref[half:, cs], preferred_element_type=F32) for cs in halves]
    mu = (jnp.sum(z[0], axis=1, keepdims=True) + jnp.sum(z[1], axis=1, keepdims=True)) / D_MODEL
    var = (jnp.sum(jnp.square(z[0] - mu), axis=1, keepdims=True)
           + jnp.sum(jnp.square(z[1] - mu), axis=1, keepdims=True)) / D_MODEL
    inv = lax.rsqrt(var + LN_EPS)
    x1 = [(z[n] - mu) * inv * lns_ref[:, cs] + lnb_ref[:, cs] for n, cs in enumerate(halves)]
    for n, cs in enumerate(halves):
        x1_ref[:, cs] = x1[n]
    packed = _pack_bf16_pair(x1[0], x1[1])
    for c in range(PACKED_ROW_PIECES):
        x1p_ref[pl.ds(c, rc, stride=PACKED_ROW_PIECES), :] = (
            packed[:, c * LANES:(c + 1) * LANES])

    logits = brt_ref[...]
    for n, cs in enumerate(halves):
        x_hi, x_lo = _split_bf16(x1[n])
        hl = jnp.dot(x_hi, wrt_ref[cs, :], preferred_element_type=F32)
        logits = (logits + hl[:, :LANES] + hl[:, LANES:]
                  + jnp.dot(x_lo, wrt_ref[cs, :LANES], preferred_element_type=F32))
    lane = lax.broadcasted_iota(jnp.int32, logits.shape, 1)
    is_g = lane < N_EXPERT_GROUPS
    gl = jnp.where(is_g, logits, NEG_INF)
    gmax = jnp.max(gl, axis=1, keepdims=True)
    g_sel = jnp.min(jnp.where(is_g & (gl == gmax), lane, LANES), axis=1, keepdims=True)
    p_group = 1.0 / jnp.sum(jnp.where(is_g, jnp.exp(gl - gmax), 0.0), axis=1, keepdims=True)
    lo = N_EXPERT_GROUPS + EXPERTS_PER_GROUP * g_sel
    in_grp = (lane >= lo) & (lane < lo + EXPERTS_PER_GROUP)
    el = jnp.where(in_grp, logits, NEG_INF)
    top1 = jnp.max(el, axis=1, keepdims=True)
    idx1 = jnp.min(jnp.where(in_grp & (el == top1), lane, LANES), axis=1, keepdims=True)
    el2 = jnp.where(lane == idx1, NEG_INF, el)
    top2 = jnp.max(el2, axis=1, keepdims=True)
    idx2 = jnp.min(jnp.where(in_grp & (lane != idx1) & (el2 == top2), lane, LANES), axis=1, keepdims=True)
    t = jnp.exp(top2 - top1)
    wt1 = p_group * (1.0 / (1.0 + t))
    wt2 = p_group * (t / (1.0 + t))
    slab = jnp.where(lane == 0, (idx1 - N_EXPERT_GROUPS).astype(F32), 0.0)
    slab = jnp.where(lane == 1, (idx2 - N_EXPERT_GROUPS).astype(F32), slab)
    slab = jnp.where(lane == 2, wt1, slab)
    slab = jnp.where(lane == 3, wt2, slab)

    hot = jnp.where((lane == idx1) | (lane == idx2), 1.0, 0.0)
    ri = lax.broadcasted_iota(jnp.int32, (rc, rc), 0)
    ci = lax.broadcasted_iota(jnp.int32, (rc, rc), 1)
    ltri = jnp.where(ri > ci, 1.0, 0.0).astype(BF16)
    prefix = jnp.dot(ltri, hot.astype(BF16), preferred_element_type=F32) + cnt_ref[...]
    rank1 = jnp.sum(jnp.where(lane == idx1, prefix, 0.0), axis=1, keepdims=True)
    rank2 = jnp.sum(jnp.where(lane == idx2, prefix, 0.0), axis=1, keepdims=True)
    cnt_ref[...] = cnt_ref[...] + jnp.sum(hot, axis=0, keepdims=True)
    slab = jnp.where(lane == 4, rank1, slab)
    slab = jnp.where(lane == 5, rank2, slab)
    route_ref[...] = slab


def _row_spec(tm, width):
    return pl.BlockSpec((tm, width), lambda i: (i, 0))


def _const_spec(shape):
    return pl.BlockSpec(shape, lambda i: (0, 0), pipeline_mode=pl.Buffered(1))


def _mix(o_g, lse_g, o_r, nat, wa, wr, S):
    tm = MERGE_ROWS
    D = D_MODEL
    W = ATT_GROUP_WIDTH
    head_expand = (jnp.arange(W, dtype=jnp.int32)[None, :] // ATT_HEAD_DIM
                   == jnp.arange(LANES, dtype=jnp.int32)[:, None]).astype(BF16)

    def streams(g, width):
        d = DILATIONS[g]
        per = (ATT_BLOCK * d) // tm
        return pl.BlockSpec((None, d, tm // d, width), lambda i: (i // per, 0, i % per, 0))

    def ncol(cb):
        return pl.BlockSpec((tm, COL_BLOCK), lambda i: (i, cb))

    return pl.pallas_call(
        _mix_kernel,
        grid=(S // tm,),
        in_specs=[
            _row_spec(tm, W), streams(1, W), streams(2, W),
            _row_spec(tm, LANES), streams(1, LANES), streams(2, LANES),
            _row_spec(tm, RET_HEADS * RET_V_DIM),
            ncol(NC_GATE_A), ncol(NC_GATE_A + 1), ncol(NC_GATE_B), ncol(NC_GATE_B + 1),
            _const_spec((W, D)), _const_spec((RET_HEADS * RET_V_DIM, D)), _const_spec((LANES, W)),
        ],
        out_specs=_row_spec(tm, D),
        out_shape=jax.ShapeDtypeStruct((S, D), BF16),
        scratch_shapes=[pltpu.VMEM((ATT_HEADS_PER_GROUP, tm, ATT_HEAD_DIM), F32),
                        pltpu.VMEM((ATT_HEADS_PER_GROUP, tm, ATT_HEAD_DIM), F32),
                        pltpu.VMEM((tm, LANES), F32), pltpu.VMEM((tm, LANES), F32)],
        compiler_params=_cparams(1),
        name="mix_branches",
    )(*o_g, *lse_g, o_r, nat, nat, nat, nat, wa, wr, head_expand)


def _finish(merged, x2d, wo, lns, lnb, wrt, brt):
    S, D = x2d.shape
    tm = MERGE_ROWS
    return pl.pallas_call(
        _finish_kernel,
        grid=(S // tm,),
        in_specs=[
            _row_spec(tm, D), _row_spec(tm, D),
            _const_spec((D, D)), _const_spec((1, D)), _const_spec((1, D)),
            _const_spec((D, 2 * LANES)), _const_spec((1, LANES)),
        ],
        out_specs=[_row_spec(tm, D), pl.BlockSpec((tm * PACKED_ROW_PIECES, LANES), lambda i: (i, 0)),
                   _row_spec(tm, LANES), pl.BlockSpec((1, LANES), lambda i: (0, 0))],
        out_shape=[jax.ShapeDtypeStruct((S, D), F32),
                   jax.ShapeDtypeStruct((S * PACKED_ROW_PIECES, LANES), jnp.uint32),
                   jax.ShapeDtypeStruct((S, LANES), F32),
                   jax.ShapeDtypeStruct((1, LANES), F32)],
        compiler_params=_cparams(1),
        name="out_proj_ln_router",
    )(merged, x2d, wo, lns, lnb, wrt, brt)


META_N_USED, META_NEXT = 0, 1
META_SIZE = META_NEXT + N_EXPERTS
WEIGHT_DMA_PRIORITY = 1
GATHER_SLOTS = 4
GATHER_GROUP = 2
WEIGHT_SLOTS = 2


def _plan_kernel(dest_ref, cnt_ref, rt_ref, be_ref, meta_ref):
    n_blocks = be_ref.shape[0]

    def pad_row(r, carry):
        rt_ref[r] = 0
        return carry

    def per_expert(e, start):
        nb = lax.shift_right_logical(cnt_ref[e] + (MOE_BLOCK - 1), MOE_BLOCK_SHIFT)

        def fill(b, carry):
            be_ref[lax.shift_right_logical(start, MOE_BLOCK_SHIFT) + b] = e
            return carry
        lax.fori_loop(0, nb, fill, 0)
        lax.fori_loop(start + cnt_ref[e], start + nb * MOE_BLOCK, pad_row, 0)
        return start + nb * MOE_BLOCK
    total = lax.fori_loop(0, N_EXPERTS, per_expert, 0)
    n_used = lax.shift_right_logical(total, MOE_BLOCK_SHIFT)
    meta_ref[META_N_USED] = n_used

    def tail(b, carry):
        be_ref[b] = be_ref[jnp.maximum(n_used - 1, 0)]
        return carry
    lax.fori_loop(n_used, n_blocks, tail, 0)
    lax.fori_loop(total, rt_ref.shape[0], pad_row, 0)

    def nxt(k, cur):
        e = N_EXPERTS - 1 - k
        meta_ref[META_NEXT + e] = cur
        return jnp.where(cnt_ref[e] > 0, e, cur)
    lax.fori_loop(0, N_EXPERTS, nxt, N_EXPERTS)

    def assign(a, carry):
        rt_ref[dest_ref[a]] = lax.shift_right_logical(a, 1)
        return carry
    lax.fori_loop(0, dest_ref.shape[0], assign, 0, unroll=8)


def _plan(dest, counts):
    A = dest.shape[0]
    n_blocks = A // MOE_BLOCK + N_EXPERTS
    smem = pl.BlockSpec(memory_space=pltpu.SMEM)
    return pl.pallas_call(
        _plan_kernel,
        in_specs=[smem, smem],
        out_specs=[smem, smem, smem],
        out_shape=[
            jax.ShapeDtypeStruct((n_blocks * MOE_BLOCK,), jnp.int32),
            jax.ShapeDtypeStruct((n_blocks,), jnp.int32),
            jax.ShapeDtypeStruct((META_SIZE,), jnp.int32),
        ],
        name="moe_plan",
    )(dest, counts)


def _expert_kernel(be_ref, meta_ref, rt_ref, x_hbm, wg_hbm, wu_hbm, wd_hbm, y_ref,
                   xg_ref, wgf_ref, wuf_ref, wdf_ref, wgb_ref, wub_ref, wdb_ref, xsem, sem, slot_ref):
    b = pl.program_id(0)
    n_used = meta_ref[META_N_USED]
    e = be_ref[b]
    first = (b == 0) | (e != be_ref[jnp.maximum(b - 1, 0)])

    xp = PACKED_ROW_PIECES

    last = jnp.maximum(n_used - 1, 0)
    group_rows = GATHER_GROUP * MOE_BLOCK
    group = b // GATHER_GROUP
    last_group = last // GATHER_GROUP

    def row_copy(grp, slot, i):
        tok = rt_ref[grp * group_rows + i]
        return pltpu.make_async_copy(x_hbm.at[pl.ds(pl.multiple_of(tok * xp, xp), xp)],
                                     xg_ref.at[slot, pl.ds(pl.multiple_of(i * xp, xp), xp)],
                                     xsem.at[slot])

    def wait_rows(slot):
        pltpu.make_async_copy(x_hbm.at[pl.ds(0, group_rows * xp)], xg_ref.at[slot], xsem.at[slot]).wait()

    @pl.when(b == 0)
    def _():
        for ahead in range(GATHER_SLOTS - 1):
            def body(i, carry, ahead=ahead):
                row_copy(jnp.minimum(ahead, last_group), ahead, i).start()
                return carry
            lax.fori_loop(0, group_rows, body, 0, unroll=8)

    def copies(ex, s):
        return (pltpu.make_async_copy(wg_hbm.at[ex], wgf_ref.at[s], sem.at[s]),
                pltpu.make_async_copy(wu_hbm.at[ex], wuf_ref.at[s], sem.at[s]),
                pltpu.make_async_copy(wd_hbm.at[ex], wdf_ref.at[s], sem.at[s]))

    def next_expert(ex):
        return jnp.where(ex < N_EXPERTS, meta_ref[META_NEXT + jnp.minimum(ex, N_EXPERTS - 1)], N_EXPERTS)

    def fetch(ex, s):
        @pl.when(ex < N_EXPERTS)
        def _():
            for c in copies(ex, s):
                c.start(priority=WEIGHT_DMA_PRIORITY)

    @pl.when((b == 0) & (n_used > 0))
    def _():
        slot_ref[0] = 0
        ex = e
        for s in range(WEIGHT_SLOTS - 1):
            fetch(ex, s)
            ex = next_expert(ex)

    @pl.when(first & (b < n_used))
    def _():
        s = slot_ref[0]
        ex = e
        for _ in range(WEIGHT_SLOTS - 1):
            ex = next_expert(ex)
        fetch(ex, lax.rem(s + WEIGHT_SLOTS - 1, WEIGHT_SLOTS))
        for c in copies(e, s):
            c.wait()
        wgb_ref[...] = wgf_ref[s].astype(BF16)
        wub_ref[...] = wuf_ref[s].astype(BF16)
        wdb_ref[...] = wdf_ref[s].astype(BF16)
        slot_ref[0] = lax.rem(s + 1, WEIGHT_SLOTS)

    def step(slot, sub):
        if sub == 0:
            wait_rows(slot)
            nxt = jnp.minimum(group + (GATHER_SLOTS - 1), last_group)
            for i in range(group_rows):
                row_copy(nxt, (slot + GATHER_SLOTS - 1) % GATHER_SLOTS, i).start()
        base = sub * MOE_BLOCK * xp
        pieces = [_unpack_bf16_pair(xg_ref[slot, pl.ds(base + c, MOE_BLOCK, stride=xp), :])
                  for c in range(xp)]
        xb = jnp.concatenate([p[0] for p in pieces] + [p[1] for p in pieces], axis=1)
        hg = jnp.dot(xb, wgb_ref[...], preferred_element_type=F32)
        hu = jnp.dot(xb, wub_ref[...], preferred_element_type=F32)
        hid = (hg * _sigmoid(hg) * hu).astype(BF16)
        y = jnp.dot(hid, wdb_ref[...], preferred_element_type=F32)
        for c in range(ROW_PIECES):
            y_ref[pl.ds(c, MOE_BLOCK, stride=ROW_PIECES), :] = y[:, c * LANES:(c + 1) * LANES]

    for s in range(GATHER_SLOTS):
        for sub in range(GATHER_GROUP):
            pl.when((b < n_used) & (lax.rem(group, GATHER_SLOTS) == s)
                    & (lax.rem(b, GATHER_GROUP) == sub))(functools.partial(step, s, sub))

    @pl.when(b == last)
    def _():
        for ahead in range(1, GATHER_SLOTS):
            wait_rows(lax.rem(group + ahead, GATHER_SLOTS))

    @pl.when(b >= n_used)
    def _():
        y_ref[...] = jnp.zeros_like(y_ref)


def _experts(x1p, block_expert, meta, row_tok, w_gate, w_up, w_down):
    D = D_MODEL
    assert x1p.shape[1] == LANES and x1p.dtype == jnp.uint32
    n_blocks = block_expert.shape[0]
    R = n_blocks * MOE_BLOCK
    H = EXPERT_HIDDEN
    grid_spec = pltpu.PrefetchScalarGridSpec(
        num_scalar_prefetch=3,
        grid=(n_blocks,),
        in_specs=[
            pl.BlockSpec(memory_space=pl.ANY),
            pl.BlockSpec(memory_space=pl.ANY),
            pl.BlockSpec(memory_space=pl.ANY),
            pl.BlockSpec(memory_space=pl.ANY),
        ],
        out_specs=pl.BlockSpec((MOE_BLOCK * ROW_PIECES, LANES), lambda b, be, meta, rt: (b, 0)),
        scratch_shapes=[
            pltpu.VMEM((GATHER_SLOTS, GATHER_GROUP * MOE_BLOCK * PACKED_ROW_PIECES, LANES), jnp.uint32),
            pltpu.VMEM((WEIGHT_SLOTS, D, H), F32),
            pltpu.VMEM((WEIGHT_SLOTS, D, H), F32),
            pltpu.VMEM((WEIGHT_SLOTS, H, D), F32),
            pltpu.VMEM((D, H), BF16),
            pltpu.VMEM((D, H), BF16),
            pltpu.VMEM((H, D), BF16),
            pltpu.SemaphoreType.DMA((GATHER_SLOTS,)),
            pltpu.SemaphoreType.DMA((WEIGHT_SLOTS,)),
            pltpu.SMEM((1,), jnp.int32),
        ],
    )
    return pl.pallas_call(
        _expert_kernel,
        grid_spec=grid_spec,
        out_shape=jax.ShapeDtypeStruct((R * ROW_PIECES, LANES), F32),
        compiler_params=_cparams(1),
        name="moe_experts",
    )(block_expert, meta, row_tok, x1p, w_gate, w_up, w_down)


COMBINE_TOKENS = 128


def _combine_kernel(dest_ref, y_hbm, x1_ref, route_ref, lns_ref, lnb_ref, o_ref, yg_ref, sem):
    i = pl.program_id(0)
    tt = COMBINE_TOKENS
    n = dest_ref.shape[0] // (TOP_K * tt)

    rp = ROW_PIECES

    def start_token(step, slot, r):
        for k in range(TOP_K):
            row = dest_ref[(step * tt + r) * TOP_K + k]
            pltpu.make_async_copy(y_hbm.at[pl.ds(pl.multiple_of(row * rp, rp), rp)],
                                  yg_ref.at[slot, k, pl.ds(pl.multiple_of(r * rp, rp), rp)],
                                  sem.at[slot]).start(priority=k)

    def wait_slot(slot):
        for k in range(TOP_K):
            pltpu.make_async_copy(y_hbm.at[pl.ds(0, tt * rp)], yg_ref.at[slot, k], sem.at[slot]).wait()

    @pl.when(i == 0)
    def _():
        for ahead in range(GATHER_SLOTS - 1):
            def body(r, carry, ahead=ahead):
                start_token(min(ahead, n - 1), ahead, r)
                return carry
            lax.fori_loop(0, tt, body, 0, unroll=4)

    def step(slot):
        wait_slot(slot)
        nxt = jnp.minimum(i + (GATHER_SLOTS - 1), n - 1)
        for r in range(tt):
            start_token(nxt, (slot + GATHER_SLOTS - 1) % GATHER_SLOTS, r)
        route = route_ref[...]
        w1, w2 = route[:, 2:3], route[:, 3:4]
        moe = jnp.concatenate(
            [w1 * yg_ref[slot, 0, pl.ds(c, tt, stride=rp), :] + w2 * yg_ref[slot, 1, pl.ds(c, tt, stride=rp), :]
             for c in range(rp)], axis=1)
        o_ref[...] = _layer_norm(ALPHA * x1_ref[...] + moe, lns_ref[...], lnb_ref[...])

    for s in range(GATHER_SLOTS):
        pl.when(lax.rem(i, GATHER_SLOTS) == s)(functools.partial(step, s))

    @pl.when(i == n - 1)
    def _():
        for ahead in range(1, GATHER_SLOTS):
            wait_slot((n - 1 + ahead) % GATHER_SLOTS)


def _combine(dest, y, x1, route, lns, lnb):
    T, D = x1.shape
    tt = COMBINE_TOKENS
    grid_spec = pltpu.PrefetchScalarGridSpec(
        num_scalar_prefetch=1,
        grid=(T // tt,),
        in_specs=[
            pl.BlockSpec(memory_space=pl.ANY),
            pl.BlockSpec((tt, D), lambda i, d: (i, 0)),
            pl.BlockSpec((tt, LANES), lambda i, d: (i, 0)),
            pl.BlockSpec((1, D), lambda i, d: (0, 0)),
            pl.BlockSpec((1, D), lambda i, d: (0, 0)),
        ],
        out_specs=pl.BlockSpec((tt, D), lambda i, d: (i, 0)),
        scratch_shapes=[
            pltpu.VMEM((GATHER_SLOTS, TOP_K, tt * ROW_PIECES, LANES), F32),
            pltpu.SemaphoreType.DMA((GATHER_SLOTS,)),
        ],
    )
    return pl.pallas_call(
        _combine_kernel,
        grid_spec=grid_spec,
        out_shape=jax.ShapeDtypeStruct((T, D), F32),
        compiler_params=_cparams(1),
        name="moe_combine_ln",
    )(dest, y, x1, route, lns, lnb)


def _rotary_tables(positions):
    pos = positions.reshape(-1).astype(F32)
    S = pos.shape[0]
    inv_a = ROPE_THETA ** (-jnp.arange(0, ROPE_DIM, 2, dtype=F32) / ROPE_DIM)
    inv_r = RET_ROT_BASE ** (-jnp.linspace(0.0, 1.0, RET_QK_DIM // 2, dtype=F32))
    ang = jnp.concatenate([inv_a, inv_r])[:, None] * pos[None, :]
    cos_t, sin_t = lax.optimization_barrier((jnp.cos(ang), jnp.sin(ang)))
    na = ROPE_DIM // 2
    cos, sin, cr, sr = cos_t[:na].T, sin_t[:na].T, cos_t[na:].T, sin_t[na:].T
    tabs_a = jnp.concatenate([cos, sin, jnp.zeros((S, LANES - ROPE_DIM), F32)], axis=1)
    tabs_r = jnp.concatenate([cr, sr], axis=1)
    return tabs_a, tabs_r


def kernel(x, positions, w_in, w_branch_attn, w_branch_ret, w_out, ln1_scale, ln1_bias, w_group_router, b_group_router, w_expert_router, b_expert_router, w_expert_gate, w_expert_up, w_expert_down, ln2_scale, ln2_bias):
    B, S, D = x.shape
    assert B == 1 and D == D_MODEL and w_in.shape[0] == DEPTH == 1
    assert S % (ATT_BLOCK * DILATIONS[2]) == 0
    x2d = x.reshape(S, D)
    xb = x2d.astype(BF16)
    tabs_a, tabs_r = _rotary_tables(positions)

    nat = _input_projection(xb, w_in[0], tabs_a, tabs_r, NAT_COLS, NAT_MODES, 1)
    qkv_modes = (MODE_ATT_Q, MODE_ATT_K, MODE_PLAIN)
    qkv = [None] + [
        _input_projection(xb, w_in[0], tabs_a, tabs_r, (CB_QA + g, CB_KA + g, CB_VA + g), qkv_modes,
                          DILATIONS[g])
        for g in (1, 2)]
    o0, l0 = _dilated_attention(nat.reshape(S // ATT_BLOCK, 1, ATT_BLOCK, nat.shape[1]), 0,
                                NC_Q0, NC_K0, NC_V0)
    att = [(o0.reshape(S, ATT_GROUP_WIDTH), l0.reshape(S, LANES))]
    att += [_dilated_attention(qkv[g], g, 0, 1, 2) for g in (1, 2)]
    o_r = _retention(nat, S)

    w_route = jnp.concatenate(
        [w_group_router[0], w_expert_router[0].transpose(1, 0, 2).reshape(D, N_EXPERTS),
         jnp.zeros((D, LANES - N_EXPERT_GROUPS - N_EXPERTS), F32)], axis=1)
    b_route = jnp.concatenate(
        [b_group_router[0], b_expert_router[0].reshape(N_EXPERTS),
         jnp.zeros((LANES - N_EXPERT_GROUPS - N_EXPERTS,), F32)]).reshape(1, LANES)
    merged = _mix([a[0] for a in att], [a[1] for a in att], o_r, nat,
                  w_branch_attn[0].astype(BF16), w_branch_ret[0].astype(BF16), S)
    x1, x1p, route, cnt = _finish(
        merged, x2d, w_out[0].astype(BF16), ln1_scale[0].reshape(1, D), ln1_bias[0].reshape(1, D),
        jnp.concatenate(_split_bf16(w_route), axis=1), b_route)

    e_flat = route[:, 0:TOP_K].astype(jnp.int32).reshape(S * TOP_K)
    rank_flat = route[:, 4:4 + TOP_K].astype(jnp.int32).reshape(S * TOP_K)
    counts = cnt[0, N_EXPERT_GROUPS:N_EXPERT_GROUPS + N_EXPERTS].astype(jnp.int32)
    padded = (counts + (MOE_BLOCK - 1)) // MOE_BLOCK * MOE_BLOCK
    pstart = jnp.cumsum(padded) - padded
    hot = e_flat[:, None] == jnp.arange(N_EXPERTS, dtype=jnp.int32)[None, :]
    dest = jnp.sum(jnp.where(hot, pstart[None, :], 0), axis=1) + rank_flat
    row_tok, block_expert, meta = _plan(dest, counts)
    y = _experts(x1p, block_expert, meta, row_tok, w_expert_gate[0], w_expert_up[0], w_expert_down[0])
    out = _combine(dest, y, x1, route, ln2_scale[0].reshape(1, D), ln2_bias[0].reshape(1, D))
    return out.reshape(B, S, D)
```

```python
import functools
import math

import jax
import jax.numpy as jnp
from jax import lax
from jax.experimental import pallas as pl
from jax.experimental.pallas import tpu as pltpu

F32 = jnp.float32
BF16 = jnp.bfloat16

D_MODEL = 2048
ATT_HEAD_DIM = 128
ATT_HEADS_PER_GROUP = 8
DILATIONS = (1, 4, 16)
ATT_STEPS = 128
ATT_GROUP_WIDTH = ATT_HEADS_PER_GROUP * ATT_HEAD_DIM
ROPE_DIM = ATT_HEAD_DIM // 4
ROPE_THETA = 500000.0
ATT_BLOCK = 128

RET_HEADS = 8
RET_QK_DIM = 128
RET_V_DIM = 256
RET_CHUNK = 128
RET_ROT_BASE = 10000.0
RET_LOG_DECAY = tuple(math.log1p(-(2.0 ** (-5.0 - h))) for h in range(RET_HEADS))

N_EXPERT_GROUPS = 4
EXPERTS_PER_GROUP = 8
N_EXPERTS = N_EXPERT_GROUPS * EXPERTS_PER_GROUP
TOP_K = 2
EXPERT_HIDDEN = 512
MOE_BLOCK = 128
MOE_BLOCK_SHIFT = MOE_BLOCK.bit_length() - 1
assert MOE_BLOCK == 1 << MOE_BLOCK_SHIFT

DEPTH = 1
ALPHA = (2.0 * DEPTH) ** 0.25
LN_EPS = 1e-5
NEG_INF = -1e30

COL_BLOCK = 1024
CB_QA, CB_KA, CB_VA = 0, 3, 6
CB_QR, CB_KR, CB_VR, CB_GR = 9, 10, 11, 13
CB_GATE_A, CB_GATE_B = 15, 17

MODE_PLAIN, MODE_ATT_Q, MODE_ATT_K, MODE_RET_Q, MODE_RET_K = 0, 1, 2, 3, 4

NAT_COLS = (CB_QA, CB_KA, CB_VA, CB_QR, CB_KR, CB_VR, CB_VR + 1, CB_GR, CB_GR + 1,
            CB_GATE_A, CB_GATE_A + 1, CB_GATE_B, CB_GATE_B + 1)
NAT_MODES = (MODE_ATT_Q, MODE_ATT_K, MODE_PLAIN, MODE_RET_Q, MODE_RET_K) + (MODE_PLAIN,) * 8
NC_Q0, NC_K0, NC_V0, NC_QR, NC_KR, NC_VR, NC_GR, NC_GATE_A, NC_GATE_B = 0, 1, 2, 3, 4, 5, 7, 9, 11

LANES = 128
ROW_PIECES = D_MODEL // LANES
PACKED_ROW_PIECES = ROW_PIECES // 2
VMEM_LIMIT = 56 * 1024 * 1024
PROJ_ROWS = 1024
ROW_PITCH = 24
ACC_PITCH = 24


def _cparams(n_grid_dims):
    return pltpu.CompilerParams(
        dimension_semantics=("arbitrary",) * n_grid_dims,
        vmem_limit_bytes=VMEM_LIMIT,
    )


def _inproj_kernel(cols_ref, modes_ref, x_ref, w_ref, ta_ref, tr_ref, pa_ref, pr_ref, o_ref,
                   wbf_ref, *acc, d):
    del cols_ref
    j = pl.program_id(0)

    @pl.when(pl.program_id(1) == 0)
    def _():
        wbf_ref[...] = w_ref[...].astype(BF16)

    mode = modes_ref[j]
    tm, tn = x_ref.shape[0], wbf_ref.shape[1]
    pair = 2 * LANES

    def project():
        return jnp.dot(x_ref[...], wbf_ref[...], preferred_element_type=F32)

    def emit(p, piece):
        if d == 1:
            o_ref[:, p * pair:(p + 1) * pair] = piece.astype(o_ref.dtype)
        elif d == 4:
            acc[0][2 * p] = piece[:, :LANES]
            acc[0][2 * p + 1] = piece[:, LANES:]
        else:
            for g in range(tm // d):
                rows = slice(g * d, (g + 1) * d)
                dst = slice(g * ACC_PITCH, g * ACC_PITCH + d)
                acc[0][2 * p, dst, :] = piece[rows, :LANES]
                acc[0][2 * p + 1, dst, :] = piece[rows, LANES:]

    def plain():
        res = project()
        for p in range(tn // pair):
            emit(p, res[:, p * pair:(p + 1) * pair])

    def rotary(tab_ref, perm_ref, scale, half):
        res = project()
        tab = tab_ref[...]
        lane = lax.broadcasted_iota(jnp.int32, tab.shape, 1)
        c = jnp.where(lane < half, tab, pltpu.roll(tab, half, 1))
        s = jnp.where(lane < half, pltpu.roll(tab, LANES - half, 1), tab)
        c = jnp.concatenate([jnp.where(lane < 2 * half, c, 1.0)] * 2, axis=1) * scale
        s = jnp.concatenate([jnp.where(lane < 2 * half, s, 0.0)] * 2, axis=1) * scale
        for p in range(tn // pair):
            a = res[:, p * pair:(p + 1) * pair]
            partner = jnp.dot(a.astype(BF16), perm_ref[...], preferred_element_type=F32)
            emit(p, a * c + partner * s)

    is_att = (mode == MODE_ATT_Q) | (mode == MODE_ATT_K)
    is_ret = (mode == MODE_RET_Q) | (mode == MODE_RET_K)
    pl.when(mode == MODE_PLAIN)(plain)
    pl.when(is_att)(lambda: rotary(ta_ref, pa_ref, jnp.where(mode == MODE_ATT_Q, ATT_HEAD_DIM ** -0.5, 1.0),
                                   ROPE_DIM // 2))
    if d == 1:
        pl.when(is_ret)(lambda: rotary(tr_ref, pr_ref, jnp.where(mode == MODE_RET_K, RET_QK_DIM ** -0.5, 1.0),
                                       RET_QK_DIM // 2))
        return

    acc_ref = acc[0]
    for h in range(tn // LANES):
        sl = slice(h * LANES, (h + 1) * LANES)
        if d == 4:
            P = ATT_BLOCK * d
            for sb in range(tm // P):
                for r in range(d):
                    o_ref[sb, r, :, sl] = acc_ref[h, pl.ds(sb * P + r, ATT_BLOCK, stride=d), :].astype(o_ref.dtype)
        else:
            for r in range(d):
                o_ref[r, :, sl] = acc_ref[h, pl.ds(r, tm // d, stride=ACC_PITCH), :].astype(o_ref.dtype)


def _rotation_matrix(half, width):
    k = jnp.arange(2 * LANES, dtype=jnp.int32)[:, None]
    l = jnp.arange(2 * LANES, dtype=jnp.int32)[None, :]
    same_head = (k // LANES) == (l // LANES)
    kk, ll = k % LANES, l % LANES
    neg = same_head & (ll < half) & (kk == ll + half)
    pos = same_head & (ll >= half) & (ll < width) & (kk == ll - half)
    return (pos.astype(F32) - neg.astype(F32)).astype(BF16)


def _input_projection(xb, w_in, tabs_a, tabs_r, cols, modes, d):
    S, D = xb.shape
    tm, tn = PROJ_ROWS, COL_BLOCK
    n = len(cols)
    P = ATT_BLOCK * d
    if d == 1:
        out_shape = jax.ShapeDtypeStruct((S, n * tn), BF16)
        out_spec = pl.BlockSpec((tm, tn), lambda j, i, c, m: (i, j))
    elif d == 4:
        out_shape = jax.ShapeDtypeStruct((S // P, d, ATT_BLOCK, n * tn), BF16)
        out_spec = pl.BlockSpec((tm // P, d, ATT_BLOCK, tn), lambda j, i, c, m: (i, 0, 0, j))
    else:
        assert d == 16 and P == 2 * tm
        out_shape = jax.ShapeDtypeStruct((S // P, d, ATT_BLOCK, n * tn), BF16)
        out_spec = pl.BlockSpec((None, d, tm // d, tn), lambda j, i, c, m: (i // 2, 0, i % 2, j))

    def tab_rows(j, i, c, m):
        return (jnp.where(m[j] != MODE_PLAIN, i, 0), 0)

    grid_spec = pltpu.PrefetchScalarGridSpec(
        num_scalar_prefetch=2,
        grid=(n, S // tm),
        in_specs=[
            pl.BlockSpec((tm, D), lambda j, i, c, m: (i, 0)),
            pl.BlockSpec((D, tn), lambda j, i, c, m: (0, c[j])),
            pl.BlockSpec((tm, LANES), tab_rows),
            pl.BlockSpec((tm, LANES), tab_rows),
            pl.BlockSpec((2 * LANES, 2 * LANES), lambda j, i, c, m: (0, 0)),
            pl.BlockSpec((2 * LANES, 2 * LANES), lambda j, i, c, m: (0, 0)),
        ],
        out_specs=out_spec,
        scratch_shapes=[pltpu.VMEM((D, tn), BF16)]
        + {1: [], 4: [pltpu.VMEM((tn // LANES, tm, LANES), F32)],
           16: [pltpu.VMEM((tn // LANES, tm // 16 * ACC_PITCH, LANES), F32)]}[d],
    )
    return pl.pallas_call(
        functools.partial(_inproj_kernel, d=d),
        grid_spec=grid_spec,
        out_shape=out_shape,
        compiler_params=_cparams(2),
        name=f"in_proj_d{d}",
    )(jnp.asarray(cols, jnp.int32), jnp.asarray(modes, jnp.int32), xb, w_in, tabs_a, tabs_r,
      _rotation_matrix(ROPE_DIM // 2, ROPE_DIM), _rotation_matrix(RET_QK_DIM // 2, RET_QK_DIM))


ATT_Q_BLOCKS = 8


def _attn_kernel(q_ref, k_ref, kp_ref, v_ref, vp_ref, o_ref, lse_ref):
    n = pl.program_id(1)
    qi = lax.broadcasted_iota(jnp.int32, (ATT_BLOCK, 2 * ATT_BLOCK), 0)
    kj = lax.broadcasted_iota(jnp.int32, (ATT_BLOCK, 2 * ATT_BLOCK), 1)
    dist = qi + ATT_BLOCK - kj
    band = (dist >= 0) & (dist <= ATT_STEPS)
    lane = lax.broadcasted_iota(jnp.int32, (ATT_BLOCK, LANES), 1)
    for sub in range(q_ref.shape[0]):
        mask = band & ((kj >= ATT_BLOCK) | (n > 0)) if sub == 0 else band
        lse_slab = jnp.zeros((ATT_BLOCK, LANES), F32)
        for h in range(ATT_HEADS_PER_GROUP):
            sl = slice(h * ATT_HEAD_DIM, (h + 1) * ATT_HEAD_DIM)
            k_prev = kp_ref[:, sl] if sub == 0 else k_ref[sub - 1, :, sl]
            v_prev = vp_ref[:, sl] if sub == 0 else v_ref[sub - 1, :, sl]
            k2 = jnp.concatenate([k_prev, k_ref[sub, :, sl]], axis=0)
            v2 = jnp.concatenate([v_prev, v_ref[sub, :, sl]], axis=0)
            s = lax.dot_general(q_ref[sub, :, sl], k2, (((1,), (1,)), ((), ())),
                                preferred_element_type=F32)
            s = jnp.where(mask, s, NEG_INF)
            m = jnp.max(s, axis=1, keepdims=True)
            p = jnp.exp(s - m)
            l = jnp.sum(p, axis=1, keepdims=True)
            o = jnp.dot(p.astype(BF16), v2, preferred_element_type=F32)
            o_ref[sub, :, sl] = o / l
            lse_slab = jnp.where(lane == h, m + jnp.log(l), lse_slab)
        lse_ref[sub] = lse_slab


def _dilated_attention(qkv, g, cq, ck, cv):
    d = DILATIONS[g]
    W = ATT_GROUP_WIDTH
    nb = qkv.shape[0]
    nq = min(ATT_Q_BLOCKS, nb)
    assert qkv.shape[1] == d and nb % nq == 0
    blk = (nq, None, ATT_BLOCK, W)
    pblk = (None, None, ATT_BLOCK, W)
    cur = lambda c: (lambda r, n: (n, r, 0, c))
    prev = lambda c: (lambda r, n: (jnp.maximum(nq * n - 1, 0), r, 0, c))
    return pl.pallas_call(
        _attn_kernel,
        grid=(d, nb // nq),
        in_specs=[
            pl.BlockSpec(blk, cur(cq)),
            pl.BlockSpec(blk, cur(ck)),
            pl.BlockSpec(pblk, prev(ck)),
            pl.BlockSpec(blk, cur(cv)),
            pl.BlockSpec(pblk, prev(cv)),
        ],
        out_specs=[pl.BlockSpec(blk, lambda r, n: (n, r, 0, 0)),
                   pl.BlockSpec((nq, None, ATT_BLOCK, LANES), lambda r, n: (n, r, 0, 0))],
        out_shape=[jax.ShapeDtypeStruct((nb, d, ATT_BLOCK, W), F32),
                   jax.ShapeDtypeStruct((nb, d, ATT_BLOCK, LANES), F32)],
        compiler_params=_cparams(2),
        name=f"dilated_attn_g{g}",
    )(qkv, qkv, qkv, qkv, qkv)


RET_CHUNKS_PER_STEP = 4


def _retention_kernel(q_ref, k_ref, vlo_ref, vhi_ref, glo_ref, ghi_ref, o_ref,
                      state_ref, intra_ref, dec_ref):
    @pl.when(pl.program_id(0) == 0)
    def _():
        state_ref[...] = jnp.zeros_like(state_ref)
        ri = lax.broadcasted_iota(jnp.int32, (RET_CHUNK, RET_CHUNK), 0)
        ci = lax.broadcasted_iota(jnp.int32, (RET_CHUNK, RET_CHUNK), 1)
        diff = (ri - ci).astype(F32)
        nrow = ri.astype(F32)
        for h in range(RET_HEADS):
            ld = RET_LOG_DECAY[h]
            intra_ref[h] = jnp.where(diff >= 0, jnp.exp(jnp.maximum(diff, 0.0) * ld), 0.0)
            dec_ref[h, 0] = jnp.exp((nrow + 1.0) * ld)
            dec_ref[h, 1] = jnp.exp((RET_CHUNK - 1.0 - nrow) * ld)

    half = RET_HEADS // 2
    for h in range(RET_HEADS):
        ld = RET_LOG_DECAY[h]
        sl = slice(h * RET_QK_DIM, (h + 1) * RET_QK_DIM)
        vsl = slice((h % half) * RET_V_DIM, (h % half + 1) * RET_V_DIM)
        v_ref, g_ref = (vlo_ref, glo_ref) if h < half else (vhi_ref, ghi_ref)
        state = state_ref[h]
        for c in range(q_ref.shape[0] // RET_CHUNK):
            rs = slice(c * RET_CHUNK, (c + 1) * RET_CHUNK)
            q = q_ref[rs, sl]
            k = k_ref[rs, sl]
            v = v_ref[rs, vsl]
            scores = (lax.dot_general(q, k, (((1,), (1,)), ((), ())), preferred_element_type=F32)
                      * intra_ref[h])
            inner = jnp.dot(scores.astype(BF16), v, preferred_element_type=F32)
            cross = jnp.dot((q.astype(F32) * dec_ref[h, 0]).astype(BF16), state.astype(BF16),
                            preferred_element_type=F32)
            kv = lax.dot_general((k.astype(F32) * dec_ref[h, 1]).astype(BF16), v,
                                 (((0,), (0,)), ((), ())), preferred_element_type=F32)
            state = math.exp(RET_CHUNK * ld) * state + kv
            o = inner + cross
            mu = jnp.mean(o, axis=1, keepdims=True)
            var = jnp.mean(jnp.square(o - mu), axis=1, keepdims=True)
            o = (o - mu) * lax.rsqrt(var + LN_EPS)
            gate = g_ref[rs, vsl].astype(F32)
            gate = gate * (1.0 / (1.0 + jnp.exp(-gate)))
            o_ref[rs, h * RET_V_DIM:(h + 1) * RET_V_DIM] = (gate * o).astype(o_ref.dtype)
        state_ref[h] = state


def _retention(nat, S):
    rows = RET_CHUNK * RET_CHUNKS_PER_STEP
    blk = (rows, COL_BLOCK)

    def col(cb):
        return pl.BlockSpec(blk, lambda c: (c, cb))

    return pl.pallas_call(
        _retention_kernel,
        grid=(S // rows,),
        in_specs=[col(NC_QR), col(NC_KR), col(NC_VR), col(NC_VR + 1), col(NC_GR), col(NC_GR + 1)],
        out_specs=pl.BlockSpec((rows, RET_HEADS * RET_V_DIM), lambda c: (c, 0)),
        out_shape=jax.ShapeDtypeStruct((S, RET_HEADS * RET_V_DIM), BF16),
        scratch_shapes=[pltpu.VMEM((RET_HEADS, RET_QK_DIM, RET_V_DIM), F32),
                        pltpu.VMEM((RET_HEADS, RET_CHUNK, RET_CHUNK), F32),
                        pltpu.VMEM((RET_HEADS, 2, RET_CHUNK, RET_QK_DIM), F32)],
        compiler_params=_cparams(1),
        name="retention",
    )(nat, nat, nat, nat, nat, nat)


MERGE_ROWS = 512
COL_HALVES = (slice(0, D_MODEL // 2), slice(D_MODEL // 2, D_MODEL))


def _layer_norm(z, scale, bias):
    mu = jnp.mean(z, axis=1, keepdims=True)
    var = jnp.mean(jnp.square(z - mu), axis=1, keepdims=True)
    return (z - mu) * lax.rsqrt(var + LN_EPS) * scale + bias


def _sigmoid(x):
    return 1.0 / (1.0 + jnp.exp(-x))


def _pack_bf16_pair(a, b):
    abits = lax.bitcast_convert_type(a.astype(BF16).astype(F32), jnp.uint32)
    bbits = lax.bitcast_convert_type(b.astype(BF16).astype(F32), jnp.uint32)
    return (abits >> 16) | (bbits & jnp.uint32(0xFFFF0000))


def _unpack_bf16_pair(w):
    a = lax.bitcast_convert_type(w << 16, F32).astype(BF16)
    b = lax.bitcast_convert_type(w & jnp.uint32(0xFFFF0000), F32).astype(BF16)
    return a, b


def _split_bf16(v):
    hi = v.astype(BF16)
    return hi, (v - hi.astype(F32)).astype(BF16)


def _mix_kernel(o0_ref, o1_ref, o2_ref, l0_ref, l1_ref, l2_ref, or_ref,
                ga0_ref, ga1_ref, gb0_ref, gb1_ref, wa_ref, wr_ref, hexp_ref,
                m_ref, s1_ref, s2_ref, ls1_ref, ls2_ref):
    tm = o0_ref.shape[0]
    for d, src, lsrc, dst, ldst in ((DILATIONS[1], o1_ref, l1_ref, s1_ref, ls1_ref),
                                    (DILATIONS[2], o2_ref, l2_ref, s2_ref, ls2_ref)):
        for r in range(d):
            for h in range(ATT_HEADS_PER_GROUP):
                dst[h, pl.ds(r, tm // d, stride=d), :] = src[r, :, h * ATT_HEAD_DIM:(h + 1) * ATT_HEAD_DIM]
            ldst[pl.ds(r, tm // d, stride=d), :] = lsrc[r]

    l0, l1, l2 = l0_ref[...], ls1_ref[...], ls2_ref[...]
    lm = jnp.maximum(jnp.maximum(l0, l1), l2)
    e0, e1, e2 = jnp.exp(l0 - lm), jnp.exp(l1 - lm), jnp.exp(l2 - lm)
    den = e0 + e1 + e2

    def lanes_of_head(w):
        hi, lo = _split_bf16(w)
        return (jnp.dot(hi, hexp_ref[...], preferred_element_type=F32)
                + jnp.dot(lo, hexp_ref[...], preferred_element_type=F32))

    o_r = or_ref[...]
    y_r = [jnp.dot(o_r, wr_ref[:, cs], preferred_element_type=F32) for cs in COL_HALVES]
    heads = range(ATT_HEADS_PER_GROUP)
    o1 = jnp.concatenate([s1_ref[h] for h in heads], axis=1)
    o2 = jnp.concatenate([s2_ref[h] for h in heads], axis=1)
    o_a = (lanes_of_head(e0 / den) * o0_ref[...] + lanes_of_head(e1 / den) * o1
           + lanes_of_head(e2 / den) * o2).astype(BF16)
    for n, cs in enumerate(COL_HALVES):
        y_a = jnp.dot(o_a, wa_ref[:, cs], preferred_element_type=F32)
        gate_a = (ga0_ref, ga1_ref)[n][...].astype(F32)
        gate_b = (gb0_ref, gb1_ref)[n][...].astype(F32)
        m_ref[:, cs] = (_sigmoid(gate_a) * y_a + _sigmoid(gate_b) * y_r[n]).astype(m_ref.dtype)


def _finish_kernel(m_ref, x_ref, wo_ref, lns_ref, lnb_ref, wrt_ref, brt_ref,
                   x1_ref, x1p_ref, route_ref, cnt_ref):
    rc = x_ref.shape[0]
    half = D_MODEL // 2
    halves = COL_HALVES

    @pl.when(pl.program_id(0) == 0)
    def _():
        cnt_ref[...] = jnp.zeros_like(cnt_ref)

    z = [ALPHA * x_ref[:, cs]
         + jnp.dot(m_ref[:, :half], wo_ref[:half, cs], preferred_element_type=F32)
         + jnp.dot(m_ref[:, half:], wo_ref[half:, cs], preferred_element_type=F32) for cs in halves]
    mu = (jnp.sum(z[0], axis=1, keepdims=True) + jnp.sum(z[1], axis=1, keepdims=True)) / D_MODEL
    var = (jnp.sum(jnp.square(z[0] - mu), axis=1, keepdims=True)
           + jnp.sum(jnp.square(z[1] - mu), axis=1, keepdims=True)) / D_MODEL
    inv = lax.rsqrt(var + LN_EPS)
    x1 = [(z[n] - mu) * inv * lns_ref[:, cs] + lnb_ref[:, cs] for n, cs in enumerate(halves)]
    for n, cs in enumerate(halves):
        x1_ref[:, cs] = x1[n]
    packed = _pack_bf16_pair(x1[0], x1[1])
    for c in range(PACKED_ROW_PIECES):
        x1p_ref[pl.ds(c, rc, stride=PACKED_ROW_PIECES), :] = (
            packed[:, c * LANES:(c + 1) * LANES])

    logits = brt_ref[...]
    for n, cs in enumerate(halves):
        x_hi, x_lo = _split_bf16(x1[n])
        hl = jnp.dot(x_hi, wrt_ref[cs, :], preferred_element_type=F32)
        logits = (logits + hl[:, :LANES] + hl[:, LANES:]
                  + jnp.dot(x_lo, wrt_ref[cs, :LANES], preferred_element_type=F32))
    lane = lax.broadcasted_iota(jnp.int32, logits.shape, 1)
    is_g = lane < N_EXPERT_GROUPS
    gl = jnp.where(is_g, logits, NEG_INF)
    gmax = jnp.max(gl, axis=1, keepdims=True)
    g_sel = jnp.min(jnp.where(is_g & (gl == gmax), lane, LANES), axis=1, keepdims=True)
    p_group = 1.0 / jnp.sum(jnp.where(is_g, jnp.exp(gl - gmax), 0.0), axis=1, keepdims=True)
    lo = N_EXPERT_GROUPS + EXPERTS_PER_GROUP * g_sel
    in_grp = (lane >= lo) & (lane < lo + EXPERTS_PER_GROUP)
    el = jnp.where(in_grp, logits, NEG_INF)
    top1 = jnp.max(el, axis=1, keepdims=True)
    idx1 = jnp.min(jnp.where(in_grp & (el == top1), lane, LANES), axis=1, keepdims=True)
    el2 = jnp.where(lane == idx1, NEG_INF, el)
    top2 = jnp.max(el2, axis=1, keepdims=True)
    idx2 = jnp.min(jnp.where(in_grp & (lane != idx1) & (el2 == top2), lane, LANES), axis=1, keepdims=True)
    t = jnp.exp(top2 - top1)
    wt1 = p_group * (1.0 / (1.0 + t))
    wt2 = p_group * (t / (1.0 + t))
    slab = jnp.where(lane == 0, (idx1 - N_EXPERT_GROUPS).astype(F32), 0.0)
    slab = jnp.where(lane == 1, (idx2 - N_EXPERT_GROUPS).astype(F32), slab)
    slab = jnp.where(lane == 2, wt1, slab)
    slab = jnp.where(lane == 3, wt2, slab)

    hot = jnp.where((lane == idx1) | (lane == idx2), 1.0, 0.0)
    ri = lax.broadcasted_iota(jnp.int32, (rc, rc), 0)
    ci = lax.broadcasted_iota(jnp.int32, (rc, rc), 1)
    ltri = jnp.where(ri > ci, 1.0, 0.0).astype(BF16)
    prefix = jnp.dot(ltri, hot.astype(BF16), preferred_element_type=F32) + cnt_ref[...]
    rank1 = jnp.sum(jnp.where(lane == idx1, prefix, 0.0), axis=1, keepdims=True)
    rank2 = jnp.sum(jnp.where(lane == idx2, prefix, 0.0), axis=1, keepdims=True)
    cnt_ref[...] = cnt_ref[...] + jnp.sum(hot, axis=0, keepdims=True)
    slab = jnp.where(lane == 4, rank1, slab)
    slab = jnp.where(lane == 5, rank2, slab)
    route_ref[...] = slab


def _row_spec(tm, width):
    return pl.BlockSpec((tm, width), lambda i: (i, 0))


def _const_spec(shape):
    return pl.BlockSpec(shape, lambda i: (0, 0), pipeline_mode=pl.Buffered(1))


def _mix(o_g, lse_g, o_r, nat, wa, wr, S):
    tm = MERGE_ROWS
    D = D_MODEL
    W = ATT_GROUP_WIDTH
    head_expand = (jnp.arange(W, dtype=jnp.int32)[None, :] // ATT_HEAD_DIM
                   == jnp.arange(LANES, dtype=jnp.int32)[:, None]).astype(BF16)

    def streams(g, width):
        d = DILATIONS[g]
        per = (ATT_BLOCK * d) // tm
        return pl.BlockSpec((None, d, tm // d, width), lambda i: (i // per, 0, i % per, 0))

    def ncol(cb):
        return pl.BlockSpec((tm, COL_BLOCK), lambda i: (i, cb))

    return pl.pallas_call(
        _mix_kernel,
        grid=(S // tm,),
        in_specs=[
            _row_spec(tm, W), streams(1, W), streams(2, W),
            _row_spec(tm, LANES), streams(1, LANES), streams(2, LANES),
            _row_spec(tm, RET_HEADS * RET_V_DIM),
            ncol(NC_GATE_A), ncol(NC_GATE_A + 1), ncol(NC_GATE_B), ncol(NC_GATE_B + 1),
            _const_spec((W, D)), _const_spec((RET_HEADS * RET_V_DIM, D)), _const_spec((LANES, W)),
        ],
        out_specs=_row_spec(tm, D),
        out_shape=jax.ShapeDtypeStruct((S, D), BF16),
        scratch_shapes=[pltpu.VMEM((ATT_HEADS_PER_GROUP, tm, ATT_HEAD_DIM), F32),
                        pltpu.VMEM((ATT_HEADS_PER_GROUP, tm, ATT_HEAD_DIM), F32),
                        pltpu.VMEM((tm, LANES), F32), pltpu.VMEM((tm, LANES), F32)],
        compiler_params=_cparams(1),
        name="mix_branches",
    )(*o_g, *lse_g, o_r, nat, nat, nat, nat, wa, wr, head_expand)


def _finish(merged, x2d, wo, lns, lnb, wrt, brt):
    S, D = x2d.shape
    tm = MERGE_ROWS
    return pl.pallas_call(
        _finish_kernel,
        grid=(S // tm,),
        in_specs=[
            _row_spec(tm, D), _row_spec(tm, D),
            _const_spec((D, D)), _const_spec((1, D)), _const_spec((1, D)),
            _const_spec((D, 2 * LANES)), _const_spec((1, LANES)),
        ],
        out_specs=[_row_spec(tm, D), pl.BlockSpec((tm * PACKED_ROW_PIECES, LANES), lambda i: (i, 0)),
                   _row_spec(tm, LANES), pl.BlockSpec((1, LANES), lambda i: (0, 0))],
        out_shape=[jax.ShapeDtypeStruct((S, D), F32),
                   jax.ShapeDtypeStruct((S * PACKED_ROW_PIECES, LANES), jnp.uint32),
                   jax.ShapeDtypeStruct((S, LANES), F32),
                   jax.ShapeDtypeStruct((1, LANES), F32)],
        compiler_params=_cparams(1),
        name="out_proj_ln_router",
    )(merged, x2d, wo, lns, lnb, wrt, brt)


META_N_USED, META_NEXT = 0, 1
META_SIZE = META_NEXT + N_EXPERTS
GATHER_SLOTS = 4
GATHER_GROUP = 2
WEIGHT_SLOTS = 2


def _plan_kernel(dest_ref, cnt_ref, rt_ref, be_ref, meta_ref):
    n_blocks = be_ref.shape[0]

    def pad_row(r, carry):
        rt_ref[r] = 0
        return carry

    def per_expert(e, start):
        nb = lax.shift_right_logical(cnt_ref[e] + (MOE_BLOCK - 1), MOE_BLOCK_SHIFT)

        def fill(b, carry):
            be_ref[lax.shift_right_logical(start, MOE_BLOCK_SHIFT) + b] = e
            return carry
        lax.fori_loop(0, nb, fill, 0)
        lax.fori_loop(start + cnt_ref[e], start + nb * MOE_BLOCK, pad_row, 0)
        return start + nb * MOE_BLOCK
    total = lax.fori_loop(0, N_EXPERTS, per_expert, 0)
    n_used = lax.shift_right_logical(total, MOE_BLOCK_SHIFT)
    meta_ref[META_N_USED] = n_used

    def tail(b, carry):
        be_ref[b] = be_ref[jnp.maximum(n_used - 1, 0)]
        return carry
    lax.fori_loop(n_used, n_blocks, tail, 0)
    lax.fori_loop(total, rt_ref.shape[0], pad_row, 0)

    def nxt(k, cur):
        e = N_EXPERTS - 1 - k
        meta_ref[META_NEXT + e] = cur
        return jnp.where(cnt_ref[e] > 0, e, cur)
    lax.fori_loop(0, N_EXPERTS, nxt, N_EXPERTS)

    def assign(a, carry):
        rt_ref[dest_ref[a]] = lax.shift_right_logical(a, 1)
        return carry
    lax.fori_loop(0, dest_ref.shape[0], assign, 0, unroll=8)


def _plan(dest, counts):
    A = dest.shape[0]
    n_blocks = A // MOE_BLOCK + N_EXPERTS
    smem = pl.BlockSpec(memory_space=pltpu.SMEM)
    return pl.pallas_call(
        _plan_kernel,
        in_specs=[smem, smem],
        out_specs=[smem, smem, smem],
        out_shape=[
            jax.ShapeDtypeStruct((n_blocks * MOE_BLOCK,), jnp.int32),
            jax.ShapeDtypeStruct((n_blocks,), jnp.int32),
            jax.ShapeDtypeStruct((META_SIZE,), jnp.int32),
        ],
        name="moe_plan",
    )(dest, counts)


def _expert_kernel(be_ref, meta_ref, rt_ref, x_hbm, wg_hbm, wu_hbm, wd_hbm, y_ref,
                   xg_ref, wgf_ref, wuf_ref, wdf_ref, wgb_ref, wub_ref, wdb_ref, xsem, sem, slot_ref):
    b = pl.program_id(0)
    n_used = meta_ref[META_N_USED]
    e = be_ref[b]
    first = (b == 0) | (e != be_ref[jnp.maximum(b - 1, 0)])

    xp = PACKED_ROW_PIECES

    last = jnp.maximum(n_used - 1, 0)
    group_rows = GATHER_GROUP * MOE_BLOCK
    group = b // GATHER_GROUP
    last_group = last // GATHER_GROUP

    def row_copy(grp, slot, i):
        tok = rt_ref[grp * group_rows + i]
        return pltpu.make_async_copy(x_hbm.at[pl.ds(pl.multiple_of(tok * xp, xp), xp)],
                                     xg_ref.at[slot, pl.ds(pl.multiple_of(i * xp, xp), xp)],
                                     xsem.at[slot])

    def wait_rows(slot):
        pltpu.make_async_copy(x_hbm.at[pl.ds(0, group_rows * xp)], xg_ref.at[slot], xsem.at[slot]).wait()

    @pl.when(b == 0)
    def _():
        for ahead in range(GATHER_SLOTS - 1):
            def body(i, carry, ahead=ahead):
                row_copy(jnp.minimum(ahead, last_group), ahead, i).start()
                return carry
            lax.fori_loop(0, group_rows, body, 0, unroll=8)

    def copies(ex, s):
        return (pltpu.make_async_copy(wg_hbm.at[ex], wgf_ref.at[s], sem.at[s]),
                pltpu.make_async_copy(wu_hbm.at[ex], wuf_ref.at[s], sem.at[s]),
                pltpu.make_async_copy(wd_hbm.at[ex], wdf_ref.at[s], sem.at[s]))

    def next_expert(ex):
        return jnp.where(ex < N_EXPERTS, meta_ref[META_NEXT + jnp.minimum(ex, N_EXPERTS - 1)], N_EXPERTS)

    def fetch(ex, s):
        @pl.when(ex < N_EXPERTS)
        def _():
            for c in copies(ex, s):
                c.start()

    @pl.when((b == 0) & (n_used > 0))
    def _():
        slot_ref[0] = 0
        ex = e
        for s in range(WEIGHT_SLOTS - 1):
            fetch(ex, s)
            ex = next_expert(ex)

    @pl.when(first & (b < n_used))
    def _():
        s = slot_ref[0]
        ex = e
        for _ in range(WEIGHT_SLOTS - 1):
            ex = next_expert(ex)
        fetch(ex, lax.rem(s + WEIGHT_SLOTS - 1, WEIGHT_SLOTS))
        for c in copies(e, s):
            c.wait()
        wgb_ref[...] = wgf_ref[s].astype(BF16)
        wub_ref[...] = wuf_ref[s].astype(BF16)
        wdb_ref[...] = wdf_ref[s].astype(BF16)
        slot_ref[0] = lax.rem(s + 1, WEIGHT_SLOTS)

    def step(slot, sub):
        if sub == 0:
            wait_rows(slot)
            nxt = jnp.minimum(group + (GATHER_SLOTS - 1), last_group)
            for i in range(group_rows):
                row_copy(nxt, (slot + GATHER_SLOTS - 1) % GATHER_SLOTS, i).start()
        base = sub * MOE_BLOCK * xp
        pieces = [_unpack_bf16_pair(xg_ref[slot, pl.ds(base + c, MOE_BLOCK, stride=xp), :])
                  for c in range(xp)]
        xb = jnp.concatenate([p[0] for p in pieces] + [p[1] for p in pieces], axis=1)
        hg = jnp.dot(xb, wgb_ref[...], preferred_element_type=F32)
        hu = jnp.dot(xb, wub_ref[...], preferred_element_type=F32)
        hid = (hg * _sigmoid(hg) * hu).astype(BF16)
        y = jnp.dot(hid, wdb_ref[...], preferred_element_type=F32)
        for c in range(ROW_PIECES):
            y_ref[pl.ds(c, MOE_BLOCK, stride=ROW_PIECES), :] = y[:, c * LANES:(c + 1) * LANES]

    for s in range(GATHER_SLOTS):
        for sub in range(GATHER_GROUP):
            pl.when((b < n_used) & (lax.rem(group, GATHER_SLOTS) == s)
                    & (lax.rem(b, GATHER_GROUP) == sub))(functools.partial(step, s, sub))

    @pl.when(b == last)
    def _():
        for ahead in range(1, GATHER_SLOTS):
            wait_rows(lax.rem(group + ahead, GATHER_SLOTS))

    @pl.when(b >= n_used)
    def _():
        y_ref[...] = jnp.zeros_like(y_ref)


def _experts(x1p, block_expert, meta, row_tok, w_gate, w_up, w_down):
    D = D_MODEL
    assert x1p.shape[1] == LANES and x1p.dtype == jnp.uint32
    n_blocks = block_expert.shape[0]
    R = n_blocks * MOE_BLOCK
    H = EXPERT_HIDDEN
    grid_spec = pltpu.PrefetchScalarGridSpec(
        num_scalar_prefetch=3,
        grid=(n_blocks,),
        in_specs=[
            pl.BlockSpec(memory_space=pl.ANY),
            pl.BlockSpec(memory_space=pl.ANY),
            pl.BlockSpec(memory_space=pl.ANY),
            pl.BlockSpec(memory_space=pl.ANY),
        ],
        out_specs=pl.BlockSpec((MOE_BLOCK * ROW_PIECES, LANES), lambda b, be, meta, rt: (b, 0)),
        scratch_shapes=[
            pltpu.VMEM((GATHER_SLOTS, GATHER_GROUP * MOE_BLOCK * PACKED_ROW_PIECES, LANES), jnp.uint32),
            pltpu.VMEM((WEIGHT_SLOTS, D, H), F32),
            pltpu.VMEM((WEIGHT_SLOTS, D, H), F32),
            pltpu.VMEM((WEIGHT_SLOTS, H, D), F32),
            pltpu.VMEM((D, H), BF16),
            pltpu.VMEM((D, H), BF16),
            pltpu.VMEM((H, D), BF16),
            pltpu.SemaphoreType.DMA((GATHER_SLOTS,)),
            pltpu.SemaphoreType.DMA((WEIGHT_SLOTS,)),
            pltpu.SMEM((1,), jnp.int32),
        ],
    )
    return pl.pallas_call(
        _expert_kernel,
        grid_spec=grid_spec,
        out_shape=jax.ShapeDtypeStruct((R * ROW_PIECES, LANES), F32),
        compiler_params=_cparams(1),
        name="moe_experts",
    )(block_expert, meta, row_tok, x1p, w_gate, w_up, w_down)


COMBINE_TOKENS = 256


def _combine_kernel(dest_ref, y_hbm, x1_ref, route_ref, lns_ref, lnb_ref, o_ref, yg_ref, sem):
    i = pl.program_id(0)
    tt = COMBINE_TOKENS
    n = dest_ref.shape[0] // (TOP_K * tt)

    rp = ROW_PIECES

    def start_token(step, slot, r):
        for k in range(TOP_K):
            row = dest_ref[(step * tt + r) * TOP_K + k]
            pltpu.make_async_copy(y_hbm.at[pl.ds(pl.multiple_of(row * rp, rp), rp)],
                                  yg_ref.at[slot, k, pl.ds(pl.multiple_of(r * ROW_PITCH, 8), rp)],
                                  sem.at[slot]).start(priority=k)

    def wait_slot(slot):
        for k in range(TOP_K):
            pltpu.make_async_copy(y_hbm.at[pl.ds(0, tt * rp)], yg_ref.at[slot, k, pl.ds(0, tt * rp)],
                                  sem.at[slot]).wait()

    @pl.when(i == 0)
    def _():
        for ahead in range(GATHER_SLOTS - 1):
            def body(r, carry, ahead=ahead):
                start_token(min(ahead, n - 1), ahead, r)
                return carry
            lax.fori_loop(0, tt, body, 0, unroll=4)

    def step(slot):
        wait_slot(slot)
        nxt = jnp.minimum(i + (GATHER_SLOTS - 1), n - 1)
        for r in range(tt):
            start_token(nxt, (slot + GATHER_SLOTS - 1) % GATHER_SLOTS, r)
        route = route_ref[...]
        w1, w2 = route[:, 2:3], route[:, 3:4]
        moe = jnp.concatenate(
            [w1 * yg_ref[slot, 0, pl.ds(c, tt, stride=ROW_PITCH), :]
             + w2 * yg_ref[slot, 1, pl.ds(c, tt, stride=ROW_PITCH), :] for c in range(rp)], axis=1)
        o_ref[...] = _layer_norm(ALPHA * x1_ref[...] + moe, lns_ref[...], lnb_ref[...])

    for s in range(GATHER_SLOTS):
        pl.when(lax.rem(i, GATHER_SLOTS) == s)(functools.partial(step, s))

    @pl.when(i == n - 1)
    def _():
        for ahead in range(1, GATHER_SLOTS):
            wait_slot((n - 1 + ahead) % GATHER_SLOTS)


def _combine(dest, y, x1, route, lns, lnb):
    T, D = x1.shape
    tt = COMBINE_TOKENS
    grid_spec = pltpu.PrefetchScalarGridSpec(
        num_scalar_prefetch=1,
        grid=(T // tt,),
        in_specs=[
            pl.BlockSpec(memory_space=pl.ANY),
            pl.BlockSpec((tt, D), lambda i, d: (i, 0)),
            pl.BlockSpec((tt, LANES), lambda i, d: (i, 0)),
            pl.BlockSpec((1, D), lambda i, d: (0, 0)),
            pl.BlockSpec((1, D), lambda i, d: (0, 0)),
        ],
        out_specs=pl.BlockSpec((tt, D), lambda i, d: (i, 0)),
        scratch_shapes=[
            pltpu.VMEM((GATHER_SLOTS, TOP_K, tt * ROW_PITCH, LANES), F32),
            pltpu.SemaphoreType.DMA((GATHER_SLOTS,)),
        ],
    )
    return pl.pallas_call(
        _combine_kernel,
        grid_spec=grid_spec,
        out_shape=jax.ShapeDtypeStruct((T, D), F32),
        compiler_params=_cparams(1),
        name="moe_combine_ln",
    )(dest, y, x1, route, lns, lnb)


def _rotary_tables(positions):
    pos = positions.reshape(-1).astype(F32)
    S = pos.shape[0]
    inv_a = ROPE_THETA ** (-jnp.arange(0, ROPE_DIM, 2, dtype=F32) / ROPE_DIM)
    inv_r = RET_ROT_BASE ** (-jnp.linspace(0.0, 1.0, RET_QK_DIM // 2, dtype=F32))
    ang = jnp.concatenate([inv_a, inv_r])[:, None] * pos[None, :]
    cos_t, sin_t = lax.optimization_barrier((jnp.cos(ang), jnp.sin(ang)))
    na = ROPE_DIM // 2
    cos, sin, cr, sr = cos_t[:na].T, sin_t[:na].T, cos_t[na:].T, sin_t[na:].T
    tabs_a = jnp.concatenate([cos, sin, jnp.zeros((S, LANES - ROPE_DIM), F32)], axis=1)
    tabs_r = jnp.concatenate([cr, sr], axis=1)
    return tabs_a, tabs_r


def kernel(x, positions, w_in, w_branch_attn, w_branch_ret, w_out, ln1_scale, ln1_bias, w_group_router, b_group_router, w_expert_router, b_expert_router, w_expert_gate, w_expert_up, w_expert_down, ln2_scale, ln2_bias):
    B, S, D = x.shape
    assert B == 1 and D == D_MODEL and w_in.shape[0] == DEPTH == 1
    assert S % (ATT_BLOCK * DILATIONS[2]) == 0
    x2d = x.reshape(S, D)
    xb = x2d.astype(BF16)
    tabs_a, tabs_r = _rotary_tables(positions)

    nat = _input_projection(xb, w_in[0], tabs_a, tabs_r, NAT_COLS, NAT_MODES, 1)
    qkv_modes = (MODE_ATT_Q, MODE_ATT_K, MODE_PLAIN)
    qkv = [None] + [
        _input_projection(xb, w_in[0], tabs_a, tabs_r, (CB_QA + g, CB_KA + g, CB_VA + g), qkv_modes,
                          DILATIONS[g])
        for g in (1, 2)]
    o0, l0 = _dilated_attention(nat.reshape(S // ATT_BLOCK, 1, ATT_BLOCK, nat.shape[1]), 0,
                                NC_Q0, NC_K0, NC_V0)
    att = [(o0.reshape(S, ATT_GROUP_WIDTH), l0.reshape(S, LANES))]
    att += [_dilated_attention(qkv[g], g, 0, 1, 2) for g in (1, 2)]
    o_r = _retention(nat, S)

    w_route = jnp.concatenate(
        [w_group_router[0], w_expert_router[0].transpose(1, 0, 2).reshape(D, N_EXPERTS),
         jnp.zeros((D, LANES - N_EXPERT_GROUPS - N_EXPERTS), F32)], axis=1)
    b_route = jnp.concatenate(
        [b_group_router[0], b_expert_router[0].reshape(N_EXPERTS),
         jnp.zeros((LANES - N_EXPERT_GROUPS - N_EXPERTS,), F32)]).reshape(1, LANES)
    merged = _mix([a[0] for a in att], [a[1] for a in att], o_r, nat,
                  w_branch_attn[0].astype(BF16), w_branch_ret[0].astype(BF16), S)
    x1, x1p, route, cnt = _finish(
        merged, x2d, w_out[0].astype(BF16), ln1_scale[0].reshape(1, D), ln1_bias[0].reshape(1, D),
        jnp.concatenate(_split_bf16(w_route), axis=1), b_route)

    e_flat = route[:, 0:TOP_K].astype(jnp.int32).reshape(S * TOP_K)
    rank_flat = route[:, 4:4 + TOP_K].astype(jnp.int32).reshape(S * TOP_K)
    counts = cnt[0, N_EXPERT_GROUPS:N_EXPERT_GROUPS + N_EXPERTS].astype(jnp.int32)
    padded = (counts + (MOE_BLOCK - 1)) // MOE_BLOCK * MOE_BLOCK
    pstart = jnp.cumsum(padded) - padded
    hot = e_flat[:, None] == jnp.arange(N_EXPERTS, dtype=jnp.int32)[None, :]
    dest = jnp.sum(jnp.where(hot, pstart[None, :], 0), axis=1) + rank_flat
    row_tok, block_expert, meta = _plan(dest, counts)
    y = _experts(x1p, block_expert, meta, row_tok, w_expert_gate[0], w_expert_up[0], w_expert_down[0])
    out = _combine(dest, y, x1, route, ln2_scale[0].reshape(1, D), ln2_bias[0].reshape(1, D))
    return out.reshape(B, S, D)
```

```python
import functools
import math

import jax
import jax.numpy as jnp
from jax import lax
from jax.experimental import pallas as pl
from jax.experimental.pallas import tpu as pltpu

F32 = jnp.float32
BF16 = jnp.bfloat16

D_MODEL = 2048
ATT_HEAD_DIM = 128
ATT_HEADS_PER_GROUP = 8
DILATIONS = (1, 4, 16)
ATT_STEPS = 128
ATT_GROUP_WIDTH = ATT_HEADS_PER_GROUP * ATT_HEAD_DIM
ROPE_DIM = ATT_HEAD_DIM // 4
ROPE_THETA = 500000.0
ATT_BLOCK = 128

RET_HEADS = 8
RET_QK_DIM = 128
RET_V_DIM = 256
RET_CHUNK = 128
RET_ROT_BASE = 10000.0
RET_LOG_DECAY = tuple(math.log1p(-(2.0 ** (-5.0 - h))) for h in range(RET_HEADS))

N_EXPERT_GROUPS = 4
EXPERTS_PER_GROUP = 8
N_EXPERTS = N_EXPERT_GROUPS * EXPERTS_PER_GROUP
TOP_K = 2
EXPERT_HIDDEN = 512
MOE_BLOCK = 128
MOE_BLOCK_SHIFT = MOE_BLOCK.bit_length() - 1
assert MOE_BLOCK == 1 << MOE_BLOCK_SHIFT

DEPTH = 1
ALPHA = (2.0 * DEPTH) ** 0.25
LN_EPS = 1e-5
NEG_INF = -1e30

COL_BLOCK = 1024
CB_QA, CB_KA, CB_VA = 0, 3, 6
CB_QR, CB_KR, CB_VR, CB_GR = 9, 10, 11, 13
CB_GATE_A, CB_GATE_B = 15, 17

MODE_PLAIN, MODE_ATT_Q, MODE_ATT_K, MODE_RET_Q, MODE_RET_K = 0, 1, 2, 3, 4

NAT_COLS = (CB_QA, CB_KA, CB_VA, CB_QR, CB_KR, CB_VR, CB_VR + 1, CB_GR, CB_GR + 1,
            CB_GATE_A, CB_GATE_A + 1, CB_GATE_B, CB_GATE_B + 1)
NAT_MODES = (MODE_ATT_Q, MODE_ATT_K, MODE_PLAIN, MODE_RET_Q, MODE_RET_K) + (MODE_PLAIN,) * 8
NC_Q0, NC_K0, NC_V0, NC_QR, NC_KR, NC_VR, NC_GR, NC_GATE_A, NC_GATE_B = 0, 1, 2, 3, 4, 5, 7, 9, 11

LANES = 128
ROW_PIECES = D_MODEL // LANES
PACKED_ROW_PIECES = ROW_PIECES // 2
VMEM_LIMIT = 56 * 1024 * 1024
PROJ_ROWS = 1024
ROW_PITCH = 24
ACC_PITCH = 24


def _cparams(n_grid_dims):
    return pltpu.CompilerParams(
        dimension_semantics=("arbitrary",) * n_grid_dims,
        vmem_limit_bytes=VMEM_LIMIT,
    )


def _inproj_kernel(cols_ref, modes_ref, x_ref, w_ref, ta_ref, tr_ref, pa_ref, pr_ref, o_ref,
                   wbf_ref, *acc, d):
    del cols_ref
    j = pl.program_id(0)

    @pl.when(pl.program_id(1) == 0)
    def _():
        wbf_ref[...] = w_ref[...].astype(BF16)

    mode = modes_ref[j]
    tm, tn = x_ref.shape[0], wbf_ref.shape[1]
    pair = 2 * LANES

    def project():
        return jnp.dot(x_ref[...], wbf_ref[...], preferred_element_type=F32)

    def emit(p, piece):
        if d == 1:
            o_ref[:, p * pair:(p + 1) * pair] = piece.astype(o_ref.dtype)
        elif d == 4:
            acc[0][2 * p] = piece[:, :LANES]
            acc[0][2 * p + 1] = piece[:, LANES:]
        else:
            for g in range(tm // d):
                rows = slice(g * d, (g + 1) * d)
                dst = slice(g * ACC_PITCH, g * ACC_PITCH + d)
                acc[0][2 * p, dst, :] = piece[rows, :LANES]
                acc[0][2 * p + 1, dst, :] = piece[rows, LANES:]

    def plain():
        res = project()
        for p in range(tn // pair):
            emit(p, res[:, p * pair:(p + 1) * pair])

    def rotary(tab_ref, perm_ref, scale, half):
        res = project()
        tab = tab_ref[...]
        lane = lax.broadcasted_iota(jnp.int32, tab.shape, 1)
        c = jnp.where(lane < half, tab, pltpu.roll(tab, half, 1))
        s = jnp.where(lane < half, pltpu.roll(tab, LANES - half, 1), tab)
        c = jnp.concatenate([jnp.where(lane < 2 * half, c, 1.0)] * 2, axis=1) * scale
        s = jnp.concatenate([jnp.where(lane < 2 * half, s, 0.0)] * 2, axis=1) * scale
        for p in range(tn // pair):
            a = res[:, p * pair:(p + 1) * pair]
            partner = jnp.dot(a.astype(BF16), perm_ref[...], preferred_element_type=F32)
            emit(p, a * c + partner * s)

    is_att = (mode == MODE_ATT_Q) | (mode == MODE_ATT_K)
    is_ret = (mode == MODE_RET_Q) | (mode == MODE_RET_K)
    pl.when(mode == MODE_PLAIN)(plain)
    pl.when(is_att)(lambda: rotary(ta_ref, pa_ref, jnp.where(mode == MODE_ATT_Q, ATT_HEAD_DIM ** -0.5, 1.0),
                                   ROPE_DIM // 2))
    if d == 1:
        pl.when(is_ret)(lambda: rotary(tr_ref, pr_ref, jnp.where(mode == MODE_RET_K, RET_QK_DIM ** -0.5, 1.0),
                                       RET_QK_DIM // 2))
        return

    acc_ref = acc[0]
    for h in range(tn // LANES):
        sl = slice(h * LANES, (h + 1) * LANES)
        if d == 4:
            P = ATT_BLOCK * d
            for sb in range(tm // P):
                for r in range(d):
                    o_ref[sb, r, :, sl] = acc_ref[h, pl.ds(sb * P + r, ATT_BLOCK, stride=d), :].astype(o_ref.dtype)
        else:
            for r in range(d):
                o_ref[r, :, sl] = acc_ref[h, pl.ds(r, tm // d, stride=ACC_PITCH), :].astype(o_ref.dtype)


def _rotation_matrix(half, width):
    k = jnp.arange(2 * LANES, dtype=jnp.int32)[:, None]
    l = jnp.arange(2 * LANES, dtype=jnp.int32)[None, :]
    same_head = (k // LANES) == (l // LANES)
    kk, ll = k % LANES, l % LANES
    neg = same_head & (ll < half) & (kk == ll + half)
    pos = same_head & (ll >= half) & (ll < width) & (kk == ll - half)
    return (pos.astype(F32) - neg.astype(F32)).astype(BF16)


def _input_projection(xb, w_in, tabs_a, tabs_r, cols, modes, d):
    S, D = xb.shape
    tm, tn = PROJ_ROWS, COL_BLOCK
    n = len(cols)
    P = ATT_BLOCK * d
    if d == 1:
        out_shape = jax.ShapeDtypeStruct((S, n * tn), BF16)
        out_spec = pl.BlockSpec((tm, tn), lambda j, i, c, m: (i, j))
    elif d == 4:
        out_shape = jax.ShapeDtypeStruct((S // P, d, ATT_BLOCK, n * tn), BF16)
        out_spec = pl.BlockSpec((tm // P, d, ATT_BLOCK, tn), lambda j, i, c, m: (i, 0, 0, j))
    else:
        assert d == 16 and P == 2 * tm
        out_shape = jax.ShapeDtypeStruct((S // P, d, ATT_BLOCK, n * tn), BF16)
        out_spec = pl.BlockSpec((None, d, tm // d, tn), lambda j, i, c, m: (i // 2, 0, i % 2, j))

    def tab_rows(j, i, c, m):
        return (jnp.where(m[j] != MODE_PLAIN, i, 0), 0)

    grid_spec = pltpu.PrefetchScalarGridSpec(
        num_scalar_prefetch=2,
        grid=(n, S // tm),
        in_specs=[
            pl.BlockSpec((tm, D), lambda j, i, c, m: (i, 0)),
            pl.BlockSpec((D, tn), lambda j, i, c, m: (0, c[j])),
            pl.BlockSpec((tm, LANES), tab_rows),
            pl.BlockSpec((tm, LANES), tab_rows),
            pl.BlockSpec((2 * LANES, 2 * LANES), lambda j, i, c, m: (0, 0)),
            pl.BlockSpec((2 * LANES, 2 * LANES), lambda j, i, c, m: (0, 0)),
        ],
        out_specs=out_spec,
        scratch_shapes=[pltpu.VMEM((D, tn), BF16)]
        + {1: [], 4: [pltpu.VMEM((tn // LANES, tm, LANES), F32)],
           16: [pltpu.VMEM((tn // LANES, tm // 16 * ACC_PITCH, LANES), F32)]}[d],
    )
    return pl.pallas_call(
        functools.partial(_inproj_kernel, d=d),
        grid_spec=grid_spec,
        out_shape=out_shape,
        compiler_params=_cparams(2),
        name=f"in_proj_d{d}",
    )(jnp.asarray(cols, jnp.int32), jnp.asarray(modes, jnp.int32), xb, w_in, tabs_a, tabs_r,
      _rotation_matrix(ROPE_DIM // 2, ROPE_DIM), _rotation_matrix(RET_QK_DIM // 2, RET_QK_DIM))


ATT_Q_BLOCKS = 8


def _attn_kernel(q_ref, k_ref, kp_ref, v_ref, vp_ref, o_ref, lse_ref):
    n = pl.program_id(1)
    qi = lax.broadcasted_iota(jnp.int32, (ATT_BLOCK, 2 * ATT_BLOCK), 0)
    kj = lax.broadcasted_iota(jnp.int32, (ATT_BLOCK, 2 * ATT_BLOCK), 1)
    dist = qi + ATT_BLOCK - kj
    band = (dist >= 0) & (dist <= ATT_STEPS)
    lane = lax.broadcasted_iota(jnp.int32, (ATT_BLOCK, LANES), 1)
    for sub in range(q_ref.shape[0]):
        mask = band & ((kj >= ATT_BLOCK) | (n > 0)) if sub == 0 else band
        lse_slab = jnp.zeros((ATT_BLOCK, LANES), F32)
        for h in range(ATT_HEADS_PER_GROUP):
            sl = slice(h * ATT_HEAD_DIM, (h + 1) * ATT_HEAD_DIM)
            k_prev = kp_ref[:, sl] if sub == 0 else k_ref[sub - 1, :, sl]
            v_prev = vp_ref[:, sl] if sub == 0 else v_ref[sub - 1, :, sl]
            k2 = jnp.concatenate([k_prev, k_ref[sub, :, sl]], axis=0)
            v2 = jnp.concatenate([v_prev, v_ref[sub, :, sl]], axis=0)
            s = lax.dot_general(q_ref[sub, :, sl], k2, (((1,), (1,)), ((), ())),
                                preferred_element_type=F32)
            s = jnp.where(mask, s, NEG_INF)
            m = jnp.max(s, axis=1, keepdims=True)
            p = jnp.exp(s - m)
            l = jnp.sum(p, axis=1, keepdims=True)
            o = jnp.dot(p.astype(BF16), v2, preferred_element_type=F32)
            o_ref[sub, :, sl] = o / l
            lse_slab = jnp.where(lane == h, m + jnp.log(l), lse_slab)
        lse_ref[sub] = lse_slab


def _dilated_attention(qkv, g, cq, ck, cv):
    d = DILATIONS[g]
    W = ATT_GROUP_WIDTH
    nb = qkv.shape[0]
    nq = min(ATT_Q_BLOCKS, nb)
    assert qkv.shape[1] == d and nb % nq == 0
    blk = (nq, None, ATT_BLOCK, W)
    pblk = (None, None, ATT_BLOCK, W)
    cur = lambda c: (lambda r, n: (n, r, 0, c))
    prev = lambda c: (lambda r, n: (jnp.maximum(nq * n - 1, 0), r, 0, c))
    return pl.pallas_call(
        _attn_kernel,
        grid=(d, nb // nq),
        in_specs=[
            pl.BlockSpec(blk, cur(cq)),
            pl.BlockSpec(blk, cur(ck)),
            pl.BlockSpec(pblk, prev(ck)),
            pl.BlockSpec(blk, cur(cv)),
            pl.BlockSpec(pblk, prev(cv)),
        ],
        out_specs=[pl.BlockSpec(blk, lambda r, n: (n, r, 0, 0)),
                   pl.BlockSpec((nq, None, ATT_BLOCK, LANES), lambda r, n: (n, r, 0, 0))],
        out_shape=[jax.ShapeDtypeStruct((nb, d, ATT_BLOCK, W), F32),
                   jax.ShapeDtypeStruct((nb, d, ATT_BLOCK, LANES), F32)],
        compiler_params=_cparams(2),
        name=f"dilated_attn_g{g}",
    )(qkv, qkv, qkv, qkv, qkv)


RET_CHUNKS_PER_STEP = 4


def _retention_kernel(q_ref, k_ref, vlo_ref, vhi_ref, glo_ref, ghi_ref, o_ref,
                      state_ref, intra_ref, dec_ref):
    @pl.when(pl.program_id(0) == 0)
    def _():
        state_ref[...] = jnp.zeros_like(state_ref)
        ri = lax.broadcasted_iota(jnp.int32, (RET_CHUNK, RET_CHUNK), 0)
        ci = lax.broadcasted_iota(jnp.int32, (RET_CHUNK, RET_CHUNK), 1)
        diff = (ri - ci).astype(F32)
        nrow = ri.astype(F32)
        for h in range(RET_HEADS):
            ld = RET_LOG_DECAY[h]
            intra_ref[h] = jnp.where(diff >= 0, jnp.exp(jnp.maximum(diff, 0.0) * ld), 0.0)
            dec_ref[h, 0] = jnp.exp((nrow + 1.0) * ld)
            dec_ref[h, 1] = jnp.exp((RET_CHUNK - 1.0 - nrow) * ld)

    half = RET_HEADS // 2
    for h in range(RET_HEADS):
        ld = RET_LOG_DECAY[h]
        sl = slice(h * RET_QK_DIM, (h + 1) * RET_QK_DIM)
        vsl = slice((h % half) * RET_V_DIM, (h % half + 1) * RET_V_DIM)
        v_ref, g_ref = (vlo_ref, glo_ref) if h < half else (vhi_ref, ghi_ref)
        state = state_ref[h]
        for c in range(q_ref.shape[0] // RET_CHUNK):
            rs = slice(c * RET_CHUNK, (c + 1) * RET_CHUNK)
            q = q_ref[rs, sl]
            k = k_ref[rs, sl]
            v = v_ref[rs, vsl]
            scores = (lax.dot_general(q, k, (((1,), (1,)), ((), ())), preferred_element_type=F32)
                      * intra_ref[h])
            inner = jnp.dot(scores.astype(BF16), v, preferred_element_type=F32)
            cross = jnp.dot((q.astype(F32) * dec_ref[h, 0]).astype(BF16), state.astype(BF16),
                            preferred_element_type=F32)
            kv = lax.dot_general((k.astype(F32) * dec_ref[h, 1]).astype(BF16), v,
                                 (((0,), (0,)), ((), ())), preferred_element_type=F32)
            state = math.exp(RET_CHUNK * ld) * state + kv
            o = inner + cross
            mu = jnp.mean(o, axis=1, keepdims=True)
            var = jnp.mean(jnp.square(o - mu), axis=1, keepdims=True)
            o = (o - mu) * lax.rsqrt(var + LN_EPS)
            gate = g_ref[rs, vsl].astype(F32)
            gate = gate * (1.0 / (1.0 + jnp.exp(-gate)))
            o_ref[rs, h * RET_V_DIM:(h + 1) * RET_V_DIM] = (gate * o).astype(o_ref.dtype)
        state_ref[h] = state


def _retention(nat, S):
    rows = RET_CHUNK * RET_CHUNKS_PER_STEP
    blk = (rows, COL_BLOCK)

    def col(cb):
        return pl.BlockSpec(blk, lambda c: (c, cb))

    return pl.pallas_call(
        _retention_kernel,
        grid=(S // rows,),
        in_specs=[col(NC_QR), col(NC_KR), col(NC_VR), col(NC_VR + 1), col(NC_GR), col(NC_GR + 1)],
        out_specs=pl.BlockSpec((rows, RET_HEADS * RET_V_DIM), lambda c: (c, 0)),
        out_shape=jax.ShapeDtypeStruct((S, RET_HEADS * RET_V_DIM), BF16),
        scratch_shapes=[pltpu.VMEM((RET_HEADS, RET_QK_DIM, RET_V_DIM), F32),
                        pltpu.VMEM((RET_HEADS, RET_CHUNK, RET_CHUNK), F32),
                        pltpu.VMEM((RET_HEADS, 2, RET_CHUNK, RET_QK_DIM), F32)],
        compiler_params=_cparams(1),
        name="retention",
    )(nat, nat, nat, nat, nat, nat)


MERGE_ROWS = 512
COL_HALVES = (slice(0, D_MODEL // 2), slice(D_MODEL // 2, D_MODEL))


def _layer_norm(z, scale, bias):
    mu = jnp.mean(z, axis=1, keepdims=True)
    var = jnp.mean(jnp.square(z - mu), axis=1, keepdims=True)
    return (z - mu) * lax.rsqrt(var + LN_EPS) * scale + bias


def _sigmoid(x):
    return 1.0 / (1.0 + jnp.exp(-x))


def _pack_bf16_pair(a, b):
    abits = lax.bitcast_convert_type(a.astype(BF16).astype(F32), jnp.uint32)
    bbits = lax.bitcast_convert_type(b.astype(BF16).astype(F32), jnp.uint32)
    return (abits >> 16) | (bbits & jnp.uint32(0xFFFF0000))


def _unpack_bf16_pair(w):
    a = lax.bitcast_convert_type(w << 16, F32).astype(BF16)
    b = lax.bitcast_convert_type(w & jnp.uint32(0xFFFF0000), F32).astype(BF16)
    return a, b


def _split_bf16(v):
    hi = v.astype(BF16)
    return hi, (v - hi.astype(F32)).astype(BF16)


def _mix_kernel(o0_ref, o1_ref, o2_ref, l0_ref, l1_ref, l2_ref, or_ref,
                ga0_ref, ga1_ref, gb0_ref, gb1_ref, wa_ref, wr_ref, hexp_ref,
                m_ref, s1_ref, s2_ref, ls1_ref, ls2_ref):
    tm = o0_ref.shape[0]
    for d, src, lsrc, dst, ldst in ((DILATIONS[1], o1_ref, l1_ref, s1_ref, ls1_ref),
                                    (DILATIONS[2], o2_ref, l2_ref, s2_ref, ls2_ref)):
        for r in range(d):
            for h in range(ATT_HEADS_PER_GROUP):
                dst[h, pl.ds(r, tm // d, stride=d), :] = src[r, :, h * ATT_HEAD_DIM:(h + 1) * ATT_HEAD_DIM]
            ldst[pl.ds(r, tm // d, stride=d), :] = lsrc[r]

    l0, l1, l2 = l0_ref[...], ls1_ref[...], ls2_ref[...]
    lm = jnp.maximum(jnp.maximum(l0, l1), l2)
    e0, e1, e2 = jnp.exp(l0 - lm), jnp.exp(l1 - lm), jnp.exp(l2 - lm)
    den = e0 + e1 + e2

    def lanes_of_head(w):
        hi, lo = _split_bf16(w)
        return (jnp.dot(hi, hexp_ref[...], preferred_element_type=F32)
                + jnp.dot(lo, hexp_ref[...], preferred_element_type=F32))

    o_r = or_ref[...]
    y_r = [jnp.dot(o_r, wr_ref[:, cs], preferred_element_type=F32) for cs in COL_HALVES]
    heads = range(ATT_HEADS_PER_GROUP)
    o1 = jnp.concatenate([s1_ref[h] for h in heads], axis=1)
    o2 = jnp.concatenate([s2_ref[h] for h in heads], axis=1)
    o_a = (lanes_of_head(e0 / den) * o0_ref[...] + lanes_of_head(e1 / den) * o1
           + lanes_of_head(e2 / den) * o2).astype(BF16)
    for n, cs in enumerate(COL_HALVES):
        y_a = jnp.dot(o_a, wa_ref[:, cs], preferred_element_type=F32)
        gate_a = (ga0_ref, ga1_ref)[n][...].astype(F32)
        gate_b = (gb0_ref, gb1_ref)[n][...].astype(F32)
        m_ref[:, cs] = (_sigmoid(gate_a) * y_a + _sigmoid(gate_b) * y_r[n]).astype(m_ref.dtype)


def _finish_kernel(m_ref, x_ref, wo_ref, lns_ref, lnb_ref, wrt_ref, brt_ref,
                   x1_ref, x1p_ref, route_ref, cnt_ref):
    rc = x_ref.shape[0]
    half = D_MODEL // 2
    halves = COL_HALVES

    @pl.when(pl.program_id(0) == 0)
    def _():
        cnt_ref[...] = jnp.zeros_like(cnt_ref)

    z = [ALPHA * x_ref[:, cs]
         + jnp.dot(m_ref[:, :half], wo_ref[:half, cs], preferred_element_type=F32)
         + jnp.dot(m_ref[:, half:], wo_ref[half:, cs], preferred_element_type=F32) for cs in halves]
    mu = (jnp.sum(z[0], axis=1, keepdims=True) + jnp.sum(z[1], axis=1, keepdims=True)) / D_MODEL
    var = (jnp.sum(jnp.square(z[0] - mu), axis=1, keepdims=True)
           + jnp.sum(jnp.square(z[1] - mu), axis=1, keepdims=True)) / D_MODEL
    inv = lax.rsqrt(var + LN_EPS)
    x1 = [(z[n] - mu) * inv * lns_ref[:, cs] + lnb_ref[:, cs] for n, cs in enumerate(halves)]
    for n, cs in enumerate(halves):
        x1_ref[:, cs] = x1[n]
    packed = _pack_bf16_pair(x1[0], x1[1])
    for c in range(PACKED_ROW_PIECES):
        x1p_ref[pl.ds(c, rc, stride=PACKED_ROW_PIECES), :] = (
            packed[:, c * LANES:(c + 1) * LANES])

    logits = brt_ref[...]
    for n, cs in enumerate(halves):
        x_hi, x_lo = _split_bf16(x1[n])
        hl = jnp.dot(x_hi, wrt_ref[cs, :], preferred_element_type=F32)
        logits = (logits + hl[:, :LANES] + hl[:, LANES:]
                  + jnp.dot(x_lo, wrt_ref[cs, :LANES], preferred_element_type=F32))
    lane = lax.broadcasted_iota(jnp.int32, logits.shape, 1)
    is_g = lane < N_EXPERT_GROUPS
    gl = jnp.where(is_g, logits, NEG_INF)
    gmax = jnp.max(gl, axis=1, keepdims=True)
    g_sel = jnp.min(jnp.where(is_g & (gl == gmax), lane, LANES), axis=1, keepdims=True)
    p_group = 1.0 / jnp.sum(jnp.where(is_g, jnp.exp(gl - gmax), 0.0), axis=1, keepdims=True)
    lo = N_EXPERT_GROUPS + EXPERTS_PER_GROUP * g_sel
    in_grp = (lane >= lo) & (lane < lo + EXPERTS_PER_GROUP)
    el = jnp.where(in_grp, logits, NEG_INF)
    top1 = jnp.max(el, axis=1, keepdims=True)
    idx1 = jnp.min(jnp.where(in_grp & (el == top1), lane, LANES), axis=1, keepdims=True)
    el2 = jnp.where(lane == idx1, NEG_INF, el)
    top2 = jnp.max(el2, axis=1, keepdims=True)
    idx2 = jnp.min(jnp.where(in_grp & (lane != idx1) & (el2 == top2), lane, LANES), axis=1, keepdims=True)
    t = jnp.exp(top2 - top1)
    wt1 = p_group * (1.0 / (1.0 + t))
    wt2 = p_group * (t / (1.0 + t))
    slab = jnp.where(lane == 0, (idx1 - N_EXPERT_GROUPS).astype(F32), 0.0)
    slab = jnp.where(lane == 1, (idx2 - N_EXPERT_GROUPS).astype(F32), slab)
    slab = jnp.where(lane == 2, wt1, slab)
    slab = jnp.where(lane == 3, wt2, slab)

    hot = jnp.where((lane == idx1) | (lane == idx2), 1.0, 0.0)
    ri = lax.broadcasted_iota(jnp.int32, (rc, rc), 0)
    ci = lax.broadcasted_iota(jnp.int32, (rc, rc), 1)
    ltri = jnp.where(ri > ci, 1.0, 0.0).astype(BF16)
    prefix = jnp.dot(ltri, hot.astype(BF16), preferred_element_type=F32) + cnt_ref[...]
    rank1 = jnp.sum(jnp.where(lane == idx1, prefix, 0.0), axis=1, keepdims=True)
    rank2 = jnp.sum(jnp.where(lane == idx2, prefix, 0.0), axis=1, keepdims=True)
    cnt_ref[...] = cnt_ref[...] + jnp.sum(hot, axis=0, keepdims=True)
    slab = jnp.where(lane == 4, rank1, slab)
    slab = jnp.where(lane == 5, rank2, slab)
    route_ref[...] = slab


def _row_spec(tm, width):
    return pl.BlockSpec((tm, width), lambda i: (i, 0))


def _const_spec(shape):
    return pl.BlockSpec(shape, lambda i: (0, 0), pipeline_mode=pl.Buffered(1))


def _mix(o_g, lse_g, o_r, nat, wa, wr, S):
    tm = MERGE_ROWS
    D = D_MODEL
    W = ATT_GROUP_WIDTH
    head_expand = (jnp.arange(W, dtype=jnp.int32)[None, :] // ATT_HEAD_DIM
                   == jnp.arange(LANES, dtype=jnp.int32)[:, None]).astype(BF16)

    def streams(g, width):
        d = DILATIONS[g]
        per = (ATT_BLOCK * d) // tm
        return pl.BlockSpec((None, d, tm // d, width), lambda i: (i // per, 0, i % per, 0))

    def ncol(cb):
        return pl.BlockSpec((tm, COL_BLOCK), lambda i: (i, cb))

    return pl.pallas_call(
        _mix_kernel,
        grid=(S // tm,),
        in_specs=[
            _row_spec(tm, W), streams(1, W), streams(2, W),
            _row_spec(tm, LANES), streams(1, LANES), streams(2, LANES),
            _row_spec(tm, RET_HEADS * RET_V_DIM),
            ncol(NC_GATE_A), ncol(NC_GATE_A + 1), ncol(NC_GATE_B), ncol(NC_GATE_B + 1),
            _const_spec((W, D)), _const_spec((RET_HEADS * RET_V_DIM, D)), _const_spec((LANES, W)),
        ],
        out_specs=_row_spec(tm, D),
        out_shape=jax.ShapeDtypeStruct((S, D), BF16),
        scratch_shapes=[pltpu.VMEM((ATT_HEADS_PER_GROUP, tm, ATT_HEAD_DIM), F32),
                        pltpu.VMEM((ATT_HEADS_PER_GROUP, tm, ATT_HEAD_DIM), F32),
                        pltpu.VMEM((tm, LANES), F32), pltpu.VMEM((tm, LANES), F32)],
        compiler_params=_cparams(1),
        name="mix_branches",
    )(*o_g, *lse_g, o_r, nat, nat, nat, nat, wa, wr, head_expand)


def _finish(merged, x2d, wo, lns, lnb, wrt, brt):
    S, D = x2d.shape
    tm = MERGE_ROWS
    return pl.pallas_call(
        _finish_kernel,
        grid=(S // tm,),
        in_specs=[
            _row_spec(tm, D), _row_spec(tm, D),
            _const_spec((D, D)), _const_spec((1, D)), _const_spec((1, D)),
            _const_spec((D, 2 * LANES)), _const_spec((1, LANES)),
        ],
        out_specs=[_row_spec(tm, D), pl.BlockSpec((tm * PACKED_ROW_PIECES, LANES), lambda i: (i, 0)),
                   _row_spec(tm, LANES), pl.BlockSpec((1, LANES), lambda i: (0, 0))],
        out_shape=[jax.ShapeDtypeStruct((S, D), F32),
                   jax.ShapeDtypeStruct((S * PACKED_ROW_PIECES, LANES), jnp.uint32),
                   jax.ShapeDtypeStruct((S, LANES), F32),
                   jax.ShapeDtypeStruct((1, LANES), F32)],
        compiler_params=_cparams(1),
        name="out_proj_ln_router",
    )(merged, x2d, wo, lns, lnb, wrt, brt)


META_N_USED, META_NEXT = 0, 1
META_SIZE = META_NEXT + N_EXPERTS
GATHER_SLOTS = 4
GATHER_GROUP = 2
WEIGHT_SLOTS = 2


def _plan_kernel(dest_ref, cnt_ref, rt_ref, be_ref, meta_ref):
    n_blocks = be_ref.shape[0]

    def zero_block(first_row):
        def body(j, carry):
            rt_ref[first_row + j] = 0
            return carry
        lax.fori_loop(0, MOE_BLOCK, body, 0, unroll=32)

    def per_expert(e, start):
        nb = lax.shift_right_logical(cnt_ref[e] + (MOE_BLOCK - 1), MOE_BLOCK_SHIFT)

        def fill(b, carry):
            be_ref[lax.shift_right_logical(start, MOE_BLOCK_SHIFT) + b] = e
            return carry
        lax.fori_loop(0, nb, fill, 0)

        @pl.when(nb > 0)
        def _():
            zero_block(start + (nb - 1) * MOE_BLOCK)
        return start + nb * MOE_BLOCK
    total = lax.fori_loop(0, N_EXPERTS, per_expert, 0)
    n_used = lax.shift_right_logical(total, MOE_BLOCK_SHIFT)
    meta_ref[META_N_USED] = n_used

    def tail(b, carry):
        be_ref[b] = be_ref[jnp.maximum(n_used - 1, 0)]
        zero_block(b * MOE_BLOCK)
        return carry
    lax.fori_loop(n_used, n_blocks, tail, 0)

    def nxt(k, cur):
        e = N_EXPERTS - 1 - k
        meta_ref[META_NEXT + e] = cur
        return jnp.where(cnt_ref[e] > 0, e, cur)
    lax.fori_loop(0, N_EXPERTS, nxt, N_EXPERTS)

    def assign(a, carry):
        rt_ref[dest_ref[a]] = lax.shift_right_logical(a, 1)
        return carry
    lax.fori_loop(0, dest_ref.shape[0], assign, 0, unroll=128)


def _plan(dest, counts):
    A = dest.shape[0]
    n_blocks = A // MOE_BLOCK + N_EXPERTS
    smem = pl.BlockSpec(memory_space=pltpu.SMEM)
    return pl.pallas_call(
        _plan_kernel,
        in_specs=[smem, smem],
        out_specs=[smem, smem, smem],
        out_shape=[
            jax.ShapeDtypeStruct((n_blocks * MOE_BLOCK,), jnp.int32),
            jax.ShapeDtypeStruct((n_blocks,), jnp.int32),
            jax.ShapeDtypeStruct((META_SIZE,), jnp.int32),
        ],
        name="moe_plan",
    )(dest, counts)


def _expert_kernel(be_ref, meta_ref, rt_ref, x_hbm, wg_hbm, wu_hbm, wd_hbm, y_ref,
                   xg_ref, wgf_ref, wuf_ref, wdf_ref, wgb_ref, wub_ref, wdb_ref, xsem, sem, slot_ref):
    b = pl.program_id(0)
    n_used = meta_ref[META_N_USED]
    e = be_ref[b]
    first = (b == 0) | (e != be_ref[jnp.maximum(b - 1, 0)])

    xp = PACKED_ROW_PIECES

    last = jnp.maximum(n_used - 1, 0)
    group_rows = GATHER_GROUP * MOE_BLOCK
    group = b // GATHER_GROUP
    last_group = last // GATHER_GROUP

    def row_copy(grp, slot, i):
        tok = rt_ref[grp * group_rows + i]
        return pltpu.make_async_copy(x_hbm.at[pl.ds(pl.multiple_of(tok * xp, xp), xp)],
                                     xg_ref.at[slot, pl.ds(pl.multiple_of(i * xp, xp), xp)],
                                     xsem.at[slot])

    def wait_rows(slot):
        pltpu.make_async_copy(x_hbm.at[pl.ds(0, group_rows * xp)], xg_ref.at[slot], xsem.at[slot]).wait()

    @pl.when(b == 0)
    def _():
        for ahead in range(GATHER_SLOTS - 1):
            def body(i, carry, ahead=ahead):
                row_copy(jnp.minimum(ahead, last_group), ahead, i).start()
                return carry
            lax.fori_loop(0, group_rows, body, 0, unroll=8)

    def copies(ex, s):
        return (pltpu.make_async_copy(wg_hbm.at[ex], wgf_ref.at[s], sem.at[s]),
                pltpu.make_async_copy(wu_hbm.at[ex], wuf_ref.at[s], sem.at[s]),
                pltpu.make_async_copy(wd_hbm.at[ex], wdf_ref.at[s], sem.at[s]))

    def next_expert(ex):
        return jnp.where(ex < N_EXPERTS, meta_ref[META_NEXT + jnp.minimum(ex, N_EXPERTS - 1)], N_EXPERTS)

    def fetch(ex, s):
        @pl.when(ex < N_EXPERTS)
        def _():
            for c in copies(ex, s):
                c.start()

    @pl.when((b == 0) & (n_used > 0))
    def _():
        slot_ref[0] = 0
        ex = e
        for s in range(WEIGHT_SLOTS - 1):
            fetch(ex, s)
            ex = next_expert(ex)

    @pl.when(first & (b < n_used))
    def _():
        s = slot_ref[0]
        ex = e
        for _ in range(WEIGHT_SLOTS - 1):
            ex = next_expert(ex)
        fetch(ex, lax.rem(s + WEIGHT_SLOTS - 1, WEIGHT_SLOTS))
        for c in copies(e, s):
            c.wait()
        wgb_ref[...] = wgf_ref[s].astype(BF16)
        wub_ref[...] = wuf_ref[s].astype(BF16)
        wdb_ref[...] = wdf_ref[s].astype(BF16)
        slot_ref[0] = lax.rem(s + 1, WEIGHT_SLOTS)

    def step(slot, sub):
        if sub == 0:
            wait_rows(slot)
            nxt = jnp.minimum(group + (GATHER_SLOTS - 1), last_group)
            for i in range(group_rows):
                row_copy(nxt, (slot + GATHER_SLOTS - 1) % GATHER_SLOTS, i).start()
        base = sub * MOE_BLOCK * xp
        pieces = [_unpack_bf16_pair(xg_ref[slot, pl.ds(base + c, MOE_BLOCK, stride=xp), :])
                  for c in range(xp)]
        xb = jnp.concatenate([p[0] for p in pieces] + [p[1] for p in pieces], axis=1)
        hg = jnp.dot(xb, wgb_ref[...], preferred_element_type=F32)
        hu = jnp.dot(xb, wub_ref[...], preferred_element_type=F32)
        hid = (hg * _sigmoid(hg) * hu).astype(BF16)
        y = jnp.dot(hid, wdb_ref[...], preferred_element_type=F32)
        for c in range(ROW_PIECES):
            y_ref[pl.ds(c, MOE_BLOCK, stride=ROW_PIECES), :] = y[:, c * LANES:(c + 1) * LANES]

    for s in range(GATHER_SLOTS):
        for sub in range(GATHER_GROUP):
            pl.when((b < n_used) & (lax.rem(group, GATHER_SLOTS) == s)
                    & (lax.rem(b, GATHER_GROUP) == sub))(functools.partial(step, s, sub))

    @pl.when(b == last)
    def _():
        for ahead in range(1, GATHER_SLOTS):
            wait_rows(lax.rem(group + ahead, GATHER_SLOTS))

    @pl.when(b >= n_used)
    def _():
        y_ref[...] = jnp.zeros_like(y_ref)


def _experts(x1p, block_expert, meta, row_tok, w_gate, w_up, w_down):
    D = D_MODEL
    assert x1p.shape[1] == LANES and x1p.dtype == jnp.uint32
    n_blocks = block_expert.shape[0]
    R = n_blocks * MOE_BLOCK
    H = EXPERT_HIDDEN
    grid_spec = pltpu.PrefetchScalarGridSpec(
        num_scalar_prefetch=3,
        grid=(n_blocks,),
        in_specs=[
            pl.BlockSpec(memory_space=pl.ANY),
            pl.BlockSpec(memory_space=pl.ANY),
            pl.BlockSpec(memory_space=pl.ANY),
            pl.BlockSpec(memory_space=pl.ANY),
        ],
        out_specs=pl.BlockSpec((MOE_BLOCK * ROW_PIECES, LANES), lambda b, be, meta, rt: (b, 0)),
        scratch_shapes=[
            pltpu.VMEM((GATHER_SLOTS, GATHER_GROUP * MOE_BLOCK * PACKED_ROW_PIECES, LANES), jnp.uint32),
            pltpu.VMEM((WEIGHT_SLOTS, D, H), F32),
            pltpu.VMEM((WEIGHT_SLOTS, D, H), F32),
            pltpu.VMEM((WEIGHT_SLOTS, H, D), F32),
            pltpu.VMEM((D, H), BF16),
            pltpu.VMEM((D, H), BF16),
            pltpu.VMEM((H, D), BF16),
            pltpu.SemaphoreType.DMA((GATHER_SLOTS,)),
            pltpu.SemaphoreType.DMA((WEIGHT_SLOTS,)),
            pltpu.SMEM((1,), jnp.int32),
        ],
    )
    return pl.pallas_call(
        _expert_kernel,
        grid_spec=grid_spec,
        out_shape=jax.ShapeDtypeStruct((R * ROW_PIECES, LANES), F32),
        compiler_params=_cparams(1),
        name="moe_experts",
    )(block_expert, meta, row_tok, x1p, w_gate, w_up, w_down)


COMBINE_TOKENS = 256


def _combine_kernel(dest_ref, y_hbm, x1_ref, route_ref, lns_ref, lnb_ref, o_ref, yg_ref, sem):
    i = pl.program_id(0)
    tt = COMBINE_TOKENS
    n = dest_ref.shape[0] // (TOP_K * tt)

    rp = ROW_PIECES

    def start_token(step, slot, r):
        for k in range(TOP_K):
            row = dest_ref[(step * tt + r) * TOP_K + k]
            pltpu.make_async_copy(y_hbm.at[pl.ds(pl.multiple_of(row * rp, rp), rp)],
                                  yg_ref.at[slot, k, pl.ds(pl.multiple_of(r * ROW_PITCH, 8), rp)],
                                  sem.at[slot]).start(priority=k)

    def wait_slot(slot):
        for k in range(TOP_K):
            pltpu.make_async_copy(y_hbm.at[pl.ds(0, tt * rp)], yg_ref.at[slot, k, pl.ds(0, tt * rp)],
                                  sem.at[slot]).wait()

    @pl.when(i == 0)
    def _():
        for ahead in range(GATHER_SLOTS - 1):
            def body(r, carry, ahead=ahead):
                start_token(min(ahead, n - 1), ahead, r)
                return carry
            lax.fori_loop(0, tt, body, 0, unroll=4)

    def step(slot):
        wait_slot(slot)
        nxt = jnp.minimum(i + (GATHER_SLOTS - 1), n - 1)
        for r in range(tt):
            start_token(nxt, (slot + GATHER_SLOTS - 1) % GATHER_SLOTS, r)
        route = route_ref[...]
        w1, w2 = route[:, 2:3], route[:, 3:4]
        moe = jnp.concatenate(
            [w1 * yg_ref[slot, 0, pl.ds(c, tt, stride=ROW_PITCH), :]
             + w2 * yg_ref[slot, 1, pl.ds(c, tt, stride=ROW_PITCH), :] for c in range(rp)], axis=1)
        o_ref[...] = _layer_norm(ALPHA * x1_ref[...] + moe, lns_ref[...], lnb_ref[...])

    for s in range(GATHER_SLOTS):
        pl.when(lax.rem(i, GATHER_SLOTS) == s)(functools.partial(step, s))

    @pl.when(i == n - 1)
    def _():
        for ahead in range(1, GATHER_SLOTS):
            wait_slot((n - 1 + ahead) % GATHER_SLOTS)


def _combine(dest, y, x1, route, lns, lnb):
    T, D = x1.shape
    tt = COMBINE_TOKENS
    grid_spec = pltpu.PrefetchScalarGridSpec(
        num_scalar_prefetch=1,
        grid=(T // tt,),
        in_specs=[
            pl.BlockSpec(memory_space=pl.ANY),
            pl.BlockSpec((tt, D), lambda i, d: (i, 0)),
            pl.BlockSpec((tt, LANES), lambda i, d: (i, 0)),
            pl.BlockSpec((1, D), lambda i, d: (0, 0)),
            pl.BlockSpec((1, D), lambda i, d: (0, 0)),
        ],
        out_specs=pl.BlockSpec((tt, D), lambda i, d: (i, 0)),
        scratch_shapes=[
            pltpu.VMEM((GATHER_SLOTS, TOP_K, tt * ROW_PITCH, LANES), F32),
            pltpu.SemaphoreType.DMA((GATHER_SLOTS,)),
        ],
    )
    return pl.pallas_call(
        _combine_kernel,
        grid_spec=grid_spec,
        out_shape=jax.ShapeDtypeStruct((T, D), F32),
        compiler_params=_cparams(1),
        name="moe_combine_ln",
    )(dest, y, x1, route, lns, lnb)


def _rotary_tables(positions):
    pos = positions.reshape(-1).astype(F32)
    S = pos.shape[0]
    inv_a = ROPE_THETA ** (-jnp.arange(0, ROPE_DIM, 2, dtype=F32) / ROPE_DIM)
    inv_r = RET_ROT_BASE ** (-jnp.linspace(0.0, 1.0, RET_QK_DIM // 2, dtype=F32))
    ang = jnp.concatenate([inv_a, inv_r])[:, None] * pos[None, :]
    cos_t, sin_t = lax.optimization_barrier((jnp.cos(ang), jnp.sin(ang)))
    na = ROPE_DIM // 2
    cos, sin, cr, sr = cos_t[:na].T, sin_t[:na].T, cos_t[na:].T, sin_t[na:].T
    tabs_a = jnp.concatenate([cos, sin, jnp.zeros((S, LANES - ROPE_DIM), F32)], axis=1)
    tabs_r = jnp.concatenate([cr, sr], axis=1)
    return tabs_a, tabs_r


def kernel(x, positions, w_in, w_branch_attn, w_branch_ret, w_out, ln1_scale, ln1_bias, w_group_router, b_group_router, w_expert_router, b_expert_router, w_expert_gate, w_expert_up, w_expert_down, ln2_scale, ln2_bias):
    B, S, D = x.shape
    assert B == 1 and D == D_MODEL and w_in.shape[0] == DEPTH == 1
    assert S % (ATT_BLOCK * DILATIONS[2]) == 0
    x2d = x.reshape(S, D)
    xb = x2d.astype(BF16)
    tabs_a, tabs_r = _rotary_tables(positions)

    nat = _input_projection(xb, w_in[0], tabs_a, tabs_r, NAT_COLS, NAT_MODES, 1)
    qkv_modes = (MODE_ATT_Q, MODE_ATT_K, MODE_PLAIN)
    qkv = [None] + [
        _input_projection(xb, w_in[0], tabs_a, tabs_r, (CB_QA + g, CB_KA + g, CB_VA + g), qkv_modes,
                          DILATIONS[g])
        for g in (1, 2)]
    o0, l0 = _dilated_attention(nat.reshape(S // ATT_BLOCK, 1, ATT_BLOCK, nat.shape[1]), 0,
                                NC_Q0, NC_K0, NC_V0)
    att = [(o0.reshape(S, ATT_GROUP_WIDTH), l0.reshape(S, LANES))]
    att += [_dilated_attention(qkv[g], g, 0, 1, 2) for g in (1, 2)]
    o_r = _retention(nat, S)

    w_route = jnp.concatenate(
        [w_group_router[0], w_expert_router[0].transpose(1, 0, 2).reshape(D, N_EXPERTS),
         jnp.zeros((D, LANES - N_EXPERT_GROUPS - N_EXPERTS), F32)], axis=1)
    b_route = jnp.concatenate(
        [b_group_router[0], b_expert_router[0].reshape(N_EXPERTS),
         jnp.zeros((LANES - N_EXPERT_GROUPS - N_EXPERTS,), F32)]).reshape(1, LANES)
    merged = _mix([a[0] for a in att], [a[1] for a in att], o_r, nat,
                  w_branch_attn[0].astype(BF16), w_branch_ret[0].astype(BF16), S)
    x1, x1p, route, cnt = _finish(
        merged, x2d, w_out[0].astype(BF16), ln1_scale[0].reshape(1, D), ln1_bias[0].reshape(1, D),
        jnp.concatenate(_split_bf16(w_route), axis=1), b_route)

    e_flat = route[:, 0:TOP_K].astype(jnp.int32).reshape(S * TOP_K)
    rank_flat = route[:, 4:4 + TOP_K].astype(jnp.int32).reshape(S * TOP_K)
    counts = cnt[0, N_EXPERT_GROUPS:N_EXPERT_GROUPS + N_EXPERTS].astype(jnp.int32)
    padded = (counts + (MOE_BLOCK - 1)) // MOE_BLOCK * MOE_BLOCK
    pstart = jnp.cumsum(padded) - padded
    hot = e_flat[:, None] == jnp.arange(N_EXPERTS, dtype=jnp.int32)[None, :]
    dest = jnp.sum(jnp.where(hot, pstart[None, :], 0), axis=1) + rank_flat
    row_tok, block_expert, meta = _plan(dest, counts)
    y = _experts(x1p, block_expert, meta, row_tok, w_expert_gate[0], w_expert_up[0], w_expert_down[0])
    out = _combine(dest, y, x1, route, ln2_scale[0].reshape(1, D), ln2_bias[0].reshape(1, D))
    return out.reshape(B, S, D)
```
